```python
import jax, jax.numpy as jnp
from jax import lax
import numpy as np

D_MODEL = 1024
BATCH = 4
SEQ = 8192
DEPTH = 1
DEC_BATCH = 128
DEC_SEQ = 1
PAST_LEN = 16384
PAGE_SIZE = 128

GLA_HEADS = 4
GLA_DV = D_MODEL // 2 // GLA_HEADS
GLA_DK = GLA_DV // 2
GLA_WIDTH = GLA_HEADS * GLA_DV
GLA_KWIDTH = GLA_HEADS * GLA_DK
GLA_RANK = 16
GLA_TAU = 16.0
GLA_CHUNK = 64
SWA_HEADS = 8
SWA_HD = 64
SWA_KV_HEADS = 2
SWA_GROUP = SWA_HEADS // SWA_KV_HEADS
SWA_WIDTH = SWA_HEADS * SWA_HD
SWA_KVWIDTH = SWA_KV_HEADS * SWA_HD
WINDOW = 128
MIX_WIDTH = GLA_WIDTH + SWA_WIDTH
EPS = 1e-6
NEG_INF = -1e30
SPLITS = (GLA_KWIDTH, GLA_KWIDTH, GLA_WIDTH, GLA_WIDTH, GLA_RANK,
          SWA_WIDTH, SWA_KVWIDTH, SWA_KVWIDTH, SWA_WIDTH)
IN_WIDTH = sum(SPLITS)
SPLIT_POINTS = [int(o) for o in np.cumsum(SPLITS)[:-1]]

kernel_name = "hymba_gla_swa_sink_step"


def _rmsnorm(x, g):
    xf = x.astype(jnp.float32)
    y = xf * lax.rsqrt(jnp.mean(xf * xf, axis=-1, keepdims=True) + EPS) * g.astype(jnp.float32)
    return y.astype(x.dtype)


def _project(h, w_in, w_gate_up, b_gate):
    B, L, _ = h.shape
    proj = jnp.einsum('bld,de->ble', h, w_in)
    gq, gk, gv, g_gate, g_low, sq, sk, sv, s_gate = jnp.split(proj, SPLIT_POINTS, axis=-1)
    z = jnp.einsum('blr,rk->blk', g_low, w_gate_up) + b_gate
    log_a = jax.nn.log_sigmoid(z.astype(jnp.float32)) / GLA_TAU
    heads = lambda t, d: t.reshape(B, L, GLA_HEADS, d)
    gla = (heads(gq, GLA_DK), heads(gk, GLA_DK), heads(gv, GLA_DV), heads(log_a, GLA_DK))
    return gla, g_gate, (sq, sk, sv), s_gate


def _gla_scan(q, k, v, log_a, s0):
    B, L = q.shape[:2]
    c = min(GLA_CHUNK, L)
    n = -(-L // c)
    pad = n * c - L

    def prep(t):
        t = jnp.pad(t.astype(jnp.float32), ((0, 0), (0, pad), (0, 0), (0, 0)))
        return t.reshape(B, n, c, GLA_HEADS, t.shape[-1]).transpose(1, 0, 3, 2, 4)

    qs, ks, vs, gs = prep(q * (GLA_DK ** -0.5)), prep(k), prep(v), prep(log_a)
    tri = jnp.tril(jnp.ones((c, c), dtype=bool))[:, :, None]

    def step(S, inp):
        qc, kc, vc, gc = inp
        b = jnp.cumsum(gc, axis=-2)
        o_inter = jnp.einsum('bhid,bhde->bhie', qc * jnp.exp(b), S)
        diff = b[..., :, None, :] - b[..., None, :, :]
        decay = jnp.where(tri, jnp.exp(jnp.where(tri, diff, 0.0)), 0.0)
        attn = jnp.einsum('bhid,bhjd,bhijd->bhij', qc, kc, decay)
        o = o_inter + jnp.einsum('bhij,bhje->bhie', attn, vc)
        b_last = b[..., -1:, :]
        S = jnp.exp(b_last[..., 0, :])[..., None] * S + \
            jnp.einsum('bhjd,bhje->bhde', kc * jnp.exp(b_last - b), vc)
        return S, o

    S, o = lax.scan(step, s0.astype(jnp.float32), (qs, ks, vs, gs))
    o = o.transpose(1, 0, 3, 2, 4).reshape(B, n * c, GLA_HEADS, GLA_DV)[:, :L]
    return o, S


def _gla_branch(q, k, v, log_a, s0, gla_norm):
    B, L = q.shape[:2]
    o, S = _gla_scan(q, k, v, log_a, s0)
    o = _rmsnorm(o, gla_norm)
    return o.reshape(B, L, GLA_WIDTH), S


def _sink_attn(q, k, v, valid, sinks):
    s = jnp.einsum('...qkgd,...skd->...kgqs', q, k).astype(jnp.float32) * (SWA_HD ** -0.5)
    s = jnp.where(valid, s, NEG_INF)
    sink = jnp.broadcast_to(sinks.astype(jnp.float32).reshape(SWA_KV_HEADS, SWA_GROUP, 1, 1),
                            s.shape[:-1] + (1,))
    p = jax.nn.softmax(jnp.concatenate([s, sink], axis=-1), axis=-1)[..., :-1]
    return jnp.einsum('...kgqs,...skd->...qkgd', p.astype(v.dtype), v)


def _swa_prompt(sq, sk, sv, sinks):
    B, L, _ = sq.shape
    nb = L // WINDOW
    q = sq.reshape(B, nb, WINDOW, SWA_KV_HEADS, SWA_GROUP, SWA_HD)
    k = sk.reshape(B, nb, WINDOW, SWA_KV_HEADS, SWA_HD)
    v = sv.reshape(B, nb, WINDOW, SWA_KV_HEADS, SWA_HD)
    shift = lambda t: jnp.pad(t, ((0, 0), (1, 0), (0, 0), (0, 0), (0, 0)))[:, :-1]
    kk = jnp.concatenate([shift(k), k], axis=2)
    vv = jnp.concatenate([shift(v), v], axis=2)
    qi = jnp.arange(WINDOW)[:, None]
    sj = jnp.arange(2 * WINDOW)[None, :]
    d = qi + WINDOW - sj
    band = (d >= 0) & (d < WINDOW)
    has_prev = (jnp.arange(nb) > 0)[:, None, None] | (sj >= WINDOW)[None]
    valid = (band[None] & has_prev)[:, None, None]
    o = _sink_attn(q, kk, vv, valid, sinks)
    new_k = sk.reshape(B, L, SWA_KV_HEADS, SWA_HD)[:, L - WINDOW:]
    new_v = sv.reshape(B, L, SWA_KV_HEADS, SWA_HD)[:, L - WINDOW:]
    return o.reshape(B, L, SWA_WIDTH), new_k, new_v


def _swa_sample(sq, sk, sv, k_buf, v_buf, sinks):
    DB, L, _ = sq.shape
    wb = k_buf.shape[1]
    q = sq.reshape(DB, L, SWA_KV_HEADS, SWA_GROUP, SWA_HD)
    kk = jnp.concatenate([k_buf.astype(sk.dtype), sk.reshape(DB, L, SWA_KV_HEADS, SWA_HD)], axis=1)
    vv = jnp.concatenate([v_buf.astype(sv.dtype), sv.reshape(DB, L, SWA_KV_HEADS, SWA_HD)], axis=1)
    rel_k = jnp.concatenate([jnp.arange(wb) - wb, jnp.arange(L)])
    d = jnp.arange(L)[:, None] - rel_k[None, :]
    valid = (d >= 0) & (d < WINDOW)
    o = _sink_attn(q, kk, vv, valid, sinks)
    return o.reshape(DB, L, SWA_WIDTH), kk[:, -wb:], vv[:, -wb:]


def _merge(o_gla, g_gate, o_swa, s_gate, w_out):
    dt = g_gate.dtype
    u = jnp.concatenate([o_gla.astype(dt) * jax.nn.silu(g_gate),
                         o_swa.astype(dt) * jax.nn.silu(s_gate)], axis=-1)
    return jnp.einsum('ble,ed->bld', u, w_out)


def _layer(h, s0, swa_fn, norm_g, w_in, w_gate_up, b_gate, gla_norm, w_out):
    u = _rmsnorm(h, norm_g)
    gla, g_gate, (sq, sk, sv), s_gate = _project(u, w_in, w_gate_up, b_gate)
    o_gla, S = _gla_branch(*gla, s0, gla_norm)
    o_swa, k_new, v_new = swa_fn(sq, sk, sv)
    h = h + _merge(o_gla, g_gate, o_swa, s_gate, w_out)
    return h, S, k_new, v_new


def setup_inputs(seed: int = 0) -> dict:
    key = jax.random.key(seed)
    ks = jax.random.split(key, 16)
    w_buf = min(WINDOW, PAST_LEN)
    f32 = jnp.float32
    return {
        "x_prompt": jax.random.normal(ks[0], (BATCH, SEQ, D_MODEL), f32),
        "x_sample": jax.random.normal(ks[1], (DEC_BATCH, DEC_SEQ, D_MODEL), f32),
        "state_gla": 0.5 * jax.random.normal(ks[2], (DEPTH, DEC_BATCH, GLA_HEADS, GLA_DK, GLA_DV), f32),
        "cache_win_k": jax.random.normal(ks[3], (DEPTH, DEC_BATCH, w_buf, SWA_KV_HEADS, SWA_HD), f32),
        "cache_win_v": jax.random.normal(ks[4], (DEPTH, DEC_BATCH, w_buf, SWA_KV_HEADS, SWA_HD), f32),
        "norm_in": 1.0 + 0.01 * jax.random.normal(ks[5], (DEPTH, D_MODEL), f32),
        "w_in": jax.random.normal(ks[6], (DEPTH, D_MODEL, IN_WIDTH), f32) * D_MODEL ** -0.5,
        "w_gate_up": jax.random.normal(ks[7], (DEPTH, GLA_RANK, GLA_KWIDTH), f32) * GLA_RANK ** -0.5,
        "b_gate": 0.1 * jax.random.normal(ks[8], (DEPTH, GLA_KWIDTH), f32),
        "gla_norm": 1.0 + 0.01 * jax.random.normal(ks[9], (DEPTH, GLA_DV), f32),
        "attn_sinks": 0.5 * jax.random.normal(ks[10], (DEPTH, SWA_HEADS), f32),
        "w_out": jax.random.normal(ks[11], (DEPTH, MIX_WIDTH, D_MODEL), f32) * MIX_WIDTH ** -0.5,
        "norm_f": 1.0 + 0.01 * jax.random.normal(ks[12], (D_MODEL,), f32),
    }


def reference(x_prompt, x_sample, state_gla, cache_win_k, cache_win_v, norm_in, w_in,
              w_gate_up, b_gate, gla_norm, attn_sinks, w_out, norm_f):
    hp, hs = x_prompt, x_sample
    sp_l, kp_l, vp_l, ss_l, ks_l, vs_l = [], [], [], [], [], []
    for l in range(DEPTH):
        params = (norm_in[l], w_in[l], w_gate_up[l], b_gate[l], gla_norm[l], w_out[l])
        sinks = attn_sinks[l]
        s0 = jnp.zeros((hp.shape[0], GLA_HEADS, GLA_DK, GLA_DV), jnp.float32)
        hp, S_p, k_p, v_p = _layer(hp, s0, lambda q, k, v: _swa_prompt(q, k, v, sinks), *params)
        kb, vb = cache_win_k[l], cache_win_v[l]
        hs, S_s, k_s, v_s = _layer(hs, state_gla[l],
                                   lambda q, k, v: _swa_sample(q, k, v, kb, vb, sinks), *params)
        sp_l.append(S_p.astype(state_gla.dtype)); kp_l.append(k_p); vp_l.append(v_p)
        ss_l.append(S_s.astype(state_gla.dtype)); ks_l.append(k_s); vs_l.append(v_s)
    y_prompt = _rmsnorm(hp, norm_f)
    y_sample = _rmsnorm(hs, norm_f)
    return (y_prompt, y_sample,
            jnp.stack(sp_l), jnp.stack(kp_l), jnp.stack(vp_l),
            jnp.stack(ss_l), jnp.stack(ks_l), jnp.stack(vs_l))
```

```python
import functools

import numpy as np
import jax
import jax.numpy as jnp
from jax import lax
from jax.experimental import pallas as pl
from jax.experimental.pallas import tpu as pltpu

D_MODEL = 1024
GLA_HEADS = 4
GLA_DK = 64
GLA_DV = 128
GLA_KW = GLA_HEADS * GLA_DK
GLA_W = GLA_HEADS * GLA_DV
GLA_RANK = 16
GLA_TAU = 16.0
CHUNK = 64
SWA_HEADS = 8
SWA_HD = 64
SWA_KV = 2
SWA_GROUP = SWA_HEADS // SWA_KV
SWA_W = SWA_HEADS * SWA_HD
SWA_KVW = SWA_KV * SWA_HD
WINDOW = 128
EPS = 1e-6
NEG_INF = -1e30
LANES = 128

O_Q, O_K, O_V, O_GG = 0, 256, 512, 1024
O_SQ, O_SK, O_SV, O_SG, O_LOW = 1536, 2048, 2176, 2304, 2816
W_MAIN = 2816
W_ALL = W_MAIN + LANES

N_LEVELS = 6
TOK_BLOCK = 256
SAMPLE_GROUP = 8
VMEM_LIMIT = 48 * 1024 * 1024

f32 = jnp.float32
bf16 = jnp.bfloat16


def _chunk_tables():
    c = CHUNK
    t = np.arange(c)[None, :]
    i = np.arange(c)[:, None]
    blocks = [(t <= i), (t > i)]
    masks = []
    for l in range(N_LEVELS):
        h = c >> (l + 1)
        m = (i // (2 * h)) * (2 * h) + h
        upper = i >= m
        blocks.append(np.where(upper, (t > m) & (t <= i), (t > i) & (t <= m)))
        jj = np.arange(c)[None, :]
        masks.append((i // (2 * h) == jj // (2 * h)) & (i % (2 * h) >= h) & (jj % (2 * h) < h))
    masks.append(np.eye(c, dtype=bool))
    cm = np.concatenate(blocks, axis=0).astype(np.float32)
    cm3 = np.concatenate([cm, cm, cm], axis=1)
    lm = np.stack(masks).astype(np.float32)
    lm = np.concatenate([lm, lm], axis=1)
    return jnp.asarray(cm3, dtype=bf16), jnp.asarray(lm, dtype=f32)


def _swa_bias():
    qi = (np.arange(2 * WINDOW) % WINDOW)[:, None]
    cj = np.arange(2 * WINDOW)[None, :]
    prev_ok = (cj < WINDOW) & (cj > qi)
    cur_ok = (cj >= WINDOW) & (cj - WINDOW <= qi)
    full = np.where(prev_ok | cur_ok, 0.0, NEG_INF)
    first = np.where(cur_ok, 0.0, NEG_INF)
    return jnp.asarray(np.stack([full, first]), dtype=f32)


def _head_perm():
    return np.array([kv * SWA_GROUP + g for g in range(SWA_GROUP) for kv in range(SWA_KV)])


def _col_perm():
    return (_head_perm()[:, None] * SWA_HD + np.arange(SWA_HD)[None, :]).reshape(-1)


def _prep_weights(norm_in, w_in, w_gate_up, b_gate, gla_norm, attn_sinks, w_out, norm_f):
    sp = np.cumsum([0, 256, 256, 512, 512, 16, 512, 128, 128, 512])
    seg = [w_in[:, sp[i]:sp[i + 1]] for i in range(9)]
    gq, gk, gv, gg, glow, sq, sk, sv, sg = seg
    cp = _col_perm()
    low = jnp.pad(glow, ((0, 0), (0, LANES - GLA_RANK)))
    w_all = jnp.concatenate(
        [gq * (GLA_DK ** -0.5), gk, gv, gg, sq[:, cp] * (SWA_HD ** -0.5), sk, sv, sg[:, cp], low],
        axis=1).astype(bf16)
    w_up = jnp.pad(w_gate_up, ((0, LANES - GLA_RANK), (0, 0))).astype(bf16)
    w_out_p = jnp.concatenate([w_out[:GLA_W], w_out[GLA_W:][cp]], axis=0).astype(bf16)
    return dict(
        norm_in=norm_in.reshape(1, D_MODEL), w_all=w_all, w_up=w_up,
        b_gate=b_gate.reshape(1, GLA_KW), gla_norm=jnp.tile(gla_norm, GLA_HEADS).reshape(1, GLA_W),
        sinks=attn_sinks[_head_perm()], w_out=w_out_p, norm_f=norm_f.reshape(1, D_MODEL))


def _rms(x, gain):
    return x * lax.rsqrt(jnp.mean(x * x, axis=-1, keepdims=True) + EPS) * gain


def _log_decay(glow, wup_ref, bg_ref):
    z = jnp.dot(glow.astype(bf16), wup_ref[...], preferred_element_type=f32) + bg_ref[...]
    return (jnp.minimum(z, 0.0) - jnp.log1p(jnp.exp(-jnp.abs(z)))) * (1.0 / GLA_TAU)


def _silu(x):
    return x * jax.nn.sigmoid(x)


def _merge(x, o_gla, gg, o_swa, sg, gn_ref, wout_ref, nf_ref):
    parts = []
    for h in range(GLA_HEADS):
        sl = slice(h * GLA_DV, (h + 1) * GLA_DV)
        parts.append(_rms(o_gla[:, sl], gn_ref[:, sl]) * _silu(gg[:, sl]))
    parts.append(o_swa * _silu(sg))
    um = jnp.concatenate(parts, axis=1).astype(bf16)
    hres = x + jnp.dot(um, wout_ref[...], preferred_element_type=f32)
    return _rms(hres, nf_ref[...])


def _prompt_kernel(sinks_ref, x_ref, nin_ref, w_ref, wup_ref, bg_ref, cmat_ref, lmask_ref, bias_ref,
                   gn_ref, wout_ref, nf_ref,
                   y_ref, sp_ref, kn_ref, vn_ref,
                   s_ref, kprev_ref, vprev_ref, q_s, k_s, g_s, v_s, ogla_s, oswa_s, *, n_t):
    t = pl.program_id(1)
    tl = x_ref.shape[0]

    @pl.when(t == 0)
    def _():
        s_ref[...] = jnp.zeros_like(s_ref)
        kprev_ref[...] = jnp.zeros_like(kprev_ref)
        vprev_ref[...] = jnp.zeros_like(vprev_ref)

    x = x_ref[...]
    u = _rms(x, nin_ref[...]).astype(bf16)

    def proj(lo, width):
        return jnp.dot(u, w_ref[:, lo:lo + width], preferred_element_type=f32)

    q_s[...] = proj(O_Q, GLA_KW)
    k_s[...] = proj(O_K, GLA_KW)
    v_s[...] = proj(O_V, GLA_W)
    g_s[...] = _log_decay(proj(O_LOW, LANES), wup_ref, bg_ref)

    lane_lo = lax.broadcasted_iota(jnp.int32, (CHUNK, LANES), 1) < GLA_DK

    def chunk_body(c, carry):
        r0 = pl.multiple_of(c * CHUNK, CHUNK)
        rows = pl.ds(r0, CHUNK)
        qc, kc, gc, vc = q_s[rows, :], k_s[rows, :], g_s[rows, :], v_s[rows, :]
        g_hi = gc.astype(bf16)
        r1 = gc - g_hi.astype(f32)
        g_mid = r1.astype(bf16)
        g_lo = (r1 - g_mid.astype(f32)).astype(bf16)
        sums = jnp.dot(cmat_ref[...], jnp.concatenate([g_hi, g_mid, g_lo], axis=0),
                       preferred_element_type=f32)
        e_b = jnp.exp(sums[0:CHUNK])
        e_suf = jnp.exp(sums[CHUNK:2 * CHUNK])
        qb = qc * e_b
        k_suf = kc * e_suf
        vb = vc.astype(bf16)
        for p in range(GLA_HEADS // 2):
            ln = slice(p * LANES, (p + 1) * LANES)
            qp, kp = qc[:, ln], kc[:, ln]
            a = jnp.zeros((2 * CHUNK, CHUNK), f32)
            for l in range(N_LEVELS + 1):
                if l < N_LEVELS:
                    e = jnp.exp(sums[(2 + l) * CHUNK:(3 + l) * CHUNK, ln])
                    qe, ke = (qp * e).astype(bf16), (kp * e).astype(bf16)
                else:
                    qe, ke = qp.astype(bf16), kp.astype(bf16)
                lhs = jnp.concatenate([jnp.where(lane_lo, qe, 0), jnp.where(lane_lo, 0, qe)], axis=0)
                sc = lax.dot_general(lhs, ke, (((1,), (1,)), ((), ())), preferred_element_type=f32)
                a = a + sc * lmask_ref[l]
            ab = a.astype(bf16)
            s_pair = s_ref[ln, :]
            s_pair_b = s_pair.astype(bf16)
            qbp = qb[:, ln].astype(bf16)
            for hh in range(2):
                h = 2 * p + hh
                qbm = jnp.where(lane_lo, qbp, 0) if hh == 0 else jnp.where(lane_lo, 0, qbp)
                lhs = jnp.concatenate([qbm, ab[hh * CHUNK:(hh + 1) * CHUNK, :]], axis=1)
                rhs = jnp.concatenate([s_pair_b, vb[:, h * GLA_DV:(h + 1) * GLA_DV]], axis=0)
                ogla_s[rows, h * GLA_DV:(h + 1) * GLA_DV] = jnp.dot(lhs, rhs, preferred_element_type=f32)
            upd = lax.dot_general(k_suf[:, ln].astype(bf16), vb[:, p * 2 * GLA_DV:(p + 1) * 2 * GLA_DV],
                                  (((0,), (0,)), ((), ())), preferred_element_type=f32)
            e_col = jnp.broadcast_to(e_b[CHUNK - 1:CHUNK, ln], (LANES, LANES)).T
            s_ref[ln, :] = e_col * s_pair + jnp.concatenate(
                [upd[0:GLA_DK, 0:GLA_DV], upd[GLA_DK:2 * GLA_DK, GLA_DV:2 * GLA_DV]], axis=0)
        return carry

    lax.fori_loop(0, tl // CHUNK, chunk_body, 0)

    lane_lo_w = lax.broadcasted_iota(jnp.int32, (WINDOW, LANES), 1) < SWA_HD
    row_lo = lax.broadcasted_iota(jnp.int32, (2 * WINDOW, 1), 0) < WINDOW
    for blk in range(tl // WINDOW):
        rs = slice(blk * WINDOW, (blk + 1) * WINDOW)
        ub = u[rs]
        sq = jnp.dot(ub, w_ref[:, O_SQ:O_SQ + SWA_W], preferred_element_type=f32).astype(bf16)
        kv = jnp.dot(ub, w_ref[:, O_SK:O_SK + 2 * SWA_KVW], preferred_element_type=f32)
        k_cur, v_cur = kv[:, :SWA_KVW], kv[:, SWA_KVW:]
        k2 = jnp.concatenate([kprev_ref[...], k_cur], axis=0).astype(bf16)
        v2 = jnp.concatenate([vprev_ref[...], v_cur], axis=0).astype(bf16)
        if blk == 0:
            bias = bias_ref[jnp.where(t > 0, 0, 1)]
        else:
            bias = bias_ref[0]
        for tt in range(SWA_GROUP):
            qt = sq[:, tt * LANES:(tt + 1) * LANES]
            lhs = jnp.concatenate([jnp.where(lane_lo_w, qt, 0), jnp.where(lane_lo_w, 0, qt)], axis=0)
            s = lax.dot_general(lhs, k2, (((1,), (1,)), ((), ())), preferred_element_type=f32) + bias
            sink = jnp.where(row_lo, sinks_ref[2 * tt], sinks_ref[2 * tt + 1])
            m = jnp.maximum(jnp.max(s, axis=-1, keepdims=True), sink)
            e = jnp.exp(s - m)
            den = jnp.sum(e, axis=-1, keepdims=True) + jnp.exp(sink - m)
            pr = (e / den).astype(bf16)
            o2 = jnp.dot(pr, v2, preferred_element_type=f32)
            oswa_s[rs, tt * LANES:(tt + 1) * LANES] = jnp.where(lane_lo_w, o2[:WINDOW], o2[WINDOW:])
        kprev_ref[...] = k_cur
        vprev_ref[...] = v_cur

    gg = proj(O_GG, GLA_W)
    sg = proj(O_SG, SWA_W)
    y_ref[...] = _merge(x, ogla_s[...], gg, oswa_s[...], sg, gn_ref, wout_ref, nf_ref)

    @pl.when(t == n_t - 1)
    def _():
        sp_ref[...] = s_ref[...]
        kn_ref[...] = kprev_ref[...]
        vn_ref[...] = vprev_ref[...]


def _prompt_call(x, w, cmat, lmask, bias):
    bsz, seq, _ = x.shape
    tl = TOK_BLOCK
    n_t = seq // tl
    const = lambda shape: pl.BlockSpec(shape, lambda b, t: (0,) * len(shape))
    return pl.pallas_call(
        functools.partial(_prompt_kernel, n_t=n_t),
        grid=(bsz, n_t),
        in_specs=[
            pl.BlockSpec(memory_space=pltpu.SMEM),
            pl.BlockSpec((None, tl, D_MODEL), lambda b, t: (b, t, 0)),
            const((1, D_MODEL)), const((D_MODEL, W_ALL)), const((LANES, GLA_KW)), const((1, GLA_KW)),
            const(cmat.shape), const(lmask.shape), const(bias.shape),
            const((1, GLA_W)), const((D_MODEL, D_MODEL)), const((1, D_MODEL)),
        ],
        out_specs=[
            pl.BlockSpec((None, tl, D_MODEL), lambda b, t: (b, t, 0)),
            pl.BlockSpec((None, GLA_KW, GLA_DV), lambda b, t: (b, 0, 0)),
            pl.BlockSpec((None, WINDOW, SWA_KVW), lambda b, t: (b, 0, 0)),
            pl.BlockSpec((None, WINDOW, SWA_KVW), lambda b, t: (b, 0, 0)),
        ],
        out_shape=[
            jax.ShapeDtypeStruct((bsz, seq, D_MODEL), f32),
            jax.ShapeDtypeStruct((bsz, GLA_KW, GLA_DV), f32),
            jax.ShapeDtypeStruct((bsz, WINDOW, SWA_KVW), f32),
            jax.ShapeDtypeStruct((bsz, WINDOW, SWA_KVW), f32),
        ],
        scratch_shapes=[
            pltpu.VMEM((GLA_KW, GLA_DV), f32),
            pltpu.VMEM((WINDOW, SWA_KVW), f32), pltpu.VMEM((WINDOW, SWA_KVW), f32),
            pltpu.VMEM((tl, GLA_KW), f32), pltpu.VMEM((tl, GLA_KW), f32), pltpu.VMEM((tl, GLA_KW), f32),
            pltpu.VMEM((tl, GLA_W), f32), pltpu.VMEM((tl, GLA_W), f32), pltpu.VMEM((tl, SWA_W), f32),
        ],
        compiler_params=pltpu.CompilerParams(
            dimension_semantics=("arbitrary", "arbitrary"), vmem_limit_bytes=VMEM_LIMIT),
        name="prompt_layer",
    )(w["sinks"], x, w["norm_in"], w["w_all"], w["w_up"], w["b_gate"], cmat, lmask, bias,
      w["gla_norm"], w["w_out"], w["norm_f"])


def _sample_proj_kernel(x_ref, nin_ref, w_ref, wup_ref, bg_ref, proj_ref, qka_ref):
    u = _rms(x_ref[...], nin_ref[...]).astype(bf16)
    proj_ref[...] = jnp.dot(u, w_ref[:, :W_MAIN], preferred_element_type=f32)
    glow = jnp.dot(u, w_ref[:, O_LOW:O_LOW + LANES], preferred_element_type=f32)
    qka_ref[:, 0:2 * GLA_KW] = proj_ref[:, 0:2 * GLA_KW]
    qka_ref[:, 2 * GLA_KW:3 * GLA_KW] = jnp.exp(_log_decay(glow, wup_ref, bg_ref))


def _sample_proj_call(xs, w):
    n = xs.shape[0]
    return pl.pallas_call(
        _sample_proj_kernel,
        out_shape=[jax.ShapeDtypeStruct((n, W_MAIN), f32), jax.ShapeDtypeStruct((n, 3 * GLA_KW), f32)],
        compiler_params=pltpu.CompilerParams(vmem_limit_bytes=VMEM_LIMIT),
        name="sample_proj",
    )(xs, w["norm_in"], w["w_all"], w["w_up"], w["b_gate"])


def _sample_state_kernel(qka_ref, v_ref, qblk_ref, kn_ref, vn_ref, sink_ref, s_ref, ck_ref, cv_ref,
                         so_ref, cko_ref, cvo_ref, og_ref, os_ref):
    last_row = lax.broadcasted_iota(jnp.int32, (WINDOW, SWA_KVW), 0) == WINDOW - 1
    pos_ok = lax.broadcasted_iota(jnp.int32, (SWA_HEADS, WINDOW), 1) >= 1
    sink = sink_ref[...]
    for j in range(SAMPLE_GROUP):
        for h in range(GLA_HEADS):
            rs = slice(h * GLA_DK, (h + 1) * GLA_DK)
            q_col = qka_ref[h * GLA_DK:(h + 1) * GLA_DK, j:j + 1]
            k_col = qka_ref[GLA_KW + h * GLA_DK:GLA_KW + (h + 1) * GLA_DK, j:j + 1]
            a_col = qka_ref[2 * GLA_KW + h * GLA_DK:2 * GLA_KW + (h + 1) * GLA_DK, j:j + 1]
            v_row = v_ref[j:j + 1, h * GLA_DV:(h + 1) * GLA_DV]
            s_new = a_col * s_ref[j, rs, :] + k_col * v_row
            so_ref[j, rs, :] = s_new
            og_ref[j, h:h + 1, :] = jnp.sum(q_col * s_new, axis=0, keepdims=True)
        kb, vb = ck_ref[j], cv_ref[j]
        qb = qblk_ref[j]
        kn, vn = kn_ref[j], vn_ref[j]
        s = lax.dot_general(qb.astype(bf16), kb.astype(bf16), (((1,), (1,)), ((), ())),
                            preferred_element_type=f32)
        s = jnp.where(pos_ok, s, NEG_INF)
        s_self = jnp.sum(qb * kn, axis=-1, keepdims=True)
        m = jnp.maximum(jnp.maximum(jnp.max(s, axis=-1, keepdims=True), s_self), sink)
        e = jnp.exp(s - m)
        e_self = jnp.exp(s_self - m)
        den = jnp.sum(e, axis=-1, keepdims=True) + e_self + jnp.exp(sink - m)
        o = jnp.dot(e.astype(bf16), vb.astype(bf16), preferred_element_type=f32) + e_self * vn
        os_ref[j] = o / den
        cko_ref[j] = jnp.where(last_row, kn, pltpu.roll(kb, WINDOW - 1, axis=0))
        cvo_ref[j] = jnp.where(last_row, vn, pltpu.roll(vb, WINDOW - 1, axis=0))


def _sample_state_call(qka_t, v, qblk, kn, vn, sinks_col, state, ck, cv):
    n = state.shape[0]
    g = SAMPLE_GROUP
    blk = lambda *tail: pl.BlockSpec((g,) + tail, lambda i: (i,) + (0,) * len(tail))
    return pl.pallas_call(
        _sample_state_kernel,
        grid=(n // g,),
        in_specs=[
            pl.BlockSpec((None, 3 * GLA_KW, g), lambda i: (i, 0, 0)),
            pl.BlockSpec((None, g, GLA_W), lambda i: (i, 0, 0)),
            blk(SWA_HEADS, LANES), blk(1, SWA_KVW), blk(1, SWA_KVW),
            pl.BlockSpec((SWA_HEADS, 1), lambda i: (0, 0)),
            blk(GLA_KW, GLA_DV), blk(WINDOW, SWA_KVW), blk(WINDOW, SWA_KVW),
        ],
        out_specs=[blk(GLA_KW, GLA_DV), blk(WINDOW, SWA_KVW), blk(WINDOW, SWA_KVW),
                   blk(GLA_HEADS, GLA_DV), blk(SWA_HEADS, LANES)],
        out_shape=[
            jax.ShapeDtypeStruct(state.shape, f32), jax.ShapeDtypeStruct(ck.shape, f32),
            jax.ShapeDtypeStruct(cv.shape, f32),
            jax.ShapeDtypeStruct((n, GLA_HEADS, GLA_DV), f32), jax.ShapeDtypeStruct((n, SWA_HEADS, LANES), f32),
        ],
        compiler_params=pltpu.CompilerParams(dimension_semantics=("arbitrary",), vmem_limit_bytes=VMEM_LIMIT),
        name="sample_state",
    )(qka_t, v, qblk, kn, vn, sinks_col, state, ck, cv)


def _sample_merge_kernel(x_ref, og_ref, gg_ref, os_ref, sg_ref, gn_ref, wout_ref, nf_ref, y_ref):
    y_ref[...] = _merge(x_ref[...], og_ref[...], gg_ref[...], os_ref[...], sg_ref[...],
                        gn_ref, wout_ref, nf_ref)


def _sample_merge_call(xs, og, gg, osw, sg, w):
    return pl.pallas_call(
        _sample_merge_kernel,
        out_shape=jax.ShapeDtypeStruct(xs.shape, f32),
        compiler_params=pltpu.CompilerParams(vmem_limit_bytes=VMEM_LIMIT),
        name="sample_merge",
    )(xs, og, gg, osw, sg, w["gla_norm"], w["w_out"], w["norm_f"])


def _sample_path(x_sample, state, ck, cv, w):
    n = x_sample.shape[0]
    g = SAMPLE_GROUP
    xs = x_sample.reshape(n, D_MODEL)
    proj, qka = _sample_proj_call(xs, w)
    qka_t = qka.T.reshape(3 * GLA_KW, n // g, g).transpose(1, 0, 2)
    v = proj[:, O_V:O_V + GLA_W].reshape(n // g, g, GLA_W)
    sq = proj[:, O_SQ:O_SQ + SWA_W].reshape(n, SWA_HEADS, SWA_HD)
    kv_of_pos = (jnp.arange(SWA_HEADS) % SWA_KV)[None, :, None]
    zero = jnp.zeros_like(sq)
    qblk = jnp.concatenate([jnp.where(kv_of_pos == 0, sq, zero), jnp.where(kv_of_pos == 1, sq, zero)], axis=-1)
    kn = proj[:, O_SK:O_SK + SWA_KVW].reshape(n, 1, SWA_KVW)
    vn = proj[:, O_SV:O_SV + SWA_KVW].reshape(n, 1, SWA_KVW)
    s_new, ck_new, cv_new, og, os_raw = _sample_state_call(
        qka_t, v, qblk, kn, vn, w["sinks"].reshape(SWA_HEADS, 1), state, ck, cv)
    os_sel = jnp.where(kv_of_pos == 0, os_raw[..., :SWA_HD], os_raw[..., SWA_HD:]).reshape(n, SWA_W)
    y = _sample_merge_call(xs, og.reshape(n, GLA_W), proj[:, O_GG:O_GG + GLA_W], os_sel,
                           proj[:, O_SG:O_SG + SWA_W], w)
    return y, s_new, ck_new, cv_new


def kernel(x_prompt, x_sample, state_gla, cache_win_k, cache_win_v, norm_in, w_in, w_gate_up, b_gate,
           gla_norm, attn_sinks, w_out, norm_f):
    bsz = x_prompt.shape[0]
    n = x_sample.shape[0]
    w = _prep_weights(norm_in[0], w_in[0], w_gate_up[0], b_gate[0], gla_norm[0], attn_sinks[0],
                      w_out[0], norm_f)
    cmat, lmask = _chunk_tables()
    y_p, s_p, k_p, v_p = _prompt_call(x_prompt, w, cmat, lmask, _swa_bias())
    y_s, s_s, k_s, v_s = _sample_path(
        x_sample, state_gla[0].reshape(n, GLA_KW, GLA_DV),
        cache_win_k[0].reshape(n, WINDOW, SWA_KVW), cache_win_v[0].reshape(n, WINDOW, SWA_KVW), w)
    return (y_p, y_s.reshape(n, 1, D_MODEL),
            s_p.reshape(1, bsz, GLA_HEADS, GLA_DK, GLA_DV),
            k_p.reshape(1, bsz, WINDOW, SWA_KV, SWA_HD), v_p.reshape(1, bsz, WINDOW, SWA_KV, SWA_HD),
            s_s.reshape(1, n, GLA_HEADS, GLA_DK, GLA_DV),
            k_s.reshape(1, n, WINDOW, SWA_KV, SWA_HD), v_s.reshape(1, n, WINDOW, SWA_KV, SWA_HD))
```

```python
import functools

import numpy as np
import jax
import jax.numpy as jnp
from jax import lax
from jax.experimental import pallas as pl
from jax.experimental.pallas import tpu as pltpu

D_MODEL = 1024
GLA_HEADS = 4
GLA_DK = 64
GLA_DV = 128
GLA_KW = GLA_HEADS * GLA_DK
GLA_W = GLA_HEADS * GLA_DV
GLA_RANK = 16
GLA_TAU = 16.0
CHUNK = 64
SWA_HEADS = 8
SWA_HD = 64
SWA_KV = 2
SWA_GROUP = SWA_HEADS // SWA_KV
SWA_W = SWA_HEADS * SWA_HD
SWA_KVW = SWA_KV * SWA_HD
WINDOW = 128
EPS = 1e-6
NEG_INF = -1e30
LANES = 128

O_Q, O_K, O_V, O_GG = 0, 256, 512, 1024
O_SQ, O_SK, O_SV, O_SG, O_LOW = 1536, 2048, 2176, 2304, 2816
W_MAIN = 2816
W_ALL = W_MAIN + LANES

N_LEVELS = 6
TOK_BLOCK = 256
SAMPLE_GROUP = 8
VMEM_LIMIT = 48 * 1024 * 1024

f32 = jnp.float32
bf16 = jnp.bfloat16


def _chunk_tables():
    c = CHUNK
    t = np.arange(c)[None, :]
    i = np.arange(c)[:, None]
    blocks = [(t <= i), (t > i)]
    masks = []
    for l in range(N_LEVELS):
        h = c >> (l + 1)
        m = (i // (2 * h)) * (2 * h) + h
        upper = i >= m
        blocks.append(np.where(upper, (t > m) & (t <= i), (t > i) & (t <= m)))
        jj = np.arange(c)[None, :]
        masks.append((i // (2 * h) == jj // (2 * h)) & (i % (2 * h) >= h) & (jj % (2 * h) < h))
    masks.append(np.eye(c, dtype=bool))
    cm = np.concatenate(blocks, axis=0).astype(np.float32)
    cm3 = np.concatenate([cm, cm, cm], axis=1)
    lm = np.stack(masks).astype(np.float32)
    lm = np.concatenate([lm, lm], axis=1)
    return jnp.asarray(cm3, dtype=bf16), jnp.asarray(lm, dtype=f32)


def _swa_bias():
    qi = (np.arange(2 * WINDOW) % WINDOW)[:, None]
    cj = np.arange(2 * WINDOW)[None, :]
    prev_ok = (cj < WINDOW) & (cj > qi)
    cur_ok = (cj >= WINDOW) & (cj - WINDOW <= qi)
    full = np.where(prev_ok | cur_ok, 0.0, NEG_INF)
    first = np.where(cur_ok, 0.0, NEG_INF)
    return jnp.asarray(np.stack([full, first]), dtype=f32)


def _head_perm():
    return np.array([kv * SWA_GROUP + g for g in range(SWA_GROUP) for kv in range(SWA_KV)])


def _col_perm():
    return (_head_perm()[:, None] * SWA_HD + np.arange(SWA_HD)[None, :]).reshape(-1)


def _prep_weights(norm_in, w_in, w_gate_up, b_gate, gla_norm, attn_sinks, w_out, norm_f):
    sp = np.cumsum([0, 256, 256, 512, 512, 16, 512, 128, 128, 512])
    seg = [w_in[:, sp[i]:sp[i + 1]] for i in range(9)]
    gq, gk, gv, gg, glow, sq, sk, sv, sg = seg
    cp = _col_perm()
    low = jnp.pad(glow, ((0, 0), (0, LANES - GLA_RANK)))
    w_all = jnp.concatenate(
        [gq * (GLA_DK ** -0.5), gk, gv, gg, sq[:, cp] * (SWA_HD ** -0.5), sk, sv, sg[:, cp], low],
        axis=1).astype(bf16)
    w_up = jnp.pad(w_gate_up, ((0, LANES - GLA_RANK), (0, 0))).astype(bf16)
    w_out_p = jnp.concatenate([w_out[:GLA_W], w_out[GLA_W:][cp]], axis=0).astype(bf16)
    return dict(
        norm_in=norm_in.reshape(1, D_MODEL), w_all=w_all, w_up=w_up,
        b_gate=b_gate.reshape(1, GLA_KW), gla_norm=jnp.tile(gla_norm, GLA_HEADS).reshape(1, GLA_W),
        sinks=attn_sinks[_head_perm()], w_out=w_out_p, norm_f=norm_f.reshape(1, D_MODEL))


def _rms(x, gain):
    return x * lax.rsqrt(jnp.mean(x * x, axis=-1, keepdims=True) + EPS) * gain


def _log_decay(glow, wup_ref, bg_ref):
    z = jnp.dot(glow.astype(bf16), wup_ref[...], preferred_element_type=f32) + bg_ref[...]
    return (jnp.minimum(z, 0.0) - jnp.log(1.0 + jnp.exp(-jnp.abs(z)))) * (1.0 / GLA_TAU)


def _silu(x):
    return x * jax.nn.sigmoid(x)


def _merge(x, o_gla, gg, o_swa, sg, gn_ref, wout_ref, nf_ref):
    parts = []
    for h in range(GLA_HEADS):
        sl = slice(h * GLA_DV, (h + 1) * GLA_DV)
        parts.append(_rms(o_gla[:, sl], gn_ref[:, sl]) * _silu(gg[:, sl]))
    parts.append(o_swa * _silu(sg))
    um = jnp.concatenate(parts, axis=1).astype(bf16)
    hres = x + jnp.dot(um, wout_ref[...], preferred_element_type=f32)
    return _rms(hres, nf_ref[...])


def _prompt_kernel(sinks_ref, x_ref, nin_ref, w_ref, wup_ref, bg_ref, cmat_ref, lmask_ref, bias_ref,
                   gn_ref, wout_ref, nf_ref,
                   y_ref, sp_ref, kn_ref, vn_ref,
                   s_ref, kprev_ref, vprev_ref, q_s, k_s, g_s, v_s, ogla_s, oswa_s, *, n_t):
    t = pl.program_id(1)
    tl = x_ref.shape[0]

    @pl.when(t == 0)
    def _():
        s_ref[...] = jnp.zeros_like(s_ref)
        kprev_ref[...] = jnp.zeros_like(kprev_ref)
        vprev_ref[...] = jnp.zeros_like(vprev_ref)

    x = x_ref[...]
    u = _rms(x, nin_ref[...]).astype(bf16)

    def proj(lo, width):
        return jnp.dot(u, w_ref[:, lo:lo + width], preferred_element_type=f32)

    q_s[...] = proj(O_Q, GLA_KW)
    k_s[...] = proj(O_K, GLA_KW)
    v_s[...] = proj(O_V, GLA_W)
    g_s[...] = _log_decay(proj(O_LOW, LANES), wup_ref, bg_ref)

    lane_lo = lax.broadcasted_iota(jnp.int32, (CHUNK, LANES), 1) < GLA_DK

    n_pairs = GLA_HEADS // 2

    def chunk_terms(c):
        rows = slice(c * CHUNK, (c + 1) * CHUNK)
        qc, kc, gc = q_s[rows, :], k_s[rows, :], g_s[rows, :]
        vb = v_s[rows, :].astype(bf16)
        g_hi = gc.astype(bf16)
        r1 = gc - g_hi.astype(f32)
        g_mid = r1.astype(bf16)
        g_lo = (r1 - g_mid.astype(f32)).astype(bf16)
        sums = jnp.dot(cmat_ref[...], jnp.concatenate([g_hi, g_mid, g_lo], axis=0),
                       preferred_element_type=f32)
        e_b = jnp.exp(sums[0:CHUNK])
        qb = qc * e_b
        k_suf = kc * jnp.exp(sums[CHUNK:2 * CHUNK])
        terms = []
        for p in range(n_pairs):
            ln = slice(p * LANES, (p + 1) * LANES)
            qp, kp = qc[:, ln], kc[:, ln]
            a = jnp.zeros((2 * CHUNK, CHUNK), f32)
            for l in range(N_LEVELS + 1):
                if l < N_LEVELS:
                    e = jnp.exp(sums[(2 + l) * CHUNK:(3 + l) * CHUNK, ln])
                    qe, ke = (qp * e).astype(bf16), (kp * e).astype(bf16)
                else:
                    qe, ke = qp.astype(bf16), kp.astype(bf16)
                lhs = jnp.concatenate([jnp.where(lane_lo, qe, 0), jnp.where(lane_lo, 0, qe)], axis=0)
                sc = lax.dot_general(lhs, ke, (((1,), (1,)), ((), ())), preferred_element_type=f32)
                a = a + sc * lmask_ref[l]
            ab = a.astype(bf16)
            qbp = qb[:, ln].astype(bf16)
            lhs_heads = []
            for hh in range(2):
                qbm = jnp.where(lane_lo, qbp, 0) if hh == 0 else jnp.where(lane_lo, 0, qbp)
                lhs_heads.append(jnp.concatenate([qbm, ab[hh * CHUNK:(hh + 1) * CHUNK, :]], axis=1))
            upd = lax.dot_general(k_suf[:, ln].astype(bf16), vb[:, p * 2 * GLA_DV:(p + 1) * 2 * GLA_DV],
                                  (((0,), (0,)), ((), ())), preferred_element_type=f32)
            upd = jnp.concatenate([upd[0:GLA_DK, 0:GLA_DV], upd[GLA_DK:2 * GLA_DK, GLA_DV:2 * GLA_DV]], axis=0)
            e_col = jnp.broadcast_to(e_b[CHUNK - 1:CHUNK, ln], (LANES, LANES)).T
            terms.append((lhs_heads, vb, upd, e_col))
        return terms

    all_terms = [chunk_terms(c) for c in range(tl // CHUNK)]
    for p in range(n_pairs):
        ln = slice(p * LANES, (p + 1) * LANES)
        s_pair = s_ref[ln, :]
        for c in range(tl // CHUNK):
            lhs_heads, vb, upd, e_col = all_terms[c][p]
            s_pair_b = s_pair.astype(bf16)
            for hh in range(2):
                h = 2 * p + hh
                rhs = jnp.concatenate([s_pair_b, vb[:, h * GLA_DV:(h + 1) * GLA_DV]], axis=0)
                ogla_s[c * CHUNK:(c + 1) * CHUNK, h * GLA_DV:(h + 1) * GLA_DV] = jnp.dot(
                    lhs_heads[hh], rhs, preferred_element_type=f32)
            s_pair = e_col * s_pair + upd
        s_ref[ln, :] = s_pair

    lane_lo_w = lax.broadcasted_iota(jnp.int32, (WINDOW, LANES), 1) < SWA_HD
    row_lo = lax.broadcasted_iota(jnp.int32, (2 * WINDOW, 1), 0) < WINDOW
    for blk in range(tl // WINDOW):
        rs = slice(blk * WINDOW, (blk + 1) * WINDOW)
        ub = u[rs]
        sq = jnp.dot(ub, w_ref[:, O_SQ:O_SQ + SWA_W], preferred_element_type=f32).astype(bf16)
        kv = jnp.dot(ub, w_ref[:, O_SK:O_SK + 2 * SWA_KVW], preferred_element_type=f32)
        k_cur, v_cur = kv[:, :SWA_KVW], kv[:, SWA_KVW:]
        k2 = jnp.concatenate([kprev_ref[...], k_cur], axis=0).astype(bf16)
        v2 = jnp.concatenate([vprev_ref[...], v_cur], axis=0).astype(bf16)
        if blk == 0:
            bias = bias_ref[jnp.where(t > 0, 0, 1)]
        else:
            bias = bias_ref[0]
        for tt in range(SWA_GROUP):
            qt = sq[:, tt * LANES:(tt + 1) * LANES]
            lhs = jnp.concatenate([jnp.where(lane_lo_w, qt, 0), jnp.where(lane_lo_w, 0, qt)], axis=0)
            s = lax.dot_general(lhs, k2, (((1,), (1,)), ((), ())), preferred_element_type=f32) + bias
            sink = jnp.where(row_lo, sinks_ref[2 * tt], sinks_ref[2 * tt + 1])
            m = jnp.maximum(jnp.max(s, axis=-1, keepdims=True), sink)
            e = jnp.exp(s - m)
            den = jnp.sum(e, axis=-1, keepdims=True) + jnp.exp(sink - m)
            pr = (e / den).astype(bf16)
            o2 = jnp.dot(pr, v2, preferred_element_type=f32)
            oswa_s[rs, tt * LANES:(tt + 1) * LANES] = jnp.where(lane_lo_w, o2[:WINDOW], o2[WINDOW:])
        kprev_ref[...] = k_cur
        vprev_ref[...] = v_cur

    gg = proj(O_GG, GLA_W)
    sg = proj(O_SG, SWA_W)
    y_ref[...] = _merge(x, ogla_s[...], gg, oswa_s[...], sg, gn_ref, wout_ref, nf_ref)

    @pl.when(t == n_t - 1)
    def _():
        sp_ref[...] = s_ref[...]
        kn_ref[...] = kprev_ref[...]
        vn_ref[...] = vprev_ref[...]


def _prompt_call(x, w, cmat, lmask, bias):
    bsz, seq, _ = x.shape
    tl = TOK_BLOCK
    n_t = seq // tl
    const = lambda shape: pl.BlockSpec(shape, lambda b, t: (0,) * len(shape))
    return pl.pallas_call(
        functools.partial(_prompt_kernel, n_t=n_t),
        grid=(bsz, n_t),
        in_specs=[
            pl.BlockSpec(memory_space=pltpu.SMEM),
            pl.BlockSpec((None, tl, D_MODEL), lambda b, t: (b, t, 0)),
            const((1, D_MODEL)), const((D_MODEL, W_ALL)), const((LANES, GLA_KW)), const((1, GLA_KW)),
            const(cmat.shape), const(lmask.shape), const(bias.shape),
            const((1, GLA_W)), const((D_MODEL, D_MODEL)), const((1, D_MODEL)),
        ],
        out_specs=[
            pl.BlockSpec((None, tl, D_MODEL), lambda b, t: (b, t, 0)),
            pl.BlockSpec((None, GLA_KW, GLA_DV), lambda b, t: (b, 0, 0)),
            pl.BlockSpec((None, WINDOW, SWA_KVW), lambda b, t: (b, 0, 0)),
            pl.BlockSpec((None, WINDOW, SWA_KVW), lambda b, t: (b, 0, 0)),
        ],
        out_shape=[
            jax.ShapeDtypeStruct((bsz, seq, D_MODEL), f32),
            jax.ShapeDtypeStruct((bsz, GLA_KW, GLA_DV), f32),
            jax.ShapeDtypeStruct((bsz, WINDOW, SWA_KVW), f32),
            jax.ShapeDtypeStruct((bsz, WINDOW, SWA_KVW), f32),
        ],
        scratch_shapes=[
            pltpu.VMEM((GLA_KW, GLA_DV), f32),
            pltpu.VMEM((WINDOW, SWA_KVW), f32), pltpu.VMEM((WINDOW, SWA_KVW), f32),
            pltpu.VMEM((tl, GLA_KW), f32), pltpu.VMEM((tl, GLA_KW), f32), pltpu.VMEM((tl, GLA_KW), f32),
            pltpu.VMEM((tl, GLA_W), f32), pltpu.VMEM((tl, GLA_W), f32), pltpu.VMEM((tl, SWA_W), f32),
        ],
        compiler_params=pltpu.CompilerParams(
            dimension_semantics=("arbitrary", "arbitrary"), vmem_limit_bytes=VMEM_LIMIT),
        name="prompt_layer",
    )(w["sinks"], x, w["norm_in"], w["w_all"], w["w_up"], w["b_gate"], cmat, lmask, bias,
      w["gla_norm"], w["w_out"], w["norm_f"])


def _sample_proj_kernel(x_ref, nin_ref, w_ref, wup_ref, bg_ref, proj_ref, qka_ref):
    u = _rms(x_ref[...], nin_ref[...]).astype(bf16)
    proj_ref[...] = jnp.dot(u, w_ref[:, :W_MAIN], preferred_element_type=f32)
    glow = jnp.dot(u, w_ref[:, O_LOW:O_LOW + LANES], preferred_element_type=f32)
    qka_ref[:, 0:2 * GLA_KW] = proj_ref[:, 0:2 * GLA_KW]
    qka_ref[:, 2 * GLA_KW:3 * GLA_KW] = jnp.exp(_log_decay(glow, wup_ref, bg_ref))


def _sample_proj_call(xs, w):
    n = xs.shape[0]
    return pl.pallas_call(
        _sample_proj_kernel,
        out_shape=[jax.ShapeDtypeStruct((n, W_MAIN), f32), jax.ShapeDtypeStruct((n, 3 * GLA_KW), f32)],
        compiler_params=pltpu.CompilerParams(vmem_limit_bytes=VMEM_LIMIT),
        name="sample_proj",
    )(xs, w["norm_in"], w["w_all"], w["w_up"], w["b_gate"])


def _sample_state_kernel(qka_ref, v_ref, qblk_ref, kn_ref, vn_ref, sink_ref, s_ref, ck_ref, cv_ref,
                         so_ref, cko_ref, cvo_ref, og_ref, os_ref):
    last_row = lax.broadcasted_iota(jnp.int32, (WINDOW, SWA_KVW), 0) == WINDOW - 1
    pos_ok = lax.broadcasted_iota(jnp.int32, (SWA_HEADS, WINDOW), 1) >= 1
    sink = sink_ref[...]
    for j in range(SAMPLE_GROUP):
        for h in range(GLA_HEADS):
            rs = slice(h * GLA_DK, (h + 1) * GLA_DK)
            q_col = qka_ref[h * GLA_DK:(h + 1) * GLA_DK, j:j + 1]
            k_col = qka_ref[GLA_KW + h * GLA_DK:GLA_KW + (h + 1) * GLA_DK, j:j + 1]
            a_col = qka_ref[2 * GLA_KW + h * GLA_DK:2 * GLA_KW + (h + 1) * GLA_DK, j:j + 1]
            v_row = v_ref[j:j + 1, h * GLA_DV:(h + 1) * GLA_DV]
            s_new = a_col * s_ref[j, rs, :] + k_col * v_row
            so_ref[j, rs, :] = s_new
            og_ref[j, h:h + 1, :] = jnp.sum(q_col * s_new, axis=0, keepdims=True)
        kb, vb = ck_ref[j], cv_ref[j]
        qb = qblk_ref[j]
        kn, vn = kn_ref[j], vn_ref[j]
        s = lax.dot_general(qb.astype(bf16), kb.astype(bf16), (((1,), (1,)), ((), ())),
                            preferred_element_type=f32)
        s = jnp.where(pos_ok, s, NEG_INF)
        s_self = jnp.sum(qb * kn, axis=-1, keepdims=True)
        m = jnp.maximum(jnp.maximum(jnp.max(s, axis=-1, keepdims=True), s_self), sink)
        e = jnp.exp(s - m)
        e_self = jnp.exp(s_self - m)
        den = jnp.sum(e, axis=-1, keepdims=True) + e_self + jnp.exp(sink - m)
        o = jnp.dot(e.astype(bf16), vb.astype(bf16), preferred_element_type=f32) + e_self * vn
        os_ref[j] = o / den
        cko_ref[j] = jnp.where(last_row, kn, pltpu.roll(kb, WINDOW - 1, axis=0))
        cvo_ref[j] = jnp.where(last_row, vn, pltpu.roll(vb, WINDOW - 1, axis=0))


def _sample_state_call(qka_t, v, qblk, kn, vn, sinks_col, state, ck, cv):
    n = state.shape[0]
    g = SAMPLE_GROUP
    blk = lambda *tail: pl.BlockSpec((g,) + tail, lambda i: (i,) + (0,) * len(tail))
    return pl.pallas_call(
        _sample_state_kernel,
        grid=(n // g,),
        in_specs=[
            pl.BlockSpec((None, 3 * GLA_KW, g), lambda i: (i, 0, 0)),
            pl.BlockSpec((None, g, GLA_W), lambda i: (i, 0, 0)),
            blk(SWA_HEADS, LANES), blk(1, SWA_KVW), blk(1, SWA_KVW),
            pl.BlockSpec((SWA_HEADS, 1), lambda i: (0, 0)),
            blk(GLA_KW, GLA_DV), blk(WINDOW, SWA_KVW), blk(WINDOW, SWA_KVW),
        ],
        out_specs=[blk(GLA_KW, GLA_DV), blk(WINDOW, SWA_KVW), blk(WINDOW, SWA_KVW),
                   blk(GLA_HEADS, GLA_DV), blk(SWA_HEADS, LANES)],
        out_shape=[
            jax.ShapeDtypeStruct(state.shape, f32), jax.ShapeDtypeStruct(ck.shape, f32),
            jax.ShapeDtypeStruct(cv.shape, f32),
            jax.ShapeDtypeStruct((n, GLA_HEADS, GLA_DV), f32), jax.ShapeDtypeStruct((n, SWA_HEADS, LANES), f32),
        ],
        compiler_params=pltpu.CompilerParams(dimension_semantics=("arbitrary",), vmem_limit_bytes=VMEM_LIMIT),
        name="sample_state",
    )(qka_t, v, qblk, kn, vn, sinks_col, state, ck, cv)


def _sample_merge_kernel(x_ref, og_ref, gg_ref, os_ref, sg_ref, gn_ref, wout_ref, nf_ref, y_ref):
    y_ref[...] = _merge(x_ref[...], og_ref[...], gg_ref[...], os_ref[...], sg_ref[...],
                        gn_ref, wout_ref, nf_ref)


def _sample_merge_call(xs, og, gg, osw, sg, w):
    return pl.pallas_call(
        _sample_merge_kernel,
        out_shape=jax.ShapeDtypeStruct(xs.shape, f32),
        compiler_params=pltpu.CompilerParams(vmem_limit_bytes=VMEM_LIMIT),
        name="sample_merge",
    )(xs, og, gg, osw, sg, w["gla_norm"], w["w_out"], w["norm_f"])


def _sample_path(x_sample, state, ck, cv, w):
    n = x_sample.shape[0]
    g = SAMPLE_GROUP
    xs = x_sample.reshape(n, D_MODEL)
    proj, qka = _sample_proj_call(xs, w)
    qka_t = qka.T.reshape(3 * GLA_KW, n // g, g).transpose(1, 0, 2)
    v = proj[:, O_V:O_V + GLA_W].reshape(n // g, g, GLA_W)
    sq = proj[:, O_SQ:O_SQ + SWA_W].reshape(n, SWA_HEADS, SWA_HD)
    kv_of_pos = (jnp.arange(SWA_HEADS) % SWA_KV)[None, :, None]
    zero = jnp.zeros_like(sq)
    qblk = jnp.concatenate([jnp.where(kv_of_pos == 0, sq, zero), jnp.where(kv_of_pos == 1, sq, zero)], axis=-1)
    kn = proj[:, O_SK:O_SK + SWA_KVW].reshape(n, 1, SWA_KVW)
    vn = proj[:, O_SV:O_SV + SWA_KVW].reshape(n, 1, SWA_KVW)
    s_new, ck_new, cv_new, og, os_raw = _sample_state_call(
        qka_t, v, qblk, kn, vn, w["sinks"].reshape(SWA_HEADS, 1), state, ck, cv)
    os_sel = jnp.where(kv_of_pos == 0, os_raw[..., :SWA_HD], os_raw[..., SWA_HD:]).reshape(n, SWA_W)
    y = _sample_merge_call(xs, og.reshape(n, GLA_W), proj[:, O_GG:O_GG + GLA_W], os_sel,
                           proj[:, O_SG:O_SG + SWA_W], w)
    return y, s_new, ck_new, cv_new


def kernel(x_prompt, x_sample, state_gla, cache_win_k, cache_win_v, norm_in, w_in, w_gate_up, b_gate,
           gla_norm, attn_sinks, w_out, norm_f):
    bsz = x_prompt.shape[0]
    n = x_sample.shape[0]
    w = _prep_weights(norm_in[0], w_in[0], w_gate_up[0], b_gate[0], gla_norm[0], attn_sinks[0],
                      w_out[0], norm_f)
    cmat, lmask = _chunk_tables()
    y_p, s_p, k_p, v_p = _prompt_call(x_prompt, w, cmat, lmask, _swa_bias())
    y_s, s_s, k_s, v_s = _sample_path(
        x_sample, state_gla[0].reshape(n, GLA_KW, GLA_DV),
        cache_win_k[0].reshape(n, WINDOW, SWA_KVW), cache_win_v[0].reshape(n, WINDOW, SWA_KVW), w)
    return (y_p, y_s.reshape(n, 1, D_MODEL),
            s_p.reshape(1, bsz, GLA_HEADS, GLA_DK, GLA_DV),
            k_p.reshape(1, bsz, WINDOW, SWA_KV, SWA_HD), v_p.reshape(1, bsz, WINDOW, SWA_KV, SWA_HD),
            s_s.reshape(1, n, GLA_HEADS, GLA_DK, GLA_DV),
            k_s.reshape(1, n, WINDOW, SWA_KV, SWA_HD), v_s.reshape(1, n, WINDOW, SWA_KV, SWA_HD))
```

```python
import functools

import numpy as np
import jax
import jax.numpy as jnp
from jax import lax
from jax.experimental import pallas as pl
from jax.experimental.pallas import tpu as pltpu

D_MODEL = 1024
GLA_HEADS = 4
GLA_DK = 64
GLA_DV = 128
GLA_KW = GLA_HEADS * GLA_DK
GLA_W = GLA_HEADS * GLA_DV
GLA_RANK = 16
GLA_TAU = 16.0
CHUNK = 64
SWA_HEADS = 8
SWA_HD = 64
SWA_KV = 2
SWA_GROUP = SWA_HEADS // SWA_KV
SWA_W = SWA_HEADS * SWA_HD
SWA_KVW = SWA_KV * SWA_HD
WINDOW = 128
EPS = 1e-6
NEG_INF = -1e30
LANES = 128

O_Q, O_K, O_V, O_GG = 0, 256, 512, 1024
O_SQ, O_SK, O_SV, O_SG, O_LOW = 1536, 2048, 2176, 2304, 2816
W_MAIN = 2816
W_ALL = W_MAIN + LANES

N_LEVELS = 6
TOK_BLOCK = 256
SAMPLE_GROUP = 8
C_Q, C_K, C_A, C_KN, C_VN = 0, 256, 512, 768, 896
N_COLS = 1024
VMEM_LIMIT = 48 * 1024 * 1024

f32 = jnp.float32
bf16 = jnp.bfloat16


def _chunk_tables():
    c = CHUNK
    t = np.arange(c)[None, :]
    i = np.arange(c)[:, None]
    blocks = [(t <= i), (t > i)]
    masks = []
    for l in range(N_LEVELS):
        h = c >> (l + 1)
        m = (i // (2 * h)) * (2 * h) + h
        upper = i >= m
        blocks.append(np.where(upper, (t > m) & (t <= i), (t > i) & (t <= m)))
        jj = np.arange(c)[None, :]
        masks.append((i // (2 * h) == jj // (2 * h)) & (i % (2 * h) >= h) & (jj % (2 * h) < h))
    masks.append(np.eye(c, dtype=bool))
    cm = np.concatenate(blocks, axis=0).astype(np.float32)
    cm3 = np.concatenate([cm, cm, cm], axis=1)
    lm = np.stack(masks).astype(np.float32)
    lm = np.concatenate([lm, lm], axis=1)
    return jnp.asarray(cm3, dtype=bf16), jnp.asarray(lm, dtype=f32)


def _swa_bias():
    qi = (np.arange(2 * WINDOW) % WINDOW)[:, None]
    cj = np.arange(2 * WINDOW)[None, :]
    prev_ok = (cj < WINDOW) & (cj > qi)
    cur_ok = (cj >= WINDOW) & (cj - WINDOW <= qi)
    full = np.where(prev_ok | cur_ok, 0.0, NEG_INF)
    first = np.where(cur_ok, 0.0, NEG_INF)
    return jnp.asarray(np.stack([full, first]), dtype=f32)


def _interleave_heads(a, axis):
    shp = a.shape
    a = a.reshape(shp[:axis] + (SWA_KV, SWA_GROUP, SWA_HD) + shp[axis + 1:])
    return jnp.swapaxes(a, axis, axis + 1).reshape(shp)


def _prep_weights(norm_in, w_in, w_gate_up, b_gate, gla_norm, attn_sinks, w_out, norm_f):
    sp = np.cumsum([0, 256, 256, 512, 512, 16, 512, 128, 128, 512])
    seg = [w_in[:, sp[i]:sp[i + 1]] for i in range(9)]
    gq, gk, gv, gg, glow, sq, sk, sv, sg = seg
    low = jnp.pad(glow, ((0, 0), (0, LANES - GLA_RANK)))
    w_all = jnp.concatenate(
        [gq * (GLA_DK ** -0.5), gk, gv, gg, _interleave_heads(sq, 1) * (SWA_HD ** -0.5), sk, sv,
         _interleave_heads(sg, 1), low], axis=1).astype(bf16)
    w_up = jnp.pad(w_gate_up, ((0, LANES - GLA_RANK), (0, 0))).astype(bf16)
    w_out_p = jnp.concatenate([w_out[:GLA_W], _interleave_heads(w_out[GLA_W:], 0)], axis=0).astype(bf16)
    return dict(
        norm_in=norm_in.reshape(1, D_MODEL), w_all=w_all, w_up=w_up,
        b_gate=b_gate.reshape(1, GLA_KW), gla_norm=jnp.tile(gla_norm, GLA_HEADS).reshape(1, GLA_W),
        sinks=attn_sinks, w_out=w_out_p, norm_f=norm_f.reshape(1, D_MODEL))


def _rms(x, gain):
    return x * lax.rsqrt(jnp.mean(x * x, axis=-1, keepdims=True) + EPS) * gain


def _log_decay(glow, wup_ref, bg_ref):
    z = jnp.dot(glow.astype(bf16), wup_ref[...], preferred_element_type=f32) + bg_ref[...]
    return (jnp.minimum(z, 0.0) - jnp.log(1.0 + jnp.exp(-jnp.abs(z)))) * (1.0 / GLA_TAU)


def _silu(x):
    return x * jax.nn.sigmoid(x)


def _merge(x, o_gla, gg, o_swa, sg, gn_ref, wout_ref, nf_ref):
    parts = []
    for h in range(GLA_HEADS):
        sl = slice(h * GLA_DV, (h + 1) * GLA_DV)
        parts.append(_rms(o_gla[:, sl], gn_ref[:, sl]) * _silu(gg[:, sl]))
    parts.append(o_swa * _silu(sg))
    um = jnp.concatenate(parts, axis=1).astype(bf16)
    hres = x + jnp.dot(um, wout_ref[...], preferred_element_type=f32)
    return _rms(hres, nf_ref[...])


def _prompt_kernel(sinks_ref, x_ref, nin_ref, w_ref, wup_ref, bg_ref, cmat_ref, lmask_ref, bias_ref,
                   gn_ref, wout_ref, nf_ref,
                   y_ref, sp_ref, kn_ref, vn_ref,
                   s_ref, kprev_ref, vprev_ref, q_s, k_s, g_s, v_s, ogla_s, oswa_s, *, n_t):
    t = pl.program_id(1)
    tl = x_ref.shape[0]

    @pl.when(t == 0)
    def _():
        s_ref[...] = jnp.zeros_like(s_ref)
        kprev_ref[...] = jnp.zeros_like(kprev_ref)
        vprev_ref[...] = jnp.zeros_like(vprev_ref)

    x = x_ref[...]
    u = _rms(x, nin_ref[...]).astype(bf16)

    def proj(lo, width):
        return jnp.dot(u, w_ref[:, lo:lo + width], preferred_element_type=f32)

    q_s[...] = proj(O_Q, GLA_KW)
    k_s[...] = proj(O_K, GLA_KW)
    v_s[...] = proj(O_V, GLA_W)
    g_s[...] = _log_decay(proj(O_LOW, LANES), wup_ref, bg_ref)

    lane_lo = lax.broadcasted_iota(jnp.int32, (CHUNK, LANES), 1) < GLA_DK

    n_pairs = GLA_HEADS // 2

    def chunk_terms(c):
        rows = slice(c * CHUNK, (c + 1) * CHUNK)
        qc, kc, gc = q_s[rows, :], k_s[rows, :], g_s[rows, :]
        vb = v_s[rows, :].astype(bf16)
        g_hi = gc.astype(bf16)
        r1 = gc - g_hi.astype(f32)
        g_mid = r1.astype(bf16)
        g_lo = (r1 - g_mid.astype(f32)).astype(bf16)
        sums = jnp.dot(cmat_ref[...], jnp.concatenate([g_hi, g_mid, g_lo], axis=0),
                       preferred_element_type=f32)
        e_b = jnp.exp(sums[0:CHUNK])
        qb = qc * e_b
        k_suf = kc * jnp.exp(sums[CHUNK:2 * CHUNK])
        terms = []
        for p in range(n_pairs):
            ln = slice(p * LANES, (p + 1) * LANES)
            qp, kp = qc[:, ln], kc[:, ln]
            a = jnp.zeros((2 * CHUNK, CHUNK), f32)
            for l in range(N_LEVELS + 1):
                if l < N_LEVELS:
                    e = jnp.exp(sums[(2 + l) * CHUNK:(3 + l) * CHUNK, ln])
                    qe, ke = (qp * e).astype(bf16), (kp * e).astype(bf16)
                else:
                    qe, ke = qp.astype(bf16), kp.astype(bf16)
                lhs = jnp.concatenate([jnp.where(lane_lo, qe, 0), jnp.where(lane_lo, 0, qe)], axis=0)
                sc = lax.dot_general(lhs, ke, (((1,), (1,)), ((), ())), preferred_element_type=f32)
                a = a + sc * lmask_ref[l]
            ab = a.astype(bf16)
            qbp = qb[:, ln].astype(bf16)
            lhs_heads = []
            for hh in range(2):
                qbm = jnp.where(lane_lo, qbp, 0) if hh == 0 else jnp.where(lane_lo, 0, qbp)
                lhs_heads.append(jnp.concatenate([qbm, ab[hh * CHUNK:(hh + 1) * CHUNK, :]], axis=1))
            upd = lax.dot_general(k_suf[:, ln].astype(bf16), vb[:, p * 2 * GLA_DV:(p + 1) * 2 * GLA_DV],
                                  (((0,), (0,)), ((), ())), preferred_element_type=f32)
            upd = jnp.concatenate([upd[0:GLA_DK, 0:GLA_DV], upd[GLA_DK:2 * GLA_DK, GLA_DV:2 * GLA_DV]], axis=0)
            e_col = jnp.broadcast_to(e_b[CHUNK - 1:CHUNK, ln], (LANES, LANES)).T
            terms.append((lhs_heads, vb, upd, e_col))
        return terms

    all_terms = [chunk_terms(c) for c in range(tl // CHUNK)]
    for p in range(n_pairs):
        ln = slice(p * LANES, (p + 1) * LANES)
        s_pair = s_ref[ln, :]
        for c in range(tl // CHUNK):
            lhs_heads, vb, upd, e_col = all_terms[c][p]
            s_pair_b = s_pair.astype(bf16)
            for hh in range(2):
                h = 2 * p + hh
                rhs = jnp.concatenate([s_pair_b, vb[:, h * GLA_DV:(h + 1) * GLA_DV]], axis=0)
                ogla_s[c * CHUNK:(c + 1) * CHUNK, h * GLA_DV:(h + 1) * GLA_DV] = jnp.dot(
                    lhs_heads[hh], rhs, preferred_element_type=f32)
            s_pair = e_col * s_pair + upd
        s_ref[ln, :] = s_pair

    lane_lo_w = lax.broadcasted_iota(jnp.int32, (WINDOW, LANES), 1) < SWA_HD
    row_lo = lax.broadcasted_iota(jnp.int32, (2 * WINDOW, 1), 0) < WINDOW
    for blk in range(tl // WINDOW):
        rs = slice(blk * WINDOW, (blk + 1) * WINDOW)
        ub = u[rs]
        sq = jnp.dot(ub, w_ref[:, O_SQ:O_SQ + SWA_W], preferred_element_type=f32).astype(bf16)
        kv = jnp.dot(ub, w_ref[:, O_SK:O_SK + 2 * SWA_KVW], preferred_element_type=f32)
        k_cur, v_cur = kv[:, :SWA_KVW], kv[:, SWA_KVW:]
        k2 = jnp.concatenate([kprev_ref[...], k_cur], axis=0).astype(bf16)
        v2 = jnp.concatenate([vprev_ref[...], v_cur], axis=0).astype(bf16)
        if blk == 0:
            bias = bias_ref[jnp.where(t > 0, 0, 1)]
        else:
            bias = bias_ref[0]
        for tt in range(SWA_GROUP):
            qt = sq[:, tt * LANES:(tt + 1) * LANES]
            lhs = jnp.concatenate([jnp.where(lane_lo_w, qt, 0), jnp.where(lane_lo_w, 0, qt)], axis=0)
            s = lax.dot_general(lhs, k2, (((1,), (1,)), ((), ())), preferred_element_type=f32) + bias
            sink = jnp.where(row_lo, sinks_ref[tt], sinks_ref[SWA_GROUP + tt])
            m = jnp.maximum(jnp.max(s, axis=-1, keepdims=True), sink)
            e = jnp.exp(s - m)
            den = jnp.sum(e, axis=-1, keepdims=True) + jnp.exp(sink - m)
            pr = (e / den).astype(bf16)
            o2 = jnp.dot(pr, v2, preferred_element_type=f32)
            oswa_s[rs, tt * LANES:(tt + 1) * LANES] = jnp.where(lane_lo_w, o2[:WINDOW], o2[WINDOW:])
        kprev_ref[...] = k_cur
        vprev_ref[...] = v_cur

    gg = proj(O_GG, GLA_W)
    sg = proj(O_SG, SWA_W)
    y_ref[...] = _merge(x, ogla_s[...], gg, oswa_s[...], sg, gn_ref, wout_ref, nf_ref)

    @pl.when(t == n_t - 1)
    def _():
        sp_ref[...] = s_ref[...]
        kn_ref[...] = kprev_ref[...].T
        vn_ref[...] = vprev_ref[...].T


def _prompt_call(x, w, cmat, lmask, bias):
    bsz, seq, _ = x.shape
    tl = TOK_BLOCK
    n_t = seq // tl
    const = lambda shape: pl.BlockSpec(shape, lambda b, t: (0,) * len(shape))
    return pl.pallas_call(
        functools.partial(_prompt_kernel, n_t=n_t),
        grid=(bsz, n_t),
        in_specs=[
            pl.BlockSpec(memory_space=pltpu.SMEM),
            pl.BlockSpec((None, tl, D_MODEL), lambda b, t: (b, t, 0)),
            const((1, D_MODEL)), const((D_MODEL, W_ALL)), const((LANES, GLA_KW)), const((1, GLA_KW)),
            const(cmat.shape), const(lmask.shape), const(bias.shape),
            const((1, GLA_W)), const((D_MODEL, D_MODEL)), const((1, D_MODEL)),
        ],
        out_specs=[
            pl.BlockSpec((None, tl, D_MODEL), lambda b, t: (b, t, 0)),
            pl.BlockSpec((None, GLA_KW, GLA_DV), lambda b, t: (b, 0, 0)),
            pl.BlockSpec((None, WINDOW, SWA_KVW), lambda b, t: (b, 0, 0)),
            pl.BlockSpec((None, WINDOW, SWA_KVW), lambda b, t: (b, 0, 0)),
        ],
        out_shape=[
            jax.ShapeDtypeStruct((bsz, seq, D_MODEL), f32),
            jax.ShapeDtypeStruct((bsz, GLA_KW, GLA_DV), f32),
            jax.ShapeDtypeStruct((bsz, WINDOW, SWA_KVW), f32),
            jax.ShapeDtypeStruct((bsz, WINDOW, SWA_KVW), f32),
        ],
        scratch_shapes=[
            pltpu.VMEM((GLA_KW, GLA_DV), f32),
            pltpu.VMEM((WINDOW, SWA_KVW), f32), pltpu.VMEM((WINDOW, SWA_KVW), f32),
            pltpu.VMEM((tl, GLA_KW), f32), pltpu.VMEM((tl, GLA_KW), f32), pltpu.VMEM((tl, GLA_KW), f32),
            pltpu.VMEM((tl, GLA_W), f32), pltpu.VMEM((tl, GLA_W), f32), pltpu.VMEM((tl, SWA_W), f32),
        ],
        compiler_params=pltpu.CompilerParams(
            dimension_semantics=("arbitrary", "arbitrary"), vmem_limit_bytes=VMEM_LIMIT),
        name="prompt_layer",
    )(w["sinks"], x, w["norm_in"], w["w_all"], w["w_up"], w["b_gate"], cmat, lmask, bias,
      w["gla_norm"], w["w_out"], w["norm_f"])


def _sample_proj_kernel(x_ref, nin_ref, w_ref, wup_ref, bg_ref, proj_ref, cols_ref):
    u = _rms(x_ref[...], nin_ref[...]).astype(bf16)
    proj = jnp.dot(u, w_ref[:, :W_MAIN], preferred_element_type=f32)
    proj_ref[...] = proj
    glow = jnp.dot(u, w_ref[:, O_LOW:O_LOW + LANES], preferred_element_type=f32)
    decay = jnp.exp(_log_decay(glow, wup_ref, bg_ref))
    cols_t = jnp.concatenate([proj[:, O_Q:O_V], decay, proj[:, O_SK:O_SG]], axis=1).T
    for i in range(cols_ref.shape[0]):
        cols_ref[i] = cols_t[:, i * SAMPLE_GROUP:(i + 1) * SAMPLE_GROUP]


def _sample_proj_call(xs, w):
    n = xs.shape[0]
    return pl.pallas_call(
        _sample_proj_kernel,
        out_shape=[jax.ShapeDtypeStruct((n, W_MAIN), f32),
                   jax.ShapeDtypeStruct((n // SAMPLE_GROUP, N_COLS, SAMPLE_GROUP), f32)],
        compiler_params=pltpu.CompilerParams(vmem_limit_bytes=VMEM_LIMIT),
        name="sample_proj",
    )(xs, w["norm_in"], w["w_all"], w["w_up"], w["b_gate"])


def _sample_state_kernel(cols_ref, v_ref, q_ref, sink_ref, s_ref, ck_ref, cv_ref,
                         so_ref, cko_ref, cvo_ref, og_ref, os_ref):
    newest = lax.broadcasted_iota(jnp.int32, (SWA_HD, WINDOW), 1) == WINDOW - 1
    for j in range(SAMPLE_GROUP):
        col = lambda base, h, width: cols_ref[base + h * width:base + (h + 1) * width, j:j + 1]
        for h in range(GLA_HEADS):
            rs = slice(h * GLA_DK, (h + 1) * GLA_DK)
            v_row = v_ref[j:j + 1, h * GLA_DV:(h + 1) * GLA_DV]
            s_new = col(C_A, h, GLA_DK) * s_ref[j, rs, :] + col(C_K, h, GLA_DK) * v_row
            so_ref[j, rs, :] = s_new
            og_ref[j, h:h + 1, :] = jnp.sum(col(C_Q, h, GLA_DK) * s_new, axis=0, keepdims=True)
        for kv in range(SWA_KV):
            kt = jnp.where(newest, col(C_KN, kv, SWA_HD), pltpu.roll(ck_ref[j, kv], WINDOW - 1, axis=1))
            vt = jnp.where(newest, col(C_VN, kv, SWA_HD), pltpu.roll(cv_ref[j, kv], WINDOW - 1, axis=1))
            cko_ref[j, kv] = kt
            cvo_ref[j, kv] = vt
            hs = slice(kv * SWA_GROUP, (kv + 1) * SWA_GROUP)
            s = jnp.dot(q_ref[j, hs, :].astype(bf16), kt.astype(bf16), preferred_element_type=f32)
            sink = sink_ref[hs, :]
            m = jnp.maximum(jnp.max(s, axis=-1, keepdims=True), sink)
            e = jnp.exp(s - m)
            den = jnp.sum(e, axis=-1, keepdims=True) + jnp.exp(sink - m)
            o = lax.dot_general(e.astype(bf16), vt.astype(bf16), (((1,), (1,)), ((), ())),
                                preferred_element_type=f32)
            os_ref[j, hs, :] = o / den


def _sample_state_call(cols_t, v, q, sinks_col, state, ck, cv):
    n = state.shape[0]
    g = SAMPLE_GROUP
    blk = lambda *tail: pl.BlockSpec((g,) + tail, lambda i: (i,) + (0,) * len(tail))
    cache = blk(SWA_KV, SWA_HD, WINDOW)
    return pl.pallas_call(
        _sample_state_kernel,
        grid=(n // g,),
        in_specs=[
            pl.BlockSpec((None, N_COLS, g), lambda i: (i, 0, 0)),
            pl.BlockSpec((None, g, GLA_W), lambda i: (i, 0, 0)),
            blk(SWA_HEADS, SWA_HD),
            pl.BlockSpec((SWA_HEADS, 1), lambda i: (0, 0)),
            blk(GLA_KW, GLA_DV), cache, cache,
        ],
        out_specs=[blk(GLA_KW, GLA_DV), cache, cache, blk(GLA_HEADS, GLA_DV), blk(SWA_HEADS, SWA_HD)],
        out_shape=[
            jax.ShapeDtypeStruct(state.shape, f32), jax.ShapeDtypeStruct(ck.shape, f32),
            jax.ShapeDtypeStruct(cv.shape, f32),
            jax.ShapeDtypeStruct((n, GLA_HEADS, GLA_DV), f32), jax.ShapeDtypeStruct((n, SWA_HEADS, SWA_HD), f32),
        ],
        compiler_params=pltpu.CompilerParams(dimension_semantics=("arbitrary",), vmem_limit_bytes=VMEM_LIMIT),
        name="sample_state",
    )(cols_t, v, q, sinks_col, state, ck, cv)


def _sample_merge_kernel(x_ref, og_ref, gg_ref, os_ref, sg_ref, gn_ref, wout_ref, nf_ref, y_ref):
    y_ref[...] = _merge(x_ref[...], og_ref[...], gg_ref[...], os_ref[...], sg_ref[...],
                        gn_ref, wout_ref, nf_ref)


def _sample_merge_call(xs, og, gg, osw, sg, w):
    return pl.pallas_call(
        _sample_merge_kernel,
        out_shape=jax.ShapeDtypeStruct(xs.shape, f32),
        compiler_params=pltpu.CompilerParams(vmem_limit_bytes=VMEM_LIMIT),
        name="sample_merge",
    )(xs, og, gg, osw, sg, w["gla_norm"], w["w_out"], w["norm_f"])


def _sample_path(x_sample, state, ck_t, cv_t, w):
    n = x_sample.shape[0]
    g = SAMPLE_GROUP
    xs = x_sample.reshape(n, D_MODEL)
    proj, cols_t = _sample_proj_call(xs, w)
    v = proj[:, O_V:O_V + GLA_W].reshape(n // g, g, GLA_W)
    q = proj[:, O_SQ:O_SQ + SWA_W].reshape(n, SWA_GROUP, SWA_KV, SWA_HD).swapaxes(1, 2)
    s_new, ck_new, cv_new, og, os_nat = _sample_state_call(
        cols_t, v, q.reshape(n, SWA_HEADS, SWA_HD), w["sinks"].reshape(SWA_HEADS, 1), state, ck_t, cv_t)
    os_il = os_nat.reshape(n, SWA_KV, SWA_GROUP, SWA_HD).swapaxes(1, 2).reshape(n, SWA_W)
    y = _sample_merge_call(xs, og.reshape(n, GLA_W), proj[:, O_GG:O_GG + GLA_W], os_il,
                           proj[:, O_SG:O_SG + SWA_W], w)
    return y, s_new, ck_new, cv_new


def _cache_view(c):
    return jnp.transpose(c[0], (0, 2, 3, 1))


def _cache_unview(c):
    return jnp.transpose(c, (0, 3, 1, 2))[None]


def kernel(x_prompt, x_sample, state_gla, cache_win_k, cache_win_v, norm_in, w_in, w_gate_up, b_gate,
           gla_norm, attn_sinks, w_out, norm_f):
    bsz = x_prompt.shape[0]
    n = x_sample.shape[0]
    w = _prep_weights(norm_in[0], w_in[0], w_gate_up[0], b_gate[0], gla_norm[0], attn_sinks[0],
                      w_out[0], norm_f)
    cmat, lmask = _chunk_tables()
    y_p, s_p, k_p, v_p = _prompt_call(x_prompt, w, cmat, lmask, _swa_bias())
    y_s, s_s, k_s, v_s = _sample_path(x_sample, state_gla[0].reshape(n, GLA_KW, GLA_DV),
                                      _cache_view(cache_win_k), _cache_view(cache_win_v), w)
    return (y_p, y_s.reshape(n, 1, D_MODEL),
            s_p.reshape(1, bsz, GLA_HEADS, GLA_DK, GLA_DV),
            _cache_unview(k_p.reshape(bsz, SWA_KV, SWA_HD, WINDOW)),
            _cache_unview(v_p.reshape(bsz, SWA_KV, SWA_HD, WINDOW)),
            s_s.reshape(1, n, GLA_HEADS, GLA_DK, GLA_DV),
            _cache_unview(k_s), _cache_unview(v_s))
```

```python
import functools

import numpy as np
import jax
import jax.numpy as jnp
from jax import lax
from jax.experimental import pallas as pl
from jax.experimental.pallas import tpu as pltpu

D_MODEL = 1024
GLA_HEADS = 4
GLA_DK = 64
GLA_DV = 128
GLA_KW = GLA_HEADS * GLA_DK
GLA_W = GLA_HEADS * GLA_DV
GLA_RANK = 16
GLA_TAU = 16.0
CHUNK = 64
SWA_HEADS = 8
SWA_HD = 64
SWA_KV = 2
SWA_GROUP = SWA_HEADS // SWA_KV
SWA_W = SWA_HEADS * SWA_HD
SWA_KVW = SWA_KV * SWA_HD
WINDOW = 128
EPS = 1e-6
NEG_INF = -1e30
LANES = 128

O_Q, O_K, O_V, O_GG = 0, 256, 512, 1024
O_SQ, O_SK, O_SV, O_SG, O_LOW = 1536, 2048, 2176, 2304, 2816
W_MAIN = 2816
W_ALL = W_MAIN + LANES

N_LEVELS = 6
TOK_BLOCK = 512
SAMPLE_GROUP = 8
C_Q, C_K, C_A, C_KN, C_VN = 0, 256, 512, 768, 896
N_COLS = 1024
VMEM_LIMIT = 48 * 1024 * 1024

f32 = jnp.float32
bf16 = jnp.bfloat16


def _chunk_tables():
    c = CHUNK
    t = np.arange(c)[None, :]
    i = np.arange(c)[:, None]
    blocks = [(t <= i), (t > i)]
    masks = []
    for l in range(N_LEVELS):
        h = c >> (l + 1)
        m = (i // (2 * h)) * (2 * h) + h
        upper = i >= m
        blocks.append(np.where(upper, (t > m) & (t <= i), (t > i) & (t <= m)))
        jj = np.arange(c)[None, :]
        masks.append((i // (2 * h) == jj // (2 * h)) & (i % (2 * h) >= h) & (jj % (2 * h) < h))
    masks.append(np.eye(c, dtype=bool))
    cm = np.concatenate(blocks, axis=0).astype(np.float32)
    cm3 = np.concatenate([cm, cm, cm], axis=1)
    lm = np.stack(masks).astype(np.float32)
    lm = np.concatenate([lm, lm], axis=1)
    return jnp.asarray(cm3, dtype=bf16), jnp.asarray(lm, dtype=f32)


def _swa_bias():
    qi = (np.arange(2 * WINDOW) % WINDOW)[:, None]
    cj = np.arange(2 * WINDOW)[None, :]
    prev_ok = (cj < WINDOW) & (cj > qi)
    cur_ok = (cj >= WINDOW) & (cj - WINDOW <= qi)
    full = np.where(prev_ok | cur_ok, 0.0, NEG_INF)
    first = np.where(cur_ok, 0.0, NEG_INF)
    return jnp.asarray(np.stack([full, first]), dtype=f32)


def _interleave_heads(a, axis):
    shp = a.shape
    a = a.reshape(shp[:axis] + (SWA_KV, SWA_GROUP, SWA_HD) + shp[axis + 1:])
    return jnp.swapaxes(a, axis, axis + 1).reshape(shp)


def _prep_weights(norm_in, w_in, w_gate_up, b_gate, gla_norm, attn_sinks, w_out, norm_f):
    sp = np.cumsum([0, 256, 256, 512, 512, 16, 512, 128, 128, 512])
    seg = [w_in[:, sp[i]:sp[i + 1]] for i in range(9)]
    gq, gk, gv, gg, glow, sq, sk, sv, sg = seg
    low = jnp.pad(glow, ((0, 0), (0, LANES - GLA_RANK)))
    w_all = jnp.concatenate(
        [gq * (GLA_DK ** -0.5), gk, gv, gg, _interleave_heads(sq, 1) * (SWA_HD ** -0.5), sk, sv,
         _interleave_heads(sg, 1), low], axis=1).astype(bf16)
    w_up = jnp.pad(w_gate_up, ((0, LANES - GLA_RANK), (0, 0))).astype(bf16)
    w_out_p = jnp.concatenate([w_out[:GLA_W], _interleave_heads(w_out[GLA_W:], 0)], axis=0).astype(bf16)
    return dict(
        norm_in=norm_in.reshape(1, D_MODEL), w_all=w_all, w_up=w_up,
        b_gate=b_gate.reshape(1, GLA_KW), gla_norm=jnp.tile(gla_norm, GLA_HEADS).reshape(1, GLA_W),
        sinks=attn_sinks, w_out=w_out_p, norm_f=norm_f.reshape(1, D_MODEL))


def _rms(x, gain):
    return x * lax.rsqrt(jnp.mean(x * x, axis=-1, keepdims=True) + EPS) * gain


def _log_decay(glow, wup_ref, bg_ref):
    z = jnp.dot(glow.astype(bf16), wup_ref[...], preferred_element_type=f32) + bg_ref[...]
    return (jnp.minimum(z, 0.0) - jnp.log(1.0 + jnp.exp(-jnp.abs(z)))) * (1.0 / GLA_TAU)


def _silu(x):
    return x * jax.nn.sigmoid(x)


def _merge(x, o_gla, gg, o_swa, sg, gn_ref, wout_ref, nf_ref):
    parts = []
    for h in range(GLA_HEADS):
        sl = slice(h * GLA_DV, (h + 1) * GLA_DV)
        parts.append(_rms(o_gla[:, sl], gn_ref[:, sl]) * _silu(gg[:, sl]))
    parts.append(o_swa * _silu(sg))
    um = jnp.concatenate(parts, axis=1).astype(bf16)
    hres = x + jnp.dot(um, wout_ref[...], preferred_element_type=f32)
    return _rms(hres, nf_ref[...])


def _prompt_kernel(sinks_ref, x_ref, nin_ref, w_ref, wup_ref, bg_ref, cmat_ref, lmask_ref, bias_ref,
                   gn_ref, wout_ref, nf_ref,
                   y_ref, sp_ref, kn_ref, vn_ref,
                   s_ref, kprev_ref, vprev_ref, q_s, k_s, g_s, v_s, ogla_s, oswa_s, *, n_t):
    t = pl.program_id(1)
    tl = x_ref.shape[0]

    @pl.when(t == 0)
    def _():
        s_ref[...] = jnp.zeros_like(s_ref)
        kprev_ref[...] = jnp.zeros_like(kprev_ref)
        vprev_ref[...] = jnp.zeros_like(vprev_ref)

    x = x_ref[...]
    u = _rms(x, nin_ref[...]).astype(bf16)

    def proj(lo, width):
        return jnp.dot(u, w_ref[:, lo:lo + width], preferred_element_type=f32)

    q_s[...] = proj(O_Q, GLA_KW)
    k_s[...] = proj(O_K, GLA_KW)
    v_s[...] = proj(O_V, GLA_W)
    g_s[...] = _log_decay(proj(O_LOW, LANES), wup_ref, bg_ref)

    lane_lo = lax.broadcasted_iota(jnp.int32, (CHUNK, LANES), 1) < GLA_DK

    n_pairs = GLA_HEADS // 2

    def chunk_terms(c):
        rows = slice(c * CHUNK, (c + 1) * CHUNK)
        qc, kc, gc = q_s[rows, :], k_s[rows, :], g_s[rows, :]
        vb = v_s[rows, :].astype(bf16)
        g_hi = gc.astype(bf16)
        r1 = gc - g_hi.astype(f32)
        g_mid = r1.astype(bf16)
        g_lo = (r1 - g_mid.astype(f32)).astype(bf16)
        sums = jnp.dot(cmat_ref[...], jnp.concatenate([g_hi, g_mid, g_lo], axis=0),
                       preferred_element_type=f32)
        e_b = jnp.exp(sums[0:CHUNK])
        qb = qc * e_b
        k_suf = kc * jnp.exp(sums[CHUNK:2 * CHUNK])
        terms = []
        for p in range(n_pairs):
            ln = slice(p * LANES, (p + 1) * LANES)
            qp, kp = qc[:, ln], kc[:, ln]
            a = jnp.zeros((2 * CHUNK, CHUNK), f32)
            for l in range(N_LEVELS + 1):
                if l < N_LEVELS:
                    e = jnp.exp(sums[(2 + l) * CHUNK:(3 + l) * CHUNK, ln])
                    qe, ke = (qp * e).astype(bf16), (kp * e).astype(bf16)
                else:
                    qe, ke = qp.astype(bf16), kp.astype(bf16)
                lhs = jnp.concatenate([jnp.where(lane_lo, qe, 0), jnp.where(lane_lo, 0, qe)], axis=0)
                sc = lax.dot_general(lhs, ke, (((1,), (1,)), ((), ())), preferred_element_type=f32)
                a = a + sc * lmask_ref[l]
            ab = a.astype(bf16)
            qbp = qb[:, ln].astype(bf16)
            lhs_heads = []
            for hh in range(2):
                qbm = jnp.where(lane_lo, qbp, 0) if hh == 0 else jnp.where(lane_lo, 0, qbp)
                lhs_heads.append(jnp.concatenate([qbm, ab[hh * CHUNK:(hh + 1) * CHUNK, :]], axis=1))
            upd = lax.dot_general(k_suf[:, ln].astype(bf16), vb[:, p * 2 * GLA_DV:(p + 1) * 2 * GLA_DV],
                                  (((0,), (0,)), ((), ())), preferred_element_type=f32)
            upd = jnp.concatenate([upd[0:GLA_DK, 0:GLA_DV], upd[GLA_DK:2 * GLA_DK, GLA_DV:2 * GLA_DV]], axis=0)
            e_col = jnp.broadcast_to(e_b[CHUNK - 1:CHUNK, ln], (LANES, LANES)).T
            terms.append((lhs_heads, vb, upd, e_col))
        return terms

    all_terms = [chunk_terms(c) for c in range(tl // CHUNK)]
    for p in range(n_pairs):
        ln = slice(p * LANES, (p + 1) * LANES)
        s_pair = s_ref[ln, :]
        for c in range(tl // CHUNK):
            lhs_heads, vb, upd, e_col = all_terms[c][p]
            s_pair_b = s_pair.astype(bf16)
            for hh in range(2):
                h = 2 * p + hh
                rhs = jnp.concatenate([s_pair_b, vb[:, h * GLA_DV:(h + 1) * GLA_DV]], axis=0)
                ogla_s[c * CHUNK:(c + 1) * CHUNK, h * GLA_DV:(h + 1) * GLA_DV] = jnp.dot(
                    lhs_heads[hh], rhs, preferred_element_type=f32)
            s_pair = e_col * s_pair + upd
        s_ref[ln, :] = s_pair

    lane_lo_w = lax.broadcasted_iota(jnp.int32, (WINDOW, LANES), 1) < SWA_HD
    row_lo = lax.broadcasted_iota(jnp.int32, (2 * WINDOW, 1), 0) < WINDOW
    for blk in range(tl // WINDOW):
        rs = slice(blk * WINDOW, (blk + 1) * WINDOW)
        ub = u[rs]
        sq = jnp.dot(ub, w_ref[:, O_SQ:O_SQ + SWA_W], preferred_element_type=f32).astype(bf16)
        kv = jnp.dot(ub, w_ref[:, O_SK:O_SK + 2 * SWA_KVW], preferred_element_type=f32)
        k_cur, v_cur = kv[:, :SWA_KVW], kv[:, SWA_KVW:]
        k2 = jnp.concatenate([kprev_ref[...], k_cur], axis=0).astype(bf16)
        v2 = jnp.concatenate([vprev_ref[...], v_cur], axis=0).astype(bf16)
        if blk == 0:
            bias = bias_ref[jnp.where(t > 0, 0, 1)]
        else:
            bias = bias_ref[0]
        for tt in range(SWA_GROUP):
            qt = sq[:, tt * LANES:(tt + 1) * LANES]
            lhs = jnp.concatenate([jnp.where(lane_lo_w, qt, 0), jnp.where(lane_lo_w, 0, qt)], axis=0)
            s = lax.dot_general(lhs, k2, (((1,), (1,)), ((), ())), preferred_element_type=f32) + bias
            sink = jnp.where(row_lo, sinks_ref[tt], sinks_ref[SWA_GROUP + tt])
            m = jnp.maximum(jnp.max(s, axis=-1, keepdims=True), sink)
            e = jnp.exp(s - m)
            den = jnp.sum(e, axis=-1, keepdims=True) + jnp.exp(sink - m)
            pr = (e / den).astype(bf16)
            o2 = jnp.dot(pr, v2, preferred_element_type=f32)
            oswa_s[rs, tt * LANES:(tt + 1) * LANES] = jnp.where(lane_lo_w, o2[:WINDOW], o2[WINDOW:])
        kprev_ref[...] = k_cur
        vprev_ref[...] = v_cur

    gg = proj(O_GG, GLA_W)
    sg = proj(O_SG, SWA_W)
    y_ref[...] = _merge(x, ogla_s[...], gg, oswa_s[...], sg, gn_ref, wout_ref, nf_ref)

    @pl.when(t == n_t - 1)
    def _():
        sp_ref[...] = s_ref[...]
        kn_ref[...] = kprev_ref[...].T
        vn_ref[...] = vprev_ref[...].T


def _prompt_call(x, w, cmat, lmask, bias):
    bsz, seq, _ = x.shape
    tl = TOK_BLOCK
    n_t = seq // tl
    const = lambda shape: pl.BlockSpec(shape, lambda b, t: (0,) * len(shape))
    return pl.pallas_call(
        functools.partial(_prompt_kernel, n_t=n_t),
        grid=(bsz, n_t),
        in_specs=[
            pl.BlockSpec(memory_space=pltpu.SMEM),
            pl.BlockSpec((None, tl, D_MODEL), lambda b, t: (b, t, 0)),
            const((1, D_MODEL)), const((D_MODEL, W_ALL)), const((LANES, GLA_KW)), const((1, GLA_KW)),
            const(cmat.shape), const(lmask.shape), const(bias.shape),
            const((1, GLA_W)), const((D_MODEL, D_MODEL)), const((1, D_MODEL)),
        ],
        out_specs=[
            pl.BlockSpec((None, tl, D_MODEL), lambda b, t: (b, t, 0)),
            pl.BlockSpec((None, GLA_KW, GLA_DV), lambda b, t: (b, 0, 0)),
            pl.BlockSpec((None, WINDOW, SWA_KVW), lambda b, t: (b, 0, 0)),
            pl.BlockSpec((None, WINDOW, SWA_KVW), lambda b, t: (b, 0, 0)),
        ],
        out_shape=[
            jax.ShapeDtypeStruct((bsz, seq, D_MODEL), f32),
            jax.ShapeDtypeStruct((bsz, GLA_KW, GLA_DV), f32),
            jax.ShapeDtypeStruct((bsz, WINDOW, SWA_KVW), f32),
            jax.ShapeDtypeStruct((bsz, WINDOW, SWA_KVW), f32),
        ],
        scratch_shapes=[
            pltpu.VMEM((GLA_KW, GLA_DV), f32),
            pltpu.VMEM((WINDOW, SWA_KVW), f32), pltpu.VMEM((WINDOW, SWA_KVW), f32),
            pltpu.VMEM((tl, GLA_KW), f32), pltpu.VMEM((tl, GLA_KW), f32), pltpu.VMEM((tl, GLA_KW), f32),
            pltpu.VMEM((tl, GLA_W), f32), pltpu.VMEM((tl, GLA_W), f32), pltpu.VMEM((tl, SWA_W), f32),
        ],
        compiler_params=pltpu.CompilerParams(
            dimension_semantics=("arbitrary", "arbitrary"), vmem_limit_bytes=VMEM_LIMIT),
        name="prompt_layer",
    )(w["sinks"], x, w["norm_in"], w["w_all"], w["w_up"], w["b_gate"], cmat, lmask, bias,
      w["gla_norm"], w["w_out"], w["norm_f"])


def _sample_proj_kernel(x_ref, nin_ref, w_ref, wup_ref, bg_ref, proj_ref, cols_ref):
    u = _rms(x_ref[...], nin_ref[...]).astype(bf16)
    proj = jnp.dot(u, w_ref[:, :W_MAIN], preferred_element_type=f32)
    proj_ref[...] = proj
    glow = jnp.dot(u, w_ref[:, O_LOW:O_LOW + LANES], preferred_element_type=f32)
    decay = jnp.exp(_log_decay(glow, wup_ref, bg_ref))
    cols_t = jnp.concatenate([proj[:, O_Q:O_V], decay, proj[:, O_SK:O_SG]], axis=1).T
    for i in range(cols_ref.shape[0]):
        cols_ref[i] = cols_t[:, i * SAMPLE_GROUP:(i + 1) * SAMPLE_GROUP]


def _sample_proj_call(xs, w):
    n = xs.shape[0]
    return pl.pallas_call(
        _sample_proj_kernel,
        out_shape=[jax.ShapeDtypeStruct((n, W_MAIN), f32),
                   jax.ShapeDtypeStruct((n // SAMPLE_GROUP, N_COLS, SAMPLE_GROUP), f32)],
        compiler_params=pltpu.CompilerParams(vmem_limit_bytes=VMEM_LIMIT),
        name="sample_proj",
    )(xs, w["norm_in"], w["w_all"], w["w_up"], w["b_gate"])


def _sample_state_kernel(cols_ref, v_ref, q_ref, sink_ref, s_ref, ck_ref, cv_ref,
                         so_ref, cko_ref, cvo_ref, og_ref, os_ref):
    newest = lax.broadcasted_iota(jnp.int32, (SWA_HD, WINDOW), 1) == WINDOW - 1
    for j in range(SAMPLE_GROUP):
        col = lambda base, h, width: cols_ref[base + h * width:base + (h + 1) * width, j:j + 1]
        for h in range(GLA_HEADS):
            rs = slice(h * GLA_DK, (h + 1) * GLA_DK)
            v_row = v_ref[j:j + 1, h * GLA_DV:(h + 1) * GLA_DV]
            s_new = col(C_A, h, GLA_DK) * s_ref[j, rs, :] + col(C_K, h, GLA_DK) * v_row
            so_ref[j, rs, :] = s_new
            og_ref[j, h:h + 1, :] = jnp.sum(col(C_Q, h, GLA_DK) * s_new, axis=0, keepdims=True)
        for kv in range(SWA_KV):
            kt = jnp.where(newest, col(C_KN, kv, SWA_HD), pltpu.roll(ck_ref[j, kv], WINDOW - 1, axis=1))
            vt = jnp.where(newest, col(C_VN, kv, SWA_HD), pltpu.roll(cv_ref[j, kv], WINDOW - 1, axis=1))
            cko_ref[j, kv] = kt
            cvo_ref[j, kv] = vt
            hs = slice(kv * SWA_GROUP, (kv + 1) * SWA_GROUP)
            s = jnp.dot(q_ref[j, hs, :].astype(bf16), kt.astype(bf16), preferred_element_type=f32)
            sink = sink_ref[hs, :]
            m = jnp.maximum(jnp.max(s, axis=-1, keepdims=True), sink)
            e = jnp.exp(s - m)
            den = jnp.sum(e, axis=-1, keepdims=True) + jnp.exp(sink - m)
            o = lax.dot_general(e.astype(bf16), vt.astype(bf16), (((1,), (1,)), ((), ())),
                                preferred_element_type=f32)
            os_ref[j, hs, :] = o / den


def _sample_state_call(cols_t, v, q, sinks_col, state, ck, cv):
    n = state.shape[0]
    g = SAMPLE_GROUP
    blk = lambda *tail: pl.BlockSpec((g,) + tail, lambda i: (i,) + (0,) * len(tail))
    cache = blk(SWA_KV, SWA_HD, WINDOW)
    return pl.pallas_call(
        _sample_state_kernel,
        grid=(n // g,),
        in_specs=[
            pl.BlockSpec((None, N_COLS, g), lambda i: (i, 0, 0)),
            pl.BlockSpec((None, g, GLA_W), lambda i: (i, 0, 0)),
            blk(SWA_HEADS, SWA_HD),
            pl.BlockSpec((SWA_HEADS, 1), lambda i: (0, 0)),
            blk(GLA_KW, GLA_DV), cache, cache,
        ],
        out_specs=[blk(GLA_KW, GLA_DV), cache, cache, blk(GLA_HEADS, GLA_DV), blk(SWA_HEADS, SWA_HD)],
        out_shape=[
            jax.ShapeDtypeStruct(state.shape, f32), jax.ShapeDtypeStruct(ck.shape, f32),
            jax.ShapeDtypeStruct(cv.shape, f32),
            jax.ShapeDtypeStruct((n, GLA_HEADS, GLA_DV), f32), jax.ShapeDtypeStruct((n, SWA_HEADS, SWA_HD), f32),
        ],
        compiler_params=pltpu.CompilerParams(dimension_semantics=("arbitrary",), vmem_limit_bytes=VMEM_LIMIT),
        name="sample_state",
    )(cols_t, v, q, sinks_col, state, ck, cv)


def _sample_merge_kernel(x_ref, og_ref, gg_ref, os_ref, sg_ref, gn_ref, wout_ref, nf_ref, y_ref):
    y_ref[...] = _merge(x_ref[...], og_ref[...], gg_ref[...], os_ref[...], sg_ref[...],
                        gn_ref, wout_ref, nf_ref)


def _sample_merge_call(xs, og, gg, osw, sg, w):
    return pl.pallas_call(
        _sample_merge_kernel,
        out_shape=jax.ShapeDtypeStruct(xs.shape, f32),
        compiler_params=pltpu.CompilerParams(vmem_limit_bytes=VMEM_LIMIT),
        name="sample_merge",
    )(xs, og, gg, osw, sg, w["gla_norm"], w["w_out"], w["norm_f"])


def _sample_path(x_sample, state, ck_t, cv_t, w):
    n = x_sample.shape[0]
    g = SAMPLE_GROUP
    xs = x_sample.reshape(n, D_MODEL)
    proj, cols_t = _sample_proj_call(xs, w)
    v = proj[:, O_V:O_V + GLA_W].reshape(n // g, g, GLA_W)
    q = proj[:, O_SQ:O_SQ + SWA_W].reshape(n, SWA_GROUP, SWA_KV, SWA_HD).swapaxes(1, 2)
    s_new, ck_new, cv_new, og, os_nat = _sample_state_call(
        cols_t, v, q.reshape(n, SWA_HEADS, SWA_HD), w["sinks"].reshape(SWA_HEADS, 1), state, ck_t, cv_t)
    os_il = os_nat.reshape(n, SWA_KV, SWA_GROUP, SWA_HD).swapaxes(1, 2).reshape(n, SWA_W)
    y = _sample_merge_call(xs, og.reshape(n, GLA_W), proj[:, O_GG:O_GG + GLA_W], os_il,
                           proj[:, O_SG:O_SG + SWA_W], w)
    return y, s_new, ck_new, cv_new


def _cache_view(c):
    return jnp.transpose(c[0], (0, 2, 3, 1))


def _cache_unview(c):
    return jnp.transpose(c, (0, 3, 1, 2))[None]


def kernel(x_prompt, x_sample, state_gla, cache_win_k, cache_win_v, norm_in, w_in, w_gate_up, b_gate,
           gla_norm, attn_sinks, w_out, norm_f):
    bsz = x_prompt.shape[0]
    n = x_sample.shape[0]
    w = _prep_weights(norm_in[0], w_in[0], w_gate_up[0], b_gate[0], gla_norm[0], attn_sinks[0],
                      w_out[0], norm_f)
    cmat, lmask = _chunk_tables()
    y_p, s_p, k_p, v_p = _prompt_call(x_prompt, w, cmat, lmask, _swa_bias())
    y_s, s_s, k_s, v_s = _sample_path(x_sample, state_gla[0].reshape(n, GLA_KW, GLA_DV),
                                      _cache_view(cache_win_k), _cache_view(cache_win_v), w)
    return (y_p, y_s.reshape(n, 1, D_MODEL),
            s_p.reshape(1, bsz, GLA_HEADS, GLA_DK, GLA_DV),
            _cache_unview(k_p.reshape(bsz, SWA_KV, SWA_HD, WINDOW)),
            _cache_unview(v_p.reshape(bsz, SWA_KV, SWA_HD, WINDOW)),
            s_s.reshape(1, n, GLA_HEADS, GLA_DK, GLA_DV),
            _cache_unview(k_s), _cache_unview(v_s))
```

```python
import functools

import numpy as np
import jax
import jax.numpy as jnp
from jax import lax
from jax.experimental import pallas as pl
from jax.experimental.pallas import tpu as pltpu

D_MODEL = 1024
GLA_HEADS = 4
GLA_DK = 64
GLA_DV = 128
GLA_KW = GLA_HEADS * GLA_DK
GLA_W = GLA_HEADS * GLA_DV
GLA_RANK = 16
GLA_TAU = 16.0
CHUNK = 64
SWA_HEADS = 8
SWA_HD = 64
SWA_KV = 2
SWA_GROUP = SWA_HEADS // SWA_KV
SWA_W = SWA_HEADS * SWA_HD
SWA_KVW = SWA_KV * SWA_HD
WINDOW = 128
EPS = 1e-6
NEG_INF = -1e30
LOG2E = 1.4426950408889634
LANES = 128

O_Q, O_K, O_V, O_GG = 0, 256, 512, 1024
O_SQ, O_SK, O_SV, O_SG, O_LOW = 1536, 2048, 2176, 2304, 2816
W_MAIN = 2816
W_ALL = W_MAIN + LANES

N_LEVELS = 6
TOK_BLOCK = 512
SAMPLE_GROUP = 8
C_Q, C_K, C_A, C_KN, C_VN = 0, 256, 512, 768, 896
N_COLS = 1024
VMEM_LIMIT = 48 * 1024 * 1024

f32 = jnp.float32
bf16 = jnp.bfloat16


def _chunk_tables():
    c = CHUNK
    t = np.arange(c)[None, :]
    i = np.arange(c)[:, None]
    blocks = [(t <= i), (t > i)]
    masks = []
    for l in range(N_LEVELS):
        h = c >> (l + 1)
        m = (i // (2 * h)) * (2 * h) + h
        upper = i >= m
        blocks.append(np.where(upper, (t > m) & (t <= i), (t > i) & (t <= m)))
        jj = np.arange(c)[None, :]
        masks.append((i // (2 * h) == jj // (2 * h)) & (i % (2 * h) >= h) & (jj % (2 * h) < h))
    masks.append(np.eye(c, dtype=bool))
    cm = np.concatenate(blocks, axis=0).astype(np.float32)
    cm3 = np.concatenate([cm, cm, cm], axis=1)
    lm = np.stack(masks).astype(np.float32)
    lm = np.concatenate([lm, lm], axis=1)
    return jnp.asarray(cm3, dtype=bf16), jnp.asarray(lm, dtype=f32)


def _swa_mask_tables():
    key = np.arange(2 * WINDOW)[:, None]
    qi = np.arange(WINDOW)[None, :]
    prev_ok = (key < WINDOW) & (key > qi)
    cur_ok = (key >= WINDOW) & (key - WINDOW <= qi)
    full = np.where(prev_ok | cur_ok, 0.0, NEG_INF)
    first = np.where(cur_ok, 0.0, NEG_INF)
    onehot = (np.arange(2 * WINDOW)[:, None] % WINDOW == qi).astype(np.float32)
    return jnp.asarray(np.stack([full, first]), dtype=bf16), jnp.asarray(onehot, dtype=bf16)


def _interleave_heads(a, axis):
    shp = a.shape
    a = a.reshape(shp[:axis] + (SWA_KV, SWA_GROUP, SWA_HD) + shp[axis + 1:])
    return jnp.swapaxes(a, axis, axis + 1).reshape(shp)


def _prep_weights(norm_in, w_in, w_gate_up, b_gate, gla_norm, attn_sinks, w_out, norm_f):
    sp = np.cumsum([0, 256, 256, 512, 512, 16, 512, 128, 128, 512])
    seg = [w_in[:, sp[i]:sp[i + 1]] for i in range(9)]
    gq, gk, gv, gg, glow, sq, sk, sv, sg = seg
    low = jnp.pad(glow, ((0, 0), (0, LANES - GLA_RANK)))
    w_all = jnp.concatenate(
        [gq * (GLA_DK ** -0.5), gk, gv, gg, _interleave_heads(sq, 1) * (SWA_HD ** -0.5 * LOG2E), sk, sv,
         _interleave_heads(sg, 1), low], axis=1).astype(bf16)
    w_up = jnp.pad(w_gate_up, ((0, LANES - GLA_RANK), (0, 0))).astype(bf16)
    w_out_p = jnp.concatenate([w_out[:GLA_W], _interleave_heads(w_out[GLA_W:], 0)], axis=0).astype(bf16)
    return dict(
        norm_in=norm_in.reshape(1, D_MODEL), w_all=w_all, w_up=w_up,
        b_gate=b_gate.reshape(1, GLA_KW), gla_norm=jnp.tile(gla_norm, GLA_HEADS).reshape(1, GLA_W),
        sinks=attn_sinks * LOG2E, w_out=w_out_p, norm_f=norm_f.reshape(1, D_MODEL))


def _rms(x, gain):
    return x * lax.rsqrt(jnp.mean(x * x, axis=-1, keepdims=True) + EPS) * gain


def _log_decay(glow, wup_ref, bg_ref):
    z = jnp.dot(glow.astype(bf16), wup_ref[...], preferred_element_type=f32) + bg_ref[...]
    return (jnp.minimum(z, 0.0) - jnp.log(1.0 + jnp.exp(-jnp.abs(z)))) * (1.0 / GLA_TAU)


def _silu(x):
    return x * jax.nn.sigmoid(x)


def _merge(x, o_gla, gg, o_swa, sg, gn_ref, wout_ref, nf_ref):
    parts = []
    for h in range(GLA_HEADS):
        sl = slice(h * GLA_DV, (h + 1) * GLA_DV)
        parts.append(_rms(o_gla[:, sl], gn_ref[:, sl]) * _silu(gg[:, sl]))
    parts.append(o_swa * _silu(sg))
    um = jnp.concatenate(parts, axis=1).astype(bf16)
    hres = x + jnp.dot(um, wout_ref[...], preferred_element_type=f32)
    return _rms(hres, nf_ref[...])


def _prompt_kernel(sinks_ref, x_ref, nin_ref, w_ref, wup_ref, bg_ref, cmat_ref, lmask_ref, kmask_ref, qhot_ref,
                   gn_ref, wout_ref, nf_ref,
                   y_ref, sp_ref, kn_ref, vn_ref,
                   s_ref, kprev_ref, vprev_ref, p_s, g_s, ogla_s, oswa_s, *, n_t):
    t = pl.program_id(1)
    tl = x_ref.shape[0]

    @pl.when(t == 0)
    def _():
        s_ref[...] = jnp.zeros_like(s_ref)
        kprev_ref[...] = jnp.zeros_like(kprev_ref)
        vprev_ref[...] = jnp.zeros_like(vprev_ref)

    x = x_ref[...]
    u = _rms(x, nin_ref[...]).astype(bf16)

    p_s[...] = jnp.dot(u, w_ref[...], preferred_element_type=f32)
    g_s[...] = _log_decay(p_s[:, O_LOW:O_LOW + LANES], wup_ref, bg_ref)

    lane_lo = lax.broadcasted_iota(jnp.int32, (CHUNK, LANES), 1) < GLA_DK

    n_pairs = GLA_HEADS // 2

    def chunk_terms(c):
        rows = slice(c * CHUNK, (c + 1) * CHUNK)
        qc, kc, gc = p_s[rows, O_Q:O_Q + GLA_KW], p_s[rows, O_K:O_K + GLA_KW], g_s[rows, :]
        vb = p_s[rows, O_V:O_V + GLA_W].astype(bf16)
        g_hi = gc.astype(bf16)
        r1 = gc - g_hi.astype(f32)
        g_mid = r1.astype(bf16)
        g_lo = (r1 - g_mid.astype(f32)).astype(bf16)
        sums = jnp.dot(cmat_ref[...], jnp.concatenate([g_hi, g_mid, g_lo], axis=0),
                       preferred_element_type=f32)
        e_b = jnp.exp(sums[0:CHUNK])
        qb = qc * e_b
        k_suf = kc * jnp.exp(sums[CHUNK:2 * CHUNK])
        terms = []
        for p in range(n_pairs):
            ln = slice(p * LANES, (p + 1) * LANES)
            qp, kp = qc[:, ln], kc[:, ln]
            a = jnp.zeros((2 * CHUNK, CHUNK), f32)
            for l in range(N_LEVELS + 1):
                if l < N_LEVELS:
                    e = jnp.exp(sums[(2 + l) * CHUNK:(3 + l) * CHUNK, ln])
                    qe, ke = (qp * e).astype(bf16), (kp * e).astype(bf16)
                else:
                    qe, ke = qp.astype(bf16), kp.astype(bf16)
                lhs = jnp.concatenate([jnp.where(lane_lo, qe, 0), jnp.where(lane_lo, 0, qe)], axis=0)
                sc = lax.dot_general(lhs, ke, (((1,), (1,)), ((), ())), preferred_element_type=f32)
                a = a + sc * lmask_ref[l]
            ab = a.astype(bf16)
            qbp = qb[:, ln].astype(bf16)
            lhs_heads = []
            for hh in range(2):
                qbm = jnp.where(lane_lo, qbp, 0) if hh == 0 else jnp.where(lane_lo, 0, qbp)
                lhs_heads.append(jnp.concatenate([qbm, ab[hh * CHUNK:(hh + 1) * CHUNK, :]], axis=1))
            upd = lax.dot_general(k_suf[:, ln].astype(bf16), vb[:, p * 2 * GLA_DV:(p + 1) * 2 * GLA_DV],
                                  (((0,), (0,)), ((), ())), preferred_element_type=f32)
            upd = jnp.concatenate([upd[0:GLA_DK, 0:GLA_DV], upd[GLA_DK:2 * GLA_DK, GLA_DV:2 * GLA_DV]], axis=0)
            e_col = jnp.broadcast_to(e_b[CHUNK - 1:CHUNK, ln], (LANES, LANES)).T
            terms.append((lhs_heads, vb, upd, e_col))
        return terms

    all_terms = [chunk_terms(c) for c in range(tl // CHUNK)]
    for p in range(n_pairs):
        ln = slice(p * LANES, (p + 1) * LANES)
        s_pair = s_ref[ln, :]
        for c in range(tl // CHUNK):
            lhs_heads, vb, upd, e_col = all_terms[c][p]
            s_pair_b = s_pair.astype(bf16)
            for hh in range(2):
                h = 2 * p + hh
                rhs = jnp.concatenate([s_pair_b, vb[:, h * GLA_DV:(h + 1) * GLA_DV]], axis=0)
                ogla_s[c * CHUNK:(c + 1) * CHUNK, h * GLA_DV:(h + 1) * GLA_DV] = jnp.dot(
                    lhs_heads[hh], rhs, preferred_element_type=f32)
            s_pair = e_col * s_pair + upd
        s_ref[ln, :] = s_pair

    lane_lo_w = lax.broadcasted_iota(jnp.int32, (WINDOW, LANES), 1) < SWA_HD
    row_lo = lax.broadcasted_iota(jnp.int32, (2 * WINDOW, 1), 0) < WINDOW
    for blk in range(tl // WINDOW):
        rs = slice(blk * WINDOW, (blk + 1) * WINDOW)
        sq = p_s[rs, O_SQ:O_SQ + SWA_W].astype(bf16)
        k_cur, v_cur = p_s[rs, O_SK:O_SK + SWA_KVW], p_s[rs, O_SV:O_SV + SWA_KVW]
        k2 = jnp.concatenate([kprev_ref[...], k_cur], axis=0).astype(bf16)
        v2 = jnp.concatenate([vprev_ref[...], v_cur], axis=0).astype(bf16)
        kmask = kmask_ref[jnp.where(t > 0, 0, 1)] if blk == 0 else kmask_ref[0]
        k2m = jnp.concatenate([k2, kmask], axis=1)
        for tt in range(SWA_GROUP):
            qt = sq[:, tt * LANES:(tt + 1) * LANES]
            lhs = jnp.concatenate([jnp.where(lane_lo_w, qt, 0), jnp.where(lane_lo_w, 0, qt)], axis=0)
            lhs = jnp.concatenate([lhs, qhot_ref[...]], axis=1)
            s = lax.dot_general(lhs, k2m, (((1,), (1,)), ((), ())), preferred_element_type=f32)
            sink = jnp.where(row_lo, sinks_ref[tt], sinks_ref[SWA_GROUP + tt])
            m = jnp.maximum(jnp.max(s, axis=-1, keepdims=True), sink)
            e = jnp.exp2(s - m)
            r = 1.0 / (jnp.sum(e, axis=-1, keepdims=True) + jnp.exp2(sink - m))
            o2 = jnp.dot(e.astype(bf16), v2, preferred_element_type=f32)
            oswa_s[rs, tt * LANES:(tt + 1) * LANES] = jnp.where(
                lane_lo_w, o2[:WINDOW] * r[:WINDOW], o2[WINDOW:] * r[WINDOW:])
        kprev_ref[...] = k_cur
        vprev_ref[...] = v_cur

    y_ref[...] = _merge(x, ogla_s[...], p_s[:, O_GG:O_GG + GLA_W], oswa_s[...], p_s[:, O_SG:O_SG + SWA_W],
                        gn_ref, wout_ref, nf_ref)

    @pl.when(t == n_t - 1)
    def _():
        sp_ref[...] = s_ref[...]
        kn_ref[...] = kprev_ref[...].T
        vn_ref[...] = vprev_ref[...].T


def _prompt_call(x, w, cmat, lmask, kmask, qhot):
    bsz, seq, _ = x.shape
    tl = TOK_BLOCK
    n_t = seq // tl
    const = lambda shape: pl.BlockSpec(shape, lambda b, t: (0,) * len(shape))
    return pl.pallas_call(
        functools.partial(_prompt_kernel, n_t=n_t),
        grid=(bsz, n_t),
        in_specs=[
            pl.BlockSpec(memory_space=pltpu.SMEM),
            pl.BlockSpec((None, tl, D_MODEL), lambda b, t: (b, t, 0)),
            const((1, D_MODEL)), const((D_MODEL, W_ALL)), const((LANES, GLA_KW)), const((1, GLA_KW)),
            const(cmat.shape), const(lmask.shape), const(kmask.shape), const(qhot.shape),
            const((1, GLA_W)), const((D_MODEL, D_MODEL)), const((1, D_MODEL)),
        ],
        out_specs=[
            pl.BlockSpec((None, tl, D_MODEL), lambda b, t: (b, t, 0)),
            pl.BlockSpec((None, GLA_KW, GLA_DV), lambda b, t: (b, 0, 0)),
            pl.BlockSpec((None, WINDOW, SWA_KVW), lambda b, t: (b, 0, 0)),
            pl.BlockSpec((None, WINDOW, SWA_KVW), lambda b, t: (b, 0, 0)),
        ],
        out_shape=[
            jax.ShapeDtypeStruct((bsz, seq, D_MODEL), f32),
            jax.ShapeDtypeStruct((bsz, GLA_KW, GLA_DV), f32),
            jax.ShapeDtypeStruct((bsz, WINDOW, SWA_KVW), f32),
            jax.ShapeDtypeStruct((bsz, WINDOW, SWA_KVW), f32),
        ],
        scratch_shapes=[
            pltpu.VMEM((GLA_KW, GLA_DV), f32),
            pltpu.VMEM((WINDOW, SWA_KVW), f32), pltpu.VMEM((WINDOW, SWA_KVW), f32),
            pltpu.VMEM((tl, W_ALL), f32), pltpu.VMEM((tl, GLA_KW), f32),
            pltpu.VMEM((tl, GLA_W), f32), pltpu.VMEM((tl, SWA_W), f32),
        ],
        compiler_params=pltpu.CompilerParams(
            dimension_semantics=("arbitrary", "arbitrary"), vmem_limit_bytes=VMEM_LIMIT),
        name="prompt_layer",
    )(w["sinks"], x, w["norm_in"], w["w_all"], w["w_up"], w["b_gate"], cmat, lmask, kmask, qhot,
      w["gla_norm"], w["w_out"], w["norm_f"])


def _sample_proj_kernel(x_ref, nin_ref, w_ref, wup_ref, bg_ref, proj_ref, cols_ref):
    u = _rms(x_ref[...], nin_ref[...]).astype(bf16)
    proj = jnp.dot(u, w_ref[:, :W_MAIN], preferred_element_type=f32)
    proj_ref[...] = proj
    glow = jnp.dot(u, w_ref[:, O_LOW:O_LOW + LANES], preferred_element_type=f32)
    decay = jnp.exp(_log_decay(glow, wup_ref, bg_ref))
    cols_t = jnp.concatenate([proj[:, O_Q:O_V], decay, proj[:, O_SK:O_SG]], axis=1).T
    for i in range(cols_ref.shape[0]):
        cols_ref[i] = cols_t[:, i * SAMPLE_GROUP:(i + 1) * SAMPLE_GROUP]


def _sample_proj_call(xs, w):
    n = xs.shape[0]
    return pl.pallas_call(
        _sample_proj_kernel,
        out_shape=[jax.ShapeDtypeStruct((n, W_MAIN), f32),
                   jax.ShapeDtypeStruct((n // SAMPLE_GROUP, N_COLS, SAMPLE_GROUP), f32)],
        compiler_params=pltpu.CompilerParams(vmem_limit_bytes=VMEM_LIMIT),
        name="sample_proj",
    )(xs, w["norm_in"], w["w_all"], w["w_up"], w["b_gate"])


def _sample_state_kernel(cols_ref, v_ref, q_ref, sink_ref, s_ref, ck_ref, cv_ref,
                         so_ref, cko_ref, cvo_ref, og_ref, os_ref):
    newest = lax.broadcasted_iota(jnp.int32, (SWA_HD, WINDOW), 1) == WINDOW - 1
    for j in range(SAMPLE_GROUP):
        col = lambda base, h, width: cols_ref[base + h * width:base + (h + 1) * width, j:j + 1]
        for h in range(GLA_HEADS):
            rs = slice(h * GLA_DK, (h + 1) * GLA_DK)
            v_row = v_ref[j:j + 1, h * GLA_DV:(h + 1) * GLA_DV]
            s_new = col(C_A, h, GLA_DK) * s_ref[j, rs, :] + col(C_K, h, GLA_DK) * v_row
            so_ref[j, rs, :] = s_new
            og_ref[j, h:h + 1, :] = jnp.sum(col(C_Q, h, GLA_DK) * s_new, axis=0, keepdims=True)
        for kv in range(SWA_KV):
            kt = jnp.where(newest, col(C_KN, kv, SWA_HD), pltpu.roll(ck_ref[j, kv], WINDOW - 1, axis=1))
            vt = jnp.where(newest, col(C_VN, kv, SWA_HD), pltpu.roll(cv_ref[j, kv], WINDOW - 1, axis=1))
            cko_ref[j, kv] = kt
            cvo_ref[j, kv] = vt
            hs = slice(kv * SWA_GROUP, (kv + 1) * SWA_GROUP)
            s = jnp.dot(q_ref[j, hs, :].astype(bf16), kt.astype(bf16), preferred_element_type=f32)
            sink = sink_ref[hs, :]
            m = jnp.maximum(jnp.max(s, axis=-1, keepdims=True), sink)
            e = jnp.exp2(s - m)
            den = jnp.sum(e, axis=-1, keepdims=True) + jnp.exp2(sink - m)
            o = lax.dot_general(e.astype(bf16), vt.astype(bf16), (((1,), (1,)), ((), ())),
                                preferred_element_type=f32)
            os_ref[j, hs, :] = o / den


def _sample_state_call(cols_t, v, q, sinks_col, state, ck, cv):
    n = state.shape[0]
    g = SAMPLE_GROUP
    blk = lambda *tail: pl.BlockSpec((g,) + tail, lambda i: (i,) + (0,) * len(tail))
    cache = blk(SWA_KV, SWA_HD, WINDOW)
    return pl.pallas_call(
        _sample_state_kernel,
        grid=(n // g,),
        in_specs=[
            pl.BlockSpec((None, N_COLS, g), lambda i: (i, 0, 0)),
            pl.BlockSpec((None, g, GLA_W), lambda i: (i, 0, 0)),
            blk(SWA_HEADS, SWA_HD),
            pl.BlockSpec((SWA_HEADS, 1), lambda i: (0, 0)),
            blk(GLA_KW, GLA_DV), cache, cache,
        ],
        out_specs=[blk(GLA_KW, GLA_DV), cache, cache, blk(GLA_HEADS, GLA_DV), blk(SWA_HEADS, SWA_HD)],
        out_shape=[
            jax.ShapeDtypeStruct(state.shape, f32), jax.ShapeDtypeStruct(ck.shape, f32),
            jax.ShapeDtypeStruct(cv.shape, f32),
            jax.ShapeDtypeStruct((n, GLA_HEADS, GLA_DV), f32), jax.ShapeDtypeStruct((n, SWA_HEADS, SWA_HD), f32),
        ],
        compiler_params=pltpu.CompilerParams(dimension_semantics=("arbitrary",), vmem_limit_bytes=VMEM_LIMIT),
        name="sample_state",
    )(cols_t, v, q, sinks_col, state, ck, cv)


def _sample_merge_kernel(x_ref, og_ref, gg_ref, os_ref, sg_ref, gn_ref, wout_ref, nf_ref, y_ref):
    y_ref[...] = _merge(x_ref[...], og_ref[...], gg_ref[...], os_ref[...], sg_ref[...],
                        gn_ref, wout_ref, nf_ref)


def _sample_merge_call(xs, og, gg, osw, sg, w):
    return pl.pallas_call(
        _sample_merge_kernel,
        out_shape=jax.ShapeDtypeStruct(xs.shape, f32),
        compiler_params=pltpu.CompilerParams(vmem_limit_bytes=VMEM_LIMIT),
        name="sample_merge",
    )(xs, og, gg, osw, sg, w["gla_norm"], w["w_out"], w["norm_f"])


def _sample_path(x_sample, state, ck_t, cv_t, w):
    n = x_sample.shape[0]
    g = SAMPLE_GROUP
    xs = x_sample.reshape(n, D_MODEL)
    proj, cols_t = _sample_proj_call(xs, w)
    v = proj[:, O_V:O_V + GLA_W].reshape(n // g, g, GLA_W)
    q = proj[:, O_SQ:O_SQ + SWA_W].reshape(n, SWA_GROUP, SWA_KV, SWA_HD).swapaxes(1, 2)
    s_new, ck_new, cv_new, og, os_nat = _sample_state_call(
        cols_t, v, q.reshape(n, SWA_HEADS, SWA_HD), w["sinks"].reshape(SWA_HEADS, 1), state, ck_t, cv_t)
    os_il = os_nat.reshape(n, SWA_KV, SWA_GROUP, SWA_HD).swapaxes(1, 2).reshape(n, SWA_W)
    y = _sample_merge_call(xs, og.reshape(n, GLA_W), proj[:, O_GG:O_GG + GLA_W], os_il,
                           proj[:, O_SG:O_SG + SWA_W], w)
    return y, s_new, ck_new, cv_new


def _cache_view(c):
    return jnp.transpose(c[0], (0, 2, 3, 1))


def _cache_unview(c):
    return jnp.transpose(c, (0, 3, 1, 2))[None]


def kernel(x_prompt, x_sample, state_gla, cache_win_k, cache_win_v, norm_in, w_in, w_gate_up, b_gate,
           gla_norm, attn_sinks, w_out, norm_f):
    bsz = x_prompt.shape[0]
    n = x_sample.shape[0]
    w = _prep_weights(norm_in[0], w_in[0], w_gate_up[0], b_gate[0], gla_norm[0], attn_sinks[0],
                      w_out[0], norm_f)
    cmat, lmask = _chunk_tables()
    y_p, s_p, k_p, v_p = _prompt_call(x_prompt, w, cmat, lmask, *_swa_mask_tables())
    y_s, s_s, k_s, v_s = _sample_path(x_sample, state_gla[0].reshape(n, GLA_KW, GLA_DV),
                                      _cache_view(cache_win_k), _cache_view(cache_win_v), w)
    return (y_p, y_s.reshape(n, 1, D_MODEL),
            s_p.reshape(1, bsz, GLA_HEADS, GLA_DK, GLA_DV),
            _cache_unview(k_p.reshape(bsz, SWA_KV, SWA_HD, WINDOW)),
            _cache_unview(v_p.reshape(bsz, SWA_KV, SWA_HD, WINDOW)),
            s_s.reshape(1, n, GLA_HEADS, GLA_DK, GLA_DV),
            _cache_unview(k_s), _cache_unview(v_s))
```

```python
import functools

import numpy as np
import jax
import jax.numpy as jnp
from jax import lax
from jax.experimental import pallas as pl
from jax.experimental.pallas import tpu as pltpu

D_MODEL = 1024
GLA_HEADS = 4
GLA_DK = 64
GLA_DV = 128
GLA_KW = GLA_HEADS * GLA_DK
GLA_W = GLA_HEADS * GLA_DV
GLA_RANK = 16
GLA_TAU = 16.0
CHUNK = 64
SWA_HEADS = 8
SWA_HD = 64
SWA_KV = 2
SWA_GROUP = SWA_HEADS // SWA_KV
SWA_W = SWA_HEADS * SWA_HD
SWA_KVW = SWA_KV * SWA_HD
WINDOW = 128
EPS = 1e-6
NEG_INF = -1e30
LOG2E = 1.4426950408889634
LANES = 128

O_Q, O_K, O_V, O_GG = 0, 256, 512, 1024
O_SQ, O_SK, O_SV, O_SG, O_LOW = 1536, 2048, 2176, 2304, 2816
W_MAIN = 2816
W_ALL = W_MAIN + LANES

N_LEVELS = 6
TOK_BLOCK = 512
ROW_SPLIT = 2
SAMPLE_GROUP = 8
PACK = 16
VMEM_LIMIT = 48 * 1024 * 1024

f32 = jnp.float32
bf16 = jnp.bfloat16


def _chunk_tables():
    c = CHUNK
    t = np.arange(c)[None, :]
    i = np.arange(c)[:, None]
    blocks = [(t <= i), (t > i)]
    masks = []
    for l in range(N_LEVELS):
        h = c >> (l + 1)
        m = (i // (2 * h)) * (2 * h) + h
        upper = i >= m
        blocks.append(np.where(upper, (t > m) & (t <= i), (t > i) & (t <= m)))
        jj = np.arange(c)[None, :]
        masks.append((i // (2 * h) == jj // (2 * h)) & (i % (2 * h) >= h) & (jj % (2 * h) < h))
    masks.append(np.eye(c, dtype=bool))
    cm = np.concatenate(blocks, axis=0).astype(np.float32)
    cm3 = np.concatenate([cm, cm, cm], axis=1)
    lm = np.stack(masks).astype(np.float32)
    lm = np.concatenate([lm, lm], axis=1)
    return jnp.asarray(cm3, dtype=bf16), jnp.asarray(lm, dtype=f32)


def _swa_mask_tables():
    key = np.arange(2 * WINDOW)[:, None]
    qi = np.arange(WINDOW)[None, :]
    prev_ok = (key < WINDOW) & (key > qi)
    cur_ok = (key >= WINDOW) & (key - WINDOW <= qi)
    full = np.where(prev_ok | cur_ok, 0.0, NEG_INF)
    first = np.where(cur_ok, 0.0, NEG_INF)
    onehot = (np.arange(SWA_HEADS * WINDOW)[:, None] % WINDOW == qi).astype(np.float32)
    return jnp.asarray(np.stack([full, first]), dtype=bf16), jnp.asarray(onehot, dtype=bf16)


def _interleave_heads(a, axis):
    shp = a.shape
    a = a.reshape(shp[:axis] + (SWA_KV, SWA_GROUP, SWA_HD) + shp[axis + 1:])
    return jnp.swapaxes(a, axis, axis + 1).reshape(shp)


def _prep_weights(norm_in, w_in, w_gate_up, b_gate, gla_norm, attn_sinks, w_out, norm_f):
    sp = np.cumsum([0, 256, 256, 512, 512, 16, 512, 128, 128, 512])
    seg = [w_in[:, sp[i]:sp[i + 1]] for i in range(9)]
    gq, gk, gv, gg, glow, sq, sk, sv, sg = seg
    low = jnp.pad(glow, ((0, 0), (0, LANES - GLA_RANK)))
    w_all = jnp.concatenate(
        [gq * (GLA_DK ** -0.5), gk, gv, gg, _interleave_heads(sq, 1) * (SWA_HD ** -0.5 * LOG2E), sk, sv,
         _interleave_heads(sg, 1), low], axis=1).astype(bf16)
    w_up = jnp.pad(w_gate_up, ((0, LANES - GLA_RANK), (0, 0))).astype(bf16)
    w_out_p = jnp.concatenate([w_out[:GLA_W], _interleave_heads(w_out[GLA_W:], 0)], axis=0).astype(bf16)
    return dict(
        norm_in=norm_in.reshape(1, D_MODEL), w_all=w_all, w_up=w_up,
        b_gate=b_gate.reshape(1, GLA_KW), gla_norm=jnp.tile(gla_norm, GLA_HEADS).reshape(1, GLA_W),
        sinks=attn_sinks * LOG2E, w_out=w_out_p, norm_f=norm_f.reshape(1, D_MODEL))


def _rms(x, gain):
    return x * lax.rsqrt(jnp.mean(x * x, axis=-1, keepdims=True) + EPS) * gain


def _log_decay(glow, wup_ref, bg_ref):
    z = jnp.dot(glow.astype(bf16), wup_ref[...], preferred_element_type=f32) + bg_ref[...]
    return (jnp.minimum(z, 0.0) - jnp.log(1.0 + jnp.exp(-jnp.abs(z)))) * (1.0 / GLA_TAU)


def _silu(x):
    return x * jax.nn.sigmoid(x)


def _merge(x, o_gla, gg, o_swa, sg, gn_ref, wout_ref, nf_ref):
    parts = []
    for h in range(GLA_HEADS):
        sl = slice(h * GLA_DV, (h + 1) * GLA_DV)
        parts.append(_rms(o_gla[:, sl], gn_ref[:, sl]) * _silu(gg[:, sl]))
    parts.append(o_swa * _silu(sg))
    um = jnp.concatenate(parts, axis=1).astype(bf16)
    hres = x + jnp.dot(um, wout_ref[...], preferred_element_type=f32)
    return _rms(hres, nf_ref[...])


def _prompt_kernel(sinks_ref, x_ref, nin_ref, w_ref, wup_ref, bg_ref, cmat_ref, lmask_ref, kmask_ref, qhot_ref,
                   gn_ref, wout_ref, nf_ref,
                   y_ref, sp_ref, kn_ref, vn_ref,
                   s_ref, kprev_ref, vprev_ref, p_s, g_s, ogla_s, oswa_s, *, n_t):
    t = pl.program_id(1)
    tl = x_ref.shape[0]

    @pl.when(t == 0)
    def _():
        s_ref[...] = jnp.zeros_like(s_ref)
        kprev_ref[...] = jnp.zeros_like(kprev_ref)
        vprev_ref[...] = jnp.zeros_like(vprev_ref)

    row_groups = [slice(i * tl // ROW_SPLIT, (i + 1) * tl // ROW_SPLIT) for i in range(ROW_SPLIT)]
    for rg in row_groups:
        u = _rms(x_ref[rg, :], nin_ref[...]).astype(bf16)
        p_s[rg, :] = jnp.dot(u, w_ref[...], preferred_element_type=f32)
        g_s[rg, :] = _log_decay(p_s[rg, O_LOW:O_LOW + LANES], wup_ref, bg_ref)

    lane_lo = lax.broadcasted_iota(jnp.int32, (CHUNK, LANES), 1) < GLA_DK

    n_pairs = GLA_HEADS // 2

    def chunk_terms(c):
        rows = slice(c * CHUNK, (c + 1) * CHUNK)
        qc, kc, gc = p_s[rows, O_Q:O_Q + GLA_KW], p_s[rows, O_K:O_K + GLA_KW], g_s[rows, :]
        vb = p_s[rows, O_V:O_V + GLA_W].astype(bf16)
        g_hi = gc.astype(bf16)
        r1 = gc - g_hi.astype(f32)
        g_mid = r1.astype(bf16)
        g_lo = (r1 - g_mid.astype(f32)).astype(bf16)
        sums = jnp.dot(cmat_ref[...], jnp.concatenate([g_hi, g_mid, g_lo], axis=0),
                       preferred_element_type=f32)
        e_b = jnp.exp(sums[0:CHUNK])
        qb = qc * e_b
        k_suf = kc * jnp.exp(sums[CHUNK:2 * CHUNK])
        terms = []
        for p in range(n_pairs):
            ln = slice(p * LANES, (p + 1) * LANES)
            qp, kp = qc[:, ln], kc[:, ln]
            a = jnp.zeros((2 * CHUNK, CHUNK), f32)
            for l in range(N_LEVELS + 1):
                if l < N_LEVELS:
                    e = jnp.exp(sums[(2 + l) * CHUNK:(3 + l) * CHUNK, ln])
                    qe, ke = (qp * e).astype(bf16), (kp * e).astype(bf16)
                else:
                    qe, ke = qp.astype(bf16), kp.astype(bf16)
                lhs = jnp.concatenate([jnp.where(lane_lo, qe, 0), jnp.where(lane_lo, 0, qe)], axis=0)
                sc = lax.dot_general(lhs, ke, (((1,), (1,)), ((), ())), preferred_element_type=f32)
                a = a + sc * lmask_ref[l]
            ab = a.astype(bf16)
            qbp = qb[:, ln].astype(bf16)
            lhs_heads = []
            for hh in range(2):
                qbm = jnp.where(lane_lo, qbp, 0) if hh == 0 else jnp.where(lane_lo, 0, qbp)
                lhs_heads.append(jnp.concatenate([qbm, ab[hh * CHUNK:(hh + 1) * CHUNK, :]], axis=1))
            upd = lax.dot_general(k_suf[:, ln].astype(bf16), vb[:, p * 2 * GLA_DV:(p + 1) * 2 * GLA_DV],
                                  (((0,), (0,)), ((), ())), preferred_element_type=f32)
            upd = jnp.concatenate([upd[0:GLA_DK, 0:GLA_DV], upd[GLA_DK:2 * GLA_DK, GLA_DV:2 * GLA_DV]], axis=0)
            e_col = jnp.broadcast_to(e_b[CHUNK - 1:CHUNK, ln], (LANES, LANES)).T
            terms.append((lhs_heads, vb, upd, e_col))
        return terms

    lane_lo_w = lax.broadcasted_iota(jnp.int32, (WINDOW, LANES), 1) < SWA_HD
    row_lo = lax.broadcasted_iota(jnp.int32, (2 * WINDOW, 1), 0) < WINDOW

    def swa_block(blk, k_prev, v_prev):
        rs = slice(blk * WINDOW, (blk + 1) * WINDOW)
        sq = p_s[rs, O_SQ:O_SQ + SWA_W].astype(bf16)
        k_cur, v_cur = p_s[rs, O_SK:O_SK + SWA_KVW], p_s[rs, O_SV:O_SV + SWA_KVW]
        k2 = jnp.concatenate([k_prev, k_cur], axis=0).astype(bf16)
        v2 = jnp.concatenate([v_prev, v_cur], axis=0).astype(bf16)
        kmask = kmask_ref[jnp.where(t > 0, 0, 1)] if blk == 0 else kmask_ref[0]
        k2m = jnp.concatenate([k2, kmask], axis=1)
        lhs = []
        for tt in range(SWA_GROUP):
            qt = sq[:, tt * LANES:(tt + 1) * LANES]
            lhs += [jnp.where(lane_lo_w, qt, 0), jnp.where(lane_lo_w, 0, qt)]
        lhs = jnp.concatenate([jnp.concatenate(lhs, axis=0), qhot_ref[...]], axis=1)
        s_all = lax.dot_general(lhs, k2m, (((1,), (1,)), ((), ())), preferred_element_type=f32)
        es, rs_inv = [], []
        for tt in range(SWA_GROUP):
            s = s_all[tt * 2 * WINDOW:(tt + 1) * 2 * WINDOW]
            sink = jnp.where(row_lo, sinks_ref[tt], sinks_ref[SWA_GROUP + tt])
            m = jnp.maximum(jnp.max(s, axis=-1, keepdims=True), sink)
            e = jnp.exp2(s - m)
            rs_inv.append(1.0 / (jnp.sum(e, axis=-1, keepdims=True) + jnp.exp2(sink - m)))
            es.append(e.astype(bf16))
        o_all = jnp.dot(jnp.concatenate(es, axis=0), v2, preferred_element_type=f32)
        for tt in range(SWA_GROUP):
            o2 = o_all[tt * 2 * WINDOW:(tt + 1) * 2 * WINDOW] * rs_inv[tt]
            oswa_s[rs, tt * LANES:(tt + 1) * LANES] = jnp.where(lane_lo_w, o2[:WINDOW], o2[WINDOW:])
        return k_cur, v_cur

    all_terms = []
    k_prev, v_prev = kprev_ref[...], vprev_ref[...]
    for blk in range(tl // WINDOW):
        for c in range(blk * WINDOW // CHUNK, (blk + 1) * WINDOW // CHUNK):
            all_terms.append(chunk_terms(c))
        k_prev, v_prev = swa_block(blk, k_prev, v_prev)
    kprev_ref[...] = k_prev
    vprev_ref[...] = v_prev

    for p in range(n_pairs):
        ln = slice(p * LANES, (p + 1) * LANES)
        s_pair = s_ref[ln, :]
        for c in range(tl // CHUNK):
            lhs_heads, vb, upd, e_col = all_terms[c][p]
            s_pair_b = s_pair.astype(bf16)
            for hh in range(2):
                h = 2 * p + hh
                rhs = jnp.concatenate([s_pair_b, vb[:, h * GLA_DV:(h + 1) * GLA_DV]], axis=0)
                ogla_s[c * CHUNK:(c + 1) * CHUNK, h * GLA_DV:(h + 1) * GLA_DV] = jnp.dot(
                    lhs_heads[hh], rhs, preferred_element_type=f32)
            s_pair = e_col * s_pair + upd
        s_ref[ln, :] = s_pair

    for rg in row_groups:
        y_ref[rg, :] = _merge(x_ref[rg, :], ogla_s[rg, :], p_s[rg, O_GG:O_GG + GLA_W], oswa_s[rg, :],
                              p_s[rg, O_SG:O_SG + SWA_W], gn_ref, wout_ref, nf_ref)

    @pl.when(t == n_t - 1)
    def _():
        sp_ref[...] = s_ref[...]
        kn_ref[...] = kprev_ref[...].T
        vn_ref[...] = vprev_ref[...].T


def _prompt_call(x, w, cmat, lmask, kmask, qhot):
    bsz, seq, _ = x.shape
    tl = TOK_BLOCK
    n_t = seq // tl
    const = lambda shape: pl.BlockSpec(shape, lambda b, t: (0,) * len(shape))
    return pl.pallas_call(
        functools.partial(_prompt_kernel, n_t=n_t),
        grid=(bsz, n_t),
        in_specs=[
            pl.BlockSpec(memory_space=pltpu.SMEM),
            pl.BlockSpec((None, tl, D_MODEL), lambda b, t: (b, t, 0)),
            const((1, D_MODEL)), const((D_MODEL, W_ALL)), const((LANES, GLA_KW)), const((1, GLA_KW)),
            const(cmat.shape), const(lmask.shape), const(kmask.shape), const(qhot.shape),
            const((1, GLA_W)), const((D_MODEL, D_MODEL)), const((1, D_MODEL)),
        ],
        out_specs=[
            pl.BlockSpec((None, tl, D_MODEL), lambda b, t: (b, t, 0)),
            pl.BlockSpec((None, GLA_KW, GLA_DV), lambda b, t: (b, 0, 0)),
            pl.BlockSpec((None, WINDOW, SWA_KVW), lambda b, t: (b, 0, 0)),
            pl.BlockSpec((None, WINDOW, SWA_KVW), lambda b, t: (b, 0, 0)),
        ],
        out_shape=[
            jax.ShapeDtypeStruct((bsz, seq, D_MODEL), f32),
            jax.ShapeDtypeStruct((bsz, GLA_KW, GLA_DV), f32),
            jax.ShapeDtypeStruct((bsz, WINDOW, SWA_KVW), f32),
            jax.ShapeDtypeStruct((bsz, WINDOW, SWA_KVW), f32),
        ],
        scratch_shapes=[
            pltpu.VMEM((GLA_KW, GLA_DV), f32),
            pltpu.VMEM((WINDOW, SWA_KVW), f32), pltpu.VMEM((WINDOW, SWA_KVW), f32),
            pltpu.VMEM((tl, W_ALL), f32), pltpu.VMEM((tl, GLA_KW), f32),
            pltpu.VMEM((tl, GLA_W), f32), pltpu.VMEM((tl, SWA_W), f32),
        ],
        compiler_params=pltpu.CompilerParams(
            dimension_semantics=("arbitrary", "arbitrary"), vmem_limit_bytes=VMEM_LIMIT),
        name="prompt_layer",
    )(w["sinks"], x, w["norm_in"], w["w_all"], w["w_up"], w["b_gate"], cmat, lmask, kmask, qhot,
      w["gla_norm"], w["w_out"], w["norm_f"])


def _sample_proj_kernel(x_ref, nin_ref, w_ref, wup_ref, bg_ref, proj_ref, decay_ref):
    u = _rms(x_ref[...], nin_ref[...]).astype(bf16)
    proj_ref[...] = jnp.dot(u, w_ref[:, :W_MAIN], preferred_element_type=f32)
    glow = jnp.dot(u, w_ref[:, O_LOW:O_LOW + LANES], preferred_element_type=f32)
    decay_ref[...] = jnp.exp(_log_decay(glow, wup_ref, bg_ref))


def _sample_proj_call(xs, w):
    n = xs.shape[0]
    return pl.pallas_call(
        _sample_proj_kernel,
        out_shape=[jax.ShapeDtypeStruct((n, W_MAIN), f32), jax.ShapeDtypeStruct((n, GLA_KW), f32)],
        compiler_params=pltpu.CompilerParams(vmem_limit_bytes=VMEM_LIMIT),
        name="sample_proj",
    )(xs, w["norm_in"], w["w_all"], w["w_up"], w["b_gate"])


def _split3(x):
    as_bf16 = lambda v: v.astype(bf16).astype(f32)
    hi = as_bf16(x)
    mid = as_bf16(x - hi)
    return hi, mid, as_bf16(x - hi - mid)


def _sample_state_kernel(proj_ref, decay_ref, sink_ref, s_ref, ck_ref, cv_ref,
                         so_ref, cko_ref, cvo_ref, og_ref, os_ref):
    row = lax.broadcasted_iota(jnp.int32, (PACK, GLA_KW), 0)
    head_of_lane = lax.broadcasted_iota(jnp.int32, (PACK, GLA_KW), 1) // GLA_DK
    own_head = head_of_lane == row
    row_v = lax.broadcasted_iota(jnp.int32, (PACK, GLA_DV), 0)
    lane_v = lax.broadcasted_iota(jnp.int32, (PACK, GLA_DV), 1)
    piece_rows = ((row_v >= GLA_HEADS) & (row_v < GLA_HEADS + 3)).astype(f32)
    last_lane_rows = ((row_v < 3) & (lane_v == WINDOW - 1)).astype(bf16)
    newest = lax.broadcasted_iota(jnp.int32, (SWA_KVW, WINDOW), 1) == WINDOW - 1
    row_q = lax.broadcasted_iota(jnp.int32, (SWA_HEADS, LANES), 0)
    own_kv = (lax.broadcasted_iota(jnp.int32, (SWA_HEADS, LANES), 1) // SWA_HD) == (row_q // SWA_GROUP)
    sink = sink_ref[...]
    contract_rows = (((0,), (0,)), ((), ()))
    seqs = range(SAMPLE_GROUP)
    lts, rts, qms, lt2s, q8s = [], [], [], [], []
    for j in seqs:
        pr = proj_ref[j:j + 1, :]
        bcast = lambda lo, width: jnp.broadcast_to(pr[:, lo:lo + width], (PACK, width))
        a_hi, a_mid, a_lo = _split3(jnp.broadcast_to(decay_ref[j:j + 1, :], (PACK, GLA_KW)))
        a_piece = jnp.where(row == GLA_HEADS, a_hi, jnp.where(row == GLA_HEADS + 1, a_mid, a_lo))
        lts.append(jnp.where(own_head, bcast(O_K, GLA_KW),
                             jnp.where((row >= GLA_HEADS) & (row < GLA_HEADS + 3), a_piece, 0.0)).astype(bf16))
        v_b = bcast(O_V, GLA_W)
        v_sel = jnp.zeros((PACK, GLA_DV), f32)
        for h in range(GLA_HEADS):
            v_sel = jnp.where(row_v == h, v_b[:, h * GLA_DV:(h + 1) * GLA_DV], v_sel)
        rts.append(jnp.concatenate([v_sel, piece_rows], axis=1).astype(bf16))
        qms.append(jnp.where(own_head, bcast(O_Q, GLA_KW), 0.0).astype(bf16))
        n_hi, n_mid, n_lo = _split3(bcast(O_SK, 2 * SWA_KVW))
        lt2s.append(jnp.where(row == 0, n_hi, jnp.where(row == 1, n_mid,
                                                        jnp.where(row == 2, n_lo, 0.0))).astype(bf16))
        sq_b = jnp.broadcast_to(pr[:, O_SQ:O_SQ + SWA_W], (SWA_HEADS, SWA_W))
        q8 = jnp.zeros((SWA_HEADS, LANES), f32)
        for gq in range(SWA_GROUP):
            q8 = jnp.where(row_q % SWA_GROUP == gq, sq_b[:, gq * LANES:(gq + 1) * LANES], q8)
        q8s.append(jnp.where(own_kv, q8, 0.0).astype(bf16))
    kv_as = [lax.dot_general(lts[j], rts[j], contract_rows, preferred_element_type=f32) for j in seqs]
    inss = [lax.dot_general(lt2s[j], last_lane_rows, contract_rows, preferred_element_type=f32) for j in seqs]
    s_news, kts, vts = [], [], []
    for j in seqs:
        s_new = kv_as[j][:, GLA_DV:] * s_ref[j] + kv_as[j][:, :GLA_DV]
        so_ref[j] = s_new
        s_news.append(s_new.astype(bf16))
        kt = jnp.where(newest, inss[j][:SWA_KVW], pltpu.roll(ck_ref[j], WINDOW - 1, axis=1))
        vt = jnp.where(newest, inss[j][SWA_KVW:], pltpu.roll(cv_ref[j], WINDOW - 1, axis=1))
        cko_ref[j] = kt
        cvo_ref[j] = vt
        kts.append(kt.astype(bf16))
        vts.append(vt.astype(bf16))
    for j in seqs:
        og_ref[j] = jnp.dot(qms[j], s_news[j], preferred_element_type=f32)[:GLA_HEADS]
    scores = [jnp.dot(q8s[j], kts[j], preferred_element_type=f32) for j in seqs]
    es, dens = [], []
    for j in seqs:
        m = jnp.maximum(jnp.max(scores[j], axis=-1, keepdims=True), sink)
        e = jnp.exp2(scores[j] - m)
        dens.append(jnp.sum(e, axis=-1, keepdims=True) + jnp.exp2(sink - m))
        es.append(e.astype(bf16))
    for j in seqs:
        o = lax.dot_general(es[j], vts[j], (((1,), (1,)), ((), ())), preferred_element_type=f32)
        os_ref[j] = o / dens[j]


def _sample_state_call(proj, decay, sinks_col, state, ck, cv):
    n = state.shape[0]
    g = SAMPLE_GROUP
    blk = lambda *tail: pl.BlockSpec((g,) + tail, lambda i: (i,) + (0,) * len(tail))
    cache = blk(SWA_KVW, WINDOW)
    return pl.pallas_call(
        _sample_state_kernel,
        grid=(n // g,),
        in_specs=[blk(W_MAIN), blk(GLA_KW), pl.BlockSpec((SWA_HEADS, 1), lambda i: (0, 0)),
                  blk(GLA_KW, GLA_DV), cache, cache],
        out_specs=[blk(GLA_KW, GLA_DV), cache, cache, blk(GLA_HEADS, GLA_DV), blk(SWA_HEADS, LANES)],
        out_shape=[
            jax.ShapeDtypeStruct(state.shape, f32), jax.ShapeDtypeStruct(ck.shape, f32),
            jax.ShapeDtypeStruct(cv.shape, f32),
            jax.ShapeDtypeStruct((n, GLA_HEADS, GLA_DV), f32), jax.ShapeDtypeStruct((n, SWA_HEADS, LANES), f32),
        ],
        compiler_params=pltpu.CompilerParams(dimension_semantics=("arbitrary",), vmem_limit_bytes=VMEM_LIMIT),
        name="sample_state",
    )(proj, decay, sinks_col, state, ck, cv)


def _sample_merge_kernel(x_ref, og_ref, gg_ref, os_ref, sg_ref, gn_ref, wout_ref, nf_ref, y_ref):
    y_ref[...] = _merge(x_ref[...], og_ref[...], gg_ref[...], os_ref[...], sg_ref[...],
                        gn_ref, wout_ref, nf_ref)


def _sample_merge_call(xs, og, gg, osw, sg, w):
    return pl.pallas_call(
        _sample_merge_kernel,
        out_shape=jax.ShapeDtypeStruct(xs.shape, f32),
        compiler_params=pltpu.CompilerParams(vmem_limit_bytes=VMEM_LIMIT),
        name="sample_merge",
    )(xs, og, gg, osw, sg, w["gla_norm"], w["w_out"], w["norm_f"])


def _sample_path(x_sample, state, ck_t, cv_t, w):
    n = x_sample.shape[0]
    xs = x_sample.reshape(n, D_MODEL)
    proj, decay = _sample_proj_call(xs, w)
    s_new, ck_new, cv_new, og, os_raw = _sample_state_call(
        proj, decay, w["sinks"].reshape(SWA_HEADS, 1), state, ck_t, cv_t)
    os5 = os_raw.reshape(n, SWA_KV, SWA_GROUP, SWA_KV, SWA_HD)
    os_il = jnp.stack([os5[:, kv, :, kv, :] for kv in range(SWA_KV)], axis=2).reshape(n, SWA_W)
    y = _sample_merge_call(xs, og.reshape(n, GLA_W), proj[:, O_GG:O_GG + GLA_W], os_il,
                           proj[:, O_SG:O_SG + SWA_W], w)
    return y, s_new, ck_new, cv_new


def _cache_view(c):
    n = c.shape[1]
    return jnp.transpose(c[0], (0, 2, 3, 1)).reshape(n, SWA_KVW, WINDOW)


def _cache_unview(c):
    n = c.shape[0]
    return jnp.transpose(c.reshape(n, SWA_KV, SWA_HD, WINDOW), (0, 3, 1, 2))[None]


def kernel(x_prompt, x_sample, state_gla, cache_win_k, cache_win_v, norm_in, w_in, w_gate_up, b_gate,
           gla_norm, attn_sinks, w_out, norm_f):
    bsz = x_prompt.shape[0]
    n = x_sample.shape[0]
    w = _prep_weights(norm_in[0], w_in[0], w_gate_up[0], b_gate[0], gla_norm[0], attn_sinks[0],
                      w_out[0], norm_f)
    cmat, lmask = _chunk_tables()
    y_p, s_p, k_p, v_p = _prompt_call(x_prompt, w, cmat, lmask, *_swa_mask_tables())
    y_s, s_s, k_s, v_s = _sample_path(x_sample, state_gla[0].reshape(n, GLA_KW, GLA_DV),
                                      _cache_view(cache_win_k), _cache_view(cache_win_v), w)
    return (y_p, y_s.reshape(n, 1, D_MODEL),
            s_p.reshape(1, bsz, GLA_HEADS, GLA_DK, GLA_DV),
            _cache_unview(k_p), _cache_unview(v_p),
            s_s.reshape(1, n, GLA_HEADS, GLA_DK, GLA_DV),
            _cache_unview(k_s), _cache_unview(v_s))
```

```python
import functools

import numpy as np
import jax
import jax.numpy as jnp
from jax import lax
from jax.experimental import pallas as pl
from jax.experimental.pallas import tpu as pltpu

D_MODEL = 1024
GLA_HEADS = 4
GLA_DK = 64
GLA_DV = 128
GLA_KW = GLA_HEADS * GLA_DK
GLA_W = GLA_HEADS * GLA_DV
GLA_RANK = 16
GLA_TAU = 16.0
CHUNK = 64
SWA_HEADS = 8
SWA_HD = 64
SWA_KV = 2
SWA_GROUP = SWA_HEADS // SWA_KV
SWA_W = SWA_HEADS * SWA_HD
SWA_KVW = SWA_KV * SWA_HD
WINDOW = 128
EPS = 1e-6
NEG_INF = -1e30
LOG2E = 1.4426950408889634
LANES = 128

O_Q, O_K, O_V, O_GG = 0, 256, 512, 1024
O_SQ, O_SK, O_SV, O_SG, O_LOW = 1536, 2048, 2176, 2304, 2816
W_MAIN = 2816
W_ALL = W_MAIN + LANES

N_LEVELS = 6
TOK_BLOCK = 512
GLA_GROUP = 4
SAMPLE_GROUP = 16
PACK = 16
VMEM_LIMIT = 48 * 1024 * 1024

f32 = jnp.float32
bf16 = jnp.bfloat16


def _chunk_tables():
    c = CHUNK
    t = np.arange(c)[None, :]
    i = np.arange(c)[:, None]
    blocks = [(t <= i), (t > i)]
    masks = []
    for l in range(N_LEVELS):
        h = c >> (l + 1)
        m = (i // (2 * h)) * (2 * h) + h
        upper = i >= m
        blocks.append(np.where(upper, (t > m) & (t <= i), (t > i) & (t <= m)))
        jj = np.arange(c)[None, :]
        masks.append((i // (2 * h) == jj // (2 * h)) & (i % (2 * h) >= h) & (jj % (2 * h) < h))
    masks.append(np.eye(c, dtype=bool))
    cm = np.concatenate(blocks, axis=0).astype(np.float32)
    cm3 = np.concatenate([cm, cm, cm], axis=1)
    lm = np.stack(masks).astype(np.float32)
    lm = np.concatenate([lm, lm], axis=1)
    return jnp.asarray(cm3, dtype=bf16), jnp.asarray(lm, dtype=f32)


def _swa_mask_tables():
    key = np.arange(2 * WINDOW)[:, None]
    qi = np.arange(WINDOW)[None, :]
    prev_ok = (key < WINDOW) & (key > qi)
    cur_ok = (key >= WINDOW) & (key - WINDOW <= qi)
    full = np.where(prev_ok | cur_ok, 0.0, NEG_INF)
    first = np.where(cur_ok, 0.0, NEG_INF)
    onehot = (np.arange(2 * WINDOW)[:, None] % WINDOW == qi).astype(np.float32)
    return jnp.asarray(np.stack([full, first]), dtype=bf16), jnp.asarray(onehot, dtype=bf16)


def _interleave_heads(a, axis):
    shp = a.shape
    a = a.reshape(shp[:axis] + (SWA_KV, SWA_GROUP, SWA_HD) + shp[axis + 1:])
    return jnp.swapaxes(a, axis, axis + 1).reshape(shp)


def _prep_weights(norm_in, w_in, w_gate_up, b_gate, gla_norm, attn_sinks, w_out, norm_f):
    sp = np.cumsum([0, 256, 256, 512, 512, 16, 512, 128, 128, 512])
    seg = [w_in[:, sp[i]:sp[i + 1]] for i in range(9)]
    gq, gk, gv, gg, glow, sq, sk, sv, sg = seg
    low = jnp.pad(glow, ((0, 0), (0, LANES - GLA_RANK)))
    w_all = jnp.concatenate(
        [gq * (GLA_DK ** -0.5), gk, gv, gg, _interleave_heads(sq, 1) * (SWA_HD ** -0.5 * LOG2E), sk, sv,
         _interleave_heads(sg, 1), low], axis=1).astype(bf16)
    w_up = jnp.pad(w_gate_up, ((0, LANES - GLA_RANK), (0, 0))).astype(bf16)
    w_out_p = jnp.concatenate([w_out[:GLA_W], _interleave_heads(w_out[GLA_W:], 0)], axis=0).astype(bf16)
    return dict(
        norm_in=norm_in.reshape(1, D_MODEL), w_all=w_all, w_up=w_up,
        b_gate=b_gate.reshape(1, GLA_KW), gla_norm=jnp.tile(gla_norm, GLA_HEADS).reshape(1, GLA_W),
        sinks=attn_sinks * LOG2E, w_out=w_out_p, norm_f=norm_f.reshape(1, D_MODEL))


def _rms(x, gain):
    return x * lax.rsqrt(jnp.mean(x * x, axis=-1, keepdims=True) + EPS) * gain


def _log_decay(glow, wup_ref, bg_ref):
    z = jnp.dot(glow.astype(bf16), wup_ref[...], preferred_element_type=f32) + bg_ref[...]
    return (jnp.minimum(z, 0.0) - jnp.log(1.0 + jnp.exp(-jnp.abs(z)))) * (1.0 / GLA_TAU)


def _silu(x):
    return x * jax.nn.sigmoid(x)


def _merge(x, o_gla, gg, o_swa, sg, gn_ref, wout_ref, nf_ref):
    parts = []
    for h in range(GLA_HEADS):
        sl = slice(h * GLA_DV, (h + 1) * GLA_DV)
        parts.append(_rms(o_gla[:, sl], gn_ref[:, sl]) * _silu(gg[:, sl]))
    parts.append(o_swa * _silu(sg))
    um = jnp.concatenate(parts, axis=1).astype(bf16)
    hres = x + jnp.dot(um, wout_ref[...], preferred_element_type=f32)
    return _rms(hres, nf_ref[...])


def _prompt_kernel(sinks_ref, x_ref, nin_ref, w_ref, wup_ref, bg_ref, cmat_ref, lmask_ref, kmask_ref, qhot_ref,
                   gn_ref, wout_ref, nf_ref,
                   y_ref, sp_ref, kn_ref, vn_ref,
                   s_ref, kprev_ref, vprev_ref, p_s, g_s, ogla_s, oswa_s, *, n_t):
    t = pl.program_id(1)
    tl = x_ref.shape[0]

    @pl.when(t == 0)
    def _():
        s_ref[...] = jnp.zeros_like(s_ref)
        kprev_ref[...] = jnp.zeros_like(kprev_ref)
        vprev_ref[...] = jnp.zeros_like(vprev_ref)

    x = x_ref[...]
    u = _rms(x, nin_ref[...]).astype(bf16)

    p_s[...] = jnp.dot(u, w_ref[...], preferred_element_type=f32)
    g_s[...] = _log_decay(p_s[:, O_LOW:O_LOW + LANES], wup_ref, bg_ref)

    lane_lo = lax.broadcasted_iota(jnp.int32, (CHUNK, LANES), 1) < GLA_DK

    n_pairs = GLA_HEADS // 2

    def group_terms(chunks):
        rows = {c: slice(c * CHUNK, (c + 1) * CHUNK) for c in chunks}
        sums = {}
        for c in chunks:
            gc = g_s[rows[c], :]
            g_hi = gc.astype(bf16)
            r1 = gc - g_hi.astype(f32)
            g_mid = r1.astype(bf16)
            g_lo = (r1 - g_mid.astype(f32)).astype(bf16)
            sums[c] = jnp.dot(cmat_ref[...], jnp.concatenate([g_hi, g_mid, g_lo], axis=0),
                              preferred_element_type=f32)
        level_ops, misc = {}, {}
        for c in chunks:
            qc, kc = p_s[rows[c], O_Q:O_Q + GLA_KW], p_s[rows[c], O_K:O_K + GLA_KW]
            e_b = jnp.exp(sums[c][0:CHUNK])
            misc[c] = (e_b, qc * e_b, kc * jnp.exp(sums[c][CHUNK:2 * CHUNK]),
                       p_s[rows[c], O_V:O_V + GLA_W].astype(bf16))
            for p in range(n_pairs):
                ln = slice(p * LANES, (p + 1) * LANES)
                qp, kp = qc[:, ln], kc[:, ln]
                for l in range(N_LEVELS + 1):
                    if l < N_LEVELS:
                        e = jnp.exp(sums[c][(2 + l) * CHUNK:(3 + l) * CHUNK, ln])
                        qe, ke = (qp * e).astype(bf16), (kp * e).astype(bf16)
                    else:
                        qe, ke = qp.astype(bf16), kp.astype(bf16)
                    lhs = jnp.concatenate([jnp.where(lane_lo, qe, 0), jnp.where(lane_lo, 0, qe)], axis=0)
                    level_ops[c, p, l] = (lhs, ke)
        scores = {key: lax.dot_general(lhs, ke, (((1,), (1,)), ((), ())), preferred_element_type=f32)
                  for key, (lhs, ke) in level_ops.items()}
        terms = {}
        for c in chunks:
            e_b, qb, k_suf, vb = misc[c]
            terms[c] = []
            for p in range(n_pairs):
                ln = slice(p * LANES, (p + 1) * LANES)
                a = scores[c, p, 0] * lmask_ref[0]
                for l in range(1, N_LEVELS + 1):
                    a = a + scores[c, p, l] * lmask_ref[l]
                ab = a.astype(bf16)
                qbp = qb[:, ln].astype(bf16)
                lhs_heads = []
                for hh in range(2):
                    qbm = jnp.where(lane_lo, qbp, 0) if hh == 0 else jnp.where(lane_lo, 0, qbp)
                    lhs_heads.append(jnp.concatenate([qbm, ab[hh * CHUNK:(hh + 1) * CHUNK, :]], axis=1))
                upd = lax.dot_general(k_suf[:, ln].astype(bf16), vb[:, p * 2 * GLA_DV:(p + 1) * 2 * GLA_DV],
                                      (((0,), (0,)), ((), ())), preferred_element_type=f32)
                upd = jnp.concatenate(
                    [upd[0:GLA_DK, 0:GLA_DV], upd[GLA_DK:2 * GLA_DK, GLA_DV:2 * GLA_DV]], axis=0)
                e_col = jnp.broadcast_to(e_b[CHUNK - 1:CHUNK, ln], (LANES, LANES)).T
                terms[c].append((lhs_heads, vb, upd, e_col))
        return terms

    all_terms = {}
    for c0 in range(0, tl // CHUNK, GLA_GROUP):
        all_terms.update(group_terms(range(c0, c0 + GLA_GROUP)))
    states = {}
    for p in range(n_pairs):
        ln = slice(p * LANES, (p + 1) * LANES)
        s_pair = s_ref[ln, :]
        for c in range(tl // CHUNK):
            states[c, p] = s_pair.astype(bf16)
            _, _, upd, e_col = all_terms[c][p]
            s_pair = e_col * s_pair + upd
        s_ref[ln, :] = s_pair
    for c in range(tl // CHUNK):
        for p in range(n_pairs):
            lhs_heads, vb, _, _ = all_terms[c][p]
            for hh in range(2):
                h = 2 * p + hh
                rhs = jnp.concatenate([states[c, p], vb[:, h * GLA_DV:(h + 1) * GLA_DV]], axis=0)
                ogla_s[c * CHUNK:(c + 1) * CHUNK, h * GLA_DV:(h + 1) * GLA_DV] = jnp.dot(
                    lhs_heads[hh], rhs, preferred_element_type=f32)

    lane_lo_w = lax.broadcasted_iota(jnp.int32, (WINDOW, LANES), 1) < SWA_HD
    row_lo = lax.broadcasted_iota(jnp.int32, (2 * WINDOW, 1), 0) < WINDOW
    for blk in range(tl // WINDOW):
        rs = slice(blk * WINDOW, (blk + 1) * WINDOW)
        sq = p_s[rs, O_SQ:O_SQ + SWA_W].astype(bf16)
        k_cur, v_cur = p_s[rs, O_SK:O_SK + SWA_KVW], p_s[rs, O_SV:O_SV + SWA_KVW]
        k2 = jnp.concatenate([kprev_ref[...], k_cur], axis=0).astype(bf16)
        v2 = jnp.concatenate([vprev_ref[...], v_cur], axis=0).astype(bf16)
        kmask = kmask_ref[jnp.where(t > 0, 0, 1)] if blk == 0 else kmask_ref[0]
        k2m = jnp.concatenate([k2, kmask], axis=1)
        for tt in range(SWA_GROUP):
            qt = sq[:, tt * LANES:(tt + 1) * LANES]
            lhs = jnp.concatenate([jnp.where(lane_lo_w, qt, 0), jnp.where(lane_lo_w, 0, qt)], axis=0)
            lhs = jnp.concatenate([lhs, qhot_ref[...]], axis=1)
            s = lax.dot_general(lhs, k2m, (((1,), (1,)), ((), ())), preferred_element_type=f32)
            sink = jnp.where(row_lo, sinks_ref[tt], sinks_ref[SWA_GROUP + tt])
            m = jnp.maximum(jnp.max(s, axis=-1, keepdims=True), sink)
            e = jnp.exp2(s - m)
            r = 1.0 / (jnp.sum(e, axis=-1, keepdims=True) + jnp.exp2(sink - m))
            o2 = jnp.dot(e.astype(bf16), v2, preferred_element_type=f32)
            oswa_s[rs, tt * LANES:(tt + 1) * LANES] = jnp.where(
                lane_lo_w, o2[:WINDOW] * r[:WINDOW], o2[WINDOW:] * r[WINDOW:])
        kprev_ref[...] = k_cur
        vprev_ref[...] = v_cur

    y_ref[...] = _merge(x, ogla_s[...], p_s[:, O_GG:O_GG + GLA_W], oswa_s[...], p_s[:, O_SG:O_SG + SWA_W],
                        gn_ref, wout_ref, nf_ref)

    @pl.when(t == n_t - 1)
    def _():
        sp_ref[...] = s_ref[...]
        kn_ref[...] = kprev_ref[...].T
        vn_ref[...] = vprev_ref[...].T


def _prompt_call(x, w, cmat, lmask, kmask, qhot):
    bsz, seq, _ = x.shape
    tl = TOK_BLOCK
    n_t = seq // tl
    const = lambda shape: pl.BlockSpec(shape, lambda b, t: (0,) * len(shape))
    return pl.pallas_call(
        functools.partial(_prompt_kernel, n_t=n_t),
        grid=(bsz, n_t),
        in_specs=[
            pl.BlockSpec(memory_space=pltpu.SMEM),
            pl.BlockSpec((None, tl, D_MODEL), lambda b, t: (b, t, 0)),
            const((1, D_MODEL)), const((D_MODEL, W_ALL)), const((LANES, GLA_KW)), const((1, GLA_KW)),
            const(cmat.shape), const(lmask.shape), const(kmask.shape), const(qhot.shape),
            const((1, GLA_W)), const((D_MODEL, D_MODEL)), const((1, D_MODEL)),
        ],
        out_specs=[
            pl.BlockSpec((None, tl, D_MODEL), lambda b, t: (b, t, 0)),
            pl.BlockSpec((None, GLA_KW, GLA_DV), lambda b, t: (b, 0, 0)),
            pl.BlockSpec((None, WINDOW, SWA_KVW), lambda b, t: (b, 0, 0)),
            pl.BlockSpec((None, WINDOW, SWA_KVW), lambda b, t: (b, 0, 0)),
        ],
        out_shape=[
            jax.ShapeDtypeStruct((bsz, seq, D_MODEL), f32),
            jax.ShapeDtypeStruct((bsz, GLA_KW, GLA_DV), f32),
            jax.ShapeDtypeStruct((bsz, WINDOW, SWA_KVW), f32),
            jax.ShapeDtypeStruct((bsz, WINDOW, SWA_KVW), f32),
        ],
        scratch_shapes=[
            pltpu.VMEM((GLA_KW, GLA_DV), f32),
            pltpu.VMEM((WINDOW, SWA_KVW), f32), pltpu.VMEM((WINDOW, SWA_KVW), f32),
            pltpu.VMEM((tl, W_ALL), f32), pltpu.VMEM((tl, GLA_KW), f32),
            pltpu.VMEM((tl, GLA_W), f32), pltpu.VMEM((tl, SWA_W), f32),
        ],
        compiler_params=pltpu.CompilerParams(
            dimension_semantics=("arbitrary", "arbitrary"), vmem_limit_bytes=VMEM_LIMIT),
        name="prompt_layer",
    )(w["sinks"], x, w["norm_in"], w["w_all"], w["w_up"], w["b_gate"], cmat, lmask, kmask, qhot,
      w["gla_norm"], w["w_out"], w["norm_f"])


def _sample_proj_kernel(x_ref, nin_ref, w_ref, wup_ref, bg_ref, proj_ref, decay_ref):
    u = _rms(x_ref[...], nin_ref[...]).astype(bf16)
    proj_ref[...] = jnp.dot(u, w_ref[:, :W_MAIN], preferred_element_type=f32)
    glow = jnp.dot(u, w_ref[:, O_LOW:O_LOW + LANES], preferred_element_type=f32)
    decay_ref[...] = jnp.exp(_log_decay(glow, wup_ref, bg_ref))


def _sample_proj_call(xs, w):
    n = xs.shape[0]
    return pl.pallas_call(
        _sample_proj_kernel,
        out_shape=[jax.ShapeDtypeStruct((n, W_MAIN), f32), jax.ShapeDtypeStruct((n, GLA_KW), f32)],
        compiler_params=pltpu.CompilerParams(vmem_limit_bytes=VMEM_LIMIT),
        name="sample_proj",
    )(xs, w["norm_in"], w["w_all"], w["w_up"], w["b_gate"])


def _split3(x):
    as_bf16 = lambda v: v.astype(bf16).astype(f32)
    hi = as_bf16(x)
    mid = as_bf16(x - hi)
    return hi, mid, as_bf16(x - hi - mid)


def _sample_state_kernel(proj_ref, decay_ref, sink_ref, s_ref, ck_ref, cv_ref,
                         so_ref, cko_ref, cvo_ref, og_ref, os_ref):
    row = lax.broadcasted_iota(jnp.int32, (PACK, GLA_KW), 0)
    head_of_lane = lax.broadcasted_iota(jnp.int32, (PACK, GLA_KW), 1) // GLA_DK
    own_head = head_of_lane == row
    row_v = lax.broadcasted_iota(jnp.int32, (PACK, GLA_DV), 0)
    lane_v = lax.broadcasted_iota(jnp.int32, (PACK, GLA_DV), 1)
    piece_rows = ((row_v >= GLA_HEADS) & (row_v < GLA_HEADS + 3)).astype(f32)
    last_lane_rows = ((row_v < 3) & (lane_v == WINDOW - 1)).astype(bf16)
    newest = lax.broadcasted_iota(jnp.int32, (SWA_KVW, WINDOW), 1) == WINDOW - 1
    row_q = lax.broadcasted_iota(jnp.int32, (SWA_HEADS, LANES), 0)
    own_kv = (lax.broadcasted_iota(jnp.int32, (SWA_HEADS, LANES), 1) // SWA_HD) == (row_q // SWA_GROUP)
    sink = sink_ref[...]
    contract_rows = (((0,), (0,)), ((), ()))
    seqs = range(SAMPLE_GROUP)
    lts, rts, qms, lt2s, q8s = [], [], [], [], []
    for j in seqs:
        pr = proj_ref[j:j + 1, :]
        bcast = lambda lo, width: jnp.broadcast_to(pr[:, lo:lo + width], (PACK, width))
        a_hi, a_mid, a_lo = _split3(jnp.broadcast_to(decay_ref[j:j + 1, :], (PACK, GLA_KW)))
        a_piece = jnp.where(row == GLA_HEADS, a_hi, jnp.where(row == GLA_HEADS + 1, a_mid, a_lo))
        lts.append(jnp.where(own_head, bcast(O_K, GLA_KW),
                             jnp.where((row >= GLA_HEADS) & (row < GLA_HEADS + 3), a_piece, 0.0)).astype(bf16))
        v_b = bcast(O_V, GLA_W)
        v_sel = jnp.zeros((PACK, GLA_DV), f32)
        for h in range(GLA_HEADS):
            v_sel = jnp.where(row_v == h, v_b[:, h * GLA_DV:(h + 1) * GLA_DV], v_sel)
        rts.append(jnp.concatenate([v_sel, piece_rows], axis=1).astype(bf16))
        qms.append(jnp.where(own_head, bcast(O_Q, GLA_KW), 0.0).astype(bf16))
        n_hi, n_mid, n_lo = _split3(bcast(O_SK, 2 * SWA_KVW))
        lt2s.append(jnp.where(row == 0, n_hi, jnp.where(row == 1, n_mid,
                                                        jnp.where(row == 2, n_lo, 0.0))).astype(bf16))
        sq_b = jnp.broadcast_to(pr[:, O_SQ:O_SQ + SWA_W], (SWA_HEADS, SWA_W))
        q8 = jnp.zeros((SWA_HEADS, LANES), f32)
        for gq in range(SWA_GROUP):
            q8 = jnp.where(row_q % SWA_GROUP == gq, sq_b[:, gq * LANES:(gq + 1) * LANES], q8)
        q8s.append(jnp.where(own_kv, q8, 0.0).astype(bf16))
    kv_as = [lax.dot_general(lts[j], rts[j], contract_rows, preferred_element_type=f32) for j in seqs]
    inss = [lax.dot_general(lt2s[j], last_lane_rows, contract_rows, preferred_element_type=f32) for j in seqs]
    s_news, kts, vts = [], [], []
    for j in seqs:
        s_new = kv_as[j][:, GLA_DV:] * s_ref[j] + kv_as[j][:, :GLA_DV]
        so_ref[j] = s_new
        s_news.append(s_new.astype(bf16))
        kt = jnp.where(newest, inss[j][:SWA_KVW], pltpu.roll(ck_ref[j], WINDOW - 1, axis=1))
        vt = jnp.where(newest, inss[j][SWA_KVW:], pltpu.roll(cv_ref[j], WINDOW - 1, axis=1))
        cko_ref[j] = kt
        cvo_ref[j] = vt
        kts.append(kt.astype(bf16))
        vts.append(vt.astype(bf16))
    for j in seqs:
        og_ref[j] = jnp.dot(qms[j], s_news[j], preferred_element_type=f32)[:GLA_HEADS]
    scores = [jnp.dot(q8s[j], kts[j], preferred_element_type=f32) for j in seqs]
    es, dens = [], []
    for j in seqs:
        m = jnp.maximum(jnp.max(scores[j], axis=-1, keepdims=True), sink)
        e = jnp.exp2(scores[j] - m)
        dens.append(jnp.sum(e, axis=-1, keepdims=True) + jnp.exp2(sink - m))
        es.append(e.astype(bf16))
    for j in seqs:
        o = lax.dot_general(es[j], vts[j], (((1,), (1,)), ((), ())), preferred_element_type=f32)
        os_ref[j] = o / dens[j]


def _sample_state_call(proj, decay, sinks_col, state, ck, cv):
    n = state.shape[0]
    g = SAMPLE_GROUP
    blk = lambda *tail: pl.BlockSpec((g,) + tail, lambda i: (i,) + (0,) * len(tail))
    cache = blk(SWA_KVW, WINDOW)
    return pl.pallas_call(
        _sample_state_kernel,
        grid=(n // g,),
        in_specs=[blk(W_MAIN), blk(GLA_KW), pl.BlockSpec((SWA_HEADS, 1), lambda i: (0, 0)),
                  blk(GLA_KW, GLA_DV), cache, cache],
        out_specs=[blk(GLA_KW, GLA_DV), cache, cache, blk(GLA_HEADS, GLA_DV), blk(SWA_HEADS, LANES)],
        out_shape=[
            jax.ShapeDtypeStruct(state.shape, f32), jax.ShapeDtypeStruct(ck.shape, f32),
            jax.ShapeDtypeStruct(cv.shape, f32),
            jax.ShapeDtypeStruct((n, GLA_HEADS, GLA_DV), f32), jax.ShapeDtypeStruct((n, SWA_HEADS, LANES), f32),
        ],
        compiler_params=pltpu.CompilerParams(dimension_semantics=("arbitrary",), vmem_limit_bytes=VMEM_LIMIT),
        name="sample_state",
    )(proj, decay, sinks_col, state, ck, cv)


def _sample_merge_kernel(x_ref, og_ref, gg_ref, os_ref, sg_ref, gn_ref, wout_ref, nf_ref, y_ref):
    y_ref[...] = _merge(x_ref[...], og_ref[...], gg_ref[...], os_ref[...], sg_ref[...],
                        gn_ref, wout_ref, nf_ref)


def _sample_merge_call(xs, og, gg, osw, sg, w):
    return pl.pallas_call(
        _sample_merge_kernel,
        out_shape=jax.ShapeDtypeStruct(xs.shape, f32),
        compiler_params=pltpu.CompilerParams(vmem_limit_bytes=VMEM_LIMIT),
        name="sample_merge",
    )(xs, og, gg, osw, sg, w["gla_norm"], w["w_out"], w["norm_f"])


def _sample_path(x_sample, state, ck_t, cv_t, w):
    n = x_sample.shape[0]
    xs = x_sample.reshape(n, D_MODEL)
    proj, decay = _sample_proj_call(xs, w)
    s_new, ck_new, cv_new, og, os_raw = _sample_state_call(
        proj, decay, w["sinks"].reshape(SWA_HEADS, 1), state, ck_t, cv_t)
    os5 = os_raw.reshape(n, SWA_KV, SWA_GROUP, SWA_KV, SWA_HD)
    os_il = jnp.stack([os5[:, kv, :, kv, :] for kv in range(SWA_KV)], axis=2).reshape(n, SWA_W)
    y = _sample_merge_call(xs, og.reshape(n, GLA_W), proj[:, O_GG:O_GG + GLA_W], os_il,
                           proj[:, O_SG:O_SG + SWA_W], w)
    return y, s_new, ck_new, cv_new


def _cache_view(c):
    n = c.shape[1]
    return jnp.transpose(c[0], (0, 2, 3, 1)).reshape(n, SWA_KVW, WINDOW)


def _cache_unview(c):
    n = c.shape[0]
    return jnp.transpose(c.reshape(n, SWA_KV, SWA_HD, WINDOW), (0, 3, 1, 2))[None]


def kernel(x_prompt, x_sample, state_gla, cache_win_k, cache_win_v, norm_in, w_in, w_gate_up, b_gate,
           gla_norm, attn_sinks, w_out, norm_f):
    bsz = x_prompt.shape[0]
    n = x_sample.shape[0]
    w = _prep_weights(norm_in[0], w_in[0], w_gate_up[0], b_gate[0], gla_norm[0], attn_sinks[0],
                      w_out[0], norm_f)
    cmat, lmask = _chunk_tables()
    y_p, s_p, k_p, v_p = _prompt_call(x_prompt, w, cmat, lmask, *_swa_mask_tables())
    y_s, s_s, k_s, v_s = _sample_path(x_sample, state_gla[0].reshape(n, GLA_KW, GLA_DV),
                                      _cache_view(cache_win_k), _cache_view(cache_win_v), w)
    return (y_p, y_s.reshape(n, 1, D_MODEL),
            s_p.reshape(1, bsz, GLA_HEADS, GLA_DK, GLA_DV),
            _cache_unview(k_p), _cache_unview(v_p),
            s_s.reshape(1, n, GLA_HEADS, GLA_DK, GLA_DV),
            _cache_unview(k_s), _cache_unview(v_s))
```

```python
import functools

import numpy as np
import jax
import jax.numpy as jnp
from jax import lax
from jax.experimental import pallas as pl
from jax.experimental.pallas import tpu as pltpu

D_MODEL = 1024
GLA_HEADS = 4
GLA_DK = 64
GLA_DV = 128
GLA_KW = GLA_HEADS * GLA_DK
GLA_W = GLA_HEADS * GLA_DV
GLA_RANK = 16
GLA_TAU = 16.0
CHUNK = 64
SWA_HEADS = 8
SWA_HD = 64
SWA_KV = 2
SWA_GROUP = SWA_HEADS // SWA_KV
SWA_W = SWA_HEADS * SWA_HD
SWA_KVW = SWA_KV * SWA_HD
WINDOW = 128
EPS = 1e-6
NEG_INF = -1e30
LOG2E = 1.4426950408889634
LANES = 128

O_Q, O_K, O_V, O_GG = 0, 256, 512, 1024
O_SQ, O_SK, O_SV, O_SG, O_LOW = 1536, 2048, 2176, 2304, 2816
W_MAIN = 2816
W_ALL = W_MAIN + LANES

N_LEVELS = 6
TOK_BLOCK = 512
GLA_GROUP = 4
SAMPLE_GROUP = 16
PACK = 16
VMEM_LIMIT = 48 * 1024 * 1024

f32 = jnp.float32
bf16 = jnp.bfloat16


def _chunk_tables():
    c = CHUNK
    t = np.arange(c)[None, :]
    i = np.arange(c)[:, None]
    blocks = [(t <= i), (t > i)]
    masks = []
    for l in range(N_LEVELS):
        h = c >> (l + 1)
        m = (i // (2 * h)) * (2 * h) + h
        upper = i >= m
        blocks.append(np.where(upper, (t > m) & (t <= i), (t > i) & (t <= m)))
        jj = np.arange(c)[None, :]
        masks.append((i // (2 * h) == jj // (2 * h)) & (i % (2 * h) >= h) & (jj % (2 * h) < h))
    masks.append(np.eye(c, dtype=bool))
    cm = np.concatenate(blocks, axis=0).astype(np.float32)
    cm3 = np.concatenate([cm, cm, cm], axis=1)
    lm = np.stack(masks).astype(np.float32)
    lm = np.concatenate([lm, lm], axis=1)
    return jnp.asarray(cm3, dtype=bf16), jnp.asarray(lm, dtype=f32)


def _swa_mask_tables():
    key = np.arange(2 * WINDOW)[:, None]
    qi = np.arange(WINDOW)[None, :]
    prev_ok = (key < WINDOW) & (key > qi)
    cur_ok = (key >= WINDOW) & (key - WINDOW <= qi)
    full = np.where(prev_ok | cur_ok, 0.0, NEG_INF)
    first = np.where(cur_ok, 0.0, NEG_INF)
    onehot = (np.arange(2 * WINDOW)[:, None] % WINDOW == qi).astype(np.float32)
    return jnp.asarray(np.stack([full, first]), dtype=bf16), jnp.asarray(onehot, dtype=bf16)


I_Q, I_K, I_V, I_GG, I_LOW, I_SQ, I_SK, I_SV, I_SG = (
    int(v) for v in np.cumsum([0, 256, 256, 512, 512, 16, 512, 128, 128])[:9])


def _interleaved_rows(base, tile):
    return [(base + kv * SWA_GROUP * SWA_HD + tile * SWA_HD, SWA_HD) for kv in range(SWA_KV)]


def _weight_layout_kernel(wt_ref, wo_ref, wall_ref, wop_ref):
    def put(col, pieces, scale=None):
        rows = [wt_ref[r:r + n, :] for r, n in pieces]
        missing = LANES - sum(n for _, n in pieces)
        if missing:
            rows.append(jnp.zeros((missing, D_MODEL), f32))
        blk = jnp.concatenate(rows, axis=0) if len(rows) > 1 else rows[0]
        if scale is not None:
            blk = blk * scale
        wall_ref[:, col:col + LANES] = blk.T.astype(bf16)

    for i in range(GLA_KW // LANES):
        put(O_Q + i * LANES, [(I_Q + i * LANES, LANES)], GLA_DK ** -0.5)
        put(O_K + i * LANES, [(I_K + i * LANES, LANES)])
    for i in range(GLA_W // LANES):
        put(O_V + i * LANES, [(I_V + i * LANES, LANES)])
        put(O_GG + i * LANES, [(I_GG + i * LANES, LANES)])
        put(O_SQ + i * LANES, _interleaved_rows(I_SQ, i), SWA_HD ** -0.5 * LOG2E)
        put(O_SG + i * LANES, _interleaved_rows(I_SG, i))
    put(O_SK, [(I_SK, SWA_KVW)])
    put(O_SV, [(I_SV, SWA_KVW)])
    put(O_LOW, [(I_LOW, GLA_RANK)])
    wop_ref[:GLA_W, :] = wo_ref[:GLA_W, :].astype(bf16)
    for i in range(SWA_GROUP):
        for kv, (r, n) in enumerate(_interleaved_rows(GLA_W, i)):
            dst = GLA_W + (i * SWA_KV + kv) * SWA_HD
            wop_ref[dst:dst + n, :] = wo_ref[r:r + n, :].astype(bf16)


def _prep_weights(norm_in, w_in, w_gate_up, b_gate, gla_norm, attn_sinks, w_out, norm_f):
    w_all, w_out_p = pl.pallas_call(
        _weight_layout_kernel,
        out_shape=[jax.ShapeDtypeStruct((D_MODEL, W_ALL), bf16), jax.ShapeDtypeStruct((D_MODEL, D_MODEL), bf16)],
        compiler_params=pltpu.CompilerParams(vmem_limit_bytes=VMEM_LIMIT),
        name="weight_layout",
    )(w_in.T, w_out)
    w_up = jnp.pad(w_gate_up, ((0, LANES - GLA_RANK), (0, 0))).astype(bf16)
    return dict(
        norm_in=norm_in.reshape(1, D_MODEL), w_all=w_all, w_up=w_up,
        b_gate=b_gate.reshape(1, GLA_KW), gla_norm=jnp.tile(gla_norm, GLA_HEADS).reshape(1, GLA_W),
        sinks=attn_sinks * LOG2E, w_out=w_out_p, norm_f=norm_f.reshape(1, D_MODEL))


def _rms(x, gain):
    return x * lax.rsqrt(jnp.mean(x * x, axis=-1, keepdims=True) + EPS) * gain


def _log_decay(glow, wup_ref, bg_ref):
    z = jnp.dot(glow.astype(bf16), wup_ref[...], preferred_element_type=f32) + bg_ref[...]
    return (jnp.minimum(z, 0.0) - jnp.log(1.0 + jnp.exp(-jnp.abs(z)))) * (1.0 / GLA_TAU)


def _silu(x):
    return x * jax.nn.sigmoid(x)


def _merge(x, o_gla, gg, o_swa, sg, gn_ref, wout_ref, nf_ref):
    parts = []
    for h in range(GLA_HEADS):
        sl = slice(h * GLA_DV, (h + 1) * GLA_DV)
        parts.append(_rms(o_gla[:, sl], gn_ref[:, sl]) * _silu(gg[:, sl]))
    parts.append(o_swa * _silu(sg))
    um = jnp.concatenate(parts, axis=1).astype(bf16)
    hres = x + jnp.dot(um, wout_ref[...], preferred_element_type=f32)
    return _rms(hres, nf_ref[...])


def _prompt_kernel(sinks_ref, x_ref, nin_ref, w_ref, wup_ref, bg_ref, cmat_ref, lmask_ref, kmask_ref, qhot_ref,
                   gn_ref, wout_ref, nf_ref,
                   y_ref, sp_ref, kn_ref, vn_ref,
                   s_ref, kprev_ref, vprev_ref, p_s, g_s, ogla_s, oswa_s, *, n_t):
    t = pl.program_id(1)
    tl = x_ref.shape[0]

    @pl.when(t == 0)
    def _():
        s_ref[...] = jnp.zeros_like(s_ref)
        kprev_ref[...] = jnp.zeros_like(kprev_ref)
        vprev_ref[...] = jnp.zeros_like(vprev_ref)

    def project(rg):
        u = _rms(x_ref[rg, :], nin_ref[...]).astype(bf16)
        p_s[rg, :] = jnp.dot(u, w_ref[...], preferred_element_type=f32)
        g_s[rg, :] = _log_decay(p_s[rg, O_LOW:O_LOW + LANES], wup_ref, bg_ref)

    lane_lo = lax.broadcasted_iota(jnp.int32, (CHUNK, LANES), 1) < GLA_DK

    n_pairs = GLA_HEADS // 2

    def group_terms(chunks):
        rows = {c: slice(c * CHUNK, (c + 1) * CHUNK) for c in chunks}
        sums = {}
        for c in chunks:
            gc = g_s[rows[c], :]
            g_hi = gc.astype(bf16)
            r1 = gc - g_hi.astype(f32)
            g_mid = r1.astype(bf16)
            g_lo = (r1 - g_mid.astype(f32)).astype(bf16)
            sums[c] = jnp.dot(cmat_ref[...], jnp.concatenate([g_hi, g_mid, g_lo], axis=0),
                              preferred_element_type=f32)
        level_ops, misc = {}, {}
        for c in chunks:
            qc, kc = p_s[rows[c], O_Q:O_Q + GLA_KW], p_s[rows[c], O_K:O_K + GLA_KW]
            e_b = jnp.exp(sums[c][0:CHUNK])
            misc[c] = (e_b, qc * e_b, kc * jnp.exp(sums[c][CHUNK:2 * CHUNK]),
                       p_s[rows[c], O_V:O_V + GLA_W].astype(bf16))
            for p in range(n_pairs):
                ln = slice(p * LANES, (p + 1) * LANES)
                qp, kp = qc[:, ln], kc[:, ln]
                for l in range(N_LEVELS + 1):
                    if l < N_LEVELS:
                        e = jnp.exp(sums[c][(2 + l) * CHUNK:(3 + l) * CHUNK, ln])
                        qe, ke = (qp * e).astype(bf16), (kp * e).astype(bf16)
                    else:
                        qe, ke = qp.astype(bf16), kp.astype(bf16)
                    lhs = jnp.concatenate([jnp.where(lane_lo, qe, 0), jnp.where(lane_lo, 0, qe)], axis=0)
                    level_ops[c, p, l] = (lhs, ke)
        scores = {key: lax.dot_general(lhs, ke, (((1,), (1,)), ((), ())), preferred_element_type=f32)
                  for key, (lhs, ke) in level_ops.items()}
        terms = {}
        for c in chunks:
            e_b, qb, k_suf, vb = misc[c]
            terms[c] = []
            for p in range(n_pairs):
                ln = slice(p * LANES, (p + 1) * LANES)
                a = scores[c, p, 0] * lmask_ref[0]
                for l in range(1, N_LEVELS + 1):
                    a = a + scores[c, p, l] * lmask_ref[l]
                ab = a.astype(bf16)
                qbp = qb[:, ln].astype(bf16)
                lhs_heads = []
                for hh in range(2):
                    qbm = jnp.where(lane_lo, qbp, 0) if hh == 0 else jnp.where(lane_lo, 0, qbp)
                    lhs_heads.append(jnp.concatenate([qbm, ab[hh * CHUNK:(hh + 1) * CHUNK, :]], axis=1))
                upd = lax.dot_general(k_suf[:, ln].astype(bf16), vb[:, p * 2 * GLA_DV:(p + 1) * 2 * GLA_DV],
                                      (((0,), (0,)), ((), ())), preferred_element_type=f32)
                upd = jnp.concatenate(
                    [upd[0:GLA_DK, 0:GLA_DV], upd[GLA_DK:2 * GLA_DK, GLA_DV:2 * GLA_DV]], axis=0)
                e_col = jnp.broadcast_to(e_b[CHUNK - 1:CHUNK, ln], (LANES, LANES)).T
                terms[c].append((lhs_heads, vb, upd, e_col))
        return terms

    def gla(chunks):
        terms = {}
        for c0 in range(chunks[0], chunks[-1] + 1, GLA_GROUP):
            terms.update(group_terms(range(c0, c0 + GLA_GROUP)))
        states = {}
        for p in range(n_pairs):
            ln = slice(p * LANES, (p + 1) * LANES)
            s_pair = s_ref[ln, :]
            for c in chunks:
                states[c, p] = s_pair.astype(bf16)
                _, _, upd, e_col = terms[c][p]
                s_pair = e_col * s_pair + upd
            s_ref[ln, :] = s_pair
        for c in chunks:
            for p in range(n_pairs):
                lhs_heads, vb, _, _ = terms[c][p]
                for hh in range(2):
                    h = 2 * p + hh
                    rhs = jnp.concatenate([states[c, p], vb[:, h * GLA_DV:(h + 1) * GLA_DV]], axis=0)
                    ogla_s[c * CHUNK:(c + 1) * CHUNK, h * GLA_DV:(h + 1) * GLA_DV] = jnp.dot(
                        lhs_heads[hh], rhs, preferred_element_type=f32)

    lane_lo_w = lax.broadcasted_iota(jnp.int32, (WINDOW, LANES), 1) < SWA_HD
    row_lo = lax.broadcasted_iota(jnp.int32, (2 * WINDOW, 1), 0) < WINDOW

    def swa(blocks):
        for blk in blocks:
            rs = slice(blk * WINDOW, (blk + 1) * WINDOW)
            sq = p_s[rs, O_SQ:O_SQ + SWA_W].astype(bf16)
            k_cur, v_cur = p_s[rs, O_SK:O_SK + SWA_KVW], p_s[rs, O_SV:O_SV + SWA_KVW]
            k2 = jnp.concatenate([kprev_ref[...], k_cur], axis=0).astype(bf16)
            v2 = jnp.concatenate([vprev_ref[...], v_cur], axis=0).astype(bf16)
            kmask = kmask_ref[jnp.where(t > 0, 0, 1)] if blk == 0 else kmask_ref[0]
            k2m = jnp.concatenate([k2, kmask], axis=1)
            for tt in range(SWA_GROUP):
                qt = sq[:, tt * LANES:(tt + 1) * LANES]
                lhs = jnp.concatenate([jnp.where(lane_lo_w, qt, 0), jnp.where(lane_lo_w, 0, qt)], axis=0)
                lhs = jnp.concatenate([lhs, qhot_ref[...]], axis=1)
                s = lax.dot_general(lhs, k2m, (((1,), (1,)), ((), ())), preferred_element_type=f32)
                sink = jnp.where(row_lo, sinks_ref[tt], sinks_ref[SWA_GROUP + tt])
                m = jnp.maximum(jnp.max(s, axis=-1, keepdims=True), sink)
                e = jnp.exp2(s - m)
                r = 1.0 / (jnp.sum(e, axis=-1, keepdims=True) + jnp.exp2(sink - m))
                o2 = jnp.dot(e.astype(bf16), v2, preferred_element_type=f32)
                oswa_s[rs, tt * LANES:(tt + 1) * LANES] = jnp.where(
                    lane_lo_w, o2[:WINDOW] * r[:WINDOW], o2[WINDOW:] * r[WINDOW:])
            kprev_ref[...] = k_cur
            vprev_ref[...] = v_cur

    def merge(rg):
        y_ref[rg, :] = _merge(x_ref[rg, :], ogla_s[rg, :], p_s[rg, O_GG:O_GG + GLA_W], oswa_s[rg, :],
                              p_s[rg, O_SG:O_SG + SWA_W], gn_ref, wout_ref, nf_ref)

    whole = slice(0, tl)
    project(whole)
    gla(range(tl // CHUNK))
    swa(range(tl // WINDOW))
    merge(whole)

    @pl.when(t == n_t - 1)
    def _():
        sp_ref[...] = s_ref[...]
        kn_ref[...] = kprev_ref[...].T
        vn_ref[...] = vprev_ref[...].T


def _prompt_call(x, w, cmat, lmask, kmask, qhot):
    bsz, seq, _ = x.shape
    tl = TOK_BLOCK
    n_t = seq // tl
    const = lambda shape: pl.BlockSpec(shape, lambda b, t: (0,) * len(shape))
    return pl.pallas_call(
        functools.partial(_prompt_kernel, n_t=n_t),
        grid=(bsz, n_t),
        in_specs=[
            pl.BlockSpec(memory_space=pltpu.SMEM),
            pl.BlockSpec((None, tl, D_MODEL), lambda b, t: (b, t, 0)),
            const((1, D_MODEL)), const((D_MODEL, W_ALL)), const((LANES, GLA_KW)), const((1, GLA_KW)),
            const(cmat.shape), const(lmask.shape), const(kmask.shape), const(qhot.shape),
            const((1, GLA_W)), const((D_MODEL, D_MODEL)), const((1, D_MODEL)),
        ],
        out_specs=[
            pl.BlockSpec((None, tl, D_MODEL), lambda b, t: (b, t, 0)),
            pl.BlockSpec((None, GLA_KW, GLA_DV), lambda b, t: (b, 0, 0)),
            pl.BlockSpec((None, WINDOW, SWA_KVW), lambda b, t: (b, 0, 0)),
            pl.BlockSpec((None, WINDOW, SWA_KVW), lambda b, t: (b, 0, 0)),
        ],
        out_shape=[
            jax.ShapeDtypeStruct((bsz, seq, D_MODEL), f32),
            jax.ShapeDtypeStruct((bsz, GLA_KW, GLA_DV), f32),
            jax.ShapeDtypeStruct((bsz, WINDOW, SWA_KVW), f32),
            jax.ShapeDtypeStruct((bsz, WINDOW, SWA_KVW), f32),
        ],
        scratch_shapes=[
            pltpu.VMEM((GLA_KW, GLA_DV), f32),
            pltpu.VMEM((WINDOW, SWA_KVW), f32), pltpu.VMEM((WINDOW, SWA_KVW), f32),
            pltpu.VMEM((tl, W_ALL), f32), pltpu.VMEM((tl, GLA_KW), f32),
            pltpu.VMEM((tl, GLA_W), f32), pltpu.VMEM((tl, SWA_W), f32),
        ],
        compiler_params=pltpu.CompilerParams(
            dimension_semantics=("arbitrary", "arbitrary"), vmem_limit_bytes=VMEM_LIMIT),
        name="prompt_layer",
    )(w["sinks"], x, w["norm_in"], w["w_all"], w["w_up"], w["b_gate"], cmat, lmask, kmask, qhot,
      w["gla_norm"], w["w_out"], w["norm_f"])


def _sample_proj_kernel(x_ref, nin_ref, w_ref, wup_ref, bg_ref, proj_ref, decay_ref):
    u = _rms(x_ref[...], nin_ref[...]).astype(bf16)
    proj_ref[...] = jnp.dot(u, w_ref[:, :W_MAIN], preferred_element_type=f32)
    glow = jnp.dot(u, w_ref[:, O_LOW:O_LOW + LANES], preferred_element_type=f32)
    decay_ref[...] = jnp.exp(_log_decay(glow, wup_ref, bg_ref))


def _sample_proj_call(xs, w):
    n = xs.shape[0]
    return pl.pallas_call(
        _sample_proj_kernel,
        out_shape=[jax.ShapeDtypeStruct((n, W_MAIN), f32), jax.ShapeDtypeStruct((n, GLA_KW), f32)],
        compiler_params=pltpu.CompilerParams(vmem_limit_bytes=VMEM_LIMIT),
        name="sample_proj",
    )(xs, w["norm_in"], w["w_all"], w["w_up"], w["b_gate"])


def _split3(x):
    as_bf16 = lambda v: v.astype(bf16).astype(f32)
    hi = as_bf16(x)
    mid = as_bf16(x - hi)
    return hi, mid, as_bf16(x - hi - mid)


def _sample_state_kernel(proj_ref, decay_ref, sink_ref, s_ref, ck_ref, cv_ref,
                         so_ref, cko_ref, cvo_ref, og_ref, os_ref):
    row = lax.broadcasted_iota(jnp.int32, (PACK, GLA_KW), 0)
    head_of_lane = lax.broadcasted_iota(jnp.int32, (PACK, GLA_KW), 1) // GLA_DK
    own_head = head_of_lane == row
    row_v = lax.broadcasted_iota(jnp.int32, (PACK, GLA_DV), 0)
    lane_v = lax.broadcasted_iota(jnp.int32, (PACK, GLA_DV), 1)
    piece_rows = ((row_v >= GLA_HEADS) & (row_v < GLA_HEADS + 3)).astype(f32)
    last_lane_rows = ((row_v < 3) & (lane_v == WINDOW - 1)).astype(bf16)
    newest = lax.broadcasted_iota(jnp.int32, (SWA_KVW, WINDOW), 1) == WINDOW - 1
    row_q = lax.broadcasted_iota(jnp.int32, (SWA_HEADS, LANES), 0)
    own_kv = (lax.broadcasted_iota(jnp.int32, (SWA_HEADS, LANES), 1) // SWA_HD) == (row_q // SWA_GROUP)
    sink = sink_ref[...]
    contract_rows = (((0,), (0,)), ((), ()))
    seqs = range(SAMPLE_GROUP)
    lts, rts, qms, lt2s, q8s = [], [], [], [], []
    for j in seqs:
        pr = proj_ref[j:j + 1, :]
        bcast = lambda lo, width: jnp.broadcast_to(pr[:, lo:lo + width], (PACK, width))
        a_hi, a_mid, a_lo = _split3(jnp.broadcast_to(decay_ref[j:j + 1, :], (PACK, GLA_KW)))
        a_piece = jnp.where(row == GLA_HEADS, a_hi, jnp.where(row == GLA_HEADS + 1, a_mid, a_lo))
        lts.append(jnp.where(own_head, bcast(O_K, GLA_KW),
                             jnp.where((row >= GLA_HEADS) & (row < GLA_HEADS + 3), a_piece, 0.0)).astype(bf16))
        v_b = bcast(O_V, GLA_W)
        v_sel = jnp.zeros((PACK, GLA_DV), f32)
        for h in range(GLA_HEADS):
            v_sel = jnp.where(row_v == h, v_b[:, h * GLA_DV:(h + 1) * GLA_DV], v_sel)
        rts.append(jnp.concatenate([v_sel, piece_rows], axis=1).astype(bf16))
        qms.append(jnp.where(own_head, bcast(O_Q, GLA_KW), 0.0).astype(bf16))
        n_hi, n_mid, n_lo = _split3(bcast(O_SK, 2 * SWA_KVW))
        lt2s.append(jnp.where(row == 0, n_hi, jnp.where(row == 1, n_mid,
                                                        jnp.where(row == 2, n_lo, 0.0))).astype(bf16))
        sq_b = jnp.broadcast_to(pr[:, O_SQ:O_SQ + SWA_W], (SWA_HEADS, SWA_W))
        q8 = jnp.zeros((SWA_HEADS, LANES), f32)
        for gq in range(SWA_GROUP):
            q8 = jnp.where(row_q % SWA_GROUP == gq, sq_b[:, gq * LANES:(gq + 1) * LANES], q8)
        q8s.append(jnp.where(own_kv, q8, 0.0).astype(bf16))
    kv_as = [lax.dot_general(lts[j], rts[j], contract_rows, preferred_element_type=f32) for j in seqs]
    inss = [lax.dot_general(lt2s[j], last_lane_rows, contract_rows, preferred_element_type=f32) for j in seqs]
    s_news, kts, vts = [], [], []
    for j in seqs:
        s_new = kv_as[j][:, GLA_DV:] * s_ref[j] + kv_as[j][:, :GLA_DV]
        so_ref[j] = s_new
        s_news.append(s_new.astype(bf16))
        kt = jnp.where(newest, inss[j][:SWA_KVW], pltpu.roll(ck_ref[j], WINDOW - 1, axis=1))
        vt = jnp.where(newest, inss[j][SWA_KVW:], pltpu.roll(cv_ref[j], WINDOW - 1, axis=1))
        cko_ref[j] = kt
        cvo_ref[j] = vt
        kts.append(kt.astype(bf16))
        vts.append(vt.astype(bf16))
    for j in seqs:
        og_ref[j] = jnp.dot(qms[j], s_news[j], preferred_element_type=f32)[:GLA_HEADS]
    scores = [jnp.dot(q8s[j], kts[j], preferred_element_type=f32) for j in seqs]
    es, dens = [], []
    for j in seqs:
        m = jnp.maximum(jnp.max(scores[j], axis=-1, keepdims=True), sink)
        e = jnp.exp2(scores[j] - m)
        dens.append(jnp.sum(e, axis=-1, keepdims=True) + jnp.exp2(sink - m))
        es.append(e.astype(bf16))
    for j in seqs:
        o = lax.dot_general(es[j], vts[j], (((1,), (1,)), ((), ())), preferred_element_type=f32)
        os_ref[j] = o / dens[j]


def _sample_state_call(proj, decay, sinks_col, state, ck, cv):
    n = state.shape[0]
    g = SAMPLE_GROUP
    blk = lambda *tail: pl.BlockSpec((g,) + tail, lambda i: (i,) + (0,) * len(tail))
    cache = blk(SWA_KVW, WINDOW)
    return pl.pallas_call(
        _sample_state_kernel,
        grid=(n // g,),
        in_specs=[blk(W_MAIN), blk(GLA_KW), pl.BlockSpec((SWA_HEADS, 1), lambda i: (0, 0)),
                  blk(GLA_KW, GLA_DV), cache, cache],
        out_specs=[blk(GLA_KW, GLA_DV), cache, cache, blk(GLA_HEADS, GLA_DV), blk(SWA_HEADS, LANES)],
        out_shape=[
            jax.ShapeDtypeStruct(state.shape, f32), jax.ShapeDtypeStruct(ck.shape, f32),
            jax.ShapeDtypeStruct(cv.shape, f32),
            jax.ShapeDtypeStruct((n, GLA_HEADS, GLA_DV), f32), jax.ShapeDtypeStruct((n, SWA_HEADS, LANES), f32),
        ],
        compiler_params=pltpu.CompilerParams(dimension_semantics=("arbitrary",), vmem_limit_bytes=VMEM_LIMIT),
        name="sample_state",
    )(proj, decay, sinks_col, state, ck, cv)


def _sample_merge_kernel(x_ref, og_ref, gg_ref, os_ref, sg_ref, gn_ref, wout_ref, nf_ref, y_ref):
    y_ref[...] = _merge(x_ref[...], og_ref[...], gg_ref[...], os_ref[...], sg_ref[...],
                        gn_ref, wout_ref, nf_ref)


def _sample_merge_call(xs, og, gg, osw, sg, w):
    return pl.pallas_call(
        _sample_merge_kernel,
        out_shape=jax.ShapeDtypeStruct(xs.shape, f32),
        compiler_params=pltpu.CompilerParams(vmem_limit_bytes=VMEM_LIMIT),
        name="sample_merge",
    )(xs, og, gg, osw, sg, w["gla_norm"], w["w_out"], w["norm_f"])


def _sample_path(x_sample, state, ck_t, cv_t, w):
    n = x_sample.shape[0]
    xs = x_sample.reshape(n, D_MODEL)
    proj, decay = _sample_proj_call(xs, w)
    s_new, ck_new, cv_new, og, os_raw = _sample_state_call(
        proj, decay, w["sinks"].reshape(SWA_HEADS, 1), state, ck_t, cv_t)
    os5 = os_raw.reshape(n, SWA_KV, SWA_GROUP, SWA_KV, SWA_HD)
    os_il = jnp.stack([os5[:, kv, :, kv, :] for kv in range(SWA_KV)], axis=2).reshape(n, SWA_W)
    y = _sample_merge_call(xs, og.reshape(n, GLA_W), proj[:, O_GG:O_GG + GLA_W], os_il,
                           proj[:, O_SG:O_SG + SWA_W], w)
    return y, s_new, ck_new, cv_new


def _cache_view(c):
    n = c.shape[1]
    return jnp.transpose(c[0], (0, 2, 3, 1)).reshape(n, SWA_KVW, WINDOW)


def _cache_unview(c):
    n = c.shape[0]
    return jnp.transpose(c.reshape(n, SWA_KV, SWA_HD, WINDOW), (0, 3, 1, 2))[None]


def kernel(x_prompt, x_sample, state_gla, cache_win_k, cache_win_v, norm_in, w_in, w_gate_up, b_gate,
           gla_norm, attn_sinks, w_out, norm_f):
    bsz = x_prompt.shape[0]
    n = x_sample.shape[0]
    w = _prep_weights(norm_in[0], w_in[0], w_gate_up[0], b_gate[0], gla_norm[0], attn_sinks[0],
                      w_out[0], norm_f)
    cmat, lmask = _chunk_tables()
    y_p, s_p, k_p, v_p = _prompt_call(x_prompt, w, cmat, lmask, *_swa_mask_tables())
    y_s, s_s, k_s, v_s = _sample_path(x_sample, state_gla[0].reshape(n, GLA_KW, GLA_DV),
                                      _cache_view(cache_win_k), _cache_view(cache_win_v), w)
    return (y_p, y_s.reshape(n, 1, D_MODEL),
            s_p.reshape(1, bsz, GLA_HEADS, GLA_DK, GLA_DV),
            _cache_unview(k_p), _cache_unview(v_p),
            s_s.reshape(1, n, GLA_HEADS, GLA_DK, GLA_DV),
            _cache_unview(k_s), _cache_unview(v_s))
```

```python
import functools

import numpy as np
import jax
import jax.numpy as jnp
from jax import lax
from jax.experimental import pallas as pl
from jax.experimental.pallas import tpu as pltpu

D_MODEL = 1024
GLA_HEADS = 4
GLA_DK = 64
GLA_DV = 128
GLA_KW = GLA_HEADS * GLA_DK
GLA_W = GLA_HEADS * GLA_DV
GLA_RANK = 16
GLA_TAU = 16.0
CHUNK = 64
SWA_HEADS = 8
SWA_HD = 64
SWA_KV = 2
SWA_GROUP = SWA_HEADS // SWA_KV
SWA_W = SWA_HEADS * SWA_HD
SWA_KVW = SWA_KV * SWA_HD
WINDOW = 128
EPS = 1e-6
NEG_INF = -1e30
LOG2E = 1.4426950408889634
LANES = 128

O_Q, O_K, O_V, O_GG = 0, 256, 512, 1024
O_SQ, O_SK, O_SV, O_SG, O_LOW = 1536, 2048, 2176, 2304, 2816
W_MAIN = 2816
W_ALL = W_MAIN + LANES

N_LEVELS = 6
TOK_BLOCK = 1024
GLA_GROUP = 4
SAMPLE_GROUP = 16
PACK = 16
VMEM_LIMIT = 56 * 1024 * 1024

f32 = jnp.float32
bf16 = jnp.bfloat16


def _chunk_tables():
    c = CHUNK
    t = np.arange(c)[None, :]
    i = np.arange(c)[:, None]
    blocks = [(t <= i), (t > i)]
    masks = []
    for l in range(N_LEVELS):
        h = c >> (l + 1)
        m = (i // (2 * h)) * (2 * h) + h
        upper = i >= m
        blocks.append(np.where(upper, (t > m) & (t <= i), (t > i) & (t <= m)))
        jj = np.arange(c)[None, :]
        masks.append((i // (2 * h) == jj // (2 * h)) & (i % (2 * h) >= h) & (jj % (2 * h) < h))
    masks.append(np.eye(c, dtype=bool))
    cm = np.concatenate(blocks, axis=0).astype(np.float32)
    cm3 = np.concatenate([cm, cm, cm], axis=1)
    lm = np.stack(masks).astype(np.float32)
    lm = np.concatenate([lm, lm], axis=1)
    return jnp.asarray(cm3, dtype=bf16), jnp.asarray(lm, dtype=f32)


def _swa_mask_tables():
    key = np.arange(2 * WINDOW)[:, None]
    qi = np.arange(WINDOW)[None, :]
    prev_ok = (key < WINDOW) & (key > qi)
    cur_ok = (key >= WINDOW) & (key - WINDOW <= qi)
    full = np.where(prev_ok | cur_ok, 0.0, NEG_INF)
    first = np.where(cur_ok, 0.0, NEG_INF)
    onehot = (np.arange(2 * WINDOW)[:, None] % WINDOW == qi).astype(np.float32)
    return jnp.asarray(np.stack([full, first]), dtype=bf16), jnp.asarray(onehot, dtype=bf16)


I_Q, I_K, I_V, I_GG, I_LOW, I_SQ, I_SK, I_SV, I_SG = (
    int(v) for v in np.cumsum([0, 256, 256, 512, 512, 16, 512, 128, 128])[:9])


def _interleaved_rows(base, tile):
    return [(base + kv * SWA_GROUP * SWA_HD + tile * SWA_HD, SWA_HD) for kv in range(SWA_KV)]


def _weight_layout_kernel(wt_ref, wo_ref, wall_ref, wop_ref):
    def put(col, pieces, scale=None):
        rows = [wt_ref[r:r + n, :] for r, n in pieces]
        missing = LANES - sum(n for _, n in pieces)
        if missing:
            rows.append(jnp.zeros((missing, D_MODEL), f32))
        blk = jnp.concatenate(rows, axis=0) if len(rows) > 1 else rows[0]
        if scale is not None:
            blk = blk * scale
        wall_ref[:, col:col + LANES] = blk.T.astype(bf16)

    for i in range(GLA_KW // LANES):
        put(O_Q + i * LANES, [(I_Q + i * LANES, LANES)], GLA_DK ** -0.5)
        put(O_K + i * LANES, [(I_K + i * LANES, LANES)])
    for i in range(GLA_W // LANES):
        put(O_V + i * LANES, [(I_V + i * LANES, LANES)])
        put(O_GG + i * LANES, [(I_GG + i * LANES, LANES)])
        put(O_SQ + i * LANES, _interleaved_rows(I_SQ, i), SWA_HD ** -0.5 * LOG2E)
        put(O_SG + i * LANES, _interleaved_rows(I_SG, i))
    put(O_SK, [(I_SK, SWA_KVW)])
    put(O_SV, [(I_SV, SWA_KVW)])
    put(O_LOW, [(I_LOW, GLA_RANK)])
    wop_ref[:GLA_W, :] = wo_ref[:GLA_W, :].astype(bf16)
    for i in range(SWA_GROUP):
        for kv, (r, n) in enumerate(_interleaved_rows(GLA_W, i)):
            dst = GLA_W + (i * SWA_KV + kv) * SWA_HD
            wop_ref[dst:dst + n, :] = wo_ref[r:r + n, :].astype(bf16)


def _prep_weights(norm_in, w_in, w_gate_up, b_gate, gla_norm, attn_sinks, w_out, norm_f):
    w_all, w_out_p = pl.pallas_call(
        _weight_layout_kernel,
        out_shape=[jax.ShapeDtypeStruct((D_MODEL, W_ALL), bf16), jax.ShapeDtypeStruct((D_MODEL, D_MODEL), bf16)],
        compiler_params=pltpu.CompilerParams(vmem_limit_bytes=VMEM_LIMIT),
        name="weight_layout",
    )(w_in.T, w_out)
    w_up = jnp.pad(w_gate_up, ((0, LANES - GLA_RANK), (0, 0))).astype(bf16)
    return dict(
        norm_in=norm_in.reshape(1, D_MODEL), w_all=w_all, w_up=w_up,
        b_gate=b_gate.reshape(1, GLA_KW), gla_norm=jnp.tile(gla_norm, GLA_HEADS).reshape(1, GLA_W),
        sinks=attn_sinks * LOG2E, w_out=w_out_p, norm_f=norm_f.reshape(1, D_MODEL))


def _rms(x, gain):
    return x * lax.rsqrt(jnp.mean(x * x, axis=-1, keepdims=True) + EPS) * gain


def _log_decay(glow, wup_ref, bg_ref):
    z = jnp.dot(glow.astype(bf16), wup_ref[...], preferred_element_type=f32) + bg_ref[...]
    return (jnp.minimum(z, 0.0) - jnp.log(1.0 + jnp.exp(-jnp.abs(z)))) * (1.0 / GLA_TAU)


def _silu(x):
    return x * jax.nn.sigmoid(x)


def _merge(x, o_gla, gg, o_swa, sg, gn_ref, wout_ref, nf_ref):
    parts = []
    for h in range(GLA_HEADS):
        sl = slice(h * GLA_DV, (h + 1) * GLA_DV)
        parts.append(_rms(o_gla[:, sl], gn_ref[:, sl]) * _silu(gg[:, sl]))
    parts.append(o_swa * _silu(sg))
    um = jnp.concatenate(parts, axis=1).astype(bf16)
    hres = x + jnp.dot(um, wout_ref[...], preferred_element_type=f32)
    return _rms(hres, nf_ref[...])


def _prompt_kernel(sinks_ref, x_ref, nin_ref, w_ref, wup_ref, bg_ref, cmat_ref, lmask_ref, kmask_ref, qhot_ref,
                   gn_ref, wout_ref, nf_ref,
                   y_ref, sp_ref, kn_ref, vn_ref,
                   s_ref, kprev_ref, vprev_ref, p_s, g_s, ogla_s, oswa_s, *, n_t):
    t = pl.program_id(1)
    tl = x_ref.shape[0]

    @pl.when(t == 0)
    def _():
        s_ref[...] = jnp.zeros_like(s_ref)
        kprev_ref[...] = jnp.zeros_like(kprev_ref)
        vprev_ref[...] = jnp.zeros_like(vprev_ref)

    def project(rg):
        u = _rms(x_ref[rg, :], nin_ref[...]).astype(bf16)
        p_s[rg, :] = jnp.dot(u, w_ref[...], preferred_element_type=f32)
        g_s[rg, :] = _log_decay(p_s[rg, O_LOW:O_LOW + LANES], wup_ref, bg_ref)

    lane_lo = lax.broadcasted_iota(jnp.int32, (CHUNK, LANES), 1) < GLA_DK

    n_pairs = GLA_HEADS // 2

    def group_terms(chunks):
        rows = {c: slice(c * CHUNK, (c + 1) * CHUNK) for c in chunks}
        sums = {}
        for c in chunks:
            gc = g_s[rows[c], :]
            g_hi = gc.astype(bf16)
            r1 = gc - g_hi.astype(f32)
            g_mid = r1.astype(bf16)
            g_lo = (r1 - g_mid.astype(f32)).astype(bf16)
            sums[c] = jnp.dot(cmat_ref[...], jnp.concatenate([g_hi, g_mid, g_lo], axis=0),
                              preferred_element_type=f32)
        level_ops, misc = {}, {}
        for c in chunks:
            qc, kc = p_s[rows[c], O_Q:O_Q + GLA_KW], p_s[rows[c], O_K:O_K + GLA_KW]
            e_b = jnp.exp(sums[c][0:CHUNK])
            misc[c] = (e_b, qc * e_b, kc * jnp.exp(sums[c][CHUNK:2 * CHUNK]),
                       p_s[rows[c], O_V:O_V + GLA_W].astype(bf16))
            for p in range(n_pairs):
                ln = slice(p * LANES, (p + 1) * LANES)
                qp, kp = qc[:, ln], kc[:, ln]
                for l in range(N_LEVELS + 1):
                    if l < N_LEVELS:
                        e = jnp.exp(sums[c][(2 + l) * CHUNK:(3 + l) * CHUNK, ln])
                        qe, ke = (qp * e).astype(bf16), (kp * e).astype(bf16)
                    else:
                        qe, ke = qp.astype(bf16), kp.astype(bf16)
                    lhs = jnp.concatenate([jnp.where(lane_lo, qe, 0), jnp.where(lane_lo, 0, qe)], axis=0)
                    level_ops[c, p, l] = (lhs, ke)
        scores = {key: lax.dot_general(lhs, ke, (((1,), (1,)), ((), ())), preferred_element_type=f32)
                  for key, (lhs, ke) in level_ops.items()}
        terms = {}
        for c in chunks:
            e_b, qb, k_suf, vb = misc[c]
            terms[c] = []
            for p in range(n_pairs):
                ln = slice(p * LANES, (p + 1) * LANES)
                a = scores[c, p, 0] * lmask_ref[0]
                for l in range(1, N_LEVELS + 1):
                    a = a + scores[c, p, l] * lmask_ref[l]
                ab = a.astype(bf16)
                qbp = qb[:, ln].astype(bf16)
                lhs_heads = []
                for hh in range(2):
                    qbm = jnp.where(lane_lo, qbp, 0) if hh == 0 else jnp.where(lane_lo, 0, qbp)
                    lhs_heads.append(jnp.concatenate([qbm, ab[hh * CHUNK:(hh + 1) * CHUNK, :]], axis=1))
                upd = lax.dot_general(k_suf[:, ln].astype(bf16), vb[:, p * 2 * GLA_DV:(p + 1) * 2 * GLA_DV],
                                      (((0,), (0,)), ((), ())), preferred_element_type=f32)
                upd = jnp.concatenate(
                    [upd[0:GLA_DK, 0:GLA_DV], upd[GLA_DK:2 * GLA_DK, GLA_DV:2 * GLA_DV]], axis=0)
                e_col = jnp.broadcast_to(e_b[CHUNK - 1:CHUNK, ln], (LANES, LANES)).T
                terms[c].append((lhs_heads, vb, upd, e_col))
        return terms

    def gla(chunks):
        terms = {}
        for c0 in range(chunks[0], chunks[-1] + 1, GLA_GROUP):
            terms.update(group_terms(range(c0, c0 + GLA_GROUP)))
        states = {}
        for p in range(n_pairs):
            ln = slice(p * LANES, (p + 1) * LANES)
            s_pair = s_ref[ln, :]
            for c in chunks:
                states[c, p] = s_pair.astype(bf16)
                _, _, upd, e_col = terms[c][p]
                s_pair = e_col * s_pair + upd
            s_ref[ln, :] = s_pair
        for c in chunks:
            for p in range(n_pairs):
                lhs_heads, vb, _, _ = terms[c][p]
                for hh in range(2):
                    h = 2 * p + hh
                    rhs = jnp.concatenate([states[c, p], vb[:, h * GLA_DV:(h + 1) * GLA_DV]], axis=0)
                    ogla_s[c * CHUNK:(c + 1) * CHUNK, h * GLA_DV:(h + 1) * GLA_DV] = jnp.dot(
                        lhs_heads[hh], rhs, preferred_element_type=f32)

    lane_lo_w = lax.broadcasted_iota(jnp.int32, (WINDOW, LANES), 1) < SWA_HD
    row_lo = lax.broadcasted_iota(jnp.int32, (2 * WINDOW, 1), 0) < WINDOW

    def swa(blocks):
        for blk in blocks:
            rs = slice(blk * WINDOW, (blk + 1) * WINDOW)
            sq = p_s[rs, O_SQ:O_SQ + SWA_W].astype(bf16)
            k_cur, v_cur = p_s[rs, O_SK:O_SK + SWA_KVW], p_s[rs, O_SV:O_SV + SWA_KVW]
            k2 = jnp.concatenate([kprev_ref[...], k_cur], axis=0).astype(bf16)
            v2 = jnp.concatenate([vprev_ref[...], v_cur], axis=0).astype(bf16)
            kmask = kmask_ref[jnp.where(t > 0, 0, 1)] if blk == 0 else kmask_ref[0]
            k2m = jnp.concatenate([k2, kmask], axis=1)
            for tt in range(SWA_GROUP):
                qt = sq[:, tt * LANES:(tt + 1) * LANES]
                lhs = jnp.concatenate([jnp.where(lane_lo_w, qt, 0), jnp.where(lane_lo_w, 0, qt)], axis=0)
                lhs = jnp.concatenate([lhs, qhot_ref[...]], axis=1)
                s = lax.dot_general(lhs, k2m, (((1,), (1,)), ((), ())), preferred_element_type=f32)
                sink = jnp.where(row_lo, sinks_ref[tt], sinks_ref[SWA_GROUP + tt])
                m = jnp.maximum(jnp.max(s, axis=-1, keepdims=True), sink)
                e = jnp.exp2(s - m)
                r = 1.0 / (jnp.sum(e, axis=-1, keepdims=True) + jnp.exp2(sink - m))
                o2 = jnp.dot(e.astype(bf16), v2, preferred_element_type=f32)
                oswa_s[rs, tt * LANES:(tt + 1) * LANES] = jnp.where(
                    lane_lo_w, o2[:WINDOW] * r[:WINDOW], o2[WINDOW:] * r[WINDOW:])
            kprev_ref[...] = k_cur
            vprev_ref[...] = v_cur

    def merge(rg):
        y_ref[rg, :] = _merge(x_ref[rg, :], ogla_s[rg, :], p_s[rg, O_GG:O_GG + GLA_W], oswa_s[rg, :],
                              p_s[rg, O_SG:O_SG + SWA_W], gn_ref, wout_ref, nf_ref)

    whole = slice(0, tl)
    project(whole)
    gla(range(tl // CHUNK))
    swa(range(tl // WINDOW))
    merge(whole)

    @pl.when(t == n_t - 1)
    def _():
        sp_ref[...] = s_ref[...]
        kn_ref[...] = kprev_ref[...].T
        vn_ref[...] = vprev_ref[...].T


def _prompt_call(x, w, cmat, lmask, kmask, qhot):
    bsz, seq, _ = x.shape
    tl = TOK_BLOCK
    n_t = seq // tl
    const = lambda shape: pl.BlockSpec(shape, lambda b, t: (0,) * len(shape), pipeline_mode=pl.Buffered(1))
    return pl.pallas_call(
        functools.partial(_prompt_kernel, n_t=n_t),
        grid=(bsz, n_t),
        in_specs=[
            pl.BlockSpec(memory_space=pltpu.SMEM),
            pl.BlockSpec((None, tl, D_MODEL), lambda b, t: (b, t, 0)),
            const((1, D_MODEL)), const((D_MODEL, W_ALL)), const((LANES, GLA_KW)), const((1, GLA_KW)),
            const(cmat.shape), const(lmask.shape), const(kmask.shape), const(qhot.shape),
            const((1, GLA_W)), const((D_MODEL, D_MODEL)), const((1, D_MODEL)),
        ],
        out_specs=[
            pl.BlockSpec((None, tl, D_MODEL), lambda b, t: (b, t, 0)),
            pl.BlockSpec((None, GLA_KW, GLA_DV), lambda b, t: (b, 0, 0)),
            pl.BlockSpec((None, WINDOW, SWA_KVW), lambda b, t: (b, 0, 0)),
            pl.BlockSpec((None, WINDOW, SWA_KVW), lambda b, t: (b, 0, 0)),
        ],
        out_shape=[
            jax.ShapeDtypeStruct((bsz, seq, D_MODEL), f32),
            jax.ShapeDtypeStruct((bsz, GLA_KW, GLA_DV), f32),
            jax.ShapeDtypeStruct((bsz, WINDOW, SWA_KVW), f32),
            jax.ShapeDtypeStruct((bsz, WINDOW, SWA_KVW), f32),
        ],
        scratch_shapes=[
            pltpu.VMEM((GLA_KW, GLA_DV), f32),
            pltpu.VMEM((WINDOW, SWA_KVW), f32), pltpu.VMEM((WINDOW, SWA_KVW), f32),
            pltpu.VMEM((tl, W_ALL), f32), pltpu.VMEM((tl, GLA_KW), f32),
            pltpu.VMEM((tl, GLA_W), f32), pltpu.VMEM((tl, SWA_W), f32),
        ],
        compiler_params=pltpu.CompilerParams(
            dimension_semantics=("arbitrary", "arbitrary"), vmem_limit_bytes=VMEM_LIMIT),
        name="prompt_layer",
    )(w["sinks"], x, w["norm_in"], w["w_all"], w["w_up"], w["b_gate"], cmat, lmask, kmask, qhot,
      w["gla_norm"], w["w_out"], w["norm_f"])


def _sample_proj_kernel(x_ref, nin_ref, w_ref, wup_ref, bg_ref, proj_ref, decay_ref):
    u = _rms(x_ref[...], nin_ref[...]).astype(bf16)
    proj_ref[...] = jnp.dot(u, w_ref[:, :W_MAIN], preferred_element_type=f32)
    glow = jnp.dot(u, w_ref[:, O_LOW:O_LOW + LANES], preferred_element_type=f32)
    decay_ref[...] = jnp.exp(_log_decay(glow, wup_ref, bg_ref))


def _sample_proj_call(xs, w):
    n = xs.shape[0]
    return pl.pallas_call(
        _sample_proj_kernel,
        out_shape=[jax.ShapeDtypeStruct((n, W_MAIN), f32), jax.ShapeDtypeStruct((n, GLA_KW), f32)],
        compiler_params=pltpu.CompilerParams(vmem_limit_bytes=VMEM_LIMIT),
        name="sample_proj",
    )(xs, w["norm_in"], w["w_all"], w["w_up"], w["b_gate"])


def _split3(x):
    as_bf16 = lambda v: v.astype(bf16).astype(f32)
    hi = as_bf16(x)
    mid = as_bf16(x - hi)
    return hi, mid, as_bf16(x - hi - mid)


def _sample_state_kernel(proj_ref, decay_ref, sink_ref, s_ref, ck_ref, cv_ref,
                         so_ref, cko_ref, cvo_ref, og_ref, os_ref):
    row = lax.broadcasted_iota(jnp.int32, (PACK, GLA_KW), 0)
    head_of_lane = lax.broadcasted_iota(jnp.int32, (PACK, GLA_KW), 1) // GLA_DK
    own_head = head_of_lane == row
    row_v = lax.broadcasted_iota(jnp.int32, (PACK, GLA_DV), 0)
    lane_v = lax.broadcasted_iota(jnp.int32, (PACK, GLA_DV), 1)
    piece_rows = ((row_v >= GLA_HEADS) & (row_v < GLA_HEADS + 3)).astype(f32)
    last_lane_rows = ((row_v < 3) & (lane_v == WINDOW - 1)).astype(bf16)
    newest = lax.broadcasted_iota(jnp.int32, (SWA_KVW, WINDOW), 1) == WINDOW - 1
    row_q = lax.broadcasted_iota(jnp.int32, (SWA_HEADS, LANES), 0)
    own_kv = (lax.broadcasted_iota(jnp.int32, (SWA_HEADS, LANES), 1) // SWA_HD) == (row_q // SWA_GROUP)
    sink = sink_ref[...]
    contract_rows = (((0,), (0,)), ((), ()))
    seqs = range(SAMPLE_GROUP)
    lts, rts, qms, lt2s, q8s = [], [], [], [], []
    for j in seqs:
        pr = proj_ref[j:j + 1, :]
        bcast = lambda lo, width: jnp.broadcast_to(pr[:, lo:lo + width], (PACK, width))
        a_hi, a_mid, a_lo = _split3(jnp.broadcast_to(decay_ref[j:j + 1, :], (PACK, GLA_KW)))
        a_piece = jnp.where(row == GLA_HEADS, a_hi, jnp.where(row == GLA_HEADS + 1, a_mid, a_lo))
        lts.append(jnp.where(own_head, bcast(O_K, GLA_KW),
                             jnp.where((row >= GLA_HEADS) & (row < GLA_HEADS + 3), a_piece, 0.0)).astype(bf16))
        v_b = bcast(O_V, GLA_W)
        v_sel = jnp.zeros((PACK, GLA_DV), f32)
        for h in range(GLA_HEADS):
            v_sel = jnp.where(row_v == h, v_b[:, h * GLA_DV:(h + 1) * GLA_DV], v_sel)
        rts.append(jnp.concatenate([v_sel, piece_rows], axis=1).astype(bf16))
        qms.append(jnp.where(own_head, bcast(O_Q, GLA_KW), 0.0).astype(bf16))
        n_hi, n_mid, n_lo = _split3(bcast(O_SK, 2 * SWA_KVW))
        lt2s.append(jnp.where(row == 0, n_hi, jnp.where(row == 1, n_mid,
                                                        jnp.where(row == 2, n_lo, 0.0))).astype(bf16))
        sq_b = jnp.broadcast_to(pr[:, O_SQ:O_SQ + SWA_W], (SWA_HEADS, SWA_W))
        q8 = jnp.zeros((SWA_HEADS, LANES), f32)
        for gq in range(SWA_GROUP):
            q8 = jnp.where(row_q % SWA_GROUP == gq, sq_b[:, gq * LANES:(gq + 1) * LANES], q8)
        q8s.append(jnp.where(own_kv, q8, 0.0).astype(bf16))
    kv_as = [lax.dot_general(lts[j], rts[j], contract_rows, preferred_element_type=f32) for j in seqs]
    inss = [lax.dot_general(lt2s[j], last_lane_rows, contract_rows, preferred_element_type=f32) for j in seqs]
    s_news, kts, vts = [], [], []
    for j in seqs:
        s_new = kv_as[j][:, GLA_DV:] * s_ref[j] + kv_as[j][:, :GLA_DV]
        so_ref[j] = s_new
        s_news.append(s_new.astype(bf16))
        kt = jnp.where(newest, inss[j][:SWA_KVW], pltpu.roll(ck_ref[j], WINDOW - 1, axis=1))
        vt = jnp.where(newest, inss[j][SWA_KVW:], pltpu.roll(cv_ref[j], WINDOW - 1, axis=1))
        cko_ref[j] = kt
        cvo_ref[j] = vt
        kts.append(kt.astype(bf16))
        vts.append(vt.astype(bf16))
    for j in seqs:
        og_ref[j] = jnp.dot(qms[j], s_news[j], preferred_element_type=f32)[:GLA_HEADS]
    scores = [jnp.dot(q8s[j], kts[j], preferred_element_type=f32) for j in seqs]
    es, dens = [], []
    for j in seqs:
        m = jnp.maximum(jnp.max(scores[j], axis=-1, keepdims=True), sink)
        e = jnp.exp2(scores[j] - m)
        dens.append(jnp.sum(e, axis=-1, keepdims=True) + jnp.exp2(sink - m))
        es.append(e.astype(bf16))
    for j in seqs:
        o = lax.dot_general(es[j], vts[j], (((1,), (1,)), ((), ())), preferred_element_type=f32)
        os_ref[j] = o / dens[j]


def _sample_state_call(proj, decay, sinks_col, state, ck, cv):
    n = state.shape[0]
    g = SAMPLE_GROUP
    blk = lambda *tail: pl.BlockSpec((g,) + tail, lambda i: (i,) + (0,) * len(tail))
    cache = blk(SWA_KVW, WINDOW)
    return pl.pallas_call(
        _sample_state_kernel,
        grid=(n // g,),
        in_specs=[blk(W_MAIN), blk(GLA_KW), pl.BlockSpec((SWA_HEADS, 1), lambda i: (0, 0)),
                  blk(GLA_KW, GLA_DV), cache, cache],
        out_specs=[blk(GLA_KW, GLA_DV), cache, cache, blk(GLA_HEADS, GLA_DV), blk(SWA_HEADS, LANES)],
        out_shape=[
            jax.ShapeDtypeStruct(state.shape, f32), jax.ShapeDtypeStruct(ck.shape, f32),
            jax.ShapeDtypeStruct(cv.shape, f32),
            jax.ShapeDtypeStruct((n, GLA_HEADS, GLA_DV), f32), jax.ShapeDtypeStruct((n, SWA_HEADS, LANES), f32),
        ],
        compiler_params=pltpu.CompilerParams(dimension_semantics=("arbitrary",), vmem_limit_bytes=VMEM_LIMIT),
        name="sample_state",
    )(proj, decay, sinks_col, state, ck, cv)


def _sample_merge_kernel(x_ref, og_ref, gg_ref, os_ref, sg_ref, gn_ref, wout_ref, nf_ref, y_ref):
    y_ref[...] = _merge(x_ref[...], og_ref[...], gg_ref[...], os_ref[...], sg_ref[...],
                        gn_ref, wout_ref, nf_ref)


def _sample_merge_call(xs, og, gg, osw, sg, w):
    return pl.pallas_call(
        _sample_merge_kernel,
        out_shape=jax.ShapeDtypeStruct(xs.shape, f32),
        compiler_params=pltpu.CompilerParams(vmem_limit_bytes=VMEM_LIMIT),
        name="sample_merge",
    )(xs, og, gg, osw, sg, w["gla_norm"], w["w_out"], w["norm_f"])


def _sample_path(x_sample, state, ck_t, cv_t, w):
    n = x_sample.shape[0]
    xs = x_sample.reshape(n, D_MODEL)
    proj, decay = _sample_proj_call(xs, w)
    s_new, ck_new, cv_new, og, os_raw = _sample_state_call(
        proj, decay, w["sinks"].reshape(SWA_HEADS, 1), state, ck_t, cv_t)
    os5 = os_raw.reshape(n, SWA_KV, SWA_GROUP, SWA_KV, SWA_HD)
    os_il = jnp.stack([os5[:, kv, :, kv, :] for kv in range(SWA_KV)], axis=2).reshape(n, SWA_W)
    y = _sample_merge_call(xs, og.reshape(n, GLA_W), proj[:, O_GG:O_GG + GLA_W], os_il,
                           proj[:, O_SG:O_SG + SWA_W], w)
    return y, s_new, ck_new, cv_new


def _cache_view(c):
    n = c.shape[1]
    return jnp.transpose(c[0], (0, 2, 3, 1)).reshape(n, SWA_KVW, WINDOW)


def _cache_unview(c):
    n = c.shape[0]
    return jnp.transpose(c.reshape(n, SWA_KV, SWA_HD, WINDOW), (0, 3, 1, 2))[None]


def kernel(x_prompt, x_sample, state_gla, cache_win_k, cache_win_v, norm_in, w_in, w_gate_up, b_gate,
           gla_norm, attn_sinks, w_out, norm_f):
    bsz = x_prompt.shape[0]
    n = x_sample.shape[0]
    w = _prep_weights(norm_in[0], w_in[0], w_gate_up[0], b_gate[0], gla_norm[0], attn_sinks[0],
                      w_out[0], norm_f)
    cmat, lmask = _chunk_tables()
    y_p, s_p, k_p, v_p = _prompt_call(x_prompt, w, cmat, lmask, *_swa_mask_tables())
    y_s, s_s, k_s, v_s = _sample_path(x_sample, state_gla[0].reshape(n, GLA_KW, GLA_DV),
                                      _cache_view(cache_win_k), _cache_view(cache_win_v), w)
    return (y_p, y_s.reshape(n, 1, D_MODEL),
            s_p.reshape(1, bsz, GLA_HEADS, GLA_DK, GLA_DV),
            _cache_unview(k_p), _cache_unview(v_p),
            s_s.reshape(1, n, GLA_HEADS, GLA_DK, GLA_DV),
            _cache_unview(k_s), _cache_unview(v_s))
```

```python
import functools

import numpy as np
import jax
import jax.numpy as jnp
from jax import lax
from jax.experimental import pallas as pl
from jax.experimental.pallas import tpu as pltpu

D_MODEL = 1024
GLA_HEADS = 4
GLA_DK = 64
GLA_DV = 128
GLA_KW = GLA_HEADS * GLA_DK
GLA_W = GLA_HEADS * GLA_DV
GLA_RANK = 16
GLA_TAU = 16.0
CHUNK = 64
SWA_HEADS = 8
SWA_HD = 64
SWA_KV = 2
SWA_GROUP = SWA_HEADS // SWA_KV
SWA_W = SWA_HEADS * SWA_HD
SWA_KVW = SWA_KV * SWA_HD
WINDOW = 128
EPS = 1e-6
NEG_INF = -1e30
LOG2E = 1.4426950408889634
LANES = 128

O_Q, O_K, O_V, O_GG = 0, 256, 512, 1024
O_SQ, O_SK, O_SV, O_SG, O_LOW = 1536, 2048, 2176, 2304, 2816
W_MAIN = 2816
W_ALL = W_MAIN + LANES

N_LEVELS = 6
TOK_BLOCK = 1024
GLA_GROUP = 4
SAMPLE_GROUP = 16
PACK = 16
PROMPT_VMEM_LIMIT = 56 * 1024 * 1024
VMEM_LIMIT = 32 * 1024 * 1024

f32 = jnp.float32
bf16 = jnp.bfloat16


def _chunk_tables():
    c = CHUNK
    t = np.arange(c)[None, :]
    i = np.arange(c)[:, None]
    blocks = [(t <= i), (t > i)]
    masks = []
    for l in range(N_LEVELS):
        h = c >> (l + 1)
        m = (i // (2 * h)) * (2 * h) + h
        upper = i >= m
        blocks.append(np.where(upper, (t > m) & (t <= i), (t > i) & (t <= m)))
        jj = np.arange(c)[None, :]
        masks.append((i // (2 * h) == jj // (2 * h)) & (i % (2 * h) >= h) & (jj % (2 * h) < h))
    masks.append(np.eye(c, dtype=bool))
    cm = np.concatenate(blocks, axis=0).astype(np.float32)
    cm3 = np.concatenate([cm, cm, cm], axis=1)
    lm = np.stack(masks).astype(np.float32)
    lm = np.concatenate([lm, lm], axis=1)
    return jnp.asarray(cm3, dtype=bf16), jnp.asarray(lm, dtype=f32)


def _swa_mask_tables():
    key = np.arange(2 * WINDOW)[:, None]
    qi = np.arange(WINDOW)[None, :]
    prev_ok = (key < WINDOW) & (key > qi)
    cur_ok = (key >= WINDOW) & (key - WINDOW <= qi)
    full = np.where(prev_ok | cur_ok, 0.0, NEG_INF)
    first = np.where(cur_ok, 0.0, NEG_INF)
    onehot = (np.arange(2 * WINDOW)[:, None] % WINDOW == qi).astype(np.float32)
    return jnp.asarray(np.stack([full, first]), dtype=bf16), jnp.asarray(onehot, dtype=bf16)


I_Q, I_K, I_V, I_GG, I_LOW, I_SQ, I_SK, I_SV, I_SG = (
    int(v) for v in np.cumsum([0, 256, 256, 512, 512, 16, 512, 128, 128])[:9])


def _interleaved_rows(base, tile):
    return [(base + kv * SWA_GROUP * SWA_HD + tile * SWA_HD, SWA_HD) for kv in range(SWA_KV)]


def _weight_layout_kernel(wt_ref, wo_ref, wall_ref, wop_ref):
    def put(col, pieces, scale=None):
        rows = [wt_ref[r:r + n, :] for r, n in pieces]
        missing = LANES - sum(n for _, n in pieces)
        if missing:
            rows.append(jnp.zeros((missing, D_MODEL), f32))
        blk = jnp.concatenate(rows, axis=0) if len(rows) > 1 else rows[0]
        if scale is not None:
            blk = blk * scale
        wall_ref[:, col:col + LANES] = blk.T.astype(bf16)

    for i in range(GLA_KW // LANES):
        put(O_Q + i * LANES, [(I_Q + i * LANES, LANES)], GLA_DK ** -0.5)
        put(O_K + i * LANES, [(I_K + i * LANES, LANES)])
    for i in range(GLA_W // LANES):
        put(O_V + i * LANES, [(I_V + i * LANES, LANES)])
        put(O_GG + i * LANES, [(I_GG + i * LANES, LANES)])
        put(O_SQ + i * LANES, _interleaved_rows(I_SQ, i), SWA_HD ** -0.5 * LOG2E)
        put(O_SG + i * LANES, _interleaved_rows(I_SG, i))
    put(O_SK, [(I_SK, SWA_KVW)])
    put(O_SV, [(I_SV, SWA_KVW)])
    put(O_LOW, [(I_LOW, GLA_RANK)])
    wop_ref[:GLA_W, :] = wo_ref[:GLA_W, :].astype(bf16)
    for i in range(SWA_GROUP):
        for kv, (r, n) in enumerate(_interleaved_rows(GLA_W, i)):
            dst = GLA_W + (i * SWA_KV + kv) * SWA_HD
            wop_ref[dst:dst + n, :] = wo_ref[r:r + n, :].astype(bf16)


def _prep_weights(norm_in, w_in, w_gate_up, b_gate, gla_norm, attn_sinks, w_out, norm_f):
    w_all, w_out_p = pl.pallas_call(
        _weight_layout_kernel,
        out_shape=[jax.ShapeDtypeStruct((D_MODEL, W_ALL), bf16), jax.ShapeDtypeStruct((D_MODEL, D_MODEL), bf16)],
        compiler_params=pltpu.CompilerParams(vmem_limit_bytes=VMEM_LIMIT),
        name="weight_layout",
    )(w_in.T, w_out)
    w_up = jnp.pad(w_gate_up, ((0, LANES - GLA_RANK), (0, 0))).astype(bf16)
    return dict(
        norm_in=norm_in.reshape(1, D_MODEL), w_all=w_all, w_up=w_up,
        b_gate=b_gate.reshape(1, GLA_KW), gla_norm=jnp.tile(gla_norm, GLA_HEADS).reshape(1, GLA_W),
        sinks=attn_sinks * LOG2E, w_out=w_out_p, norm_f=norm_f.reshape(1, D_MODEL))


def _rms(x, gain):
    return x * lax.rsqrt(jnp.mean(x * x, axis=-1, keepdims=True) + EPS) * gain


def _log_decay(glow, wup_ref, bg_ref):
    z = jnp.dot(glow.astype(bf16), wup_ref[...], preferred_element_type=f32) + bg_ref[...]
    return (jnp.minimum(z, 0.0) - jnp.log(1.0 + jnp.exp(-jnp.abs(z)))) * (1.0 / GLA_TAU)


def _silu(x):
    return x * jax.nn.sigmoid(x)


def _merge(x, o_gla, gg, o_swa, sg, gn_ref, wout_ref, nf_ref):
    parts = []
    for h in range(GLA_HEADS):
        sl = slice(h * GLA_DV, (h + 1) * GLA_DV)
        parts.append(_rms(o_gla[:, sl], gn_ref[:, sl]) * _silu(gg[:, sl]))
    parts.append(o_swa * _silu(sg))
    um = jnp.concatenate(parts, axis=1).astype(bf16)
    hres = x + jnp.dot(um, wout_ref[...], preferred_element_type=f32)
    return _rms(hres, nf_ref[...])


def _prompt_kernel(sinks_ref, x_ref, nin_ref, w_ref, wup_ref, bg_ref, cmat_ref, lmask_ref, kmask_ref, qhot_ref,
                   gn_ref, wout_ref, nf_ref,
                   y_ref, sp_ref, kn_ref, vn_ref,
                   s_ref, kprev_ref, vprev_ref, p_s, g_s, ogla_s, oswa_s, *, n_t):
    t = pl.program_id(1)
    tl = x_ref.shape[0]

    @pl.when(t == 0)
    def _():
        s_ref[...] = jnp.zeros_like(s_ref)
        kprev_ref[...] = jnp.zeros_like(kprev_ref)
        vprev_ref[...] = jnp.zeros_like(vprev_ref)

    def project(rg):
        u = _rms(x_ref[rg, :], nin_ref[...]).astype(bf16)
        p_s[rg, :] = jnp.dot(u, w_ref[...], preferred_element_type=f32)
        g_s[rg, :] = _log_decay(p_s[rg, O_LOW:O_LOW + LANES], wup_ref, bg_ref)

    lane_lo = lax.broadcasted_iota(jnp.int32, (CHUNK, LANES), 1) < GLA_DK

    n_pairs = GLA_HEADS // 2

    def group_terms(chunks):
        rows = {c: slice(c * CHUNK, (c + 1) * CHUNK) for c in chunks}
        sums = {}
        for c in chunks:
            gc = g_s[rows[c], :]
            g_hi = gc.astype(bf16)
            r1 = gc - g_hi.astype(f32)
            g_mid = r1.astype(bf16)
            g_lo = (r1 - g_mid.astype(f32)).astype(bf16)
            sums[c] = jnp.dot(cmat_ref[...], jnp.concatenate([g_hi, g_mid, g_lo], axis=0),
                              preferred_element_type=f32)
        level_ops, misc = {}, {}
        for c in chunks:
            qc, kc = p_s[rows[c], O_Q:O_Q + GLA_KW], p_s[rows[c], O_K:O_K + GLA_KW]
            e_b = jnp.exp(sums[c][0:CHUNK])
            misc[c] = (e_b, qc * e_b, kc * jnp.exp(sums[c][CHUNK:2 * CHUNK]),
                       p_s[rows[c], O_V:O_V + GLA_W].astype(bf16))
            for p in range(n_pairs):
                ln = slice(p * LANES, (p + 1) * LANES)
                qp, kp = qc[:, ln], kc[:, ln]
                for l in range(N_LEVELS + 1):
                    if l < N_LEVELS:
                        e = jnp.exp(sums[c][(2 + l) * CHUNK:(3 + l) * CHUNK, ln])
                        qe, ke = (qp * e).astype(bf16), (kp * e).astype(bf16)
                    else:
                        qe, ke = qp.astype(bf16), kp.astype(bf16)
                    lhs = jnp.concatenate([jnp.where(lane_lo, qe, 0), jnp.where(lane_lo, 0, qe)], axis=0)
                    level_ops[c, p, l] = (lhs, ke)
        scores = {key: lax.dot_general(lhs, ke, (((1,), (1,)), ((), ())), preferred_element_type=f32)
                  for key, (lhs, ke) in level_ops.items()}
        terms = {}
        for c in chunks:
            e_b, qb, k_suf, vb = misc[c]
            terms[c] = []
            for p in range(n_pairs):
                ln = slice(p * LANES, (p + 1) * LANES)
                a = scores[c, p, 0] * lmask_ref[0]
                for l in range(1, N_LEVELS + 1):
                    a = a + scores[c, p, l] * lmask_ref[l]
                ab = a.astype(bf16)
                qbp = qb[:, ln].astype(bf16)
                lhs_heads = []
                for hh in range(2):
                    qbm = jnp.where(lane_lo, qbp, 0) if hh == 0 else jnp.where(lane_lo, 0, qbp)
                    lhs_heads.append(jnp.concatenate([qbm, ab[hh * CHUNK:(hh + 1) * CHUNK, :]], axis=1))
                upd = lax.dot_general(k_suf[:, ln].astype(bf16), vb[:, p * 2 * GLA_DV:(p + 1) * 2 * GLA_DV],
                                      (((0,), (0,)), ((), ())), preferred_element_type=f32)
                upd = jnp.concatenate(
                    [upd[0:GLA_DK, 0:GLA_DV], upd[GLA_DK:2 * GLA_DK, GLA_DV:2 * GLA_DV]], axis=0)
                e_col = jnp.broadcast_to(e_b[CHUNK - 1:CHUNK, ln], (LANES, LANES)).T
                terms[c].append((lhs_heads, vb, upd, e_col))
        return terms

    def gla(chunks):
        terms = {}
        for c0 in range(chunks[0], chunks[-1] + 1, GLA_GROUP):
            terms.update(group_terms(range(c0, c0 + GLA_GROUP)))
        states = {}
        for p in range(n_pairs):
            ln = slice(p * LANES, (p + 1) * LANES)
            s_pair = s_ref[ln, :]
            for c in chunks:
                states[c, p] = s_pair.astype(bf16)
                _, _, upd, e_col = terms[c][p]
                s_pair = e_col * s_pair + upd
            s_ref[ln, :] = s_pair
        for c in chunks:
            for p in range(n_pairs):
                lhs_heads, vb, _, _ = terms[c][p]
                for hh in range(2):
                    h = 2 * p + hh
                    rhs = jnp.concatenate([states[c, p], vb[:, h * GLA_DV:(h + 1) * GLA_DV]], axis=0)
                    ogla_s[c * CHUNK:(c + 1) * CHUNK, h * GLA_DV:(h + 1) * GLA_DV] = jnp.dot(
                        lhs_heads[hh], rhs, preferred_element_type=f32)

    lane_lo_w = lax.broadcasted_iota(jnp.int32, (WINDOW, LANES), 1) < SWA_HD
    row_lo = lax.broadcasted_iota(jnp.int32, (2 * WINDOW, 1), 0) < WINDOW

    def swa(blocks):
        for blk in blocks:
            rs = slice(blk * WINDOW, (blk + 1) * WINDOW)
            sq = p_s[rs, O_SQ:O_SQ + SWA_W].astype(bf16)
            k_cur, v_cur = p_s[rs, O_SK:O_SK + SWA_KVW], p_s[rs, O_SV:O_SV + SWA_KVW]
            k2 = jnp.concatenate([kprev_ref[...], k_cur], axis=0).astype(bf16)
            v2 = jnp.concatenate([vprev_ref[...], v_cur], axis=0).astype(bf16)
            kmask = kmask_ref[jnp.where(t > 0, 0, 1)] if blk == 0 else kmask_ref[0]
            k2m = jnp.concatenate([k2, kmask], axis=1)
            for tt in range(SWA_GROUP):
                qt = sq[:, tt * LANES:(tt + 1) * LANES]
                lhs = jnp.concatenate([jnp.where(lane_lo_w, qt, 0), jnp.where(lane_lo_w, 0, qt)], axis=0)
                lhs = jnp.concatenate([lhs, qhot_ref[...]], axis=1)
                s = lax.dot_general(lhs, k2m, (((1,), (1,)), ((), ())), preferred_element_type=f32)
                sink = jnp.where(row_lo, sinks_ref[tt], sinks_ref[SWA_GROUP + tt])
                m = jnp.maximum(jnp.max(s, axis=-1, keepdims=True), sink)
                e = jnp.exp2(s - m)
                r = 1.0 / (jnp.sum(e, axis=-1, keepdims=True) + jnp.exp2(sink - m))
                o2 = jnp.dot(e.astype(bf16), v2, preferred_element_type=f32)
                oswa_s[rs, tt * LANES:(tt + 1) * LANES] = jnp.where(
                    lane_lo_w, o2[:WINDOW] * r[:WINDOW], o2[WINDOW:] * r[WINDOW:])
            kprev_ref[...] = k_cur
            vprev_ref[...] = v_cur

    def merge(rg):
        y_ref[rg, :] = _merge(x_ref[rg, :], ogla_s[rg, :], p_s[rg, O_GG:O_GG + GLA_W], oswa_s[rg, :],
                              p_s[rg, O_SG:O_SG + SWA_W], gn_ref, wout_ref, nf_ref)

    whole = slice(0, tl)
    project(whole)
    gla(range(tl // CHUNK))
    swa(range(tl // WINDOW))
    merge(whole)

    @pl.when(t == n_t - 1)
    def _():
        sp_ref[...] = s_ref[...]
        kn_ref[...] = kprev_ref[...].T
        vn_ref[...] = vprev_ref[...].T


def _prompt_call(x, w, cmat, lmask, kmask, qhot):
    bsz, seq, _ = x.shape
    tl = TOK_BLOCK
    n_t = seq // tl
    const = lambda shape: pl.BlockSpec(shape, lambda b, t: (0,) * len(shape), pipeline_mode=pl.Buffered(1))
    return pl.pallas_call(
        functools.partial(_prompt_kernel, n_t=n_t),
        grid=(bsz, n_t),
        in_specs=[
            pl.BlockSpec(memory_space=pltpu.SMEM),
            pl.BlockSpec((None, tl, D_MODEL), lambda b, t: (b, t, 0)),
            const((1, D_MODEL)), const((D_MODEL, W_ALL)), const((LANES, GLA_KW)), const((1, GLA_KW)),
            const(cmat.shape), const(lmask.shape), const(kmask.shape), const(qhot.shape),
            const((1, GLA_W)), const((D_MODEL, D_MODEL)), const((1, D_MODEL)),
        ],
        out_specs=[
            pl.BlockSpec((None, tl, D_MODEL), lambda b, t: (b, t, 0)),
            pl.BlockSpec((None, GLA_KW, GLA_DV), lambda b, t: (b, 0, 0)),
            pl.BlockSpec((None, WINDOW, SWA_KVW), lambda b, t: (b, 0, 0)),
            pl.BlockSpec((None, WINDOW, SWA_KVW), lambda b, t: (b, 0, 0)),
        ],
        out_shape=[
            jax.ShapeDtypeStruct((bsz, seq, D_MODEL), f32),
            jax.ShapeDtypeStruct((bsz, GLA_KW, GLA_DV), f32),
            jax.ShapeDtypeStruct((bsz, WINDOW, SWA_KVW), f32),
            jax.ShapeDtypeStruct((bsz, WINDOW, SWA_KVW), f32),
        ],
        scratch_shapes=[
            pltpu.VMEM((GLA_KW, GLA_DV), f32),
            pltpu.VMEM((WINDOW, SWA_KVW), f32), pltpu.VMEM((WINDOW, SWA_KVW), f32),
            pltpu.VMEM((tl, W_ALL), f32), pltpu.VMEM((tl, GLA_KW), f32),
            pltpu.VMEM((tl, GLA_W), f32), pltpu.VMEM((tl, SWA_W), f32),
        ],
        compiler_params=pltpu.CompilerParams(
            dimension_semantics=("arbitrary", "arbitrary"), vmem_limit_bytes=PROMPT_VMEM_LIMIT),
        name="prompt_layer",
    )(w["sinks"], x, w["norm_in"], w["w_all"], w["w_up"], w["b_gate"], cmat, lmask, kmask, qhot,
      w["gla_norm"], w["w_out"], w["norm_f"])


def _sample_proj_kernel(x_ref, nin_ref, w_ref, wup_ref, bg_ref, proj_ref, decay_ref):
    u = _rms(x_ref[...], nin_ref[...]).astype(bf16)
    proj_ref[...] = jnp.dot(u, w_ref[:, :W_MAIN], preferred_element_type=f32)
    glow = jnp.dot(u, w_ref[:, O_LOW:O_LOW + LANES], preferred_element_type=f32)
    decay_ref[...] = jnp.exp(_log_decay(glow, wup_ref, bg_ref))


def _sample_proj_call(xs, w):
    n = xs.shape[0]
    return pl.pallas_call(
        _sample_proj_kernel,
        out_shape=[jax.ShapeDtypeStruct((n, W_MAIN), f32), jax.ShapeDtypeStruct((n, GLA_KW), f32)],
        compiler_params=pltpu.CompilerParams(vmem_limit_bytes=VMEM_LIMIT),
        name="sample_proj",
    )(xs, w["norm_in"], w["w_all"], w["w_up"], w["b_gate"])


def _split3(x):
    as_bf16 = lambda v: v.astype(bf16).astype(f32)
    hi = as_bf16(x)
    mid = as_bf16(x - hi)
    return hi, mid, as_bf16(x - hi - mid)


def _sample_state_kernel(proj_ref, decay_ref, sink_ref, s_ref, ck_ref, cv_ref,
                         so_ref, cko_ref, cvo_ref, og_ref, os_ref):
    row = lax.broadcasted_iota(jnp.int32, (PACK, GLA_KW), 0)
    head_of_lane = lax.broadcasted_iota(jnp.int32, (PACK, GLA_KW), 1) // GLA_DK
    own_head = head_of_lane == row
    row_v = lax.broadcasted_iota(jnp.int32, (PACK, GLA_DV), 0)
    lane_v = lax.broadcasted_iota(jnp.int32, (PACK, GLA_DV), 1)
    piece_rows = ((row_v >= GLA_HEADS) & (row_v < GLA_HEADS + 3)).astype(f32)
    last_lane_rows = ((row_v < 3) & (lane_v == WINDOW - 1)).astype(bf16)
    newest = lax.broadcasted_iota(jnp.int32, (SWA_KVW, WINDOW), 1) == WINDOW - 1
    row_q = lax.broadcasted_iota(jnp.int32, (SWA_HEADS, LANES), 0)
    own_kv = (lax.broadcasted_iota(jnp.int32, (SWA_HEADS, LANES), 1) // SWA_HD) == (row_q // SWA_GROUP)
    sink = sink_ref[...]
    contract_rows = (((0,), (0,)), ((), ()))
    seqs = range(SAMPLE_GROUP)
    lts, rts, qms, lt2s, q8s = [], [], [], [], []
    for j in seqs:
        pr = proj_ref[j:j + 1, :]
        bcast = lambda lo, width: jnp.broadcast_to(pr[:, lo:lo + width], (PACK, width))
        a_hi, a_mid, a_lo = _split3(jnp.broadcast_to(decay_ref[j:j + 1, :], (PACK, GLA_KW)))
        a_piece = jnp.where(row == GLA_HEADS, a_hi, jnp.where(row == GLA_HEADS + 1, a_mid, a_lo))
        lts.append(jnp.where(own_head, bcast(O_K, GLA_KW),
                             jnp.where((row >= GLA_HEADS) & (row < GLA_HEADS + 3), a_piece, 0.0)).astype(bf16))
        v_b = bcast(O_V, GLA_W)
        v_sel = jnp.zeros((PACK, GLA_DV), f32)
        for h in range(GLA_HEADS):
            v_sel = jnp.where(row_v == h, v_b[:, h * GLA_DV:(h + 1) * GLA_DV], v_sel)
        rts.append(jnp.concatenate([v_sel, piece_rows], axis=1).astype(bf16))
        qms.append(jnp.where(own_head, bcast(O_Q, GLA_KW), 0.0).astype(bf16))
        n_hi, n_mid, n_lo = _split3(bcast(O_SK, 2 * SWA_KVW))
        lt2s.append(jnp.where(row == 0, n_hi, jnp.where(row == 1, n_mid,
                                                        jnp.where(row == 2, n_lo, 0.0))).astype(bf16))
        sq_b = jnp.broadcast_to(pr[:, O_SQ:O_SQ + SWA_W], (SWA_HEADS, SWA_W))
        q8 = jnp.zeros((SWA_HEADS, LANES), f32)
        for gq in range(SWA_GROUP):
            q8 = jnp.where(row_q % SWA_GROUP == gq, sq_b[:, gq * LANES:(gq + 1) * LANES], q8)
        q8s.append(jnp.where(own_kv, q8, 0.0).astype(bf16))
    kv_as = [lax.dot_general(lts[j], rts[j], contract_rows, preferred_element_type=f32) for j in seqs]
    inss = [lax.dot_general(lt2s[j], last_lane_rows, contract_rows, preferred_element_type=f32) for j in seqs]
    s_news, kts, vts = [], [], []
    for j in seqs:
        s_new = kv_as[j][:, GLA_DV:] * s_ref[j] + kv_as[j][:, :GLA_DV]
        so_ref[j] = s_new
        s_news.append(s_new.astype(bf16))
        kt = jnp.where(newest, inss[j][:SWA_KVW], pltpu.roll(ck_ref[j], WINDOW - 1, axis=1))
        vt = jnp.where(newest, inss[j][SWA_KVW:], pltpu.roll(cv_ref[j], WINDOW - 1, axis=1))
        cko_ref[j] = kt
        cvo_ref[j] = vt
        kts.append(kt.astype(bf16))
        vts.append(vt.astype(bf16))
    for j in seqs:
        og_ref[j] = jnp.dot(qms[j], s_news[j], preferred_element_type=f32)[:GLA_HEADS]
    scores = [jnp.dot(q8s[j], kts[j], preferred_element_type=f32) for j in seqs]
    es, dens = [], []
    for j in seqs:
        m = jnp.maximum(jnp.max(scores[j], axis=-1, keepdims=True), sink)
        e = jnp.exp2(scores[j] - m)
        dens.append(jnp.sum(e, axis=-1, keepdims=True) + jnp.exp2(sink - m))
        es.append(e.astype(bf16))
    for j in seqs:
        o = lax.dot_general(es[j], vts[j], (((1,), (1,)), ((), ())), preferred_element_type=f32)
        os_ref[j] = o / dens[j]


def _sample_state_call(proj, decay, sinks_col, state, ck, cv):
    n = state.shape[0]
    g = SAMPLE_GROUP
    blk = lambda *tail: pl.BlockSpec((g,) + tail, lambda i: (i,) + (0,) * len(tail))
    cache = blk(SWA_KVW, WINDOW)
    return pl.pallas_call(
        _sample_state_kernel,
        grid=(n // g,),
        in_specs=[blk(W_MAIN), blk(GLA_KW), pl.BlockSpec((SWA_HEADS, 1), lambda i: (0, 0)),
                  blk(GLA_KW, GLA_DV), cache, cache],
        out_specs=[blk(GLA_KW, GLA_DV), cache, cache, blk(GLA_HEADS, GLA_DV), blk(SWA_HEADS, LANES)],
        out_shape=[
            jax.ShapeDtypeStruct(state.shape, f32), jax.ShapeDtypeStruct(ck.shape, f32),
            jax.ShapeDtypeStruct(cv.shape, f32),
            jax.ShapeDtypeStruct((n, GLA_HEADS, GLA_DV), f32), jax.ShapeDtypeStruct((n, SWA_HEADS, LANES), f32),
        ],
        compiler_params=pltpu.CompilerParams(dimension_semantics=("arbitrary",), vmem_limit_bytes=VMEM_LIMIT),
        name="sample_state",
    )(proj, decay, sinks_col, state, ck, cv)


def _sample_merge_kernel(x_ref, og_ref, gg_ref, os_ref, sg_ref, gn_ref, wout_ref, nf_ref, y_ref):
    y_ref[...] = _merge(x_ref[...], og_ref[...], gg_ref[...], os_ref[...], sg_ref[...],
                        gn_ref, wout_ref, nf_ref)


def _sample_merge_call(xs, og, gg, osw, sg, w):
    return pl.pallas_call(
        _sample_merge_kernel,
        out_shape=jax.ShapeDtypeStruct(xs.shape, f32),
        compiler_params=pltpu.CompilerParams(vmem_limit_bytes=VMEM_LIMIT),
        name="sample_merge",
    )(xs, og, gg, osw, sg, w["gla_norm"], w["w_out"], w["norm_f"])


def _sample_path(x_sample, state, ck_t, cv_t, w):
    n = x_sample.shape[0]
    xs = x_sample.reshape(n, D_MODEL)
    proj, decay = _sample_proj_call(xs, w)
    s_new, ck_new, cv_new, og, os_raw = _sample_state_call(
        proj, decay, w["sinks"].reshape(SWA_HEADS, 1), state, ck_t, cv_t)
    os5 = os_raw.reshape(n, SWA_KV, SWA_GROUP, SWA_KV, SWA_HD)
    os_il = jnp.stack([os5[:, kv, :, kv, :] for kv in range(SWA_KV)], axis=2).reshape(n, SWA_W)
    y = _sample_merge_call(xs, og.reshape(n, GLA_W), proj[:, O_GG:O_GG + GLA_W], os_il,
                           proj[:, O_SG:O_SG + SWA_W], w)
    return y, s_new, ck_new, cv_new


def _cache_view(c):
    n = c.shape[1]
    return jnp.transpose(c[0], (0, 2, 3, 1)).reshape(n, SWA_KVW, WINDOW)


def _cache_unview(c):
    n = c.shape[0]
    return jnp.transpose(c.reshape(n, SWA_KV, SWA_HD, WINDOW), (0, 3, 1, 2))[None]


def kernel(x_prompt, x_sample, state_gla, cache_win_k, cache_win_v, norm_in, w_in, w_gate_up, b_gate,
           gla_norm, attn_sinks, w_out, norm_f):
    bsz = x_prompt.shape[0]
    n = x_sample.shape[0]
    w = _prep_weights(norm_in[0], w_in[0], w_gate_up[0], b_gate[0], gla_norm[0], attn_sinks[0],
                      w_out[0], norm_f)
    cmat, lmask = _chunk_tables()
    y_p, s_p, k_p, v_p = _prompt_call(x_prompt, w, cmat, lmask, *_swa_mask_tables())
    y_s, s_s, k_s, v_s = _sample_path(x_sample, state_gla[0].reshape(n, GLA_KW, GLA_DV),
                                      _cache_view(cache_win_k), _cache_view(cache_win_v), w)
    return (y_p, y_s.reshape(n, 1, D_MODEL),
            s_p.reshape(1, bsz, GLA_HEADS, GLA_DK, GLA_DV),
            _cache_unview(k_p), _cache_unview(v_p),
            s_s.reshape(1, n, GLA_HEADS, GLA_DK, GLA_DV),
            _cache_unview(k_s), _cache_unview(v_s))
```

```python
import functools

import numpy as np
import jax
import jax.numpy as jnp
from jax import lax
from jax.experimental import pallas as pl
from jax.experimental.pallas import tpu as pltpu

D_MODEL = 1024
GLA_HEADS = 4
GLA_DK = 64
GLA_DV = 128
GLA_KW = GLA_HEADS * GLA_DK
GLA_W = GLA_HEADS * GLA_DV
GLA_RANK = 16
GLA_TAU = 16.0
CHUNK = 64
SWA_HEADS = 8
SWA_HD = 64
SWA_KV = 2
SWA_GROUP = SWA_HEADS // SWA_KV
SWA_W = SWA_HEADS * SWA_HD
SWA_KVW = SWA_KV * SWA_HD
WINDOW = 128
EPS = 1e-6
NEG_INF = -1e30
LOG2E = 1.4426950408889634
LANES = 128

O_Q, O_K, O_V, O_GG = 0, 256, 512, 1024
O_SQ, O_SK, O_SV, O_SG, O_LOW = 1536, 2048, 2176, 2304, 2816
W_MAIN = 2816
W_ALL = W_MAIN + LANES

N_LEVELS = 6
TOK_BLOCK = 1024
GLA_GROUP = 4
PACK = 16
PROMPT_VMEM_LIMIT = 56 * 1024 * 1024
VMEM_LIMIT = 32 * 1024 * 1024

f32 = jnp.float32
bf16 = jnp.bfloat16


def _chunk_tables():
    c = CHUNK
    t = np.arange(c)[None, :]
    i = np.arange(c)[:, None]
    blocks = [(t <= i), (t > i)]
    masks = []
    for l in range(N_LEVELS):
        h = c >> (l + 1)
        m = (i // (2 * h)) * (2 * h) + h
        upper = i >= m
        blocks.append(np.where(upper, (t > m) & (t <= i), (t > i) & (t <= m)))
        jj = np.arange(c)[None, :]
        masks.append((i // (2 * h) == jj // (2 * h)) & (i % (2 * h) >= h) & (jj % (2 * h) < h))
    masks.append(np.eye(c, dtype=bool))
    cm = np.concatenate(blocks, axis=0).astype(np.float32)
    cm3 = np.concatenate([cm, cm, cm], axis=1)
    lm = np.stack(masks).astype(np.float32)
    lm = np.concatenate([lm, lm], axis=1)
    return jnp.asarray(cm3, dtype=bf16), jnp.asarray(lm, dtype=f32)


def _swa_mask_tables():
    key = np.arange(2 * WINDOW)[:, None]
    qi = np.arange(WINDOW)[None, :]
    prev_ok = (key < WINDOW) & (key > qi)
    cur_ok = (key >= WINDOW) & (key - WINDOW <= qi)
    full = np.where(prev_ok | cur_ok, 0.0, NEG_INF)
    first = np.where(cur_ok, 0.0, NEG_INF)
    onehot = (np.arange(2 * WINDOW)[:, None] % WINDOW == qi).astype(np.float32)
    return jnp.asarray(np.stack([full, first]), dtype=bf16), jnp.asarray(onehot, dtype=bf16)


I_Q, I_K, I_V, I_GG, I_LOW, I_SQ, I_SK, I_SV, I_SG = (
    int(v) for v in np.cumsum([0, 256, 256, 512, 512, 16, 512, 128, 128])[:9])


def _interleaved_rows(base, tile):
    return [(base + kv * SWA_GROUP * SWA_HD + tile * SWA_HD, SWA_HD) for kv in range(SWA_KV)]


def _weight_layout_kernel(wt_ref, wo_ref, wall_ref, wop_ref):
    def put(col, pieces, scale=None):
        rows = [wt_ref[r:r + n, :] for r, n in pieces]
        missing = LANES - sum(n for _, n in pieces)
        if missing:
            rows.append(jnp.zeros((missing, D_MODEL), f32))
        blk = jnp.concatenate(rows, axis=0) if len(rows) > 1 else rows[0]
        if scale is not None:
            blk = blk * scale
        wall_ref[:, col:col + LANES] = blk.T.astype(bf16)

    for i in range(GLA_KW // LANES):
        put(O_Q + i * LANES, [(I_Q + i * LANES, LANES)], GLA_DK ** -0.5)
        put(O_K + i * LANES, [(I_K + i * LANES, LANES)])
    for i in range(GLA_W // LANES):
        put(O_V + i * LANES, [(I_V + i * LANES, LANES)])
        put(O_GG + i * LANES, [(I_GG + i * LANES, LANES)])
        put(O_SQ + i * LANES, _interleaved_rows(I_SQ, i), SWA_HD ** -0.5 * LOG2E)
        put(O_SG + i * LANES, _interleaved_rows(I_SG, i))
    put(O_SK, [(I_SK, SWA_KVW)])
    put(O_SV, [(I_SV, SWA_KVW)])
    put(O_LOW, [(I_LOW, GLA_RANK)])
    wop_ref[:GLA_W, :] = wo_ref[:GLA_W, :].astype(bf16)
    for i in range(SWA_GROUP):
        for kv, (r, n) in enumerate(_interleaved_rows(GLA_W, i)):
            dst = GLA_W + (i * SWA_KV + kv) * SWA_HD
            wop_ref[dst:dst + n, :] = wo_ref[r:r + n, :].astype(bf16)


def _prep_weights(norm_in, w_in, w_gate_up, b_gate, gla_norm, attn_sinks, w_out, norm_f):
    w_all, w_out_p = pl.pallas_call(
        _weight_layout_kernel,
        out_shape=[jax.ShapeDtypeStruct((D_MODEL, W_ALL), bf16), jax.ShapeDtypeStruct((D_MODEL, D_MODEL), bf16)],
        compiler_params=pltpu.CompilerParams(vmem_limit_bytes=VMEM_LIMIT),
        name="weight_layout",
    )(w_in.T, w_out)
    w_up = jnp.pad(w_gate_up, ((0, LANES - GLA_RANK), (0, 0))).astype(bf16)
    return dict(
        norm_in=norm_in.reshape(1, D_MODEL), w_all=w_all, w_up=w_up,
        b_gate=b_gate.reshape(1, GLA_KW), gla_norm=jnp.tile(gla_norm, GLA_HEADS).reshape(1, GLA_W),
        sinks=attn_sinks * LOG2E, w_out=w_out_p, norm_f=norm_f.reshape(1, D_MODEL))


def _rms(x, gain):
    return x * lax.rsqrt(jnp.mean(x * x, axis=-1, keepdims=True) + EPS) * gain


def _log_decay(glow, wup_ref, bg_ref):
    z = jnp.dot(glow.astype(bf16), wup_ref[...], preferred_element_type=f32) + bg_ref[...]
    return (jnp.minimum(z, 0.0) - jnp.log(1.0 + jnp.exp(-jnp.abs(z)))) * (1.0 / GLA_TAU)


def _silu(x):
    return x * jax.nn.sigmoid(x)


def _merge(x, o_gla, gg, o_swa, sg, gn_ref, wout_ref, nf_ref):
    parts = []
    for h in range(GLA_HEADS):
        sl = slice(h * GLA_DV, (h + 1) * GLA_DV)
        parts.append(_rms(o_gla[:, sl], gn_ref[:, sl]) * _silu(gg[:, sl]))
    parts.append(o_swa * _silu(sg))
    um = jnp.concatenate(parts, axis=1).astype(bf16)
    hres = x + jnp.dot(um, wout_ref[...], preferred_element_type=f32)
    return _rms(hres, nf_ref[...])


def _prompt_kernel(sinks_ref, x_ref, nin_ref, w_ref, wup_ref, bg_ref, cmat_ref, lmask_ref, kmask_ref, qhot_ref,
                   gn_ref, wout_ref, nf_ref,
                   smp_proj_ref, smp_decay_ref, smp_sink_ref, smp_s_ref, smp_ck_ref, smp_cv_ref,
                   y_ref, sp_ref, kn_ref, vn_ref,
                   smp_so_ref, smp_cko_ref, smp_cvo_ref, smp_og_ref, smp_os_ref,
                   s_ref, kprev_ref, vprev_ref, p_s, g_s, ogla_s, oswa_s, *, n_t):
    t = pl.program_id(1)
    tl = x_ref.shape[0]

    @pl.when(t == 0)
    def _():
        s_ref[...] = jnp.zeros_like(s_ref)
        kprev_ref[...] = jnp.zeros_like(kprev_ref)
        vprev_ref[...] = jnp.zeros_like(vprev_ref)

    def project(rg):
        u = _rms(x_ref[rg, :], nin_ref[...]).astype(bf16)
        p_s[rg, :] = jnp.dot(u, w_ref[...], preferred_element_type=f32)
        g_s[rg, :] = _log_decay(p_s[rg, O_LOW:O_LOW + LANES], wup_ref, bg_ref)

    lane_lo = lax.broadcasted_iota(jnp.int32, (CHUNK, LANES), 1) < GLA_DK

    n_pairs = GLA_HEADS // 2

    def group_terms(chunks):
        rows = {c: slice(c * CHUNK, (c + 1) * CHUNK) for c in chunks}
        sums = {}
        for c in chunks:
            gc = g_s[rows[c], :]
            g_hi = gc.astype(bf16)
            r1 = gc - g_hi.astype(f32)
            g_mid = r1.astype(bf16)
            g_lo = (r1 - g_mid.astype(f32)).astype(bf16)
            sums[c] = jnp.dot(cmat_ref[...], jnp.concatenate([g_hi, g_mid, g_lo], axis=0),
                              preferred_element_type=f32)
        level_ops, misc = {}, {}
        for c in chunks:
            qc, kc = p_s[rows[c], O_Q:O_Q + GLA_KW], p_s[rows[c], O_K:O_K + GLA_KW]
            e_b = jnp.exp(sums[c][0:CHUNK])
            misc[c] = (e_b, qc * e_b, kc * jnp.exp(sums[c][CHUNK:2 * CHUNK]),
                       p_s[rows[c], O_V:O_V + GLA_W].astype(bf16))
            for p in range(n_pairs):
                ln = slice(p * LANES, (p + 1) * LANES)
                qp, kp = qc[:, ln], kc[:, ln]
                for l in range(N_LEVELS + 1):
                    if l < N_LEVELS:
                        e = jnp.exp(sums[c][(2 + l) * CHUNK:(3 + l) * CHUNK, ln])
                        qe, ke = (qp * e).astype(bf16), (kp * e).astype(bf16)
                    else:
                        qe, ke = qp.astype(bf16), kp.astype(bf16)
                    lhs = jnp.concatenate([jnp.where(lane_lo, qe, 0), jnp.where(lane_lo, 0, qe)], axis=0)
                    level_ops[c, p, l] = (lhs, ke)
        scores = {key: lax.dot_general(lhs, ke, (((1,), (1,)), ((), ())), preferred_element_type=f32)
                  for key, (lhs, ke) in level_ops.items()}
        terms = {}
        for c in chunks:
            e_b, qb, k_suf, vb = misc[c]
            terms[c] = []
            for p in range(n_pairs):
                ln = slice(p * LANES, (p + 1) * LANES)
                a = scores[c, p, 0] * lmask_ref[0]
                for l in range(1, N_LEVELS + 1):
                    a = a + scores[c, p, l] * lmask_ref[l]
                ab = a.astype(bf16)
                qbp = qb[:, ln].astype(bf16)
                lhs_heads = []
                for hh in range(2):
                    qbm = jnp.where(lane_lo, qbp, 0) if hh == 0 else jnp.where(lane_lo, 0, qbp)
                    lhs_heads.append(jnp.concatenate([qbm, ab[hh * CHUNK:(hh + 1) * CHUNK, :]], axis=1))
                upd = lax.dot_general(k_suf[:, ln].astype(bf16), vb[:, p * 2 * GLA_DV:(p + 1) * 2 * GLA_DV],
                                      (((0,), (0,)), ((), ())), preferred_element_type=f32)
                upd = jnp.concatenate(
                    [upd[0:GLA_DK, 0:GLA_DV], upd[GLA_DK:2 * GLA_DK, GLA_DV:2 * GLA_DV]], axis=0)
                e_col = jnp.broadcast_to(e_b[CHUNK - 1:CHUNK, ln], (LANES, LANES)).T
                terms[c].append((lhs_heads, vb, upd, e_col))
        return terms

    def gla(chunks):
        terms = {}
        for c0 in range(chunks[0], chunks[-1] + 1, GLA_GROUP):
            terms.update(group_terms(range(c0, c0 + GLA_GROUP)))
        states = {}
        for p in range(n_pairs):
            ln = slice(p * LANES, (p + 1) * LANES)
            s_pair = s_ref[ln, :]
            for c in chunks:
                states[c, p] = s_pair.astype(bf16)
                _, _, upd, e_col = terms[c][p]
                s_pair = e_col * s_pair + upd
            s_ref[ln, :] = s_pair
        for c in chunks:
            for p in range(n_pairs):
                lhs_heads, vb, _, _ = terms[c][p]
                for hh in range(2):
                    h = 2 * p + hh
                    rhs = jnp.concatenate([states[c, p], vb[:, h * GLA_DV:(h + 1) * GLA_DV]], axis=0)
                    ogla_s[c * CHUNK:(c + 1) * CHUNK, h * GLA_DV:(h + 1) * GLA_DV] = jnp.dot(
                        lhs_heads[hh], rhs, preferred_element_type=f32)

    lane_lo_w = lax.broadcasted_iota(jnp.int32, (WINDOW, LANES), 1) < SWA_HD
    row_lo = lax.broadcasted_iota(jnp.int32, (2 * WINDOW, 1), 0) < WINDOW

    def swa(blocks):
        for blk in blocks:
            rs = slice(blk * WINDOW, (blk + 1) * WINDOW)
            sq = p_s[rs, O_SQ:O_SQ + SWA_W].astype(bf16)
            k_cur, v_cur = p_s[rs, O_SK:O_SK + SWA_KVW], p_s[rs, O_SV:O_SV + SWA_KVW]
            k2 = jnp.concatenate([kprev_ref[...], k_cur], axis=0).astype(bf16)
            v2 = jnp.concatenate([vprev_ref[...], v_cur], axis=0).astype(bf16)
            kmask = kmask_ref[jnp.where(t > 0, 0, 1)] if blk == 0 else kmask_ref[0]
            k2m = jnp.concatenate([k2, kmask], axis=1)
            for tt in range(SWA_GROUP):
                qt = sq[:, tt * LANES:(tt + 1) * LANES]
                lhs = jnp.concatenate([jnp.where(lane_lo_w, qt, 0), jnp.where(lane_lo_w, 0, qt)], axis=0)
                lhs = jnp.concatenate([lhs, qhot_ref[...]], axis=1)
                s = lax.dot_general(lhs, k2m, (((1,), (1,)), ((), ())), preferred_element_type=f32)
                sink = jnp.where(row_lo, sinks_ref[tt], sinks_ref[SWA_GROUP + tt])
                m = jnp.maximum(jnp.max(s, axis=-1, keepdims=True), sink)
                e = jnp.exp2(s - m)
                r = 1.0 / (jnp.sum(e, axis=-1, keepdims=True) + jnp.exp2(sink - m))
                o2 = jnp.dot(e.astype(bf16), v2, preferred_element_type=f32)
                oswa_s[rs, tt * LANES:(tt + 1) * LANES] = jnp.where(
                    lane_lo_w, o2[:WINDOW] * r[:WINDOW], o2[WINDOW:] * r[WINDOW:])
            kprev_ref[...] = k_cur
            vprev_ref[...] = v_cur

    def merge(rg):
        y_ref[rg, :] = _merge(x_ref[rg, :], ogla_s[rg, :], p_s[rg, O_GG:O_GG + GLA_W], oswa_s[rg, :],
                              p_s[rg, O_SG:O_SG + SWA_W], gn_ref, wout_ref, nf_ref)

    whole = slice(0, tl)
    project(whole)
    _sample_state_update(smp_proj_ref, smp_decay_ref, smp_sink_ref, smp_s_ref, smp_ck_ref, smp_cv_ref,
                         smp_so_ref, smp_cko_ref, smp_cvo_ref, smp_og_ref, smp_os_ref)
    gla(range(tl // CHUNK))
    swa(range(tl // WINDOW))
    merge(whole)

    @pl.when(t == n_t - 1)
    def _():
        sp_ref[...] = s_ref[...]
        kn_ref[...] = kprev_ref[...].T
        vn_ref[...] = vprev_ref[...].T


def _prompt_call(x, w, cmat, lmask, kmask, qhot, smp_proj, smp_decay, smp_state, smp_ck, smp_cv):
    bsz, seq, _ = x.shape
    tl = TOK_BLOCK
    n_t = seq // tl
    n = smp_state.shape[0]
    g = n // (bsz * n_t)
    assert g * bsz * n_t == n
    step = lambda b, t: b * n_t + t
    smp = lambda *tail: pl.BlockSpec((g,) + tail, lambda b, t: (step(b, t),) + (0,) * len(tail))
    smp_rows = lambda width: pl.BlockSpec((None, g, width), lambda b, t: (step(b, t), 0, 0))
    const = lambda shape: pl.BlockSpec(shape, lambda b, t: (0,) * len(shape), pipeline_mode=pl.Buffered(1))
    return pl.pallas_call(
        functools.partial(_prompt_kernel, n_t=n_t),
        grid=(bsz, n_t),
        in_specs=[
            pl.BlockSpec(memory_space=pltpu.SMEM),
            pl.BlockSpec((None, tl, D_MODEL), lambda b, t: (b, t, 0)),
            const((1, D_MODEL)), const((D_MODEL, W_ALL)), const((LANES, GLA_KW)), const((1, GLA_KW)),
            const(cmat.shape), const(lmask.shape), const(kmask.shape), const(qhot.shape),
            const((1, GLA_W)), const((D_MODEL, D_MODEL)), const((1, D_MODEL)),
            smp_rows(W_MAIN), smp_rows(GLA_KW), const((SWA_HEADS, 1)),
            smp(GLA_KW, GLA_DV), smp(SWA_KVW, WINDOW), smp(SWA_KVW, WINDOW),
        ],
        out_specs=[
            pl.BlockSpec((None, tl, D_MODEL), lambda b, t: (b, t, 0)),
            pl.BlockSpec((None, GLA_KW, GLA_DV), lambda b, t: (b, 0, 0)),
            pl.BlockSpec((None, WINDOW, SWA_KVW), lambda b, t: (b, 0, 0)),
            pl.BlockSpec((None, WINDOW, SWA_KVW), lambda b, t: (b, 0, 0)),
            smp(GLA_KW, GLA_DV), smp(SWA_KVW, WINDOW), smp(SWA_KVW, WINDOW),
            smp(GLA_HEADS, GLA_DV), smp(SWA_HEADS, LANES),
        ],
        out_shape=[
            jax.ShapeDtypeStruct((bsz, seq, D_MODEL), f32),
            jax.ShapeDtypeStruct((bsz, GLA_KW, GLA_DV), f32),
            jax.ShapeDtypeStruct((bsz, WINDOW, SWA_KVW), f32),
            jax.ShapeDtypeStruct((bsz, WINDOW, SWA_KVW), f32),
            jax.ShapeDtypeStruct(smp_state.shape, f32), jax.ShapeDtypeStruct(smp_ck.shape, f32),
            jax.ShapeDtypeStruct(smp_cv.shape, f32),
            jax.ShapeDtypeStruct((n, GLA_HEADS, GLA_DV), f32), jax.ShapeDtypeStruct((n, SWA_HEADS, LANES), f32),
        ],
        scratch_shapes=[
            pltpu.VMEM((GLA_KW, GLA_DV), f32),
            pltpu.VMEM((WINDOW, SWA_KVW), f32), pltpu.VMEM((WINDOW, SWA_KVW), f32),
            pltpu.VMEM((tl, W_ALL), f32), pltpu.VMEM((tl, GLA_KW), f32),
            pltpu.VMEM((tl, GLA_W), f32), pltpu.VMEM((tl, SWA_W), f32),
        ],
        compiler_params=pltpu.CompilerParams(
            dimension_semantics=("arbitrary", "arbitrary"), vmem_limit_bytes=PROMPT_VMEM_LIMIT),
        name="prompt_layer",
    )(w["sinks"], x, w["norm_in"], w["w_all"], w["w_up"], w["b_gate"], cmat, lmask, kmask, qhot,
      w["gla_norm"], w["w_out"], w["norm_f"],
      smp_proj.reshape(bsz * n_t, g, W_MAIN), smp_decay.reshape(bsz * n_t, g, GLA_KW),
      w["sinks"].reshape(SWA_HEADS, 1), smp_state, smp_ck, smp_cv)


def _sample_proj_kernel(x_ref, nin_ref, w_ref, wup_ref, bg_ref, proj_ref, decay_ref):
    u = _rms(x_ref[...], nin_ref[...]).astype(bf16)
    proj_ref[...] = jnp.dot(u, w_ref[:, :W_MAIN], preferred_element_type=f32)
    glow = jnp.dot(u, w_ref[:, O_LOW:O_LOW + LANES], preferred_element_type=f32)
    decay_ref[...] = jnp.exp(_log_decay(glow, wup_ref, bg_ref))


def _sample_proj_call(xs, w):
    n = xs.shape[0]
    return pl.pallas_call(
        _sample_proj_kernel,
        out_shape=[jax.ShapeDtypeStruct((n, W_MAIN), f32), jax.ShapeDtypeStruct((n, GLA_KW), f32)],
        compiler_params=pltpu.CompilerParams(vmem_limit_bytes=VMEM_LIMIT),
        name="sample_proj",
    )(xs, w["norm_in"], w["w_all"], w["w_up"], w["b_gate"])


def _split3(x):
    as_bf16 = lambda v: v.astype(bf16).astype(f32)
    hi = as_bf16(x)
    mid = as_bf16(x - hi)
    return hi, mid, as_bf16(x - hi - mid)


def _sample_state_update(proj_ref, decay_ref, sink_ref, s_ref, ck_ref, cv_ref,
                         so_ref, cko_ref, cvo_ref, og_ref, os_ref):
    row = lax.broadcasted_iota(jnp.int32, (PACK, GLA_KW), 0)
    head_of_lane = lax.broadcasted_iota(jnp.int32, (PACK, GLA_KW), 1) // GLA_DK
    own_head = head_of_lane == row
    row_v = lax.broadcasted_iota(jnp.int32, (PACK, GLA_DV), 0)
    lane_v = lax.broadcasted_iota(jnp.int32, (PACK, GLA_DV), 1)
    piece_rows = ((row_v >= GLA_HEADS) & (row_v < GLA_HEADS + 3)).astype(f32)
    last_lane_rows = ((row_v < 3) & (lane_v == WINDOW - 1)).astype(bf16)
    newest = lax.broadcasted_iota(jnp.int32, (SWA_KVW, WINDOW), 1) == WINDOW - 1
    row_q = lax.broadcasted_iota(jnp.int32, (SWA_HEADS, LANES), 0)
    own_kv = (lax.broadcasted_iota(jnp.int32, (SWA_HEADS, LANES), 1) // SWA_HD) == (row_q // SWA_GROUP)
    sink = sink_ref[...]
    contract_rows = (((0,), (0,)), ((), ()))
    seqs = range(proj_ref.shape[0])
    lts, rts, qms, lt2s, q8s = [], [], [], [], []
    for j in seqs:
        pr = proj_ref[j:j + 1, :]
        bcast = lambda lo, width: jnp.broadcast_to(pr[:, lo:lo + width], (PACK, width))
        a_hi, a_mid, a_lo = _split3(jnp.broadcast_to(decay_ref[j:j + 1, :], (PACK, GLA_KW)))
        a_piece = jnp.where(row == GLA_HEADS, a_hi, jnp.where(row == GLA_HEADS + 1, a_mid, a_lo))
        lts.append(jnp.where(own_head, bcast(O_K, GLA_KW),
                             jnp.where((row >= GLA_HEADS) & (row < GLA_HEADS + 3), a_piece, 0.0)).astype(bf16))
        v_b = bcast(O_V, GLA_W)
        v_sel = jnp.zeros((PACK, GLA_DV), f32)
        for h in range(GLA_HEADS):
            v_sel = jnp.where(row_v == h, v_b[:, h * GLA_DV:(h + 1) * GLA_DV], v_sel)
        rts.append(jnp.concatenate([v_sel, piece_rows], axis=1).astype(bf16))
        qms.append(jnp.where(own_head, bcast(O_Q, GLA_KW), 0.0).astype(bf16))
        n_hi, n_mid, n_lo = _split3(bcast(O_SK, 2 * SWA_KVW))
        lt2s.append(jnp.where(row == 0, n_hi, jnp.where(row == 1, n_mid,
                                                        jnp.where(row == 2, n_lo, 0.0))).astype(bf16))
        sq_b = jnp.broadcast_to(pr[:, O_SQ:O_SQ + SWA_W], (SWA_HEADS, SWA_W))
        q8 = jnp.zeros((SWA_HEADS, LANES), f32)
        for gq in range(SWA_GROUP):
            q8 = jnp.where(row_q % SWA_GROUP == gq, sq_b[:, gq * LANES:(gq + 1) * LANES], q8)
        q8s.append(jnp.where(own_kv, q8, 0.0).astype(bf16))
    kv_as = [lax.dot_general(lts[j], rts[j], contract_rows, preferred_element_type=f32) for j in seqs]
    inss = [lax.dot_general(lt2s[j], last_lane_rows, contract_rows, preferred_element_type=f32) for j in seqs]
    s_news, kts, vts = [], [], []
    for j in seqs:
        s_new = kv_as[j][:, GLA_DV:] * s_ref[j] + kv_as[j][:, :GLA_DV]
        so_ref[j] = s_new
        s_news.append(s_new.astype(bf16))
        kt = jnp.where(newest, inss[j][:SWA_KVW], pltpu.roll(ck_ref[j], WINDOW - 1, axis=1))
        vt = jnp.where(newest, inss[j][SWA_KVW:], pltpu.roll(cv_ref[j], WINDOW - 1, axis=1))
        cko_ref[j] = kt
        cvo_ref[j] = vt
        kts.append(kt.astype(bf16))
        vts.append(vt.astype(bf16))
    for j in seqs:
        og_ref[j] = jnp.dot(qms[j], s_news[j], preferred_element_type=f32)[:GLA_HEADS]
    scores = [jnp.dot(q8s[j], kts[j], preferred_element_type=f32) for j in seqs]
    es, dens = [], []
    for j in seqs:
        m = jnp.maximum(jnp.max(scores[j], axis=-1, keepdims=True), sink)
        e = jnp.exp2(scores[j] - m)
        dens.append(jnp.sum(e, axis=-1, keepdims=True) + jnp.exp2(sink - m))
        es.append(e.astype(bf16))
    for j in seqs:
        o = lax.dot_general(es[j], vts[j], (((1,), (1,)), ((), ())), preferred_element_type=f32)
        os_ref[j] = o / dens[j]


def _sample_merge_kernel(x_ref, og_ref, gg_ref, os_ref, sg_ref, gn_ref, wout_ref, nf_ref, y_ref):
    y_ref[...] = _merge(x_ref[...], og_ref[...], gg_ref[...], os_ref[...], sg_ref[...],
                        gn_ref, wout_ref, nf_ref)


def _sample_merge_call(xs, og, gg, osw, sg, w):
    return pl.pallas_call(
        _sample_merge_kernel,
        out_shape=jax.ShapeDtypeStruct(xs.shape, f32),
        compiler_params=pltpu.CompilerParams(vmem_limit_bytes=VMEM_LIMIT),
        name="sample_merge",
    )(xs, og, gg, osw, sg, w["gla_norm"], w["w_out"], w["norm_f"])


def _sample_finish(xs, proj, og, os_raw, w):
    n = xs.shape[0]
    os5 = os_raw.reshape(n, SWA_KV, SWA_GROUP, SWA_KV, SWA_HD)
    os_il = jnp.stack([os5[:, kv, :, kv, :] for kv in range(SWA_KV)], axis=2).reshape(n, SWA_W)
    return _sample_merge_call(xs, og.reshape(n, GLA_W), proj[:, O_GG:O_GG + GLA_W], os_il,
                              proj[:, O_SG:O_SG + SWA_W], w)


def _cache_view(c):
    n = c.shape[1]
    return jnp.transpose(c[0], (0, 2, 3, 1)).reshape(n, SWA_KVW, WINDOW)


def _cache_unview(c):
    n = c.shape[0]
    return jnp.transpose(c.reshape(n, SWA_KV, SWA_HD, WINDOW), (0, 3, 1, 2))[None]


def kernel(x_prompt, x_sample, state_gla, cache_win_k, cache_win_v, norm_in, w_in, w_gate_up, b_gate,
           gla_norm, attn_sinks, w_out, norm_f):
    bsz = x_prompt.shape[0]
    n = x_sample.shape[0]
    w = _prep_weights(norm_in[0], w_in[0], w_gate_up[0], b_gate[0], gla_norm[0], attn_sinks[0],
                      w_out[0], norm_f)
    cmat, lmask = _chunk_tables()
    xs = x_sample.reshape(n, D_MODEL)
    proj, decay = _sample_proj_call(xs, w)
    y_p, s_p, k_p, v_p, s_s, k_s, v_s, og, os_raw = _prompt_call(
        x_prompt, w, cmat, lmask, *_swa_mask_tables(), proj, decay,
        state_gla[0].reshape(n, GLA_KW, GLA_DV), _cache_view(cache_win_k), _cache_view(cache_win_v))
    y_s = _sample_finish(xs, proj, og, os_raw, w)
    return (y_p, y_s.reshape(n, 1, D_MODEL),
            s_p.reshape(1, bsz, GLA_HEADS, GLA_DK, GLA_DV),
            _cache_unview(k_p), _cache_unview(v_p),
            s_s.reshape(1, n, GLA_HEADS, GLA_DK, GLA_DV),
            _cache_unview(k_s), _cache_unview(v_s))
```

```python
import functools

import numpy as np
import jax
import jax.numpy as jnp
from jax import lax
from jax.experimental import pallas as pl
from jax.experimental.pallas import tpu as pltpu

D_MODEL = 1024
GLA_HEADS = 4
GLA_DK = 64
GLA_DV = 128
GLA_KW = GLA_HEADS * GLA_DK
GLA_W = GLA_HEADS * GLA_DV
GLA_RANK = 16
GLA_TAU = 16.0
CHUNK = 64
SWA_HEADS = 8
SWA_HD = 64
SWA_KV = 2
SWA_GROUP = SWA_HEADS // SWA_KV
SWA_W = SWA_HEADS * SWA_HD
SWA_KVW = SWA_KV * SWA_HD
WINDOW = 128
EPS = 1e-6
NEG_INF = -1e30
LOG2E = 1.4426950408889634
LANES = 128

O_Q, O_K, O_V, O_GG = 0, 256, 512, 1024
O_SQ, O_SK, O_SV, O_SG, O_LOW = 1536, 2048, 2176, 2304, 2816
W_MAIN = 2816
W_ALL = W_MAIN + LANES

N_LEVELS = 6
TOK_BLOCK = 1024
GLA_GROUP = 4
PACK = 16
PROMPT_VMEM_LIMIT = 56 * 1024 * 1024
VMEM_LIMIT = 32 * 1024 * 1024

f32 = jnp.float32
bf16 = jnp.bfloat16


def _chunk_tables():
    c = CHUNK
    t = np.arange(c)[None, :]
    i = np.arange(c)[:, None]
    blocks = [(t <= i), (t > i)]
    masks = []
    for l in range(N_LEVELS):
        h = c >> (l + 1)
        m = (i // (2 * h)) * (2 * h) + h
        upper = i >= m
        blocks.append(np.where(upper, (t > m) & (t <= i), (t > i) & (t <= m)))
        jj = np.arange(c)[None, :]
        masks.append((i // (2 * h) == jj // (2 * h)) & (i % (2 * h) >= h) & (jj % (2 * h) < h))
    masks.append(np.eye(c, dtype=bool))
    cm = np.concatenate(blocks, axis=0).astype(np.float32)
    cm3 = np.concatenate([cm, cm, cm], axis=1)
    lm = np.stack(masks).astype(np.float32)
    lm = np.concatenate([lm, lm], axis=1)
    return jnp.asarray(cm3, dtype=bf16), jnp.asarray(lm, dtype=f32)


def _swa_mask_tables():
    key = np.arange(2 * WINDOW)[:, None]
    qi = np.arange(WINDOW)[None, :]
    prev_ok = (key < WINDOW) & (key > qi)
    cur_ok = (key >= WINDOW) & (key - WINDOW <= qi)
    full = np.where(prev_ok | cur_ok, 0.0, NEG_INF)
    first = np.where(cur_ok, 0.0, NEG_INF)
    onehot = (np.arange(2 * WINDOW)[:, None] % WINDOW == qi).astype(np.float32)
    return jnp.asarray(np.stack([full, first]), dtype=bf16), jnp.asarray(onehot, dtype=bf16)


I_Q, I_K, I_V, I_GG, I_LOW, I_SQ, I_SK, I_SV, I_SG = (
    int(v) for v in np.cumsum([0, 256, 256, 512, 512, 16, 512, 128, 128])[:9])


def _interleaved_rows(base, tile):
    return [(base + kv * SWA_GROUP * SWA_HD + tile * SWA_HD, SWA_HD) for kv in range(SWA_KV)]


def _weight_layout_kernel(wt_ref, wo_ref, wall_ref, wop_ref):
    def put(col, pieces, scale=None):
        rows = [wt_ref[r:r + n, :] for r, n in pieces]
        missing = LANES - sum(n for _, n in pieces)
        if missing:
            rows.append(jnp.zeros((missing, D_MODEL), f32))
        blk = jnp.concatenate(rows, axis=0) if len(rows) > 1 else rows[0]
        if scale is not None:
            blk = blk * scale
        wall_ref[:, col:col + LANES] = blk.T.astype(bf16)

    for i in range(GLA_KW // LANES):
        put(O_Q + i * LANES, [(I_Q + i * LANES, LANES)], GLA_DK ** -0.5)
        put(O_K + i * LANES, [(I_K + i * LANES, LANES)])
    for i in range(GLA_W // LANES):
        put(O_V + i * LANES, [(I_V + i * LANES, LANES)])
        put(O_GG + i * LANES, [(I_GG + i * LANES, LANES)])
        put(O_SQ + i * LANES, _interleaved_rows(I_SQ, i), SWA_HD ** -0.5 * LOG2E)
        put(O_SG + i * LANES, _interleaved_rows(I_SG, i))
    put(O_SK, [(I_SK, SWA_KVW)])
    put(O_SV, [(I_SV, SWA_KVW)])
    put(O_LOW, [(I_LOW, GLA_RANK)])
    wop_ref[:GLA_W, :] = wo_ref[:GLA_W, :].astype(bf16)
    for i in range(SWA_GROUP):
        for kv, (r, n) in enumerate(_interleaved_rows(GLA_W, i)):
            dst = GLA_W + (i * SWA_KV + kv) * SWA_HD
            wop_ref[dst:dst + n, :] = wo_ref[r:r + n, :].astype(bf16)


def _prep_weights(norm_in, w_in, w_gate_up, b_gate, gla_norm, attn_sinks, w_out, norm_f):
    w_all, w_out_p = pl.pallas_call(
        _weight_layout_kernel,
        out_shape=[jax.ShapeDtypeStruct((D_MODEL, W_ALL), bf16), jax.ShapeDtypeStruct((D_MODEL, D_MODEL), bf16)],
        compiler_params=pltpu.CompilerParams(vmem_limit_bytes=VMEM_LIMIT),
        name="weight_layout",
    )(w_in.T, w_out)
    w_up = jnp.pad(w_gate_up, ((0, LANES - GLA_RANK), (0, 0))).astype(bf16)
    return dict(
        norm_in=norm_in.reshape(1, D_MODEL), w_all=w_all, w_up=w_up,
        b_gate=b_gate.reshape(1, GLA_KW), gla_norm=jnp.tile(gla_norm, GLA_HEADS).reshape(1, GLA_W),
        sinks=attn_sinks * LOG2E, w_out=w_out_p, norm_f=norm_f.reshape(1, D_MODEL))


def _rms(x, gain):
    return x * lax.rsqrt(jnp.mean(x * x, axis=-1, keepdims=True) + EPS) * gain


def _log_decay(glow, wup_ref, bg_ref):
    z = jnp.dot(glow.astype(bf16), wup_ref[...], preferred_element_type=f32) + bg_ref[...]
    return (jnp.minimum(z, 0.0) - jnp.log(1.0 + jnp.exp(-jnp.abs(z)))) * (1.0 / GLA_TAU)


def _silu(x):
    return x * jax.nn.sigmoid(x)


def _merge(x, o_gla, gg, o_swa, sg, gn_ref, wout_ref, nf_ref):
    parts = []
    for h in range(GLA_HEADS):
        sl = slice(h * GLA_DV, (h + 1) * GLA_DV)
        parts.append(_rms(o_gla[:, sl], gn_ref[:, sl]) * _silu(gg[:, sl]))
    parts.append(o_swa * _silu(sg))
    um = jnp.concatenate(parts, axis=1).astype(bf16)
    hres = x + jnp.dot(um, wout_ref[...], preferred_element_type=f32)
    return _rms(hres, nf_ref[...])


def _prompt_kernel(sinks_ref, x_ref, nin_ref, w_ref, wup_ref, bg_ref, cmat_ref, lmask_ref, kmask_ref, qhot_ref,
                   gn_ref, wout_ref, nf_ref,
                   smp_proj_ref, smp_decay_ref, smp_sink_ref, smp_s_ref, smp_ck_ref, smp_cv_ref,
                   y_ref, sp_ref, kn_ref, vn_ref,
                   smp_so_ref, smp_cko_ref, smp_cvo_ref, smp_og_ref, smp_os_ref,
                   s_ref, kprev_ref, vprev_ref, p_s, g_s, ogla_s, oswa_s, *, n_t):
    t = pl.program_id(1)
    tl = x_ref.shape[0]

    @pl.when(t == 0)
    def _():
        s_ref[...] = jnp.zeros_like(s_ref)
        kprev_ref[...] = jnp.zeros_like(kprev_ref)
        vprev_ref[...] = jnp.zeros_like(vprev_ref)

    def project(rg):
        u = _rms(x_ref[rg, :], nin_ref[...]).astype(bf16)
        p_s[rg, :] = jnp.dot(u, w_ref[...], preferred_element_type=f32)
        g_s[rg, :] = _log_decay(p_s[rg, O_LOW:O_LOW + LANES], wup_ref, bg_ref)

    lane_lo = lax.broadcasted_iota(jnp.int32, (CHUNK, LANES), 1) < GLA_DK

    n_pairs = GLA_HEADS // 2

    def group_sums(chunks):
        sums = {}
        for c in chunks:
            gc = g_s[c * CHUNK:(c + 1) * CHUNK, :]
            g_hi = gc.astype(bf16)
            r1 = gc - g_hi.astype(f32)
            g_mid = r1.astype(bf16)
            g_lo = (r1 - g_mid.astype(f32)).astype(bf16)
            sums[c] = jnp.dot(cmat_ref[...], jnp.concatenate([g_hi, g_mid, g_lo], axis=0),
                              preferred_element_type=f32)
        return sums

    def group_terms(chunks, sums):
        rows = {c: slice(c * CHUNK, (c + 1) * CHUNK) for c in chunks}
        level_ops, misc = {}, {}
        for c in chunks:
            qc, kc = p_s[rows[c], O_Q:O_Q + GLA_KW], p_s[rows[c], O_K:O_K + GLA_KW]
            e_b = jnp.exp(sums[c][0:CHUNK])
            misc[c] = (e_b, qc * e_b, kc * jnp.exp(sums[c][CHUNK:2 * CHUNK]),
                       p_s[rows[c], O_V:O_V + GLA_W].astype(bf16))
            for p in range(n_pairs):
                ln = slice(p * LANES, (p + 1) * LANES)
                qp, kp = qc[:, ln], kc[:, ln]
                for l in range(N_LEVELS + 1):
                    if l < N_LEVELS:
                        e = jnp.exp(sums[c][(2 + l) * CHUNK:(3 + l) * CHUNK, ln])
                        qe, ke = (qp * e).astype(bf16), (kp * e).astype(bf16)
                    else:
                        qe, ke = qp.astype(bf16), kp.astype(bf16)
                    lhs = jnp.concatenate([jnp.where(lane_lo, qe, 0), jnp.where(lane_lo, 0, qe)], axis=0)
                    level_ops[c, p, l] = (lhs, ke)
        scores = {key: lax.dot_general(lhs, ke, (((1,), (1,)), ((), ())), preferred_element_type=f32)
                  for key, (lhs, ke) in level_ops.items()}
        terms = {}
        for c in chunks:
            e_b, qb, k_suf, vb = misc[c]
            terms[c] = []
            for p in range(n_pairs):
                ln = slice(p * LANES, (p + 1) * LANES)
                a = scores[c, p, 0] * lmask_ref[0]
                for l in range(1, N_LEVELS + 1):
                    a = a + scores[c, p, l] * lmask_ref[l]
                ab = a.astype(bf16)
                qbp = qb[:, ln].astype(bf16)
                lhs_heads = []
                for hh in range(2):
                    qbm = jnp.where(lane_lo, qbp, 0) if hh == 0 else jnp.where(lane_lo, 0, qbp)
                    lhs_heads.append(jnp.concatenate([qbm, ab[hh * CHUNK:(hh + 1) * CHUNK, :]], axis=1))
                upd = lax.dot_general(k_suf[:, ln].astype(bf16), vb[:, p * 2 * GLA_DV:(p + 1) * 2 * GLA_DV],
                                      (((0,), (0,)), ((), ())), preferred_element_type=f32)
                upd = jnp.concatenate(
                    [upd[0:GLA_DK, 0:GLA_DV], upd[GLA_DK:2 * GLA_DK, GLA_DV:2 * GLA_DV]], axis=0)
                e_col = jnp.broadcast_to(e_b[CHUNK - 1:CHUNK, ln], (LANES, LANES)).T
                terms[c].append((lhs_heads, vb, upd, e_col))
        return terms

    def gla(chunks):
        terms = {}
        for c0 in range(chunks[0], chunks[-1] + 1, GLA_GROUP):
            grp = range(c0, c0 + GLA_GROUP)
            terms.update(group_terms(grp, group_sums(grp)))
        states = {}
        for p in range(n_pairs):
            ln = slice(p * LANES, (p + 1) * LANES)
            s_pair = s_ref[ln, :]
            for c in chunks:
                states[c, p] = s_pair.astype(bf16)
                _, _, upd, e_col = terms[c][p]
                s_pair = e_col * s_pair + upd
            s_ref[ln, :] = s_pair
        for c in chunks:
            for p in range(n_pairs):
                lhs_heads, vb, _, _ = terms[c][p]
                for hh in range(2):
                    h = 2 * p + hh
                    rhs = jnp.concatenate([states[c, p], vb[:, h * GLA_DV:(h + 1) * GLA_DV]], axis=0)
                    ogla_s[c * CHUNK:(c + 1) * CHUNK, h * GLA_DV:(h + 1) * GLA_DV] = jnp.dot(
                        lhs_heads[hh], rhs, preferred_element_type=f32)

    lane_lo_w = lax.broadcasted_iota(jnp.int32, (WINDOW, LANES), 1) < SWA_HD
    row_lo = lax.broadcasted_iota(jnp.int32, (2 * WINDOW, 1), 0) < WINDOW

    def swa(blocks):
        for blk in blocks:
            rs = slice(blk * WINDOW, (blk + 1) * WINDOW)
            sq = p_s[rs, O_SQ:O_SQ + SWA_W].astype(bf16)
            k_cur, v_cur = p_s[rs, O_SK:O_SK + SWA_KVW], p_s[rs, O_SV:O_SV + SWA_KVW]
            k2 = jnp.concatenate([kprev_ref[...], k_cur], axis=0).astype(bf16)
            v2 = jnp.concatenate([vprev_ref[...], v_cur], axis=0).astype(bf16)
            v2e = jnp.concatenate([v2, jnp.ones((2 * WINDOW, LANES), bf16)], axis=1)
            kmask = kmask_ref[jnp.where(t > 0, 0, 1)] if blk == 0 else kmask_ref[0]
            k2m = jnp.concatenate([k2, kmask], axis=1)
            for tt in range(SWA_GROUP):
                qt = sq[:, tt * LANES:(tt + 1) * LANES]
                lhs = jnp.concatenate([jnp.where(lane_lo_w, qt, 0), jnp.where(lane_lo_w, 0, qt)], axis=0)
                lhs = jnp.concatenate([lhs, qhot_ref[...]], axis=1)
                s = lax.dot_general(lhs, k2m, (((1,), (1,)), ((), ())), preferred_element_type=f32)
                sink = jnp.where(row_lo, sinks_ref[tt], sinks_ref[SWA_GROUP + tt])
                m = jnp.maximum(jnp.max(s, axis=-1, keepdims=True), sink)
                e = jnp.exp2((s - m).astype(bf16))
                o2 = jnp.dot(e, v2e, preferred_element_type=f32)
                o = o2[:, :LANES] / (o2[:, LANES:] + jnp.exp2(sink - m))
                oswa_s[rs, tt * LANES:(tt + 1) * LANES] = jnp.where(lane_lo_w, o[:WINDOW], o[WINDOW:])
            kprev_ref[...] = k_cur
            vprev_ref[...] = v_cur

    def merge(rg):
        y_ref[rg, :] = _merge(x_ref[rg, :], ogla_s[rg, :], p_s[rg, O_GG:O_GG + GLA_W], oswa_s[rg, :],
                              p_s[rg, O_SG:O_SG + SWA_W], gn_ref, wout_ref, nf_ref)

    whole = slice(0, tl)
    project(whole)
    _sample_state_update(smp_proj_ref, smp_decay_ref, smp_sink_ref, smp_s_ref, smp_ck_ref, smp_cv_ref,
                         smp_so_ref, smp_cko_ref, smp_cvo_ref, smp_og_ref, smp_os_ref)
    gla(range(tl // CHUNK))
    swa(range(tl // WINDOW))
    merge(whole)

    @pl.when(t == n_t - 1)
    def _():
        sp_ref[...] = s_ref[...]
        kn_ref[...] = kprev_ref[...].T
        vn_ref[...] = vprev_ref[...].T


def _prompt_call(x, w, cmat, lmask, kmask, qhot, smp_proj, smp_decay, smp_state, smp_ck, smp_cv):
    bsz, seq, _ = x.shape
    tl = TOK_BLOCK
    n_t = seq // tl
    n = smp_state.shape[0]
    g = n // (bsz * n_t)
    assert g * bsz * n_t == n
    step = lambda b, t: b * n_t + t
    smp = lambda *tail: pl.BlockSpec((g,) + tail, lambda b, t: (step(b, t),) + (0,) * len(tail))
    smp_rows = lambda width: pl.BlockSpec((None, g, width), lambda b, t: (step(b, t), 0, 0))
    const = lambda shape: pl.BlockSpec(shape, lambda b, t: (0,) * len(shape), pipeline_mode=pl.Buffered(1))
    return pl.pallas_call(
        functools.partial(_prompt_kernel, n_t=n_t),
        grid=(bsz, n_t),
        in_specs=[
            pl.BlockSpec(memory_space=pltpu.SMEM),
            pl.BlockSpec((None, tl, D_MODEL), lambda b, t: (b, t, 0)),
            const((1, D_MODEL)), const((D_MODEL, W_ALL)), const((LANES, GLA_KW)), const((1, GLA_KW)),
            const(cmat.shape), const(lmask.shape), const(kmask.shape), const(qhot.shape),
            const((1, GLA_W)), const((D_MODEL, D_MODEL)), const((1, D_MODEL)),
            smp_rows(W_MAIN), smp_rows(GLA_KW), const((SWA_HEADS, 1)),
            smp(GLA_KW, GLA_DV), smp(SWA_KVW, WINDOW), smp(SWA_KVW, WINDOW),
        ],
        out_specs=[
            pl.BlockSpec((None, tl, D_MODEL), lambda b, t: (b, t, 0)),
            pl.BlockSpec((None, GLA_KW, GLA_DV), lambda b, t: (b, 0, 0)),
            pl.BlockSpec((None, WINDOW, SWA_KVW), lambda b, t: (b, 0, 0)),
            pl.BlockSpec((None, WINDOW, SWA_KVW), lambda b, t: (b, 0, 0)),
            smp(GLA_KW, GLA_DV), smp(SWA_KVW, WINDOW), smp(SWA_KVW, WINDOW),
            smp(GLA_HEADS, GLA_DV), smp(SWA_HEADS, LANES),
        ],
        out_shape=[
            jax.ShapeDtypeStruct((bsz, seq, D_MODEL), f32),
            jax.ShapeDtypeStruct((bsz, GLA_KW, GLA_DV), f32),
            jax.ShapeDtypeStruct((bsz, WINDOW, SWA_KVW), f32),
            jax.ShapeDtypeStruct((bsz, WINDOW, SWA_KVW), f32),
            jax.ShapeDtypeStruct(smp_state.shape, f32), jax.ShapeDtypeStruct(smp_ck.shape, f32),
            jax.ShapeDtypeStruct(smp_cv.shape, f32),
            jax.ShapeDtypeStruct((n, GLA_HEADS, GLA_DV), f32), jax.ShapeDtypeStruct((n, SWA_HEADS, LANES), f32),
        ],
        scratch_shapes=[
            pltpu.VMEM((GLA_KW, GLA_DV), f32),
            pltpu.VMEM((WINDOW, SWA_KVW), f32), pltpu.VMEM((WINDOW, SWA_KVW), f32),
            pltpu.VMEM((tl, W_ALL), f32), pltpu.VMEM((tl, GLA_KW), f32),
            pltpu.VMEM((tl, GLA_W), f32), pltpu.VMEM((tl, SWA_W), f32),
        ],
        compiler_params=pltpu.CompilerParams(
            dimension_semantics=("arbitrary", "arbitrary"), vmem_limit_bytes=PROMPT_VMEM_LIMIT),
        name="prompt_layer",
    )(w["sinks"], x, w["norm_in"], w["w_all"], w["w_up"], w["b_gate"], cmat, lmask, kmask, qhot,
      w["gla_norm"], w["w_out"], w["norm_f"],
      smp_proj.reshape(bsz * n_t, g, W_MAIN), smp_decay.reshape(bsz * n_t, g, GLA_KW),
      w["sinks"].reshape(SWA_HEADS, 1), smp_state, smp_ck, smp_cv)


def _sample_proj_kernel(x_ref, nin_ref, w_ref, wup_ref, bg_ref, proj_ref, decay_ref):
    u = _rms(x_ref[...], nin_ref[...]).astype(bf16)
    proj_ref[...] = jnp.dot(u, w_ref[:, :W_MAIN], preferred_element_type=f32)
    glow = jnp.dot(u, w_ref[:, O_LOW:O_LOW + LANES], preferred_element_type=f32)
    decay_ref[...] = jnp.exp(_log_decay(glow, wup_ref, bg_ref))


def _sample_proj_call(xs, w):
    n = xs.shape[0]
    return pl.pallas_call(
        _sample_proj_kernel,
        out_shape=[jax.ShapeDtypeStruct((n, W_MAIN), f32), jax.ShapeDtypeStruct((n, GLA_KW), f32)],
        compiler_params=pltpu.CompilerParams(vmem_limit_bytes=VMEM_LIMIT),
        name="sample_proj",
    )(xs, w["norm_in"], w["w_all"], w["w_up"], w["b_gate"])


def _split3(x):
    as_bf16 = lambda v: v.astype(bf16).astype(f32)
    hi = as_bf16(x)
    mid = as_bf16(x - hi)
    return hi, mid, as_bf16(x - hi - mid)


def _sample_state_update(proj_ref, decay_ref, sink_ref, s_ref, ck_ref, cv_ref,
                         so_ref, cko_ref, cvo_ref, og_ref, os_ref):
    row = lax.broadcasted_iota(jnp.int32, (PACK, GLA_KW), 0)
    head_of_lane = lax.broadcasted_iota(jnp.int32, (PACK, GLA_KW), 1) // GLA_DK
    own_head = head_of_lane == row
    row_v = lax.broadcasted_iota(jnp.int32, (PACK, GLA_DV), 0)
    lane_v = lax.broadcasted_iota(jnp.int32, (PACK, GLA_DV), 1)
    piece_rows = ((row_v >= GLA_HEADS) & (row_v < GLA_HEADS + 3)).astype(f32)
    last_lane_rows = ((row_v < 3) & (lane_v == WINDOW - 1)).astype(bf16)
    newest = lax.broadcasted_iota(jnp.int32, (SWA_KVW, WINDOW), 1) == WINDOW - 1
    row_q = lax.broadcasted_iota(jnp.int32, (SWA_HEADS, LANES), 0)
    own_kv = (lax.broadcasted_iota(jnp.int32, (SWA_HEADS, LANES), 1) // SWA_HD) == (row_q // SWA_GROUP)
    sink = sink_ref[...]
    contract_rows = (((0,), (0,)), ((), ()))
    seqs = range(proj_ref.shape[0])
    lts, rts, qms, lt2s, q8s = [], [], [], [], []
    for j in seqs:
        pr = proj_ref[j:j + 1, :]
        bcast = lambda lo, width: jnp.broadcast_to(pr[:, lo:lo + width], (PACK, width))
        a_hi, a_mid, a_lo = _split3(jnp.broadcast_to(decay_ref[j:j + 1, :], (PACK, GLA_KW)))
        a_piece = jnp.where(row == GLA_HEADS, a_hi, jnp.where(row == GLA_HEADS + 1, a_mid, a_lo))
        lts.append(jnp.where(own_head, bcast(O_K, GLA_KW),
                             jnp.where((row >= GLA_HEADS) & (row < GLA_HEADS + 3), a_piece, 0.0)).astype(bf16))
        v_b = bcast(O_V, GLA_W)
        v_sel = jnp.zeros((PACK, GLA_DV), f32)
        for h in range(GLA_HEADS):
            v_sel = jnp.where(row_v == h, v_b[:, h * GLA_DV:(h + 1) * GLA_DV], v_sel)
        rts.append(jnp.concatenate([v_sel, piece_rows], axis=1).astype(bf16))
        qms.append(jnp.where(own_head, bcast(O_Q, GLA_KW), 0.0).astype(bf16))
        n_hi, n_mid, n_lo = _split3(bcast(O_SK, 2 * SWA_KVW))
        lt2s.append(jnp.where(row == 0, n_hi, jnp.where(row == 1, n_mid,
                                                        jnp.where(row == 2, n_lo, 0.0))).astype(bf16))
        sq_b = jnp.broadcast_to(pr[:, O_SQ:O_SQ + SWA_W], (SWA_HEADS, SWA_W))
        q8 = jnp.zeros((SWA_HEADS, LANES), f32)
        for gq in range(SWA_GROUP):
            q8 = jnp.where(row_q % SWA_GROUP == gq, sq_b[:, gq * LANES:(gq + 1) * LANES], q8)
        q8s.append(jnp.where(own_kv, q8, 0.0).astype(bf16))
    kv_as = [lax.dot_general(lts[j], rts[j], contract_rows, preferred_element_type=f32) for j in seqs]
    inss = [lax.dot_general(lt2s[j], last_lane_rows, contract_rows, preferred_element_type=f32) for j in seqs]
    s_news, kts, vts = [], [], []
    for j in seqs:
        s_new = kv_as[j][:, GLA_DV:] * s_ref[j] + kv_as[j][:, :GLA_DV]
        so_ref[j] = s_new
        s_news.append(s_new.astype(bf16))
        kt = jnp.where(newest, inss[j][:SWA_KVW], pltpu.roll(ck_ref[j], WINDOW - 1, axis=1))
        vt = jnp.where(newest, inss[j][SWA_KVW:], pltpu.roll(cv_ref[j], WINDOW - 1, axis=1))
        cko_ref[j] = kt
        cvo_ref[j] = vt
        kts.append(kt.astype(bf16))
        vts.append(vt.astype(bf16))
    for j in seqs:
        og_ref[j] = jnp.dot(qms[j], s_news[j], preferred_element_type=f32)[:GLA_HEADS]
    scores = [jnp.dot(q8s[j], kts[j], preferred_element_type=f32) for j in seqs]
    es, dens = [], []
    for j in seqs:
        m = jnp.maximum(jnp.max(scores[j], axis=-1, keepdims=True), sink)
        e = jnp.exp2(scores[j] - m)
        dens.append(jnp.sum(e, axis=-1, keepdims=True) + jnp.exp2(sink - m))
        es.append(e.astype(bf16))
    for j in seqs:
        o = lax.dot_general(es[j], vts[j], (((1,), (1,)), ((), ())), preferred_element_type=f32)
        os_ref[j] = o / dens[j]


def _sample_merge_kernel(x_ref, og_ref, gg_ref, os_ref, sg_ref, gn_ref, wout_ref, nf_ref, y_ref):
    y_ref[...] = _merge(x_ref[...], og_ref[...], gg_ref[...], os_ref[...], sg_ref[...],
                        gn_ref, wout_ref, nf_ref)


def _sample_merge_call(xs, og, gg, osw, sg, w):
    return pl.pallas_call(
        _sample_merge_kernel,
        out_shape=jax.ShapeDtypeStruct(xs.shape, f32),
        compiler_params=pltpu.CompilerParams(vmem_limit_bytes=VMEM_LIMIT),
        name="sample_merge",
    )(xs, og, gg, osw, sg, w["gla_norm"], w["w_out"], w["norm_f"])


def _sample_finish(xs, proj, og, os_raw, w):
    n = xs.shape[0]
    os5 = os_raw.reshape(n, SWA_KV, SWA_GROUP, SWA_KV, SWA_HD)
    os_il = jnp.stack([os5[:, kv, :, kv, :] for kv in range(SWA_KV)], axis=2).reshape(n, SWA_W)
    return _sample_merge_call(xs, og.reshape(n, GLA_W), proj[:, O_GG:O_GG + GLA_W], os_il,
                              proj[:, O_SG:O_SG + SWA_W], w)


def _cache_view(c):
    n = c.shape[1]
    return jnp.transpose(c[0], (0, 2, 3, 1)).reshape(n, SWA_KVW, WINDOW)


def _cache_unview(c):
    n = c.shape[0]
    return jnp.transpose(c.reshape(n, SWA_KV, SWA_HD, WINDOW), (0, 3, 1, 2))[None]


def kernel(x_prompt, x_sample, state_gla, cache_win_k, cache_win_v, norm_in, w_in, w_gate_up, b_gate,
           gla_norm, attn_sinks, w_out, norm_f):
    bsz = x_prompt.shape[0]
    n = x_sample.shape[0]
    w = _prep_weights(norm_in[0], w_in[0], w_gate_up[0], b_gate[0], gla_norm[0], attn_sinks[0],
                      w_out[0], norm_f)
    cmat, lmask = _chunk_tables()
    xs = x_sample.reshape(n, D_MODEL)
    proj, decay = _sample_proj_call(xs, w)
    y_p, s_p, k_p, v_p, s_s, k_s, v_s, og, os_raw = _prompt_call(
        x_prompt, w, cmat, lmask, *_swa_mask_tables(), proj, decay,
        state_gla[0].reshape(n, GLA_KW, GLA_DV), _cache_view(cache_win_k), _cache_view(cache_win_v))
    y_s = _sample_finish(xs, proj, og, os_raw, w)
    return (y_p, y_s.reshape(n, 1, D_MODEL),
            s_p.reshape(1, bsz, GLA_HEADS, GLA_DK, GLA_DV),
            _cache_unview(k_p), _cache_unview(v_p),
            s_s.reshape(1, n, GLA_HEADS, GLA_DK, GLA_DV),
            _cache_unview(k_s), _cache_unview(v_s))
```

```python
import functools

import numpy as np
import jax
import jax.numpy as jnp
from jax import lax
from jax.experimental import pallas as pl
from jax.experimental.pallas import tpu as pltpu

D_MODEL = 1024
GLA_HEADS = 4
GLA_DK = 64
GLA_DV = 128
GLA_KW = GLA_HEADS * GLA_DK
GLA_W = GLA_HEADS * GLA_DV
GLA_RANK = 16
GLA_TAU = 16.0
CHUNK = 64
SWA_HEADS = 8
SWA_HD = 64
SWA_KV = 2
SWA_GROUP = SWA_HEADS // SWA_KV
SWA_W = SWA_HEADS * SWA_HD
SWA_KVW = SWA_KV * SWA_HD
WINDOW = 128
EPS = 1e-6
NEG_INF = -1e30
LOG2E = 1.4426950408889634
LANES = 128
SUBLANES = 8

O_Q, O_K, O_V, O_GG = 0, 256, 512, 1024
O_SQ, O_SK, O_SV, O_SG, O_LOW = 1536, 2048, 2176, 2304, 2816
W_MAIN = 2816
W_ALL = W_MAIN + LANES

N_LEVELS = 6
TOK_BLOCK = 1024
GLA_GROUP = 4
PACK = 16
PROMPT_VMEM_LIMIT = 56 * 1024 * 1024
VMEM_LIMIT = 32 * 1024 * 1024

f32 = jnp.float32
bf16 = jnp.bfloat16


def _chunk_tables():
    c = CHUNK
    t = np.arange(c)[None, :]
    i = np.arange(c)[:, None]
    blocks = [(t <= i), (t > i)]
    masks = []
    for l in range(N_LEVELS):
        h = c >> (l + 1)
        m = (i // (2 * h)) * (2 * h) + h
        upper = i >= m
        blocks.append(np.where(upper, (t > m) & (t <= i), (t > i) & (t <= m)))
        jj = np.arange(c)[None, :]
        masks.append((i // (2 * h) == jj // (2 * h)) & (i % (2 * h) >= h) & (jj % (2 * h) < h))
    masks.append(np.eye(c, dtype=bool))
    cm = np.concatenate(blocks, axis=0).astype(np.float32)
    cm3 = np.concatenate([cm, cm, cm], axis=1)
    lm = np.stack(masks).astype(np.float32)
    lm = np.concatenate([lm, lm], axis=1)
    return jnp.asarray(cm3, dtype=bf16), jnp.asarray(lm, dtype=f32)


def _swa_mask_tables():
    key = np.arange(2 * WINDOW)[:, None]
    qi = np.arange(WINDOW)[None, :]
    prev_ok = (key < WINDOW) & (key > qi)
    cur_ok = (key >= WINDOW) & (key - WINDOW <= qi)
    full = np.where(prev_ok | cur_ok, 0.0, NEG_INF)
    first = np.where(cur_ok, 0.0, NEG_INF)
    onehot = (np.arange(2 * WINDOW)[:, None] % WINDOW == qi).astype(np.float32)
    return jnp.asarray(np.stack([full, first]), dtype=bf16), jnp.asarray(onehot, dtype=bf16)


I_Q, I_K, I_V, I_GG, I_LOW, I_SQ, I_SK, I_SV, I_SG = (
    int(v) for v in np.cumsum([0, 256, 256, 512, 512, 16, 512, 128, 128])[:9])


def _interleaved_rows(base, tile):
    return [(base + kv * SWA_GROUP * SWA_HD + tile * SWA_HD, SWA_HD) for kv in range(SWA_KV)]


def _weight_layout_kernel(wt_ref, wo_ref, wall_ref, wop_ref):
    def put(col, pieces, scale=None):
        rows = [wt_ref[r:r + n, :] for r, n in pieces]
        missing = LANES - sum(n for _, n in pieces)
        if missing:
            rows.append(jnp.zeros((missing, D_MODEL), f32))
        blk = jnp.concatenate(rows, axis=0) if len(rows) > 1 else rows[0]
        if scale is not None:
            blk = blk * scale
        wall_ref[:, col:col + LANES] = blk.T.astype(bf16)

    for i in range(GLA_KW // LANES):
        put(O_Q + i * LANES, [(I_Q + i * LANES, LANES)], GLA_DK ** -0.5)
        put(O_K + i * LANES, [(I_K + i * LANES, LANES)])
    for i in range(GLA_W // LANES):
        put(O_V + i * LANES, [(I_V + i * LANES, LANES)])
        put(O_GG + i * LANES, [(I_GG + i * LANES, LANES)])
        put(O_SQ + i * LANES, _interleaved_rows(I_SQ, i), SWA_HD ** -0.5 * LOG2E)
        put(O_SG + i * LANES, _interleaved_rows(I_SG, i))
    put(O_SK, [(I_SK, SWA_KVW)])
    put(O_SV, [(I_SV, SWA_KVW)])
    put(O_LOW, [(I_LOW, GLA_RANK)])
    wop_ref[:GLA_W, :] = wo_ref[:GLA_W, :].astype(bf16)
    for i in range(SWA_GROUP):
        for kv, (r, n) in enumerate(_interleaved_rows(GLA_W, i)):
            dst = GLA_W + (i * SWA_KV + kv) * SWA_HD
            wop_ref[dst:dst + n, :] = wo_ref[r:r + n, :].astype(bf16)


def _prep_weights(norm_in, w_in, w_gate_up, b_gate, gla_norm, attn_sinks, w_out, norm_f):
    w_all, w_out_p = pl.pallas_call(
        _weight_layout_kernel,
        out_shape=[jax.ShapeDtypeStruct((D_MODEL, W_ALL), bf16), jax.ShapeDtypeStruct((D_MODEL, D_MODEL), bf16)],
        compiler_params=pltpu.CompilerParams(vmem_limit_bytes=VMEM_LIMIT),
        name="weight_layout",
    )(w_in.T, w_out)
    w_up = jnp.pad(w_gate_up, ((0, LANES - GLA_RANK), (0, 0))).astype(bf16)
    return dict(
        norm_in=norm_in.reshape(1, D_MODEL), w_all=w_all, w_up=w_up,
        b_gate=b_gate.reshape(1, GLA_KW), gla_norm=jnp.tile(gla_norm, GLA_HEADS).reshape(1, GLA_W),
        sinks=attn_sinks, w_out=w_out_p, norm_f=norm_f.reshape(1, D_MODEL))


def _rms(x, gain):
    return x * lax.rsqrt(jnp.mean(x * x, axis=-1, keepdims=True) + EPS) * gain


def _log_decay(glow, wup_ref, bg_ref):
    z = jnp.dot(glow.astype(bf16), wup_ref[...], preferred_element_type=f32) + bg_ref[...]
    return (jnp.minimum(z, 0.0) - jnp.log(1.0 + jnp.exp(-jnp.abs(z)))) * (1.0 / GLA_TAU)


def _silu(x):
    return x * jax.nn.sigmoid(x)


def _merge(x, o_gla, gg, o_swa, sg, gn_ref, wout_ref, nf_ref):
    parts = []
    for h in range(GLA_HEADS):
        sl = slice(h * GLA_DV, (h + 1) * GLA_DV)
        parts.append(_rms(o_gla[:, sl], gn_ref[:, sl]) * _silu(gg[:, sl]))
    parts.append(o_swa * _silu(sg))
    um = jnp.concatenate(parts, axis=1).astype(bf16)
    hres = x + jnp.dot(um, wout_ref[...], preferred_element_type=f32)
    return _rms(hres, nf_ref[...])


def _prompt_kernel(sinks_ref, x_ref, nin_ref, w_ref, wup_ref, bg_ref, cmat_ref, lmask_ref, kmask_ref, qhot_ref,
                   gn_ref, wout_ref, nf_ref,
                   smp_proj_ref, smp_decay_ref, smp_sink_ref, smp_s_ref, smp_ck_ref, smp_cv_ref,
                   y_ref, sp_ref, kn_ref, vn_ref,
                   smp_so_ref, smp_cko_ref, smp_cvo_ref, smp_og_ref, smp_os_ref,
                   s_ref, kprev_ref, vprev_ref, p_s, g_s, ogla_s, oswa_s, *, n_t):
    t = pl.program_id(1)
    tl = x_ref.shape[0]
    n_smp = smp_s_ref.shape[0]
    smp_row0 = ((pl.program_id(0) * n_t + t) * n_smp) % SUBLANES

    @pl.when(t == 0)
    def _():
        s_ref[...] = jnp.zeros_like(s_ref)
        kprev_ref[...] = jnp.zeros_like(kprev_ref)
        vprev_ref[...] = jnp.zeros_like(vprev_ref)

    def project(rg):
        u = _rms(x_ref[rg, :], nin_ref[...]).astype(bf16)
        p_s[rg, :] = jnp.dot(u, w_ref[...], preferred_element_type=f32)
        g_s[rg, :] = _log_decay(p_s[rg, O_LOW:O_LOW + LANES], wup_ref, bg_ref)

    lane_lo = lax.broadcasted_iota(jnp.int32, (CHUNK, LANES), 1) < GLA_DK

    n_pairs = GLA_HEADS // 2

    def group_terms(chunks):
        rows = {c: slice(c * CHUNK, (c + 1) * CHUNK) for c in chunks}
        sums = {}
        for c in chunks:
            gc = g_s[rows[c], :]
            g_hi = gc.astype(bf16)
            r1 = gc - g_hi.astype(f32)
            g_mid = r1.astype(bf16)
            g_lo = (r1 - g_mid.astype(f32)).astype(bf16)
            sums[c] = jnp.dot(cmat_ref[...], jnp.concatenate([g_hi, g_mid, g_lo], axis=0),
                              preferred_element_type=f32)
        level_ops, misc = {}, {}
        for c in chunks:
            qc, kc = p_s[rows[c], O_Q:O_Q + GLA_KW], p_s[rows[c], O_K:O_K + GLA_KW]
            e_b = jnp.exp(sums[c][0:CHUNK])
            misc[c] = (e_b, qc * e_b, kc * jnp.exp(sums[c][CHUNK:2 * CHUNK]),
                       p_s[rows[c], O_V:O_V + GLA_W].astype(bf16))
            for p in range(n_pairs):
                ln = slice(p * LANES, (p + 1) * LANES)
                qp, kp = qc[:, ln], kc[:, ln]
                for l in range(N_LEVELS + 1):
                    if l < N_LEVELS:
                        e = jnp.exp(sums[c][(2 + l) * CHUNK:(3 + l) * CHUNK, ln])
                        qe, ke = (qp * e).astype(bf16), (kp * e).astype(bf16)
                    else:
                        qe, ke = qp.astype(bf16), kp.astype(bf16)
                    lhs = jnp.concatenate([jnp.where(lane_lo, qe, 0), jnp.where(lane_lo, 0, qe)], axis=0)
                    level_ops[c, p, l] = (lhs, ke)
        scores = {key: lax.dot_general(lhs, ke, (((1,), (1,)), ((), ())), preferred_element_type=f32)
                  for key, (lhs, ke) in level_ops.items()}
        terms = {}
        for c in chunks:
            e_b, qb, k_suf, vb = misc[c]
            terms[c] = []
            for p in range(n_pairs):
                ln = slice(p * LANES, (p + 1) * LANES)
                a = scores[c, p, 0] * lmask_ref[0]
                for l in range(1, N_LEVELS + 1):
                    a = a + scores[c, p, l] * lmask_ref[l]
                ab = a.astype(bf16)
                qbp = qb[:, ln].astype(bf16)
                lhs_heads = []
                for hh in range(2):
                    qbm = jnp.where(lane_lo, qbp, 0) if hh == 0 else jnp.where(lane_lo, 0, qbp)
                    lhs_heads.append(jnp.concatenate([qbm, ab[hh * CHUNK:(hh + 1) * CHUNK, :]], axis=1))
                upd = lax.dot_general(k_suf[:, ln].astype(bf16), vb[:, p * 2 * GLA_DV:(p + 1) * 2 * GLA_DV],
                                      (((0,), (0,)), ((), ())), preferred_element_type=f32)
                upd = jnp.concatenate(
                    [upd[0:GLA_DK, 0:GLA_DV], upd[GLA_DK:2 * GLA_DK, GLA_DV:2 * GLA_DV]], axis=0)
                e_col = jnp.broadcast_to(e_b[CHUNK - 1:CHUNK, ln], (LANES, LANES)).T
                terms[c].append((lhs_heads, vb, upd, e_col))
        return terms

    def gla(chunks):
        terms = {}
        for c0 in range(chunks[0], chunks[-1] + 1, GLA_GROUP):
            terms.update(group_terms(range(c0, c0 + GLA_GROUP)))
        states = {}
        for p in range(n_pairs):
            ln = slice(p * LANES, (p + 1) * LANES)
            s_pair = s_ref[ln, :]
            for c in chunks:
                states[c, p] = s_pair.astype(bf16)
                _, _, upd, e_col = terms[c][p]
                s_pair = e_col * s_pair + upd
            s_ref[ln, :] = s_pair
        for c in chunks:
            for p in range(n_pairs):
                lhs_heads, vb, _, _ = terms[c][p]
                for hh in range(2):
                    h = 2 * p + hh
                    rhs = jnp.concatenate([states[c, p], vb[:, h * GLA_DV:(h + 1) * GLA_DV]], axis=0)
                    ogla_s[c * CHUNK:(c + 1) * CHUNK, h * GLA_DV:(h + 1) * GLA_DV] = jnp.dot(
                        lhs_heads[hh], rhs, preferred_element_type=f32)

    lane_lo_w = lax.broadcasted_iota(jnp.int32, (WINDOW, LANES), 1) < SWA_HD
    row_lo = lax.broadcasted_iota(jnp.int32, (2 * WINDOW, 1), 0) < WINDOW

    def swa(blocks):
        for blk in blocks:
            rs = slice(blk * WINDOW, (blk + 1) * WINDOW)
            sq = p_s[rs, O_SQ:O_SQ + SWA_W].astype(bf16)
            k_cur, v_cur = p_s[rs, O_SK:O_SK + SWA_KVW], p_s[rs, O_SV:O_SV + SWA_KVW]
            k2 = jnp.concatenate([kprev_ref[...], k_cur], axis=0).astype(bf16)
            v2 = jnp.concatenate([vprev_ref[...], v_cur], axis=0).astype(bf16)
            kmask = kmask_ref[jnp.where(t > 0, 0, 1)] if blk == 0 else kmask_ref[0]
            k2m = jnp.concatenate([k2, kmask], axis=1)
            for tt in range(SWA_GROUP):
                qt = sq[:, tt * LANES:(tt + 1) * LANES]
                lhs = jnp.concatenate([jnp.where(lane_lo_w, qt, 0), jnp.where(lane_lo_w, 0, qt)], axis=0)
                lhs = jnp.concatenate([lhs, qhot_ref[...]], axis=1)
                s = lax.dot_general(lhs, k2m, (((1,), (1,)), ((), ())), preferred_element_type=f32)
                sink = jnp.where(row_lo, sinks_ref[tt] * LOG2E, sinks_ref[SWA_GROUP + tt] * LOG2E)
                m = jnp.maximum(jnp.max(s, axis=-1, keepdims=True), sink)
                e = jnp.exp2(s - m)
                r = 1.0 / (jnp.sum(e, axis=-1, keepdims=True) + jnp.exp2(sink - m))
                o2 = jnp.dot(e.astype(bf16), v2, preferred_element_type=f32)
                oswa_s[rs, tt * LANES:(tt + 1) * LANES] = jnp.where(
                    lane_lo_w, o2[:WINDOW] * r[:WINDOW], o2[WINDOW:] * r[WINDOW:])
            kprev_ref[...] = k_cur
            vprev_ref[...] = v_cur

    def merge(rg):
        y_ref[rg, :] = _merge(x_ref[rg, :], ogla_s[rg, :], p_s[rg, O_GG:O_GG + GLA_W], oswa_s[rg, :],
                              p_s[rg, O_SG:O_SG + SWA_W], gn_ref, wout_ref, nf_ref)

    whole = slice(0, tl)
    project(whole)
    _sample_state_update(smp_row0, smp_proj_ref, smp_decay_ref, smp_sink_ref, smp_s_ref, smp_ck_ref, smp_cv_ref,
                         smp_so_ref, smp_cko_ref, smp_cvo_ref, smp_og_ref, smp_os_ref)
    gla(range(tl // CHUNK))
    swa(range(tl // WINDOW))
    merge(whole)

    @pl.when(t == n_t - 1)
    def _():
        sp_ref[...] = s_ref[...]
        kn_ref[...] = kprev_ref[...].T
        vn_ref[...] = vprev_ref[...].T


def _prompt_call(x, w, cmat, lmask, kmask, qhot, smp_proj, smp_decay, smp_state, smp_ck, smp_cv):
    bsz, seq, _ = x.shape
    tl = TOK_BLOCK
    n_t = seq // tl
    n = smp_state.shape[0]
    g = n // (bsz * n_t)
    assert g * bsz * n_t == n
    step = lambda b, t: b * n_t + t
    smp = lambda *tail: pl.BlockSpec((g,) + tail, lambda b, t: (step(b, t),) + (0,) * len(tail))
    assert SUBLANES % g == 0
    smp_rows = lambda width: pl.BlockSpec((SUBLANES, width), lambda b, t: (step(b, t) * g // SUBLANES, 0))
    const = lambda shape: pl.BlockSpec(shape, lambda b, t: (0,) * len(shape), pipeline_mode=pl.Buffered(1))
    return pl.pallas_call(
        functools.partial(_prompt_kernel, n_t=n_t),
        grid=(bsz, n_t),
        in_specs=[
            pl.BlockSpec(memory_space=pltpu.SMEM),
            pl.BlockSpec((None, tl, D_MODEL), lambda b, t: (b, t, 0)),
            const((1, D_MODEL)), const((D_MODEL, W_ALL)), const((LANES, GLA_KW)), const((1, GLA_KW)),
            const(cmat.shape), const(lmask.shape), const(kmask.shape), const(qhot.shape),
            const((1, GLA_W)), const((D_MODEL, D_MODEL)), const((1, D_MODEL)),
            smp_rows(W_MAIN), smp_rows(GLA_KW), const((SWA_HEADS, 1)),
            smp(GLA_KW, GLA_DV), smp(SWA_KVW, WINDOW), smp(SWA_KVW, WINDOW),
        ],
        out_specs=[
            pl.BlockSpec((None, tl, D_MODEL), lambda b, t: (b, t, 0)),
            pl.BlockSpec((None, GLA_KW, GLA_DV), lambda b, t: (b, 0, 0)),
            pl.BlockSpec((None, WINDOW, SWA_KVW), lambda b, t: (b, 0, 0)),
            pl.BlockSpec((None, WINDOW, SWA_KVW), lambda b, t: (b, 0, 0)),
            smp(GLA_KW, GLA_DV), smp(SWA_KVW, WINDOW), smp(SWA_KVW, WINDOW),
            smp(GLA_HEADS, GLA_DV), smp(SWA_HEADS, LANES),
        ],
        out_shape=[
            jax.ShapeDtypeStruct((bsz, seq, D_MODEL), f32),
            jax.ShapeDtypeStruct((bsz, GLA_KW, GLA_DV), f32),
            jax.ShapeDtypeStruct((bsz, WINDOW, SWA_KVW), f32),
            jax.ShapeDtypeStruct((bsz, WINDOW, SWA_KVW), f32),
            jax.ShapeDtypeStruct(smp_state.shape, f32), jax.ShapeDtypeStruct(smp_ck.shape, f32),
            jax.ShapeDtypeStruct(smp_cv.shape, f32),
            jax.ShapeDtypeStruct((n, GLA_HEADS, GLA_DV), f32), jax.ShapeDtypeStruct((n, SWA_HEADS, LANES), f32),
        ],
        scratch_shapes=[
            pltpu.VMEM((GLA_KW, GLA_DV), f32),
            pltpu.VMEM((WINDOW, SWA_KVW), f32), pltpu.VMEM((WINDOW, SWA_KVW), f32),
            pltpu.VMEM((tl, W_ALL), f32), pltpu.VMEM((tl, GLA_KW), f32),
            pltpu.VMEM((tl, GLA_W), f32), pltpu.VMEM((tl, SWA_W), f32),
        ],
        compiler_params=pltpu.CompilerParams(
            dimension_semantics=("arbitrary", "arbitrary"), vmem_limit_bytes=PROMPT_VMEM_LIMIT),
        name="prompt_layer",
    )(w["sinks"], x, w["norm_in"], w["w_all"], w["w_up"], w["b_gate"], cmat, lmask, kmask, qhot,
      w["gla_norm"], w["w_out"], w["norm_f"],
      smp_proj, smp_decay,
      w["sinks"].reshape(SWA_HEADS, 1), smp_state, smp_ck, smp_cv)


def _sample_proj_kernel(x_ref, nin_ref, w_ref, wup_ref, bg_ref, proj_ref, decay_ref):
    u = _rms(x_ref[:, 0, :], nin_ref[...]).astype(bf16)
    proj_ref[...] = jnp.dot(u, w_ref[:, :W_MAIN], preferred_element_type=f32)
    glow = jnp.dot(u, w_ref[:, O_LOW:O_LOW + LANES], preferred_element_type=f32)
    decay_ref[...] = jnp.exp(_log_decay(glow, wup_ref, bg_ref))


def _sample_proj_call(xs, w):
    n = xs.shape[0]
    return pl.pallas_call(
        _sample_proj_kernel,
        out_shape=[jax.ShapeDtypeStruct((n, W_MAIN), f32), jax.ShapeDtypeStruct((n, GLA_KW), f32)],
        compiler_params=pltpu.CompilerParams(vmem_limit_bytes=VMEM_LIMIT),
        name="sample_proj",
    )(xs, w["norm_in"], w["w_all"], w["w_up"], w["b_gate"])


def _split3(x):
    as_bf16 = lambda v: v.astype(bf16).astype(f32)
    hi = as_bf16(x)
    mid = as_bf16(x - hi)
    return hi, mid, as_bf16(x - hi - mid)


def _sample_state_update(row0, proj_ref, decay_ref, sink_ref, s_ref, ck_ref, cv_ref,
                         so_ref, cko_ref, cvo_ref, og_ref, os_ref):
    row = lax.broadcasted_iota(jnp.int32, (PACK, GLA_KW), 0)
    head_of_lane = lax.broadcasted_iota(jnp.int32, (PACK, GLA_KW), 1) // GLA_DK
    own_head = head_of_lane == row
    row_v = lax.broadcasted_iota(jnp.int32, (PACK, GLA_DV), 0)
    lane_v = lax.broadcasted_iota(jnp.int32, (PACK, GLA_DV), 1)
    piece_rows = ((row_v >= GLA_HEADS) & (row_v < GLA_HEADS + 3)).astype(f32)
    last_lane_rows = ((row_v < 3) & (lane_v == WINDOW - 1)).astype(bf16)
    newest = lax.broadcasted_iota(jnp.int32, (SWA_KVW, WINDOW), 1) == WINDOW - 1
    row_q = lax.broadcasted_iota(jnp.int32, (SWA_HEADS, LANES), 0)
    own_kv = (lax.broadcasted_iota(jnp.int32, (SWA_HEADS, LANES), 1) // SWA_HD) == (row_q // SWA_GROUP)
    sink = sink_ref[...] * LOG2E
    contract_rows = (((0,), (0,)), ((), ()))
    seqs = range(s_ref.shape[0])
    lts, rts, qms, lt2s, q8s = [], [], [], [], []
    for j in seqs:
        pr = proj_ref[pl.ds(row0 + j, 1), :]
        bcast = lambda lo, width: jnp.broadcast_to(pr[:, lo:lo + width], (PACK, width))
        a_hi, a_mid, a_lo = _split3(jnp.broadcast_to(decay_ref[pl.ds(row0 + j, 1), :], (PACK, GLA_KW)))
        a_piece = jnp.where(row == GLA_HEADS, a_hi, jnp.where(row == GLA_HEADS + 1, a_mid, a_lo))
        lts.append(jnp.where(own_head, bcast(O_K, GLA_KW),
                             jnp.where((row >= GLA_HEADS) & (row < GLA_HEADS + 3), a_piece, 0.0)).astype(bf16))
        v_b = bcast(O_V, GLA_W)
        v_sel = jnp.zeros((PACK, GLA_DV), f32)
        for h in range(GLA_HEADS):
            v_sel = jnp.where(row_v == h, v_b[:, h * GLA_DV:(h + 1) * GLA_DV], v_sel)
        rts.append(jnp.concatenate([v_sel, piece_rows], axis=1).astype(bf16))
        qms.append(jnp.where(own_head, bcast(O_Q, GLA_KW), 0.0).astype(bf16))
        n_hi, n_mid, n_lo = _split3(bcast(O_SK, 2 * SWA_KVW))
        lt2s.append(jnp.where(row == 0, n_hi, jnp.where(row == 1, n_mid,
                                                        jnp.where(row == 2, n_lo, 0.0))).astype(bf16))
        sq_b = jnp.broadcast_to(pr[:, O_SQ:O_SQ + SWA_W], (SWA_HEADS, SWA_W))
        q8 = jnp.zeros((SWA_HEADS, LANES), f32)
        for gq in range(SWA_GROUP):
            q8 = jnp.where(row_q % SWA_GROUP == gq, sq_b[:, gq * LANES:(gq + 1) * LANES], q8)
        q8s.append(jnp.where(own_kv, q8, 0.0).astype(bf16))
    kv_as = [lax.dot_general(lts[j], rts[j], contract_rows, preferred_element_type=f32) for j in seqs]
    inss = [lax.dot_general(lt2s[j], last_lane_rows, contract_rows, preferred_element_type=f32) for j in seqs]
    s_news, kts, vts = [], [], []
    for j in seqs:
        s_new = kv_as[j][:, GLA_DV:] * s_ref[j] + kv_as[j][:, :GLA_DV]
        so_ref[j] = s_new
        s_news.append(s_new.astype(bf16))
        kt = jnp.where(newest, inss[j][:SWA_KVW], pltpu.roll(ck_ref[j], WINDOW - 1, axis=1))
        vt = jnp.where(newest, inss[j][SWA_KVW:], pltpu.roll(cv_ref[j], WINDOW - 1, axis=1))
        cko_ref[j] = kt
        cvo_ref[j] = vt
        kts.append(kt.astype(bf16))
        vts.append(vt.astype(bf16))
    for j in seqs:
        og_ref[j] = jnp.dot(qms[j], s_news[j], preferred_element_type=f32)[:GLA_HEADS]
    scores = [jnp.dot(q8s[j], kts[j], preferred_element_type=f32) for j in seqs]
    es, dens = [], []
    for j in seqs:
        m = jnp.maximum(jnp.max(scores[j], axis=-1, keepdims=True), sink)
        e = jnp.exp2(scores[j] - m)
        dens.append(jnp.sum(e, axis=-1, keepdims=True) + jnp.exp2(sink - m))
        es.append(e.astype(bf16))
    for j in seqs:
        o = lax.dot_general(es[j], vts[j], (((1,), (1,)), ((), ())), preferred_element_type=f32)
        os_ref[j] = o / dens[j]


def _sample_merge_kernel(x_ref, og_ref, proj_ref, os_ref, gn_ref, wout_ref, nf_ref, y_ref):
    lane_lo = lax.broadcasted_iota(jnp.int32, (x_ref.shape[0], LANES), 1) < SWA_HD
    o_gla = jnp.concatenate([og_ref[:, h, :] for h in range(GLA_HEADS)], axis=1)
    o_swa = jnp.concatenate([jnp.where(lane_lo, os_ref[:, g, :], os_ref[:, SWA_GROUP + g, :])
                             for g in range(SWA_GROUP)], axis=1)
    y_ref[:, 0, :] = _merge(x_ref[:, 0, :], o_gla, proj_ref[:, O_GG:O_GG + GLA_W], o_swa,
                            proj_ref[:, O_SG:O_SG + SWA_W], gn_ref, wout_ref, nf_ref)


def _sample_finish(xs, proj, og, os_raw, w):
    return pl.pallas_call(
        _sample_merge_kernel,
        out_shape=jax.ShapeDtypeStruct(xs.shape, f32),
        compiler_params=pltpu.CompilerParams(vmem_limit_bytes=VMEM_LIMIT),
        name="sample_merge",
    )(xs, og, proj, os_raw, w["gla_norm"], w["w_out"], w["norm_f"])


def _cache_view(c):
    n = c.shape[1]
    return jnp.transpose(c[0], (0, 2, 3, 1)).reshape(n, SWA_KVW, WINDOW)


def _cache_unview(c):
    n = c.shape[0]
    return jnp.transpose(c.reshape(n, SWA_KV, SWA_HD, WINDOW), (0, 3, 1, 2))[None]


def kernel(x_prompt, x_sample, state_gla, cache_win_k, cache_win_v, norm_in, w_in, w_gate_up, b_gate,
           gla_norm, attn_sinks, w_out, norm_f):
    bsz = x_prompt.shape[0]
    n = x_sample.shape[0]
    w = _prep_weights(norm_in[0], w_in[0], w_gate_up[0], b_gate[0], gla_norm[0], attn_sinks[0],
                      w_out[0], norm_f)
    cmat, lmask = _chunk_tables()
    proj, decay = _sample_proj_call(x_sample, w)
    y_p, s_p, k_p, v_p, s_s, k_s, v_s, og, os_raw = _prompt_call(
        x_prompt, w, cmat, lmask, *_swa_mask_tables(), proj, decay,
        state_gla[0].reshape(n, GLA_KW, GLA_DV), _cache_view(cache_win_k), _cache_view(cache_win_v))
    y_s = _sample_finish(x_sample, proj, og, os_raw, w)
    return (y_p, y_s,
            s_p.reshape(1, bsz, GLA_HEADS, GLA_DK, GLA_DV),
            _cache_unview(k_p), _cache_unview(v_p),
            s_s.reshape(1, n, GLA_HEADS, GLA_DK, GLA_DV),
            _cache_unview(k_s), _cache_unview(v_s))
```

```python
import functools

import numpy as np
import jax
import jax.numpy as jnp
from jax import lax
from jax.experimental import pallas as pl
from jax.experimental.pallas import tpu as pltpu

D_MODEL = 1024
GLA_HEADS = 4
GLA_DK = 64
GLA_DV = 128
GLA_KW = GLA_HEADS * GLA_DK
GLA_W = GLA_HEADS * GLA_DV
GLA_RANK = 16
GLA_TAU = 16.0
CHUNK = 64
SWA_HEADS = 8
SWA_HD = 64
SWA_KV = 2
SWA_GROUP = SWA_HEADS // SWA_KV
SWA_W = SWA_HEADS * SWA_HD
SWA_KVW = SWA_KV * SWA_HD
WINDOW = 128
EPS = 1e-6
NEG_INF = -1e30
LOG2E = 1.4426950408889634
LANES = 128

O_Q, O_K, O_V, O_GG = 0, 256, 512, 1024
O_SQ, O_SK, O_SV, O_SG, O_LOW = 1536, 2048, 2176, 2304, 2816
W_MAIN = 2816
W_ALL = W_MAIN + LANES

N_LEVELS = 6
TOK_BLOCK = 1024
GLA_GROUP = 4
PACK = 16
PROMPT_VMEM_LIMIT = 56 * 1024 * 1024
VMEM_LIMIT = 32 * 1024 * 1024

f32 = jnp.float32
bf16 = jnp.bfloat16


def _chunk_tables():
    c = CHUNK
    t = np.arange(c)[None, :]
    i = np.arange(c)[:, None]
    blocks = [(t <= i), (t > i)]
    masks = []
    for l in range(N_LEVELS):
        h = c >> (l + 1)
        m = (i // (2 * h)) * (2 * h) + h
        upper = i >= m
        blocks.append(np.where(upper, (t > m) & (t <= i), (t > i) & (t <= m)))
        jj = np.arange(c)[None, :]
        masks.append((i // (2 * h) == jj // (2 * h)) & (i % (2 * h) >= h) & (jj % (2 * h) < h))
    masks.append(np.eye(c, dtype=bool))
    cm = np.concatenate(blocks, axis=0).astype(np.float32)
    cm3 = np.concatenate([cm, cm, cm], axis=1)
    lm = np.stack(masks).astype(np.float32)
    lm = np.concatenate([lm, lm], axis=1)
    return jnp.asarray(cm3, dtype=bf16), jnp.asarray(lm, dtype=f32)


def _swa_mask_tables():
    key = np.arange(2 * WINDOW)[:, None]
    qi = np.arange(WINDOW)[None, :]
    prev_ok = (key < WINDOW) & (key > qi)
    cur_ok = (key >= WINDOW) & (key - WINDOW <= qi)
    full = np.where(prev_ok | cur_ok, 0.0, NEG_INF)
    first = np.where(cur_ok, 0.0, NEG_INF)
    onehot = (np.arange(2 * WINDOW)[:, None] % WINDOW == qi).astype(np.float32)
    row_q = np.arange(2 * WINDOW)[:, None] % WINDOW
    slot_is_prev = (np.arange(WINDOW)[None, :] > row_q).astype(np.float32)
    split = np.stack([slot_is_prev, 1.0 - slot_is_prev])
    return (jnp.asarray(np.stack([full, first]), dtype=bf16), jnp.asarray(onehot, dtype=bf16),
            jnp.asarray(split, dtype=bf16))


I_Q, I_K, I_V, I_GG, I_LOW, I_SQ, I_SK, I_SV, I_SG = (
    int(v) for v in np.cumsum([0, 256, 256, 512, 512, 16, 512, 128, 128])[:9])


def _interleaved_rows(base, tile):
    return [(base + kv * SWA_GROUP * SWA_HD + tile * SWA_HD, SWA_HD) for kv in range(SWA_KV)]


def _weight_layout_kernel(wt_ref, wo_ref, wall_ref, wop_ref):
    def put(col, pieces, scale=None):
        rows = [wt_ref[r:r + n, :] for r, n in pieces]
        missing = LANES - sum(n for _, n in pieces)
        if missing:
            rows.append(jnp.zeros((missing, D_MODEL), f32))
        blk = jnp.concatenate(rows, axis=0) if len(rows) > 1 else rows[0]
        if scale is not None:
            blk = blk * scale
        wall_ref[:, col:col + LANES] = blk.T.astype(bf16)

    for i in range(GLA_KW // LANES):
        put(O_Q + i * LANES, [(I_Q + i * LANES, LANES)], GLA_DK ** -0.5)
        put(O_K + i * LANES, [(I_K + i * LANES, LANES)])
    for i in range(GLA_W // LANES):
        put(O_V + i * LANES, [(I_V + i * LANES, LANES)])
        put(O_GG + i * LANES, [(I_GG + i * LANES, LANES)])
        put(O_SQ + i * LANES, _interleaved_rows(I_SQ, i), SWA_HD ** -0.5 * LOG2E)
        put(O_SG + i * LANES, _interleaved_rows(I_SG, i))
    put(O_SK, [(I_SK, SWA_KVW)])
    put(O_SV, [(I_SV, SWA_KVW)])
    put(O_LOW, [(I_LOW, GLA_RANK)])
    wop_ref[:GLA_W, :] = wo_ref[:GLA_W, :].astype(bf16)
    for i in range(SWA_GROUP):
        for kv, (r, n) in enumerate(_interleaved_rows(GLA_W, i)):
            dst = GLA_W + (i * SWA_KV + kv) * SWA_HD
            wop_ref[dst:dst + n, :] = wo_ref[r:r + n, :].astype(bf16)


def _prep_weights(norm_in, w_in, w_gate_up, b_gate, gla_norm, attn_sinks, w_out, norm_f):
    w_all, w_out_p = pl.pallas_call(
        _weight_layout_kernel,
        out_shape=[jax.ShapeDtypeStruct((D_MODEL, W_ALL), bf16), jax.ShapeDtypeStruct((D_MODEL, D_MODEL), bf16)],
        compiler_params=pltpu.CompilerParams(vmem_limit_bytes=VMEM_LIMIT),
        name="weight_layout",
    )(w_in.T, w_out)
    w_up = jnp.pad(w_gate_up, ((0, LANES - GLA_RANK), (0, 0))).astype(bf16)
    return dict(
        norm_in=norm_in.reshape(1, D_MODEL), w_all=w_all, w_up=w_up,
        b_gate=b_gate.reshape(1, GLA_KW), gla_norm=jnp.tile(gla_norm, GLA_HEADS).reshape(1, GLA_W),
        sinks=attn_sinks * LOG2E, w_out=w_out_p, norm_f=norm_f.reshape(1, D_MODEL))


def _rms(x, gain):
    return x * lax.rsqrt(jnp.mean(x * x, axis=-1, keepdims=True) + EPS) * gain


def _log_decay(glow, wup_ref, bg_ref):
    z = jnp.dot(glow.astype(bf16), wup_ref[...], preferred_element_type=f32) + bg_ref[...]
    return (jnp.minimum(z, 0.0) - jnp.log(1.0 + jnp.exp(-jnp.abs(z)))) * (1.0 / GLA_TAU)


def _silu(x):
    return x * jax.nn.sigmoid(x)


def _merge(x, o_gla, gg, o_swa, sg, gn_ref, wout_ref, nf_ref):
    parts = []
    for h in range(GLA_HEADS):
        sl = slice(h * GLA_DV, (h + 1) * GLA_DV)
        parts.append(_rms(o_gla[:, sl], gn_ref[:, sl]) * _silu(gg[:, sl]))
    parts.append(o_swa * _silu(sg))
    um = jnp.concatenate(parts, axis=1).astype(bf16)
    hres = x + jnp.dot(um, wout_ref[...], preferred_element_type=f32)
    return _rms(hres, nf_ref[...])


def _prompt_kernel(sinks_ref, x_ref, nin_ref, w_ref, wup_ref, bg_ref, cmat_ref, lmask_ref, kmask_ref, qhot_ref,
                   split_ref,
                   gn_ref, wout_ref, nf_ref,
                   smp_proj_ref, smp_decay_ref, smp_sink_ref, smp_s_ref, smp_ck_ref, smp_cv_ref,
                   y_ref, sp_ref, kn_ref, vn_ref,
                   smp_so_ref, smp_cko_ref, smp_cvo_ref, smp_og_ref, smp_os_ref,
                   s_ref, kprev_ref, vprev_ref, p_s, g_s, ogla_s, oswa_s, *, n_t):
    t = pl.program_id(1)
    tl = x_ref.shape[0]

    @pl.when(t == 0)
    def _():
        s_ref[...] = jnp.zeros_like(s_ref)
        kprev_ref[...] = jnp.zeros_like(kprev_ref)
        vprev_ref[...] = jnp.zeros_like(vprev_ref)

    def project(rg):
        u = _rms(x_ref[rg, :], nin_ref[...]).astype(bf16)
        p_s[rg, :] = jnp.dot(u, w_ref[...], preferred_element_type=f32)
        g_s[rg, :] = _log_decay(p_s[rg, O_LOW:O_LOW + LANES], wup_ref, bg_ref)

    lane_lo = lax.broadcasted_iota(jnp.int32, (CHUNK, LANES), 1) < GLA_DK

    n_pairs = GLA_HEADS // 2

    def group_terms(chunks):
        rows = {c: slice(c * CHUNK, (c + 1) * CHUNK) for c in chunks}
        sums = {}
        for c in chunks:
            gc = g_s[rows[c], :]
            g_hi = gc.astype(bf16)
            r1 = gc - g_hi.astype(f32)
            g_mid = r1.astype(bf16)
            g_lo = (r1 - g_mid.astype(f32)).astype(bf16)
            sums[c] = jnp.dot(cmat_ref[...], jnp.concatenate([g_hi, g_mid, g_lo], axis=0),
                              preferred_element_type=f32)
        level_ops, misc = {}, {}
        for c in chunks:
            qc, kc = p_s[rows[c], O_Q:O_Q + GLA_KW], p_s[rows[c], O_K:O_K + GLA_KW]
            e_b = jnp.exp(sums[c][0:CHUNK])
            misc[c] = (e_b, qc * e_b, kc * jnp.exp(sums[c][CHUNK:2 * CHUNK]),
                       p_s[rows[c], O_V:O_V + GLA_W].astype(bf16))
            for p in range(n_pairs):
                ln = slice(p * LANES, (p + 1) * LANES)
                qp, kp = qc[:, ln], kc[:, ln]
                for l in range(N_LEVELS + 1):
                    if l < N_LEVELS:
                        e = jnp.exp(sums[c][(2 + l) * CHUNK:(3 + l) * CHUNK, ln])
                        qe, ke = (qp * e).astype(bf16), (kp * e).astype(bf16)
                    else:
                        qe, ke = qp.astype(bf16), kp.astype(bf16)
                    lhs = jnp.concatenate([jnp.where(lane_lo, qe, 0), jnp.where(lane_lo, 0, qe)], axis=0)
                    level_ops[c, p, l] = (lhs, ke)
        scores = {key: lax.dot_general(lhs, ke, (((1,), (1,)), ((), ())), preferred_element_type=f32)
                  for key, (lhs, ke) in level_ops.items()}
        terms = {}
        for c in chunks:
            e_b, qb, k_suf, vb = misc[c]
            terms[c] = []
            for p in range(n_pairs):
                ln = slice(p * LANES, (p + 1) * LANES)
                a = scores[c, p, 0] * lmask_ref[0]
                for l in range(1, N_LEVELS + 1):
                    a = a + scores[c, p, l] * lmask_ref[l]
                ab = a.astype(bf16)
                qbp = qb[:, ln].astype(bf16)
                lhs_heads = []
                for hh in range(2):
                    qbm = jnp.where(lane_lo, qbp, 0) if hh == 0 else jnp.where(lane_lo, 0, qbp)
                    lhs_heads.append(jnp.concatenate([qbm, ab[hh * CHUNK:(hh + 1) * CHUNK, :]], axis=1))
                upd = lax.dot_general(k_suf[:, ln].astype(bf16), vb[:, p * 2 * GLA_DV:(p + 1) * 2 * GLA_DV],
                                      (((0,), (0,)), ((), ())), preferred_element_type=f32)
                upd = jnp.concatenate(
                    [upd[0:GLA_DK, 0:GLA_DV], upd[GLA_DK:2 * GLA_DK, GLA_DV:2 * GLA_DV]], axis=0)
                e_col = jnp.broadcast_to(e_b[CHUNK - 1:CHUNK, ln], (LANES, LANES)).T
                terms[c].append((lhs_heads, vb, upd, e_col))
        return terms

    def gla(chunks):
        terms = {}
        for c0 in range(chunks[0], chunks[-1] + 1, GLA_GROUP):
            terms.update(group_terms(range(c0, c0 + GLA_GROUP)))
        states = {}
        for p in range(n_pairs):
            ln = slice(p * LANES, (p + 1) * LANES)
            s_pair = s_ref[ln, :]
            for c in chunks:
                states[c, p] = s_pair.astype(bf16)
                _, _, upd, e_col = terms[c][p]
                s_pair = e_col * s_pair + upd
            s_ref[ln, :] = s_pair
        for c in chunks:
            for p in range(n_pairs):
                lhs_heads, vb, _, _ = terms[c][p]
                for hh in range(2):
                    h = 2 * p + hh
                    rhs = jnp.concatenate([states[c, p], vb[:, h * GLA_DV:(h + 1) * GLA_DV]], axis=0)
                    ogla_s[c * CHUNK:(c + 1) * CHUNK, h * GLA_DV:(h + 1) * GLA_DV] = jnp.dot(
                        lhs_heads[hh], rhs, preferred_element_type=f32)

    lane_lo_w = lax.broadcasted_iota(jnp.int32, (WINDOW, LANES), 1) < SWA_HD
    row_lo = lax.broadcasted_iota(jnp.int32, (2 * WINDOW, 1), 0) < WINDOW

    def swa(blocks):
        for blk in blocks:
            rs = slice(blk * WINDOW, (blk + 1) * WINDOW)
            sq = p_s[rs, O_SQ:O_SQ + SWA_W].astype(bf16)
            k_cur, v_cur = p_s[rs, O_SK:O_SK + SWA_KVW], p_s[rs, O_SV:O_SV + SWA_KVW]
            k2 = jnp.concatenate([kprev_ref[...], k_cur], axis=0).astype(bf16)
            v2 = jnp.concatenate([vprev_ref[...], v_cur], axis=0).astype(bf16)
            kmask = kmask_ref[jnp.where(t > 0, 0, 1)] if blk == 0 else kmask_ref[0]
            k2m = jnp.concatenate([k2, kmask], axis=1)
            for tt in range(SWA_GROUP):
                qt = sq[:, tt * LANES:(tt + 1) * LANES]
                lhs = jnp.concatenate([jnp.where(lane_lo_w, qt, 0), jnp.where(lane_lo_w, 0, qt)], axis=0)
                lhs = jnp.concatenate([lhs, qhot_ref[...]], axis=1)
                s = lax.dot_general(lhs, k2m, (((1,), (1,)), ((), ())), preferred_element_type=f32)
                s = jnp.maximum(s[:, :WINDOW], s[:, WINDOW:])
                sink = jnp.where(row_lo, sinks_ref[tt], sinks_ref[SWA_GROUP + tt])
                m = jnp.maximum(jnp.max(s, axis=-1, keepdims=True), sink)
                e = jnp.exp2(s - m)
                r = 1.0 / (jnp.sum(e, axis=-1, keepdims=True) + jnp.exp2(sink - m))
                eb = e.astype(bf16)
                e2 = jnp.concatenate([eb * split_ref[0], eb * split_ref[1]], axis=1)
                o2 = jnp.dot(e2, v2, preferred_element_type=f32)
                oswa_s[rs, tt * LANES:(tt + 1) * LANES] = jnp.where(
                    lane_lo_w, o2[:WINDOW] * r[:WINDOW], o2[WINDOW:] * r[WINDOW:])
            kprev_ref[...] = k_cur
            vprev_ref[...] = v_cur

    def merge(rg):
        y_ref[rg, :] = _merge(x_ref[rg, :], ogla_s[rg, :], p_s[rg, O_GG:O_GG + GLA_W], oswa_s[rg, :],
                              p_s[rg, O_SG:O_SG + SWA_W], gn_ref, wout_ref, nf_ref)

    whole = slice(0, tl)
    project(whole)
    _sample_state_update(smp_proj_ref, smp_decay_ref, smp_sink_ref, smp_s_ref, smp_ck_ref, smp_cv_ref,
                         smp_so_ref, smp_cko_ref, smp_cvo_ref, smp_og_ref, smp_os_ref)
    gla(range(tl // CHUNK))
    swa(range(tl // WINDOW))
    merge(whole)

    @pl.when(t == n_t - 1)
    def _():
        sp_ref[...] = s_ref[...]
        kn_ref[...] = kprev_ref[...].T
        vn_ref[...] = vprev_ref[...].T


def _prompt_call(x, w, cmat, lmask, kmask, qhot, split, smp_proj, smp_decay, smp_state, smp_ck, smp_cv):
    bsz, seq, _ = x.shape
    tl = TOK_BLOCK
    n_t = seq // tl
    n = smp_state.shape[0]
    g = n // (bsz * n_t)
    assert g * bsz * n_t == n
    step = lambda b, t: b * n_t + t
    smp = lambda *tail: pl.BlockSpec((g,) + tail, lambda b, t: (step(b, t),) + (0,) * len(tail))
    smp_rows = lambda width: pl.BlockSpec((None, g, width), lambda b, t: (step(b, t), 0, 0))
    const = lambda shape: pl.BlockSpec(shape, lambda b, t: (0,) * len(shape), pipeline_mode=pl.Buffered(1))
    return pl.pallas_call(
        functools.partial(_prompt_kernel, n_t=n_t),
        grid=(bsz, n_t),
        in_specs=[
            pl.BlockSpec(memory_space=pltpu.SMEM),
            pl.BlockSpec((None, tl, D_MODEL), lambda b, t: (b, t, 0)),
            const((1, D_MODEL)), const((D_MODEL, W_ALL)), const((LANES, GLA_KW)), const((1, GLA_KW)),
            const(cmat.shape), const(lmask.shape), const(kmask.shape), const(qhot.shape), const(split.shape),
            const((1, GLA_W)), const((D_MODEL, D_MODEL)), const((1, D_MODEL)),
            smp_rows(W_MAIN), smp_rows(GLA_KW), const((SWA_HEADS, 1)),
            smp(GLA_KW, GLA_DV), smp(SWA_KVW, WINDOW), smp(SWA_KVW, WINDOW),
        ],
        out_specs=[
            pl.BlockSpec((None, tl, D_MODEL), lambda b, t: (b, t, 0)),
            pl.BlockSpec((None, GLA_KW, GLA_DV), lambda b, t: (b, 0, 0)),
            pl.BlockSpec((None, WINDOW, SWA_KVW), lambda b, t: (b, 0, 0)),
            pl.BlockSpec((None, WINDOW, SWA_KVW), lambda b, t: (b, 0, 0)),
            smp(GLA_KW, GLA_DV), smp(SWA_KVW, WINDOW), smp(SWA_KVW, WINDOW),
            smp(GLA_HEADS, GLA_DV), smp(SWA_HEADS, LANES),
        ],
        out_shape=[
            jax.ShapeDtypeStruct((bsz, seq, D_MODEL), f32),
            jax.ShapeDtypeStruct((bsz, GLA_KW, GLA_DV), f32),
            jax.ShapeDtypeStruct((bsz, WINDOW, SWA_KVW), f32),
            jax.ShapeDtypeStruct((bsz, WINDOW, SWA_KVW), f32),
            jax.ShapeDtypeStruct(smp_state.shape, f32), jax.ShapeDtypeStruct(smp_ck.shape, f32),
            jax.ShapeDtypeStruct(smp_cv.shape, f32),
            jax.ShapeDtypeStruct((n, GLA_HEADS, GLA_DV), f32), jax.ShapeDtypeStruct((n, SWA_HEADS, LANES), f32),
        ],
        scratch_shapes=[
            pltpu.VMEM((GLA_KW, GLA_DV), f32),
            pltpu.VMEM((WINDOW, SWA_KVW), f32), pltpu.VMEM((WINDOW, SWA_KVW), f32),
            pltpu.VMEM((tl, W_ALL), f32), pltpu.VMEM((tl, GLA_KW), f32),
            pltpu.VMEM((tl, GLA_W), f32), pltpu.VMEM((tl, SWA_W), f32),
        ],
        compiler_params=pltpu.CompilerParams(
            dimension_semantics=("arbitrary", "arbitrary"), vmem_limit_bytes=PROMPT_VMEM_LIMIT),
        name="prompt_layer",
    )(w["sinks"], x, w["norm_in"], w["w_all"], w["w_up"], w["b_gate"], cmat, lmask, kmask, qhot, split,
      w["gla_norm"], w["w_out"], w["norm_f"],
      smp_proj.reshape(bsz * n_t, g, W_MAIN), smp_decay.reshape(bsz * n_t, g, GLA_KW),
      w["sinks"].reshape(SWA_HEADS, 1), smp_state, smp_ck, smp_cv)


def _sample_proj_kernel(x_ref, nin_ref, w_ref, wup_ref, bg_ref, proj_ref, decay_ref):
    u = _rms(x_ref[...], nin_ref[...]).astype(bf16)
    proj_ref[...] = jnp.dot(u, w_ref[:, :W_MAIN], preferred_element_type=f32)
    glow = jnp.dot(u, w_ref[:, O_LOW:O_LOW + LANES], preferred_element_type=f32)
    decay_ref[...] = jnp.exp(_log_decay(glow, wup_ref, bg_ref))


def _sample_proj_call(xs, w):
    n = xs.shape[0]
    return pl.pallas_call(
        _sample_proj_kernel,
        out_shape=[jax.ShapeDtypeStruct((n, W_MAIN), f32), jax.ShapeDtypeStruct((n, GLA_KW), f32)],
        compiler_params=pltpu.CompilerParams(vmem_limit_bytes=VMEM_LIMIT),
        name="sample_proj",
    )(xs, w["norm_in"], w["w_all"], w["w_up"], w["b_gate"])


def _split3(x):
    as_bf16 = lambda v: v.astype(bf16).astype(f32)
    hi = as_bf16(x)
    mid = as_bf16(x - hi)
    return hi, mid, as_bf16(x - hi - mid)


def _sample_state_update(proj_ref, decay_ref, sink_ref, s_ref, ck_ref, cv_ref,
                         so_ref, cko_ref, cvo_ref, og_ref, os_ref):
    row = lax.broadcasted_iota(jnp.int32, (PACK, GLA_KW), 0)
    head_of_lane = lax.broadcasted_iota(jnp.int32, (PACK, GLA_KW), 1) // GLA_DK
    own_head = head_of_lane == row
    row_v = lax.broadcasted_iota(jnp.int32, (PACK, GLA_DV), 0)
    lane_v = lax.broadcasted_iota(jnp.int32, (PACK, GLA_DV), 1)
    piece_rows = ((row_v >= GLA_HEADS) & (row_v < GLA_HEADS + 3)).astype(f32)
    last_lane_rows = ((row_v < 3) & (lane_v == WINDOW - 1)).astype(bf16)
    newest = lax.broadcasted_iota(jnp.int32, (SWA_KVW, WINDOW), 1) == WINDOW - 1
    row_q = lax.broadcasted_iota(jnp.int32, (SWA_HEADS, LANES), 0)
    own_kv = (lax.broadcasted_iota(jnp.int32, (SWA_HEADS, LANES), 1) // SWA_HD) == (row_q // SWA_GROUP)
    sink = sink_ref[...]
    contract_rows = (((0,), (0,)), ((), ()))
    seqs = range(proj_ref.shape[0])
    lts, rts, qms, lt2s, q8s = [], [], [], [], []
    for j in seqs:
        pr = proj_ref[j:j + 1, :]
        bcast = lambda lo, width: jnp.broadcast_to(pr[:, lo:lo + width], (PACK, width))
        a_hi, a_mid, a_lo = _split3(jnp.broadcast_to(decay_ref[j:j + 1, :], (PACK, GLA_KW)))
        a_piece = jnp.where(row == GLA_HEADS, a_hi, jnp.where(row == GLA_HEADS + 1, a_mid, a_lo))
        lts.append(jnp.where(own_head, bcast(O_K, GLA_KW),
                             jnp.where((row >= GLA_HEADS) & (row < GLA_HEADS + 3), a_piece, 0.0)).astype(bf16))
        v_b = bcast(O_V, GLA_W)
        v_sel = jnp.zeros((PACK, GLA_DV), f32)
        for h in range(GLA_HEADS):
            v_sel = jnp.where(row_v == h, v_b[:, h * GLA_DV:(h + 1) * GLA_DV], v_sel)
        rts.append(jnp.concatenate([v_sel, piece_rows], axis=1).astype(bf16))
        qms.append(jnp.where(own_head, bcast(O_Q, GLA_KW), 0.0).astype(bf16))
        n_hi, n_mid, n_lo = _split3(bcast(O_SK, 2 * SWA_KVW))
        lt2s.append(jnp.where(row == 0, n_hi, jnp.where(row == 1, n_mid,
                                                        jnp.where(row == 2, n_lo, 0.0))).astype(bf16))
        sq_b = jnp.broadcast_to(pr[:, O_SQ:O_SQ + SWA_W], (SWA_HEADS, SWA_W))
        q8 = jnp.zeros((SWA_HEADS, LANES), f32)
        for gq in range(SWA_GROUP):
            q8 = jnp.where(row_q % SWA_GROUP == gq, sq_b[:, gq * LANES:(gq + 1) * LANES], q8)
        q8s.append(jnp.where(own_kv, q8, 0.0).astype(bf16))
    kv_as = [lax.dot_general(lts[j], rts[j], contract_rows, preferred_element_type=f32) for j in seqs]
    inss = [lax.dot_general(lt2s[j], last_lane_rows, contract_rows, preferred_element_type=f32) for j in seqs]
    s_news, kts, vts = [], [], []
    for j in seqs:
        s_new = kv_as[j][:, GLA_DV:] * s_ref[j] + kv_as[j][:, :GLA_DV]
        so_ref[j] = s_new
        s_news.append(s_new.astype(bf16))
        kt = jnp.where(newest, inss[j][:SWA_KVW], pltpu.roll(ck_ref[j], WINDOW - 1, axis=1))
        vt = jnp.where(newest, inss[j][SWA_KVW:], pltpu.roll(cv_ref[j], WINDOW - 1, axis=1))
        cko_ref[j] = kt
        cvo_ref[j] = vt
        kts.append(kt.astype(bf16))
        vts.append(vt.astype(bf16))
    for j in seqs:
        og_ref[j] = jnp.dot(qms[j], s_news[j], preferred_element_type=f32)[:GLA_HEADS]
    scores = [jnp.dot(q8s[j], kts[j], preferred_element_type=f32) for j in seqs]
    es, dens = [], []
    for j in seqs:
        m = jnp.maximum(jnp.max(scores[j], axis=-1, keepdims=True), sink)
        e = jnp.exp2(scores[j] - m)
        dens.append(jnp.sum(e, axis=-1, keepdims=True) + jnp.exp2(sink - m))
        es.append(e.astype(bf16))
    for j in seqs:
        o = lax.dot_general(es[j], vts[j], (((1,), (1,)), ((), ())), preferred_element_type=f32)
        os_ref[j] = o / dens[j]


def _sample_merge_kernel(x_ref, og_ref, gg_ref, os_ref, sg_ref, gn_ref, wout_ref, nf_ref, y_ref):
    y_ref[...] = _merge(x_ref[...], og_ref[...], gg_ref[...], os_ref[...], sg_ref[...],
                        gn_ref, wout_ref, nf_ref)


def _sample_merge_call(xs, og, gg, osw, sg, w):
    return pl.pallas_call(
        _sample_merge_kernel,
        out_shape=jax.ShapeDtypeStruct(xs.shape, f32),
        compiler_params=pltpu.CompilerParams(vmem_limit_bytes=VMEM_LIMIT),
        name="sample_merge",
    )(xs, og, gg, osw, sg, w["gla_norm"], w["w_out"], w["norm_f"])


def _sample_finish(xs, proj, og, os_raw, w):
    n = xs.shape[0]
    os5 = os_raw.reshape(n, SWA_KV, SWA_GROUP, SWA_KV, SWA_HD)
    os_il = jnp.stack([os5[:, kv, :, kv, :] for kv in range(SWA_KV)], axis=2).reshape(n, SWA_W)
    return _sample_merge_call(xs, og.reshape(n, GLA_W), proj[:, O_GG:O_GG + GLA_W], os_il,
                              proj[:, O_SG:O_SG + SWA_W], w)


def _cache_view(c):
    n = c.shape[1]
    return jnp.transpose(c[0], (0, 2, 3, 1)).reshape(n, SWA_KVW, WINDOW)


def _cache_unview(c):
    n = c.shape[0]
    return jnp.transpose(c.reshape(n, SWA_KV, SWA_HD, WINDOW), (0, 3, 1, 2))[None]


def kernel(x_prompt, x_sample, state_gla, cache_win_k, cache_win_v, norm_in, w_in, w_gate_up, b_gate,
           gla_norm, attn_sinks, w_out, norm_f):
    bsz = x_prompt.shape[0]
    n = x_sample.shape[0]
    w = _prep_weights(norm_in[0], w_in[0], w_gate_up[0], b_gate[0], gla_norm[0], attn_sinks[0],
                      w_out[0], norm_f)
    cmat, lmask = _chunk_tables()
    xs = x_sample.reshape(n, D_MODEL)
    proj, decay = _sample_proj_call(xs, w)
    y_p, s_p, k_p, v_p, s_s, k_s, v_s, og, os_raw = _prompt_call(
        x_prompt, w, cmat, lmask, *_swa_mask_tables(), proj, decay,
        state_gla[0].reshape(n, GLA_KW, GLA_DV), _cache_view(cache_win_k), _cache_view(cache_win_v))
    y_s = _sample_finish(xs, proj, og, os_raw, w)
    return (y_p, y_s.reshape(n, 1, D_MODEL),
            s_p.reshape(1, bsz, GLA_HEADS, GLA_DK, GLA_DV),
            _cache_unview(k_p), _cache_unview(v_p),
            s_s.reshape(1, n, GLA_HEADS, GLA_DK, GLA_DV),
            _cache_unview(k_s), _cache_unview(v_s))
```

```python
import functools

import numpy as np
import jax
import jax.numpy as jnp
from jax import lax
from jax.experimental import pallas as pl
from jax.experimental.pallas import tpu as pltpu

D_MODEL = 1024
GLA_HEADS = 4
GLA_DK = 64
GLA_DV = 128
GLA_KW = GLA_HEADS * GLA_DK
GLA_W = GLA_HEADS * GLA_DV
GLA_RANK = 16
GLA_TAU = 16.0
CHUNK = 64
SWA_HEADS = 8
SWA_HD = 64
SWA_KV = 2
SWA_GROUP = SWA_HEADS // SWA_KV
SWA_W = SWA_HEADS * SWA_HD
SWA_KVW = SWA_KV * SWA_HD
WINDOW = 128
EPS = 1e-6
NEG_INF = -1e30
LOG2E = 1.4426950408889634
LANES = 128

O_Q, O_K, O_V, O_GG = 0, 256, 512, 1024
O_SQ, O_SK, O_SV, O_SG, O_LOW = 1536, 2048, 2176, 2304, 2816
W_MAIN = 2816
W_ALL = W_MAIN + LANES

N_LEVELS = 6
TOK_BLOCK = 1024
GLA_GROUP = 4
PACK = 16
PROMPT_VMEM_LIMIT = 56 * 1024 * 1024
VMEM_LIMIT = 32 * 1024 * 1024

f32 = jnp.float32
bf16 = jnp.bfloat16


def _chunk_tables():
    c = CHUNK
    t = np.arange(c)[None, :]
    i = np.arange(c)[:, None]
    blocks = [(t <= i), (t > i)]
    masks = []
    for l in range(N_LEVELS):
        h = c >> (l + 1)
        m = (i // (2 * h)) * (2 * h) + h
        upper = i >= m
        blocks.append(np.where(upper, (t > m) & (t <= i), (t > i) & (t <= m)))
        jj = np.arange(c)[None, :]
        masks.append((i // (2 * h) == jj // (2 * h)) & (i % (2 * h) >= h) & (jj % (2 * h) < h))
    masks.append(np.eye(c, dtype=bool))
    cm = np.concatenate(blocks, axis=0).astype(np.float32)
    cm3 = np.concatenate([cm, cm, cm], axis=1)
    lm = np.stack(masks).astype(np.float32)
    lm = np.concatenate([lm, lm], axis=1)
    return jnp.asarray(cm3, dtype=bf16), jnp.asarray(lm, dtype=f32)


def _swa_mask_tables():
    key = np.arange(2 * WINDOW)[:, None]
    qi = np.arange(WINDOW)[None, :]
    prev_ok = (key < WINDOW) & (key > qi)
    cur_ok = (key >= WINDOW) & (key - WINDOW <= qi)
    full = np.where(prev_ok | cur_ok, 0.0, NEG_INF)
    first = np.where(cur_ok, 0.0, NEG_INF)
    onehot = (np.arange(2 * WINDOW)[:, None] % WINDOW == qi).astype(np.float32)
    row_q = np.arange(2 * WINDOW)[:, None] % WINDOW
    slot_is_prev = (np.arange(WINDOW)[None, :] > row_q).astype(np.float32)
    split = np.stack([slot_is_prev, 1.0 - slot_is_prev])
    return (jnp.asarray(np.stack([full, first]), dtype=bf16), jnp.asarray(onehot, dtype=bf16),
            jnp.asarray(split, dtype=bf16))


I_Q, I_K, I_V, I_GG, I_LOW, I_SQ, I_SK, I_SV, I_SG = (
    int(v) for v in np.cumsum([0, 256, 256, 512, 512, 16, 512, 128, 128])[:9])


def _interleaved_rows(base, tile):
    return [(base + kv * SWA_GROUP * SWA_HD + tile * SWA_HD, SWA_HD) for kv in range(SWA_KV)]


def _weight_layout_kernel(wt_ref, wo_ref, wall_ref, wop_ref):
    def put(col, pieces, scale=None):
        rows = [wt_ref[r:r + n, :] for r, n in pieces]
        missing = LANES - sum(n for _, n in pieces)
        if missing:
            rows.append(jnp.zeros((missing, D_MODEL), f32))
        blk = jnp.concatenate(rows, axis=0) if len(rows) > 1 else rows[0]
        if scale is not None:
            blk = blk * scale
        wall_ref[:, col:col + LANES] = blk.T.astype(bf16)

    for i in range(GLA_KW // LANES):
        put(O_Q + i * LANES, [(I_Q + i * LANES, LANES)], GLA_DK ** -0.5)
        put(O_K + i * LANES, [(I_K + i * LANES, LANES)])
    for i in range(GLA_W // LANES):
        put(O_V + i * LANES, [(I_V + i * LANES, LANES)])
        put(O_GG + i * LANES, [(I_GG + i * LANES, LANES)])
        put(O_SQ + i * LANES, _interleaved_rows(I_SQ, i), SWA_HD ** -0.5 * LOG2E)
        put(O_SG + i * LANES, _interleaved_rows(I_SG, i))
    put(O_SK, [(I_SK, SWA_KVW)])
    put(O_SV, [(I_SV, SWA_KVW)])
    put(O_LOW, [(I_LOW, GLA_RANK)])
    wop_ref[:GLA_W, :] = wo_ref[:GLA_W, :].astype(bf16)
    for i in range(SWA_GROUP):
        for kv, (r, n) in enumerate(_interleaved_rows(GLA_W, i)):
            dst = GLA_W + (i * SWA_KV + kv) * SWA_HD
            wop_ref[dst:dst + n, :] = wo_ref[r:r + n, :].astype(bf16)


def _prep_weights(norm_in, w_in, w_gate_up, b_gate, gla_norm, attn_sinks, w_out, norm_f):
    w_all, w_out_p = pl.pallas_call(
        _weight_layout_kernel,
        out_shape=[jax.ShapeDtypeStruct((D_MODEL, W_ALL), bf16), jax.ShapeDtypeStruct((D_MODEL, D_MODEL), bf16)],
        compiler_params=pltpu.CompilerParams(vmem_limit_bytes=VMEM_LIMIT),
        name="weight_layout",
    )(w_in.T, w_out)
    w_up = jnp.pad(w_gate_up, ((0, LANES - GLA_RANK), (0, 0))).astype(bf16)
    return dict(
        norm_in=norm_in.reshape(1, D_MODEL), w_all=w_all, w_up=w_up,
        b_gate=b_gate.reshape(1, GLA_KW), gla_norm=jnp.tile(gla_norm, GLA_HEADS).reshape(1, GLA_W),
        sinks=attn_sinks * LOG2E, w_out=w_out_p, norm_f=norm_f.reshape(1, D_MODEL))


def _rms(x, gain):
    return x * lax.rsqrt(jnp.mean(x * x, axis=-1, keepdims=True) + EPS) * gain


def _log_decay(glow, wup_ref, bg_ref):
    z = jnp.dot(glow.astype(bf16), wup_ref[...], preferred_element_type=f32) + bg_ref[...]
    return (jnp.minimum(z, 0.0) - jnp.log(1.0 + jnp.exp(-jnp.abs(z)))) * (1.0 / GLA_TAU)


def _silu(x):
    return x * jax.nn.sigmoid(x)


def _merge(x, o_gla, gg, o_swa, sg, gn_ref, wout_ref, nf_ref):
    parts = []
    for h in range(GLA_HEADS):
        sl = slice(h * GLA_DV, (h + 1) * GLA_DV)
        parts.append(_rms(o_gla[:, sl], gn_ref[:, sl]) * _silu(gg[:, sl]))
    parts.append(o_swa * _silu(sg))
    um = jnp.concatenate(parts, axis=1).astype(bf16)
    hres = x + jnp.dot(um, wout_ref[...], preferred_element_type=f32)
    return _rms(hres, nf_ref[...])


def _prompt_kernel(sinks_ref, x_ref, nin_ref, w_ref, wup_ref, bg_ref, cmat_ref, lmask_ref, kmask_ref, qhot_ref,
                   split_ref,
                   gn_ref, wout_ref, nf_ref,
                   smp_proj_ref, smp_decay_ref, smp_sink_ref, smp_s_ref, smp_ck_ref, smp_cv_ref,
                   y_ref, sp_ref, kn_ref, vn_ref,
                   smp_so_ref, smp_cko_ref, smp_cvo_ref, smp_og_ref, smp_os_ref,
                   s_ref, kprev_ref, vprev_ref, p_s, g_s, ogla_s, oswa_s, *, n_t):
    t = pl.program_id(1)
    tl = x_ref.shape[0]

    @pl.when(t == 0)
    def _():
        s_ref[...] = jnp.zeros_like(s_ref)
        kprev_ref[...] = jnp.zeros_like(kprev_ref)
        vprev_ref[...] = jnp.zeros_like(vprev_ref)

    def project(rg):
        u = _rms(x_ref[rg, :], nin_ref[...]).astype(bf16)
        p_s[rg, :] = jnp.dot(u, w_ref[...], preferred_element_type=f32)
        g_s[rg, :] = _log_decay(p_s[rg, O_LOW:O_LOW + LANES], wup_ref, bg_ref)

    lane_lo = lax.broadcasted_iota(jnp.int32, (CHUNK, LANES), 1) < GLA_DK

    n_pairs = GLA_HEADS // 2

    def group_terms(chunks):
        rows = {c: slice(c * CHUNK, (c + 1) * CHUNK) for c in chunks}
        sums = {}
        for c in chunks:
            gc = g_s[rows[c], :]
            g_hi = gc.astype(bf16)
            r1 = gc - g_hi.astype(f32)
            g_mid = r1.astype(bf16)
            g_lo = (r1 - g_mid.astype(f32)).astype(bf16)
            sums[c] = jnp.dot(cmat_ref[...], jnp.concatenate([g_hi, g_mid, g_lo], axis=0),
                              preferred_element_type=f32)
        level_ops, misc = {}, {}
        for c in chunks:
            qc, kc = p_s[rows[c], O_Q:O_Q + GLA_KW], p_s[rows[c], O_K:O_K + GLA_KW]
            e_b = jnp.exp(sums[c][0:CHUNK])
            misc[c] = (e_b, qc * e_b, kc * jnp.exp(sums[c][CHUNK:2 * CHUNK]),
                       p_s[rows[c], O_V:O_V + GLA_W].astype(bf16))
            for p in range(n_pairs):
                ln = slice(p * LANES, (p + 1) * LANES)
                qp, kp = qc[:, ln], kc[:, ln]
                for l in range(N_LEVELS + 1):
                    if l < N_LEVELS:
                        e = jnp.exp(sums[c][(2 + l) * CHUNK:(3 + l) * CHUNK, ln])
                        qe, ke = (qp * e).astype(bf16), (kp * e).astype(bf16)
                    else:
                        qe, ke = qp.astype(bf16), kp.astype(bf16)
                    lhs = jnp.concatenate([jnp.where(lane_lo, qe, 0), jnp.where(lane_lo, 0, qe)], axis=0)
                    level_ops[c, p, l] = (lhs, ke)
        scores = {key: lax.dot_general(lhs, ke, (((1,), (1,)), ((), ())), preferred_element_type=f32)
                  for key, (lhs, ke) in level_ops.items()}
        terms = {}
        for c in chunks:
            e_b, qb, k_suf, vb = misc[c]
            terms[c] = []
            for p in range(n_pairs):
                ln = slice(p * LANES, (p + 1) * LANES)
                a = scores[c, p, 0] * lmask_ref[0]
                for l in range(1, N_LEVELS + 1):
                    a = a + scores[c, p, l] * lmask_ref[l]
                ab = a.astype(bf16)
                qbp = qb[:, ln].astype(bf16)
                lhs_heads = []
                for hh in range(2):
                    qbm = jnp.where(lane_lo, qbp, 0) if hh == 0 else jnp.where(lane_lo, 0, qbp)
                    lhs_heads.append(jnp.concatenate([qbm, ab[hh * CHUNK:(hh + 1) * CHUNK, :]], axis=1))
                upd = lax.dot_general(k_suf[:, ln].astype(bf16), vb[:, p * 2 * GLA_DV:(p + 1) * 2 * GLA_DV],
                                      (((0,), (0,)), ((), ())), preferred_element_type=f32)
                upd = jnp.concatenate(
                    [upd[0:GLA_DK, 0:GLA_DV], upd[GLA_DK:2 * GLA_DK, GLA_DV:2 * GLA_DV]], axis=0)
                e_col = jnp.broadcast_to(e_b[CHUNK - 1:CHUNK, ln], (LANES, LANES)).T
                terms[c].append((lhs_heads, vb, upd, e_col))
        return terms

    def gla(chunks):
        terms = {}
        for c0 in range(chunks[0], chunks[-1] + 1, GLA_GROUP):
            terms.update(group_terms(range(c0, c0 + GLA_GROUP)))
        states = {}
        for p in range(n_pairs):
            ln = slice(p * LANES, (p + 1) * LANES)
            s_pair = s_ref[ln, :]
            for c in chunks:
                states[c, p] = s_pair.astype(bf16)
                _, _, upd, e_col = terms[c][p]
                s_pair = e_col * s_pair + upd
            s_ref[ln, :] = s_pair
        for c in chunks:
            for p in range(n_pairs):
                lhs_heads, vb, _, _ = terms[c][p]
                for hh in range(2):
                    h = 2 * p + hh
                    rhs = jnp.concatenate([states[c, p], vb[:, h * GLA_DV:(h + 1) * GLA_DV]], axis=0)
                    ogla_s[c * CHUNK:(c + 1) * CHUNK, h * GLA_DV:(h + 1) * GLA_DV] = jnp.dot(
                        lhs_heads[hh], rhs, preferred_element_type=f32)

    lane_lo_w = lax.broadcasted_iota(jnp.int32, (WINDOW, LANES), 1) < SWA_HD
    row_lo = lax.broadcasted_iota(jnp.int32, (2 * WINDOW, 1), 0) < WINDOW

    def swa(blocks):
        for blk in blocks:
            rs = slice(blk * WINDOW, (blk + 1) * WINDOW)
            sq = p_s[rs, O_SQ:O_SQ + SWA_W].astype(bf16)
            k_cur, v_cur = p_s[rs, O_SK:O_SK + SWA_KVW], p_s[rs, O_SV:O_SV + SWA_KVW]
            k2 = jnp.concatenate([kprev_ref[...], k_cur], axis=0).astype(bf16)
            v2 = jnp.concatenate([vprev_ref[...], v_cur], axis=0).astype(bf16)
            kmask = kmask_ref[jnp.where(t > 0, 0, 1)] if blk == 0 else kmask_ref[0]
            k2m = jnp.concatenate([k2, kmask], axis=1)
            for tt in range(SWA_GROUP):
                qt = sq[:, tt * LANES:(tt + 1) * LANES]
                lhs = jnp.concatenate([jnp.where(lane_lo_w, qt, 0), jnp.where(lane_lo_w, 0, qt)], axis=0)
                lhs = jnp.concatenate([lhs, qhot_ref[...]], axis=1)
                s = lax.dot_general(lhs, k2m, (((1,), (1,)), ((), ())), preferred_element_type=f32)
                s = jnp.maximum(s[:, :WINDOW], s[:, WINDOW:])
                sink = jnp.where(row_lo, sinks_ref[tt], sinks_ref[SWA_GROUP + tt])
                m = jnp.maximum(jnp.max(s, axis=-1, keepdims=True), sink)
                e = jnp.exp2(s - m)
                r = 1.0 / (jnp.sum(e, axis=-1, keepdims=True) + jnp.exp2(sink - m))
                eb = e.astype(bf16)
                e2 = jnp.concatenate([eb * split_ref[0], eb * split_ref[1]], axis=1)
                o2 = jnp.dot(e2, v2, preferred_element_type=f32)
                oswa_s[rs, tt * LANES:(tt + 1) * LANES] = jnp.where(
                    lane_lo_w, o2[:WINDOW] * r[:WINDOW], o2[WINDOW:] * r[WINDOW:])
            kprev_ref[...] = k_cur
            vprev_ref[...] = v_cur

    def merge_gla(rg):
        parts = [_rms(ogla_s[rg, h * GLA_DV:(h + 1) * GLA_DV], gn_ref[:, h * GLA_DV:(h + 1) * GLA_DV])
                 * _silu(p_s[rg, O_GG + h * GLA_DV:O_GG + (h + 1) * GLA_DV]) for h in range(GLA_HEADS)]
        um = jnp.concatenate(parts, axis=1).astype(bf16)
        y_ref[rg, :] = x_ref[rg, :] + jnp.dot(um, wout_ref[:GLA_W, :], preferred_element_type=f32)

    def merge_swa(rg):
        um = (oswa_s[rg, :] * _silu(p_s[rg, O_SG:O_SG + SWA_W])).astype(bf16)
        hres = y_ref[rg, :] + jnp.dot(um, wout_ref[GLA_W:, :], preferred_element_type=f32)
        y_ref[rg, :] = _rms(hres, nf_ref[...])

    whole = slice(0, tl)
    project(whole)
    _sample_state_update(smp_proj_ref, smp_decay_ref, smp_sink_ref, smp_s_ref, smp_ck_ref, smp_cv_ref,
                         smp_so_ref, smp_cko_ref, smp_cvo_ref, smp_og_ref, smp_os_ref)
    gla(range(tl // CHUNK))
    merge_gla(whole)
    swa(range(tl // WINDOW))
    merge_swa(whole)

    @pl.when(t == n_t - 1)
    def _():
        sp_ref[...] = s_ref[...]
        kn_ref[...] = kprev_ref[...].T
        vn_ref[...] = vprev_ref[...].T


def _prompt_call(x, w, cmat, lmask, kmask, qhot, split, smp_proj, smp_decay, smp_state, smp_ck, smp_cv):
    bsz, seq, _ = x.shape
    tl = TOK_BLOCK
    n_t = seq // tl
    n = smp_state.shape[0]
    g = n // (bsz * n_t)
    assert g * bsz * n_t == n
    step = lambda b, t: b * n_t + t
    smp = lambda *tail: pl.BlockSpec((g,) + tail, lambda b, t: (step(b, t),) + (0,) * len(tail))
    smp_rows = lambda width: pl.BlockSpec((None, g, width), lambda b, t: (step(b, t), 0, 0))
    const = lambda shape: pl.BlockSpec(shape, lambda b, t: (0,) * len(shape), pipeline_mode=pl.Buffered(1))
    return pl.pallas_call(
        functools.partial(_prompt_kernel, n_t=n_t),
        grid=(bsz, n_t),
        in_specs=[
            pl.BlockSpec(memory_space=pltpu.SMEM),
            pl.BlockSpec((None, tl, D_MODEL), lambda b, t: (b, t, 0)),
            const((1, D_MODEL)), const((D_MODEL, W_ALL)), const((LANES, GLA_KW)), const((1, GLA_KW)),
            const(cmat.shape), const(lmask.shape), const(kmask.shape), const(qhot.shape), const(split.shape),
            const((1, GLA_W)), const((D_MODEL, D_MODEL)), const((1, D_MODEL)),
            smp_rows(W_MAIN), smp_rows(GLA_KW), const((SWA_HEADS, 1)),
            smp(GLA_KW, GLA_DV), smp(SWA_KVW, WINDOW), smp(SWA_KVW, WINDOW),
        ],
        out_specs=[
            pl.BlockSpec((None, tl, D_MODEL), lambda b, t: (b, t, 0)),
            pl.BlockSpec((None, GLA_KW, GLA_DV), lambda b, t: (b, 0, 0)),
            pl.BlockSpec((None, WINDOW, SWA_KVW), lambda b, t: (b, 0, 0)),
            pl.BlockSpec((None, WINDOW, SWA_KVW), lambda b, t: (b, 0, 0)),
            smp(GLA_KW, GLA_DV), smp(SWA_KVW, WINDOW), smp(SWA_KVW, WINDOW),
            smp(GLA_HEADS, GLA_DV), smp(SWA_HEADS, LANES),
        ],
        out_shape=[
            jax.ShapeDtypeStruct((bsz, seq, D_MODEL), f32),
            jax.ShapeDtypeStruct((bsz, GLA_KW, GLA_DV), f32),
            jax.ShapeDtypeStruct((bsz, WINDOW, SWA_KVW), f32),
            jax.ShapeDtypeStruct((bsz, WINDOW, SWA_KVW), f32),
            jax.ShapeDtypeStruct(smp_state.shape, f32), jax.ShapeDtypeStruct(smp_ck.shape, f32),
            jax.ShapeDtypeStruct(smp_cv.shape, f32),
            jax.ShapeDtypeStruct((n, GLA_HEADS, GLA_DV), f32), jax.ShapeDtypeStruct((n, SWA_HEADS, LANES), f32),
        ],
        scratch_shapes=[
            pltpu.VMEM((GLA_KW, GLA_DV), f32),
            pltpu.VMEM((WINDOW, SWA_KVW), f32), pltpu.VMEM((WINDOW, SWA_KVW), f32),
            pltpu.VMEM((tl, W_ALL), f32), pltpu.VMEM((tl, GLA_KW), f32),
            pltpu.VMEM((tl, GLA_W), f32), pltpu.VMEM((tl, SWA_W), f32),
        ],
        compiler_params=pltpu.CompilerParams(
            dimension_semantics=("arbitrary", "arbitrary"), vmem_limit_bytes=PROMPT_VMEM_LIMIT),
        name="prompt_layer",
    )(w["sinks"], x, w["norm_in"], w["w_all"], w["w_up"], w["b_gate"], cmat, lmask, kmask, qhot, split,
      w["gla_norm"], w["w_out"], w["norm_f"],
      smp_proj.reshape(bsz * n_t, g, W_MAIN), smp_decay.reshape(bsz * n_t, g, GLA_KW),
      w["sinks"].reshape(SWA_HEADS, 1), smp_state, smp_ck, smp_cv)


def _sample_proj_kernel(x_ref, nin_ref, w_ref, wup_ref, bg_ref, proj_ref, decay_ref):
    u = _rms(x_ref[...], nin_ref[...]).astype(bf16)
    proj_ref[...] = jnp.dot(u, w_ref[:, :W_MAIN], preferred_element_type=f32)
    glow = jnp.dot(u, w_ref[:, O_LOW:O_LOW + LANES], preferred_element_type=f32)
    decay_ref[...] = jnp.exp(_log_decay(glow, wup_ref, bg_ref))


def _sample_proj_call(xs, w):
    n = xs.shape[0]
    return pl.pallas_call(
        _sample_proj_kernel,
        out_shape=[jax.ShapeDtypeStruct((n, W_MAIN), f32), jax.ShapeDtypeStruct((n, GLA_KW), f32)],
        compiler_params=pltpu.CompilerParams(vmem_limit_bytes=VMEM_LIMIT),
        name="sample_proj",
    )(xs, w["norm_in"], w["w_all"], w["w_up"], w["b_gate"])


def _split3(x):
    as_bf16 = lambda v: v.astype(bf16).astype(f32)
    hi = as_bf16(x)
    mid = as_bf16(x - hi)
    return hi, mid, as_bf16(x - hi - mid)


def _sample_state_update(proj_ref, decay_ref, sink_ref, s_ref, ck_ref, cv_ref,
                         so_ref, cko_ref, cvo_ref, og_ref, os_ref):
    row = lax.broadcasted_iota(jnp.int32, (PACK, GLA_KW), 0)
    head_of_lane = lax.broadcasted_iota(jnp.int32, (PACK, GLA_KW), 1) // GLA_DK
    own_head = head_of_lane == row
    row_v = lax.broadcasted_iota(jnp.int32, (PACK, GLA_DV), 0)
    lane_v = lax.broadcasted_iota(jnp.int32, (PACK, GLA_DV), 1)
    piece_rows = ((row_v >= GLA_HEADS) & (row_v < GLA_HEADS + 3)).astype(f32)
    last_lane_rows = ((row_v < 3) & (lane_v == WINDOW - 1)).astype(bf16)
    newest = lax.broadcasted_iota(jnp.int32, (SWA_KVW, WINDOW), 1) == WINDOW - 1
    row_q = lax.broadcasted_iota(jnp.int32, (SWA_HEADS, LANES), 0)
    own_kv = (lax.broadcasted_iota(jnp.int32, (SWA_HEADS, LANES), 1) // SWA_HD) == (row_q // SWA_GROUP)
    sink = sink_ref[...]
    contract_rows = (((0,), (0,)), ((), ()))
    seqs = range(proj_ref.shape[0])
    lts, rts, qms, lt2s, q8s = [], [], [], [], []
    for j in seqs:
        pr = proj_ref[j:j + 1, :]
        bcast = lambda lo, width: jnp.broadcast_to(pr[:, lo:lo + width], (PACK, width))
        a_hi, a_mid, a_lo = _split3(jnp.broadcast_to(decay_ref[j:j + 1, :], (PACK, GLA_KW)))
        a_piece = jnp.where(row == GLA_HEADS, a_hi, jnp.where(row == GLA_HEADS + 1, a_mid, a_lo))
        lts.append(jnp.where(own_head, bcast(O_K, GLA_KW),
                             jnp.where((row >= GLA_HEADS) & (row < GLA_HEADS + 3), a_piece, 0.0)).astype(bf16))
        v_b = bcast(O_V, GLA_W)
        v_sel = jnp.zeros((PACK, GLA_DV), f32)
        for h in range(GLA_HEADS):
            v_sel = jnp.where(row_v == h, v_b[:, h * GLA_DV:(h + 1) * GLA_DV], v_sel)
        rts.append(jnp.concatenate([v_sel, piece_rows], axis=1).astype(bf16))
        qms.append(jnp.where(own_head, bcast(O_Q, GLA_KW), 0.0).astype(bf16))
        n_hi, n_mid, n_lo = _split3(bcast(O_SK, 2 * SWA_KVW))
        lt2s.append(jnp.where(row == 0, n_hi, jnp.where(row == 1, n_mid,
                                                        jnp.where(row == 2, n_lo, 0.0))).astype(bf16))
        sq_b = jnp.broadcast_to(pr[:, O_SQ:O_SQ + SWA_W], (SWA_HEADS, SWA_W))
        q8 = jnp.zeros((SWA_HEADS, LANES), f32)
        for gq in range(SWA_GROUP):
            q8 = jnp.where(row_q % SWA_GROUP == gq, sq_b[:, gq * LANES:(gq + 1) * LANES], q8)
        q8s.append(jnp.where(own_kv, q8, 0.0).astype(bf16))
    kv_as = [lax.dot_general(lts[j], rts[j], contract_rows, preferred_element_type=f32) for j in seqs]
    inss = [lax.dot_general(lt2s[j], last_lane_rows, contract_rows, preferred_element_type=f32) for j in seqs]
    s_news, kts, vts = [], [], []
    for j in seqs:
        s_new = kv_as[j][:, GLA_DV:] * s_ref[j] + kv_as[j][:, :GLA_DV]
        so_ref[j] = s_new
        s_news.append(s_new.astype(bf16))
        kt = jnp.where(newest, inss[j][:SWA_KVW], pltpu.roll(ck_ref[j], WINDOW - 1, axis=1))
        vt = jnp.where(newest, inss[j][SWA_KVW:], pltpu.roll(cv_ref[j], WINDOW - 1, axis=1))
        cko_ref[j] = kt
        cvo_ref[j] = vt
        kts.append(kt.astype(bf16))
        vts.append(vt.astype(bf16))
    for j in seqs:
        og_ref[j] = jnp.dot(qms[j], s_news[j], preferred_element_type=f32)[:GLA_HEADS]
    scores = [jnp.dot(q8s[j], kts[j], preferred_element_type=f32) for j in seqs]
    es, dens = [], []
    for j in seqs:
        m = jnp.maximum(jnp.max(scores[j], axis=-1, keepdims=True), sink)
        e = jnp.exp2(scores[j] - m)
        dens.append(jnp.sum(e, axis=-1, keepdims=True) + jnp.exp2(sink - m))
        es.append(e.astype(bf16))
    for j in seqs:
        o = lax.dot_general(es[j], vts[j], (((1,), (1,)), ((), ())), preferred_element_type=f32)
        os_ref[j] = o / dens[j]


def _sample_merge_kernel(x_ref, og_ref, gg_ref, os_ref, sg_ref, gn_ref, wout_ref, nf_ref, y_ref):
    y_ref[...] = _merge(x_ref[...], og_ref[...], gg_ref[...], os_ref[...], sg_ref[...],
                        gn_ref, wout_ref, nf_ref)


def _sample_merge_call(xs, og, gg, osw, sg, w):
    return pl.pallas_call(
        _sample_merge_kernel,
        out_shape=jax.ShapeDtypeStruct(xs.shape, f32),
        compiler_params=pltpu.CompilerParams(vmem_limit_bytes=VMEM_LIMIT),
        name="sample_merge",
    )(xs, og, gg, osw, sg, w["gla_norm"], w["w_out"], w["norm_f"])


def _sample_finish(xs, proj, og, os_raw, w):
    n = xs.shape[0]
    os5 = os_raw.reshape(n, SWA_KV, SWA_GROUP, SWA_KV, SWA_HD)
    os_il = jnp.stack([os5[:, kv, :, kv, :] for kv in range(SWA_KV)], axis=2).reshape(n, SWA_W)
    return _sample_merge_call(xs, og.reshape(n, GLA_W), proj[:, O_GG:O_GG + GLA_W], os_il,
                              proj[:, O_SG:O_SG + SWA_W], w)


def _cache_view(c):
    n = c.shape[1]
    return jnp.transpose(c[0], (0, 2, 3, 1)).reshape(n, SWA_KVW, WINDOW)


def _cache_unview(c):
    n = c.shape[0]
    return jnp.transpose(c.reshape(n, SWA_KV, SWA_HD, WINDOW), (0, 3, 1, 2))[None]


def kernel(x_prompt, x_sample, state_gla, cache_win_k, cache_win_v, norm_in, w_in, w_gate_up, b_gate,
           gla_norm, attn_sinks, w_out, norm_f):
    bsz = x_prompt.shape[0]
    n = x_sample.shape[0]
    w = _prep_weights(norm_in[0], w_in[0], w_gate_up[0], b_gate[0], gla_norm[0], attn_sinks[0],
                      w_out[0], norm_f)
    cmat, lmask = _chunk_tables()
    xs = x_sample.reshape(n, D_MODEL)
    proj, decay = _sample_proj_call(xs, w)
    y_p, s_p, k_p, v_p, s_s, k_s, v_s, og, os_raw = _prompt_call(
        x_prompt, w, cmat, lmask, *_swa_mask_tables(), proj, decay,
        state_gla[0].reshape(n, GLA_KW, GLA_DV), _cache_view(cache_win_k), _cache_view(cache_win_v))
    y_s = _sample_finish(xs, proj, og, os_raw, w)
    return (y_p, y_s.reshape(n, 1, D_MODEL),
            s_p.reshape(1, bsz, GLA_HEADS, GLA_DK, GLA_DV),
            _cache_unview(k_p), _cache_unview(v_p),
            s_s.reshape(1, n, GLA_HEADS, GLA_DK, GLA_DV),
            _cache_unview(k_s), _cache_unview(v_s))
```

```python
import functools

import numpy as np
import jax
import jax.numpy as jnp
from jax import lax
from jax.experimental import pallas as pl
from jax.experimental.pallas import tpu as pltpu

D_MODEL = 1024
GLA_HEADS = 4
GLA_DK = 64
GLA_DV = 128
GLA_KW = GLA_HEADS * GLA_DK
GLA_W = GLA_HEADS * GLA_DV
GLA_RANK = 16
GLA_TAU = 16.0
CHUNK = 64
SWA_HEADS = 8
SWA_HD = 64
SWA_KV = 2
SWA_GROUP = SWA_HEADS // SWA_KV
SWA_W = SWA_HEADS * SWA_HD
SWA_KVW = SWA_KV * SWA_HD
WINDOW = 128
EPS = 1e-6
NEG_INF = -1e30
LOG2E = 1.4426950408889634
LANES = 128

O_Q, O_K, O_V, O_GG = 0, 256, 512, 1024
O_SQ, O_SK, O_SV, O_SG, O_LOW = 1536, 2048, 2176, 2304, 2816
W_MAIN = 2816
W_ALL = W_MAIN + LANES

N_LEVELS = 6
TOK_BLOCK = 1024
GLA_GROUP = 4
PACK = 16
PROMPT_VMEM_LIMIT = 56 * 1024 * 1024
VMEM_LIMIT = 32 * 1024 * 1024

f32 = jnp.float32
bf16 = jnp.bfloat16


def _chunk_tables():
    c = CHUNK
    t = np.arange(c)[None, :]
    i = np.arange(c)[:, None]
    blocks = [(t <= i), (t > i)]
    masks = []
    for l in range(N_LEVELS):
        h = c >> (l + 1)
        m = (i // (2 * h)) * (2 * h) + h
        upper = i >= m
        blocks.append(np.where(upper, (t > m) & (t <= i), (t > i) & (t <= m)))
        jj = np.arange(c)[None, :]
        masks.append((i // (2 * h) == jj // (2 * h)) & (i % (2 * h) >= h) & (jj % (2 * h) < h))
    masks.append(np.eye(c, dtype=bool))
    cm = np.concatenate(blocks, axis=0).astype(np.float32)
    cm3 = np.concatenate([cm, cm, cm], axis=1)
    lm = np.stack(masks).astype(np.float32)
    lm = np.concatenate([lm, lm], axis=1)
    return jnp.asarray(cm3, dtype=bf16), jnp.asarray(lm, dtype=f32)


def _swa_mask_tables():
    key = np.arange(2 * WINDOW)[:, None]
    qi = np.arange(WINDOW)[None, :]
    prev_ok = (key < WINDOW) & (key > qi)
    cur_ok = (key >= WINDOW) & (key - WINDOW <= qi)
    full = np.where(prev_ok | cur_ok, 0.0, NEG_INF)
    first = np.where(cur_ok, 0.0, NEG_INF)
    onehot = (np.arange(2 * WINDOW)[:, None] % WINDOW == qi).astype(np.float32)
    row_q = np.arange(2 * WINDOW)[:, None] % WINDOW
    slot_is_prev = (np.arange(WINDOW)[None, :] > row_q).astype(np.float32)
    split = np.stack([slot_is_prev, 1.0 - slot_is_prev])
    return (jnp.asarray(np.stack([full, first]), dtype=bf16), jnp.asarray(onehot, dtype=bf16),
            jnp.asarray(split, dtype=bf16))


I_Q, I_K, I_V, I_GG, I_LOW, I_SQ, I_SK, I_SV, I_SG = (
    int(v) for v in np.cumsum([0, 256, 256, 512, 512, 16, 512, 128, 128])[:9])


def _interleaved_rows(base, tile):
    return [(base + kv * SWA_GROUP * SWA_HD + tile * SWA_HD, SWA_HD) for kv in range(SWA_KV)]


def _weight_layout_kernel(wt_ref, wo_ref, wall_ref, wop_ref):
    def put(col, pieces, scale=None):
        rows = [wt_ref[r:r + n, :] for r, n in pieces]
        missing = LANES - sum(n for _, n in pieces)
        if missing:
            rows.append(jnp.zeros((missing, D_MODEL), f32))
        blk = jnp.concatenate(rows, axis=0) if len(rows) > 1 else rows[0]
        if scale is not None:
            blk = blk * scale
        wall_ref[:, col:col + LANES] = blk.T.astype(bf16)

    for i in range(GLA_KW // LANES):
        put(O_Q + i * LANES, [(I_Q + i * LANES, LANES)], GLA_DK ** -0.5)
        put(O_K + i * LANES, [(I_K + i * LANES, LANES)])
    for i in range(GLA_W // LANES):
        put(O_V + i * LANES, [(I_V + i * LANES, LANES)])
        put(O_GG + i * LANES, [(I_GG + i * LANES, LANES)])
        put(O_SQ + i * LANES, _interleaved_rows(I_SQ, i), SWA_HD ** -0.5 * LOG2E)
        put(O_SG + i * LANES, _interleaved_rows(I_SG, i))
    put(O_SK, [(I_SK, SWA_KVW)])
    put(O_SV, [(I_SV, SWA_KVW)])
    put(O_LOW, [(I_LOW, GLA_RANK)])
    wop_ref[:GLA_W, :] = wo_ref[:GLA_W, :].astype(bf16)
    for i in range(SWA_GROUP):
        for kv, (r, n) in enumerate(_interleaved_rows(GLA_W, i)):
            dst = GLA_W + (i * SWA_KV + kv) * SWA_HD
            wop_ref[dst:dst + n, :] = wo_ref[r:r + n, :].astype(bf16)


def _prep_weights(norm_in, w_in, w_gate_up, b_gate, gla_norm, attn_sinks, w_out, norm_f):
    w_all, w_out_p = pl.pallas_call(
        _weight_layout_kernel,
        out_shape=[jax.ShapeDtypeStruct((D_MODEL, W_ALL), bf16), jax.ShapeDtypeStruct((D_MODEL, D_MODEL), bf16)],
        compiler_params=pltpu.CompilerParams(vmem_limit_bytes=VMEM_LIMIT),
        name="weight_layout",
    )(w_in.T, w_out)
    w_up = jnp.pad(w_gate_up, ((0, LANES - GLA_RANK), (0, 0))).astype(bf16)
    return dict(
        norm_in=norm_in.reshape(1, D_MODEL), w_all=w_all, w_up=w_up,
        b_gate=b_gate.reshape(1, GLA_KW), gla_norm=jnp.tile(gla_norm, GLA_HEADS).reshape(1, GLA_W),
        sinks=attn_sinks * LOG2E, w_out=w_out_p, norm_f=norm_f.reshape(1, D_MODEL))


def _rms(x, gain):
    return x * lax.rsqrt(jnp.mean(x * x, axis=-1, keepdims=True) + EPS) * gain


def _log_decay(glow, wup_ref, bg_ref):
    z = jnp.dot(glow.astype(bf16), wup_ref[...], preferred_element_type=f32) + bg_ref[...]
    return (jnp.minimum(z, 0.0) - jnp.log(1.0 + jnp.exp(-jnp.abs(z)))) * (1.0 / GLA_TAU)


def _silu(x):
    return x * jax.nn.sigmoid(x)


def _merge(x, o_gla, gg, o_swa, sg, gn_ref, wout_ref, nf_ref):
    parts = []
    for h in range(GLA_HEADS):
        sl = slice(h * GLA_DV, (h + 1) * GLA_DV)
        parts.append(_rms(o_gla[:, sl], gn_ref[:, sl]) * _silu(gg[:, sl]))
    parts.append(o_swa * _silu(sg))
    um = jnp.concatenate(parts, axis=1).astype(bf16)
    hres = x + jnp.dot(um, wout_ref[...], preferred_element_type=f32)
    return _rms(hres, nf_ref[...])


def _prompt_kernel(sinks_ref, x_ref, nin_ref, w_ref, wup_ref, bg_ref, cmat_ref, lmask_ref, kmask_ref, qhot_ref,
                   split_ref,
                   gn_ref, wout_ref, nf_ref,
                   smp_proj_ref, smp_decay_ref, smp_sink_ref, smp_s_ref, smp_ck_ref, smp_cv_ref,
                   y_ref, sp_ref, kn_ref, vn_ref,
                   smp_so_ref, smp_cko_ref, smp_cvo_ref, smp_og_ref, smp_os_ref,
                   s_ref, kprev_ref, vprev_ref, p_s, g_s, ogla_s, oswa_s, *, n_t):
    t = pl.program_id(1)
    tl = x_ref.shape[0]

    @pl.when(t == 0)
    def _():
        s_ref[...] = jnp.zeros_like(s_ref)
        kprev_ref[...] = jnp.zeros_like(kprev_ref)
        vprev_ref[...] = jnp.zeros_like(vprev_ref)

    def project(rg):
        u = _rms(x_ref[rg, :], nin_ref[...]).astype(bf16)
        for lo, hi in ((O_SQ, O_SG), (0, O_SQ), (O_SG, W_ALL)):
            p_s[rg, lo:hi] = jnp.dot(u, w_ref[:, lo:hi], preferred_element_type=f32)
        g_s[rg, :] = _log_decay(p_s[rg, O_LOW:O_LOW + LANES], wup_ref, bg_ref)

    lane_lo = lax.broadcasted_iota(jnp.int32, (CHUNK, LANES), 1) < GLA_DK

    n_pairs = GLA_HEADS // 2

    def group_terms(chunks):
        rows = {c: slice(c * CHUNK, (c + 1) * CHUNK) for c in chunks}
        sums = {}
        for c in chunks:
            gc = g_s[rows[c], :]
            g_hi = gc.astype(bf16)
            r1 = gc - g_hi.astype(f32)
            g_mid = r1.astype(bf16)
            g_lo = (r1 - g_mid.astype(f32)).astype(bf16)
            sums[c] = jnp.dot(cmat_ref[...], jnp.concatenate([g_hi, g_mid, g_lo], axis=0),
                              preferred_element_type=f32)
        level_ops, misc = {}, {}
        for c in chunks:
            qc, kc = p_s[rows[c], O_Q:O_Q + GLA_KW], p_s[rows[c], O_K:O_K + GLA_KW]
            e_b = jnp.exp(sums[c][0:CHUNK])
            misc[c] = (e_b, qc * e_b, kc * jnp.exp(sums[c][CHUNK:2 * CHUNK]),
                       p_s[rows[c], O_V:O_V + GLA_W].astype(bf16))
            for p in range(n_pairs):
                ln = slice(p * LANES, (p + 1) * LANES)
                qp, kp = qc[:, ln], kc[:, ln]
                for l in range(N_LEVELS + 1):
                    if l < N_LEVELS:
                        e = jnp.exp(sums[c][(2 + l) * CHUNK:(3 + l) * CHUNK, ln])
                        qe, ke = (qp * e).astype(bf16), (kp * e).astype(bf16)
                    else:
                        qe, ke = qp.astype(bf16), kp.astype(bf16)
                    lhs = jnp.concatenate([jnp.where(lane_lo, qe, 0), jnp.where(lane_lo, 0, qe)], axis=0)
                    level_ops[c, p, l] = (lhs, ke)
        scores = {key: lax.dot_general(lhs, ke, (((1,), (1,)), ((), ())), preferred_element_type=f32)
                  for key, (lhs, ke) in level_ops.items()}
        terms = {}
        for c in chunks:
            e_b, qb, k_suf, vb = misc[c]
            terms[c] = []
            for p in range(n_pairs):
                ln = slice(p * LANES, (p + 1) * LANES)
                a = scores[c, p, 0] * lmask_ref[0]
                for l in range(1, N_LEVELS + 1):
                    a = a + scores[c, p, l] * lmask_ref[l]
                ab = a.astype(bf16)
                qbp = qb[:, ln].astype(bf16)
                lhs_heads = []
                for hh in range(2):
                    qbm = jnp.where(lane_lo, qbp, 0) if hh == 0 else jnp.where(lane_lo, 0, qbp)
                    lhs_heads.append(jnp.concatenate([qbm, ab[hh * CHUNK:(hh + 1) * CHUNK, :]], axis=1))
                upd = lax.dot_general(k_suf[:, ln].astype(bf16), vb[:, p * 2 * GLA_DV:(p + 1) * 2 * GLA_DV],
                                      (((0,), (0,)), ((), ())), preferred_element_type=f32)
                upd = jnp.concatenate(
                    [upd[0:GLA_DK, 0:GLA_DV], upd[GLA_DK:2 * GLA_DK, GLA_DV:2 * GLA_DV]], axis=0)
                e_col = jnp.broadcast_to(e_b[CHUNK - 1:CHUNK, ln], (LANES, LANES)).T
                terms[c].append((lhs_heads, vb, upd, e_col))
        return terms

    def gla(chunks):
        terms = {}
        for c0 in range(chunks[0], chunks[-1] + 1, GLA_GROUP):
            terms.update(group_terms(range(c0, c0 + GLA_GROUP)))
        states = {}
        for p in range(n_pairs):
            ln = slice(p * LANES, (p + 1) * LANES)
            s_pair = s_ref[ln, :]
            for c in chunks:
                states[c, p] = s_pair.astype(bf16)
                _, _, upd, e_col = terms[c][p]
                s_pair = e_col * s_pair + upd
            s_ref[ln, :] = s_pair
        for c in chunks:
            for p in range(n_pairs):
                lhs_heads, vb, _, _ = terms[c][p]
                for hh in range(2):
                    h = 2 * p + hh
                    rhs = jnp.concatenate([states[c, p], vb[:, h * GLA_DV:(h + 1) * GLA_DV]], axis=0)
                    ogla_s[c * CHUNK:(c + 1) * CHUNK, h * GLA_DV:(h + 1) * GLA_DV] = jnp.dot(
                        lhs_heads[hh], rhs, preferred_element_type=f32)

    lane_lo_w = lax.broadcasted_iota(jnp.int32, (WINDOW, LANES), 1) < SWA_HD
    row_lo = lax.broadcasted_iota(jnp.int32, (2 * WINDOW, 1), 0) < WINDOW

    def swa(blocks):
        for blk in blocks:
            rs = slice(blk * WINDOW, (blk + 1) * WINDOW)
            sq = p_s[rs, O_SQ:O_SQ + SWA_W].astype(bf16)
            k_cur, v_cur = p_s[rs, O_SK:O_SK + SWA_KVW], p_s[rs, O_SV:O_SV + SWA_KVW]
            k2 = jnp.concatenate([kprev_ref[...], k_cur], axis=0).astype(bf16)
            v2 = jnp.concatenate([vprev_ref[...], v_cur], axis=0).astype(bf16)
            kmask = kmask_ref[jnp.where(t > 0, 0, 1)] if blk == 0 else kmask_ref[0]
            k2m = jnp.concatenate([k2, kmask], axis=1)
            for tt in range(SWA_GROUP):
                qt = sq[:, tt * LANES:(tt + 1) * LANES]
                lhs = jnp.concatenate([jnp.where(lane_lo_w, qt, 0), jnp.where(lane_lo_w, 0, qt)], axis=0)
                lhs = jnp.concatenate([lhs, qhot_ref[...]], axis=1)
                s = lax.dot_general(lhs, k2m, (((1,), (1,)), ((), ())), preferred_element_type=f32)
                s = jnp.maximum(s[:, :WINDOW], s[:, WINDOW:])
                sink = jnp.where(row_lo, sinks_ref[tt], sinks_ref[SWA_GROUP + tt])
                m = jnp.maximum(jnp.max(s, axis=-1, keepdims=True), sink)
                e = jnp.exp2(s - m)
                r = 1.0 / (jnp.sum(e, axis=-1, keepdims=True) + jnp.exp2(sink - m))
                eb = e.astype(bf16)
                e2 = jnp.concatenate([eb * split_ref[0], eb * split_ref[1]], axis=1)
                o2 = jnp.dot(e2, v2, preferred_element_type=f32)
                oswa_s[rs, tt * LANES:(tt + 1) * LANES] = jnp.where(
                    lane_lo_w, o2[:WINDOW] * r[:WINDOW], o2[WINDOW:] * r[WINDOW:])
            kprev_ref[...] = k_cur
            vprev_ref[...] = v_cur

    def merge_gla(rg):
        parts = [_rms(ogla_s[rg, h * GLA_DV:(h + 1) * GLA_DV], gn_ref[:, h * GLA_DV:(h + 1) * GLA_DV])
                 * _silu(p_s[rg, O_GG + h * GLA_DV:O_GG + (h + 1) * GLA_DV]) for h in range(GLA_HEADS)]
        um = jnp.concatenate(parts, axis=1).astype(bf16)
        y_ref[rg, :] = x_ref[rg, :] + jnp.dot(um, wout_ref[:GLA_W, :], preferred_element_type=f32)

    def merge_swa(rg):
        um = (oswa_s[rg, :] * _silu(p_s[rg, O_SG:O_SG + SWA_W])).astype(bf16)
        hres = y_ref[rg, :] + jnp.dot(um, wout_ref[GLA_W:, :], preferred_element_type=f32)
        y_ref[rg, :] = _rms(hres, nf_ref[...])

    whole = slice(0, tl)
    project(whole)
    swa(range(tl // WINDOW))
    _sample_state_update(smp_proj_ref, smp_decay_ref, smp_sink_ref, smp_s_ref, smp_ck_ref, smp_cv_ref,
                         smp_so_ref, smp_cko_ref, smp_cvo_ref, smp_og_ref, smp_os_ref)
    gla(range(tl // CHUNK))
    merge_gla(whole)
    merge_swa(whole)

    @pl.when(t == n_t - 1)
    def _():
        sp_ref[...] = s_ref[...]
        kn_ref[...] = kprev_ref[...].T
        vn_ref[...] = vprev_ref[...].T


def _prompt_call(x, w, cmat, lmask, kmask, qhot, split, smp_proj, smp_decay, smp_state, smp_ck, smp_cv):
    bsz, seq, _ = x.shape
    tl = TOK_BLOCK
    n_t = seq // tl
    n = smp_state.shape[0]
    g = n // (bsz * n_t)
    assert g * bsz * n_t == n
    step = lambda b, t: b * n_t + t
    smp = lambda *tail: pl.BlockSpec((g,) + tail, lambda b, t: (step(b, t),) + (0,) * len(tail))
    smp_rows = lambda width: pl.BlockSpec((None, g, width), lambda b, t: (step(b, t), 0, 0))
    const = lambda shape: pl.BlockSpec(shape, lambda b, t: (0,) * len(shape), pipeline_mode=pl.Buffered(1))
    return pl.pallas_call(
        functools.partial(_prompt_kernel, n_t=n_t),
        grid=(bsz, n_t),
        in_specs=[
            pl.BlockSpec(memory_space=pltpu.SMEM),
            pl.BlockSpec((None, tl, D_MODEL), lambda b, t: (b, t, 0)),
            const((1, D_MODEL)), const((D_MODEL, W_ALL)), const((LANES, GLA_KW)), const((1, GLA_KW)),
            const(cmat.shape), const(lmask.shape), const(kmask.shape), const(qhot.shape), const(split.shape),
            const((1, GLA_W)), const((D_MODEL, D_MODEL)), const((1, D_MODEL)),
            smp_rows(W_MAIN), smp_rows(GLA_KW), const((SWA_HEADS, 1)),
            smp(GLA_KW, GLA_DV), smp(SWA_KVW, WINDOW), smp(SWA_KVW, WINDOW),
        ],
        out_specs=[
            pl.BlockSpec((None, tl, D_MODEL), lambda b, t: (b, t, 0)),
            pl.BlockSpec((None, GLA_KW, GLA_DV), lambda b, t: (b, 0, 0)),
            pl.BlockSpec((None, WINDOW, SWA_KVW), lambda b, t: (b, 0, 0)),
            pl.BlockSpec((None, WINDOW, SWA_KVW), lambda b, t: (b, 0, 0)),
            smp(GLA_KW, GLA_DV), smp(SWA_KVW, WINDOW), smp(SWA_KVW, WINDOW),
            smp(GLA_HEADS, GLA_DV), smp(SWA_HEADS, LANES),
        ],
        out_shape=[
            jax.ShapeDtypeStruct((bsz, seq, D_MODEL), f32),
            jax.ShapeDtypeStruct((bsz, GLA_KW, GLA_DV), f32),
            jax.ShapeDtypeStruct((bsz, WINDOW, SWA_KVW), f32),
            jax.ShapeDtypeStruct((bsz, WINDOW, SWA_KVW), f32),
            jax.ShapeDtypeStruct(smp_state.shape, f32), jax.ShapeDtypeStruct(smp_ck.shape, f32),
            jax.ShapeDtypeStruct(smp_cv.shape, f32),
            jax.ShapeDtypeStruct((n, GLA_HEADS, GLA_DV), f32), jax.ShapeDtypeStruct((n, SWA_HEADS, LANES), f32),
        ],
        scratch_shapes=[
            pltpu.VMEM((GLA_KW, GLA_DV), f32),
            pltpu.VMEM((WINDOW, SWA_KVW), f32), pltpu.VMEM((WINDOW, SWA_KVW), f32),
            pltpu.VMEM((tl, W_ALL), f32), pltpu.VMEM((tl, GLA_KW), f32),
            pltpu.VMEM((tl, GLA_W), f32), pltpu.VMEM((tl, SWA_W), f32),
        ],
        compiler_params=pltpu.CompilerParams(
            dimension_semantics=("arbitrary", "arbitrary"), vmem_limit_bytes=PROMPT_VMEM_LIMIT),
        name="prompt_layer",
    )(w["sinks"], x, w["norm_in"], w["w_all"], w["w_up"], w["b_gate"], cmat, lmask, kmask, qhot, split,
      w["gla_norm"], w["w_out"], w["norm_f"],
      smp_proj.reshape(bsz * n_t, g, W_MAIN), smp_decay.reshape(bsz * n_t, g, GLA_KW),
      w["sinks"].reshape(SWA_HEADS, 1), smp_state, smp_ck, smp_cv)


def _sample_proj_kernel(x_ref, nin_ref, w_ref, wup_ref, bg_ref, proj_ref, decay_ref):
    u = _rms(x_ref[...], nin_ref[...]).astype(bf16)
    proj_ref[...] = jnp.dot(u, w_ref[:, :W_MAIN], preferred_element_type=f32)
    glow = jnp.dot(u, w_ref[:, O_LOW:O_LOW + LANES], preferred_element_type=f32)
    decay_ref[...] = jnp.exp(_log_decay(glow, wup_ref, bg_ref))


def _sample_proj_call(xs, w):
    n = xs.shape[0]
    return pl.pallas_call(
        _sample_proj_kernel,
        out_shape=[jax.ShapeDtypeStruct((n, W_MAIN), f32), jax.ShapeDtypeStruct((n, GLA_KW), f32)],
        compiler_params=pltpu.CompilerParams(vmem_limit_bytes=VMEM_LIMIT),
        name="sample_proj",
    )(xs, w["norm_in"], w["w_all"], w["w_up"], w["b_gate"])


def _split3(x):
    as_bf16 = lambda v: v.astype(bf16).astype(f32)
    hi = as_bf16(x)
    mid = as_bf16(x - hi)
    return hi, mid, as_bf16(x - hi - mid)


def _sample_state_update(proj_ref, decay_ref, sink_ref, s_ref, ck_ref, cv_ref,
                         so_ref, cko_ref, cvo_ref, og_ref, os_ref):
    row = lax.broadcasted_iota(jnp.int32, (PACK, GLA_KW), 0)
    head_of_lane = lax.broadcasted_iota(jnp.int32, (PACK, GLA_KW), 1) // GLA_DK
    own_head = head_of_lane == row
    row_v = lax.broadcasted_iota(jnp.int32, (PACK, GLA_DV), 0)
    lane_v = lax.broadcasted_iota(jnp.int32, (PACK, GLA_DV), 1)
    piece_rows = ((row_v >= GLA_HEADS) & (row_v < GLA_HEADS + 3)).astype(f32)
    last_lane_rows = ((row_v < 3) & (lane_v == WINDOW - 1)).astype(bf16)
    newest = lax.broadcasted_iota(jnp.int32, (SWA_KVW, WINDOW), 1) == WINDOW - 1
    row_q = lax.broadcasted_iota(jnp.int32, (SWA_HEADS, LANES), 0)
    own_kv = (lax.broadcasted_iota(jnp.int32, (SWA_HEADS, LANES), 1) // SWA_HD) == (row_q // SWA_GROUP)
    sink = sink_ref[...]
    contract_rows = (((0,), (0,)), ((), ()))
    seqs = range(proj_ref.shape[0])
    lts, rts, qms, lt2s, q8s = [], [], [], [], []
    for j in seqs:
        pr = proj_ref[j:j + 1, :]
        bcast = lambda lo, width: jnp.broadcast_to(pr[:, lo:lo + width], (PACK, width))
        a_hi, a_mid, a_lo = _split3(jnp.broadcast_to(decay_ref[j:j + 1, :], (PACK, GLA_KW)))
        a_piece = jnp.where(row == GLA_HEADS, a_hi, jnp.where(row == GLA_HEADS + 1, a_mid, a_lo))
        lts.append(jnp.where(own_head, bcast(O_K, GLA_KW),
                             jnp.where((row >= GLA_HEADS) & (row < GLA_HEADS + 3), a_piece, 0.0)).astype(bf16))
        v_b = bcast(O_V, GLA_W)
        v_sel = jnp.zeros((PACK, GLA_DV), f32)
        for h in range(GLA_HEADS):
            v_sel = jnp.where(row_v == h, v_b[:, h * GLA_DV:(h + 1) * GLA_DV], v_sel)
        rts.append(jnp.concatenate([v_sel, piece_rows], axis=1).astype(bf16))
        qms.append(jnp.where(own_head, bcast(O_Q, GLA_KW), 0.0).astype(bf16))
        n_hi, n_mid, n_lo = _split3(bcast(O_SK, 2 * SWA_KVW))
        lt2s.append(jnp.where(row == 0, n_hi, jnp.where(row == 1, n_mid,
                                                        jnp.where(row == 2, n_lo, 0.0))).astype(bf16))
        sq_b = jnp.broadcast_to(pr[:, O_SQ:O_SQ + SWA_W], (SWA_HEADS, SWA_W))
        q8 = jnp.zeros((SWA_HEADS, LANES), f32)
        for gq in range(SWA_GROUP):
            q8 = jnp.where(row_q % SWA_GROUP == gq, sq_b[:, gq * LANES:(gq + 1) * LANES], q8)
        q8s.append(jnp.where(own_kv, q8, 0.0).astype(bf16))
    kv_as = [lax.dot_general(lts[j], rts[j], contract_rows, preferred_element_type=f32) for j in seqs]
    inss = [lax.dot_general(lt2s[j], last_lane_rows, contract_rows, preferred_element_type=f32) for j in seqs]
    s_news, kts, vts = [], [], []
    for j in seqs:
        s_new = kv_as[j][:, GLA_DV:] * s_ref[j] + kv_as[j][:, :GLA_DV]
        so_ref[j] = s_new
        s_news.append(s_new.astype(bf16))
        kt = jnp.where(newest, inss[j][:SWA_KVW], pltpu.roll(ck_ref[j], WINDOW - 1, axis=1))
        vt = jnp.where(newest, inss[j][SWA_KVW:], pltpu.roll(cv_ref[j], WINDOW - 1, axis=1))
        cko_ref[j] = kt
        cvo_ref[j] = vt
        kts.append(kt.astype(bf16))
        vts.append(vt.astype(bf16))
    for j in seqs:
        og_ref[j] = jnp.dot(qms[j], s_news[j], preferred_element_type=f32)[:GLA_HEADS]
    scores = [jnp.dot(q8s[j], kts[j], preferred_element_type=f32) for j in seqs]
    es, dens = [], []
    for j in seqs:
        m = jnp.maximum(jnp.max(scores[j], axis=-1, keepdims=True), sink)
        e = jnp.exp2(scores[j] - m)
        dens.append(jnp.sum(e, axis=-1, keepdims=True) + jnp.exp2(sink - m))
        es.append(e.astype(bf16))
    for j in seqs:
        o = lax.dot_general(es[j], vts[j], (((1,), (1,)), ((), ())), preferred_element_type=f32)
        os_ref[j] = o / dens[j]


def _sample_merge_kernel(x_ref, og_ref, gg_ref, os_ref, sg_ref, gn_ref, wout_ref, nf_ref, y_ref):
    y_ref[...] = _merge(x_ref[...], og_ref[...], gg_ref[...], os_ref[...], sg_ref[...],
                        gn_ref, wout_ref, nf_ref)


def _sample_merge_call(xs, og, gg, osw, sg, w):
    return pl.pallas_call(
        _sample_merge_kernel,
        out_shape=jax.ShapeDtypeStruct(xs.shape, f32),
        compiler_params=pltpu.CompilerParams(vmem_limit_bytes=VMEM_LIMIT),
        name="sample_merge",
    )(xs, og, gg, osw, sg, w["gla_norm"], w["w_out"], w["norm_f"])


def _sample_finish(xs, proj, og, os_raw, w):
    n = xs.shape[0]
    os5 = os_raw.reshape(n, SWA_KV, SWA_GROUP, SWA_KV, SWA_HD)
    os_il = jnp.stack([os5[:, kv, :, kv, :] for kv in range(SWA_KV)], axis=2).reshape(n, SWA_W)
    return _sample_merge_call(xs, og.reshape(n, GLA_W), proj[:, O_GG:O_GG + GLA_W], os_il,
                              proj[:, O_SG:O_SG + SWA_W], w)


def _cache_view(c):
    n = c.shape[1]
    return jnp.transpose(c[0], (0, 2, 3, 1)).reshape(n, SWA_KVW, WINDOW)


def _cache_unview(c):
    n = c.shape[0]
    return jnp.transpose(c.reshape(n, SWA_KV, SWA_HD, WINDOW), (0, 3, 1, 2))[None]


def kernel(x_prompt, x_sample, state_gla, cache_win_k, cache_win_v, norm_in, w_in, w_gate_up, b_gate,
           gla_norm, attn_sinks, w_out, norm_f):
    bsz = x_prompt.shape[0]
    n = x_sample.shape[0]
    w = _prep_weights(norm_in[0], w_in[0], w_gate_up[0], b_gate[0], gla_norm[0], attn_sinks[0],
                      w_out[0], norm_f)
    cmat, lmask = _chunk_tables()
    xs = x_sample.reshape(n, D_MODEL)
    proj, decay = _sample_proj_call(xs, w)
    y_p, s_p, k_p, v_p, s_s, k_s, v_s, og, os_raw = _prompt_call(
        x_prompt, w, cmat, lmask, *_swa_mask_tables(), proj, decay,
        state_gla[0].reshape(n, GLA_KW, GLA_DV), _cache_view(cache_win_k), _cache_view(cache_win_v))
    y_s = _sample_finish(xs, proj, og, os_raw, w)
    return (y_p, y_s.reshape(n, 1, D_MODEL),
            s_p.reshape(1, bsz, GLA_HEADS, GLA_DK, GLA_DV),
            _cache_unview(k_p), _cache_unview(v_p),
            s_s.reshape(1, n, GLA_HEADS, GLA_DK, GLA_DV),
            _cache_unview(k_s), _cache_unview(v_s))
```

```python
import functools

import numpy as np
import jax
import jax.numpy as jnp
from jax import lax
from jax.experimental import pallas as pl
from jax.experimental.pallas import tpu as pltpu

D_MODEL = 1024
GLA_HEADS = 4
GLA_DK = 64
GLA_DV = 128
GLA_KW = GLA_HEADS * GLA_DK
GLA_W = GLA_HEADS * GLA_DV
GLA_RANK = 16
GLA_TAU = 16.0
CHUNK = 64
SWA_HEADS = 8
SWA_HD = 64
SWA_KV = 2
SWA_GROUP = SWA_HEADS // SWA_KV
SWA_W = SWA_HEADS * SWA_HD
SWA_KVW = SWA_KV * SWA_HD
WINDOW = 128
EPS = 1e-6
NEG_INF = -1e30
LOG2E = 1.4426950408889634
LANES = 128

O_Q, O_K, O_V, O_GG = 0, 256, 512, 1024
O_SQ, O_SK, O_SV, O_SG, O_LOW = 1536, 2048, 2176, 2304, 2816
W_MAIN = 2816
W_ALL = W_MAIN + LANES

N_LEVELS = 6
N_WIDE_LEVELS = 3
TOK_BLOCK = 1024
GLA_GROUP = 4
PACK = 16
PROMPT_VMEM_LIMIT = 56 * 1024 * 1024
VMEM_LIMIT = 32 * 1024 * 1024

f32 = jnp.float32
bf16 = jnp.bfloat16


def _chunk_tables():
    c = CHUNK
    t = np.arange(c)[None, :]
    i = np.arange(c)[:, None]
    blocks = [(t <= i)]
    masks = []
    for l in range(N_LEVELS):
        h = c >> (l + 1)
        m = (i // (2 * h)) * (2 * h) + h
        upper = i >= m
        if l >= N_WIDE_LEVELS:
            blocks.append(np.where(upper, (t > m) & (t <= i), (t > i) & (t <= m)))
        jj = np.arange(c)[None, :]
        masks.append((i // (2 * h) == jj // (2 * h)) & (i % (2 * h) >= h) & (jj % (2 * h) < h))
    masks.append(np.eye(c, dtype=bool))
    cm = np.concatenate(blocks, axis=0).astype(np.float32)
    cm3 = np.concatenate([cm, cm, cm], axis=1)
    lm = np.stack(masks).astype(np.float32)
    lm = np.concatenate([lm, lm], axis=1)
    return jnp.asarray(cm3, dtype=bf16), jnp.asarray(lm, dtype=f32)


def _swa_mask_tables():
    key = np.arange(2 * WINDOW)[:, None]
    qi = np.arange(WINDOW)[None, :]
    prev_ok = (key < WINDOW) & (key > qi)
    cur_ok = (key >= WINDOW) & (key - WINDOW <= qi)
    full = np.where(prev_ok | cur_ok, 0.0, NEG_INF)
    first = np.where(cur_ok, 0.0, NEG_INF)
    onehot = (np.arange(2 * WINDOW)[:, None] % WINDOW == qi).astype(np.float32)
    row_q = np.arange(2 * WINDOW)[:, None] % WINDOW
    slot_is_prev = (np.arange(WINDOW)[None, :] > row_q).astype(np.float32)
    split = np.stack([slot_is_prev, 1.0 - slot_is_prev])
    return (jnp.asarray(np.stack([full, first]), dtype=bf16), jnp.asarray(onehot, dtype=bf16),
            jnp.asarray(split, dtype=bf16))


I_Q, I_K, I_V, I_GG, I_LOW, I_SQ, I_SK, I_SV, I_SG = (
    int(v) for v in np.cumsum([0, 256, 256, 512, 512, 16, 512, 128, 128])[:9])


def _interleaved_rows(base, tile):
    return [(base + kv * SWA_GROUP * SWA_HD + tile * SWA_HD, SWA_HD) for kv in range(SWA_KV)]


def _weight_layout_kernel(wt_ref, wo_ref, wall_ref, wop_ref):
    def put(col, pieces, scale=None):
        rows = [wt_ref[r:r + n, :] for r, n in pieces]
        missing = LANES - sum(n for _, n in pieces)
        if missing:
            rows.append(jnp.zeros((missing, D_MODEL), f32))
        blk = jnp.concatenate(rows, axis=0) if len(rows) > 1 else rows[0]
        if scale is not None:
            blk = blk * scale
        wall_ref[:, col:col + LANES] = blk.T.astype(bf16)

    for i in range(GLA_KW // LANES):
        put(O_Q + i * LANES, [(I_Q + i * LANES, LANES)], GLA_DK ** -0.5)
        put(O_K + i * LANES, [(I_K + i * LANES, LANES)])
    for i in range(GLA_W // LANES):
        put(O_V + i * LANES, [(I_V + i * LANES, LANES)])
        put(O_GG + i * LANES, [(I_GG + i * LANES, LANES)])
        put(O_SQ + i * LANES, _interleaved_rows(I_SQ, i), SWA_HD ** -0.5 * LOG2E)
        put(O_SG + i * LANES, _interleaved_rows(I_SG, i))
    put(O_SK, [(I_SK, SWA_KVW)])
    put(O_SV, [(I_SV, SWA_KVW)])
    put(O_LOW, [(I_LOW, GLA_RANK)])
    wop_ref[:GLA_W, :] = wo_ref[:GLA_W, :].astype(bf16)
    for i in range(SWA_GROUP):
        for kv, (r, n) in enumerate(_interleaved_rows(GLA_W, i)):
            dst = GLA_W + (i * SWA_KV + kv) * SWA_HD
            wop_ref[dst:dst + n, :] = wo_ref[r:r + n, :].astype(bf16)


def _prep_weights(norm_in, w_in, w_gate_up, b_gate, gla_norm, attn_sinks, w_out, norm_f):
    w_all, w_out_p = pl.pallas_call(
        _weight_layout_kernel,
        out_shape=[jax.ShapeDtypeStruct((D_MODEL, W_ALL), bf16), jax.ShapeDtypeStruct((D_MODEL, D_MODEL), bf16)],
        compiler_params=pltpu.CompilerParams(vmem_limit_bytes=VMEM_LIMIT),
        name="weight_layout",
    )(w_in.T, w_out)
    w_up = jnp.pad(w_gate_up, ((0, LANES - GLA_RANK), (0, 0))).astype(bf16)
    return dict(
        norm_in=norm_in.reshape(1, D_MODEL), w_all=w_all, w_up=w_up,
        b_gate=b_gate.reshape(1, GLA_KW), gla_norm=jnp.tile(gla_norm, GLA_HEADS).reshape(1, GLA_W),
        sinks=attn_sinks * LOG2E, w_out=w_out_p, norm_f=norm_f.reshape(1, D_MODEL))


def _rms(x, gain):
    return x * lax.rsqrt(jnp.mean(x * x, axis=-1, keepdims=True) + EPS) * gain


def _log_decay(glow, wup_ref, bg_ref):
    z = jnp.dot(glow.astype(bf16), wup_ref[...], preferred_element_type=f32) + bg_ref[...]
    return (jnp.minimum(z, 0.0) - jnp.log(1.0 + jnp.exp(-jnp.abs(z)))) * (1.0 / GLA_TAU)


def _silu(x):
    return x * jax.nn.sigmoid(x)


def _merge(x, o_gla, gg, o_swa, sg, gn_ref, wout_ref, nf_ref):
    parts = []
    for h in range(GLA_HEADS):
        sl = slice(h * GLA_DV, (h + 1) * GLA_DV)
        parts.append(_rms(o_gla[:, sl], gn_ref[:, sl]) * _silu(gg[:, sl]))
    parts.append(o_swa * _silu(sg))
    um = jnp.concatenate(parts, axis=1).astype(bf16)
    hres = x + jnp.dot(um, wout_ref[...], preferred_element_type=f32)
    return _rms(hres, nf_ref[...])


def _prompt_kernel(sinks_ref, x_ref, nin_ref, w_ref, wup_ref, bg_ref, cmat_ref, lmask_ref, kmask_ref, qhot_ref,
                   split_ref,
                   gn_ref, wout_ref, nf_ref,
                   smp_proj_ref, smp_decay_ref, smp_sink_ref, smp_s_ref, smp_ck_ref, smp_cv_ref,
                   y_ref, sp_ref, kn_ref, vn_ref,
                   smp_so_ref, smp_cko_ref, smp_cvo_ref, smp_og_ref, smp_os_ref,
                   s_ref, kprev_ref, vprev_ref, p_s, g_s, ogla_s, oswa_s, *, n_t):
    t = pl.program_id(1)
    tl = x_ref.shape[0]

    @pl.when(t == 0)
    def _():
        s_ref[...] = jnp.zeros_like(s_ref)
        kprev_ref[...] = jnp.zeros_like(kprev_ref)
        vprev_ref[...] = jnp.zeros_like(vprev_ref)

    def project(rg):
        u = _rms(x_ref[rg, :], nin_ref[...]).astype(bf16)
        p_s[rg, :] = jnp.dot(u, w_ref[...], preferred_element_type=f32)
        g_s[rg, :] = _log_decay(p_s[rg, O_LOW:O_LOW + LANES], wup_ref, bg_ref)

    lane_lo = lax.broadcasted_iota(jnp.int32, (CHUNK, LANES), 1) < GLA_DK

    n_pairs = GLA_HEADS // 2

    def group_terms(chunks):
        rows = {c: slice(c * CHUNK, (c + 1) * CHUNK) for c in chunks}
        sums = {}
        for c in chunks:
            gc = g_s[rows[c], :]
            g_hi = gc.astype(bf16)
            r1 = gc - g_hi.astype(f32)
            g_mid = r1.astype(bf16)
            g_lo = (r1 - g_mid.astype(f32)).astype(bf16)
            sums[c] = jnp.dot(cmat_ref[...], jnp.concatenate([g_hi, g_mid, g_lo], axis=0),
                              preferred_element_type=f32)
        level_ops, misc = {}, {}
        for c in chunks:
            qc, kc = p_s[rows[c], O_Q:O_Q + GLA_KW], p_s[rows[c], O_K:O_K + GLA_KW]
            b = sums[c][0:CHUNK]
            e_b = jnp.exp(b)
            misc[c] = (e_b, qc * e_b, kc * jnp.exp(b[CHUNK - 1:CHUNK, :] - b),
                       p_s[rows[c], O_V:O_V + GLA_W].astype(bf16))
            level_sums = []
            for l in range(N_LEVELS):
                if l < N_WIDE_LEVELS:
                    h = CHUNK >> (l + 1)
                    anchor = jnp.concatenate([jnp.broadcast_to(b[m:m + 1, :], (2 * h, GLA_KW))
                                              for m in range(h, CHUNK, 2 * h)], axis=0)
                    d = b - anchor
                    level_sums.append(jnp.minimum(d, -d))
                else:
                    r0 = (1 + l - N_WIDE_LEVELS) * CHUNK
                    level_sums.append(sums[c][r0:r0 + CHUNK])
            for p in range(n_pairs):
                ln = slice(p * LANES, (p + 1) * LANES)
                qp, kp = qc[:, ln], kc[:, ln]
                for l in range(N_LEVELS + 1):
                    if l < N_LEVELS:
                        e = jnp.exp(level_sums[l][:, ln])
                        qe, ke = (qp * e).astype(bf16), (kp * e).astype(bf16)
                    else:
                        qe, ke = qp.astype(bf16), kp.astype(bf16)
                    lhs = jnp.concatenate([jnp.where(lane_lo, qe, 0), jnp.where(lane_lo, 0, qe)], axis=0)
                    level_ops[c, p, l] = (lhs, ke)
        scores = {key: lax.dot_general(lhs, ke, (((1,), (1,)), ((), ())), preferred_element_type=f32)
                  for key, (lhs, ke) in level_ops.items()}
        terms = {}
        for c in chunks:
            e_b, qb, k_suf, vb = misc[c]
            terms[c] = []
            for p in range(n_pairs):
                ln = slice(p * LANES, (p + 1) * LANES)
                a = scores[c, p, 0] * lmask_ref[0]
                for l in range(1, N_LEVELS + 1):
                    a = a + scores[c, p, l] * lmask_ref[l]
                ab = a.astype(bf16)
                qbp = qb[:, ln].astype(bf16)
                lhs_heads = []
                for hh in range(2):
                    qbm = jnp.where(lane_lo, qbp, 0) if hh == 0 else jnp.where(lane_lo, 0, qbp)
                    lhs_heads.append(jnp.concatenate([qbm, ab[hh * CHUNK:(hh + 1) * CHUNK, :]], axis=1))
                upd = lax.dot_general(k_suf[:, ln].astype(bf16), vb[:, p * 2 * GLA_DV:(p + 1) * 2 * GLA_DV],
                                      (((0,), (0,)), ((), ())), preferred_element_type=f32)
                upd = jnp.concatenate(
                    [upd[0:GLA_DK, 0:GLA_DV], upd[GLA_DK:2 * GLA_DK, GLA_DV:2 * GLA_DV]], axis=0)
                e_col = jnp.broadcast_to(e_b[CHUNK - 1:CHUNK, ln], (LANES, LANES)).T
                terms[c].append((lhs_heads, vb, upd, e_col))
        return terms

    def gla(chunks):
        terms = {}
        for c0 in range(chunks[0], chunks[-1] + 1, GLA_GROUP):
            terms.update(group_terms(range(c0, c0 + GLA_GROUP)))
        states = {}
        for p in range(n_pairs):
            ln = slice(p * LANES, (p + 1) * LANES)
            s_pair = s_ref[ln, :]
            for c in chunks:
                states[c, p] = s_pair.astype(bf16)
                _, _, upd, e_col = terms[c][p]
                s_pair = e_col * s_pair + upd
            s_ref[ln, :] = s_pair
        for c in chunks:
            for p in range(n_pairs):
                lhs_heads, vb, _, _ = terms[c][p]
                for hh in range(2):
                    h = 2 * p + hh
                    rhs = jnp.concatenate([states[c, p], vb[:, h * GLA_DV:(h + 1) * GLA_DV]], axis=0)
                    ogla_s[c * CHUNK:(c + 1) * CHUNK, h * GLA_DV:(h + 1) * GLA_DV] = jnp.dot(
                        lhs_heads[hh], rhs, preferred_element_type=f32)

    lane_lo_w = lax.broadcasted_iota(jnp.int32, (WINDOW, LANES), 1) < SWA_HD
    row_lo = lax.broadcasted_iota(jnp.int32, (2 * WINDOW, 1), 0) < WINDOW

    def swa(blocks):
        for blk in blocks:
            rs = slice(blk * WINDOW, (blk + 1) * WINDOW)
            sq = p_s[rs, O_SQ:O_SQ + SWA_W].astype(bf16)
            k_cur, v_cur = p_s[rs, O_SK:O_SK + SWA_KVW], p_s[rs, O_SV:O_SV + SWA_KVW]
            k2 = jnp.concatenate([kprev_ref[...], k_cur], axis=0).astype(bf16)
            v2 = jnp.concatenate([vprev_ref[...], v_cur], axis=0).astype(bf16)
            kmask = kmask_ref[jnp.where(t > 0, 0, 1)] if blk == 0 else kmask_ref[0]
            k2m = jnp.concatenate([k2, kmask], axis=1)
            for tt in range(SWA_GROUP):
                qt = sq[:, tt * LANES:(tt + 1) * LANES]
                lhs = jnp.concatenate([jnp.where(lane_lo_w, qt, 0), jnp.where(lane_lo_w, 0, qt)], axis=0)
                lhs = jnp.concatenate([lhs, qhot_ref[...]], axis=1)
                s = lax.dot_general(lhs, k2m, (((1,), (1,)), ((), ())), preferred_element_type=f32)
                s = jnp.maximum(s[:, :WINDOW], s[:, WINDOW:])
                sink = jnp.where(row_lo, sinks_ref[tt], sinks_ref[SWA_GROUP + tt])
                m = jnp.maximum(jnp.max(s, axis=-1, keepdims=True), sink)
                e = jnp.exp2(s - m)
                r = 1.0 / (jnp.sum(e, axis=-1, keepdims=True) + jnp.exp2(sink - m))
                eb = e.astype(bf16)
                e2 = jnp.concatenate([eb * split_ref[0], eb * split_ref[1]], axis=1)
                o2 = jnp.dot(e2, v2, preferred_element_type=f32)
                oswa_s[rs, tt * LANES:(tt + 1) * LANES] = jnp.where(
                    lane_lo_w, o2[:WINDOW] * r[:WINDOW], o2[WINDOW:] * r[WINDOW:])
            kprev_ref[...] = k_cur
            vprev_ref[...] = v_cur

    def merge_gla(rg):
        parts = [_rms(ogla_s[rg, h * GLA_DV:(h + 1) * GLA_DV], gn_ref[:, h * GLA_DV:(h + 1) * GLA_DV])
                 * _silu(p_s[rg, O_GG + h * GLA_DV:O_GG + (h + 1) * GLA_DV]) for h in range(GLA_HEADS)]
        um = jnp.concatenate(parts, axis=1).astype(bf16)
        y_ref[rg, :] = x_ref[rg, :] + jnp.dot(um, wout_ref[:GLA_W, :], preferred_element_type=f32)

    def merge_swa(rg):
        um = (oswa_s[rg, :] * _silu(p_s[rg, O_SG:O_SG + SWA_W])).astype(bf16)
        hres = y_ref[rg, :] + jnp.dot(um, wout_ref[GLA_W:, :], preferred_element_type=f32)
        y_ref[rg, :] = _rms(hres, nf_ref[...])

    whole = slice(0, tl)
    project(whole)
    _sample_state_update(smp_proj_ref, smp_decay_ref, smp_sink_ref, smp_s_ref, smp_ck_ref, smp_cv_ref,
                         smp_so_ref, smp_cko_ref, smp_cvo_ref, smp_og_ref, smp_os_ref)
    gla(range(tl // CHUNK))
    merge_gla(whole)
    swa(range(tl // WINDOW))
    merge_swa(whole)

    @pl.when(t == n_t - 1)
    def _():
        sp_ref[...] = s_ref[...]
        kn_ref[...] = kprev_ref[...].T
        vn_ref[...] = vprev_ref[...].T


def _prompt_call(x, w, cmat, lmask, kmask, qhot, split, smp_proj, smp_decay, smp_state, smp_ck, smp_cv):
    bsz, seq, _ = x.shape
    tl = TOK_BLOCK
    n_t = seq // tl
    n = smp_state.shape[0]
    g = n // (bsz * n_t)
    assert g * bsz * n_t == n
    step = lambda b, t: b * n_t + t
    smp = lambda *tail: pl.BlockSpec((g,) + tail, lambda b, t: (step(b, t),) + (0,) * len(tail))
    smp_rows = lambda width: pl.BlockSpec((None, g, width), lambda b, t: (step(b, t), 0, 0))
    const = lambda shape: pl.BlockSpec(shape, lambda b, t: (0,) * len(shape), pipeline_mode=pl.Buffered(1))
    return pl.pallas_call(
        functools.partial(_prompt_kernel, n_t=n_t),
        grid=(bsz, n_t),
        in_specs=[
            pl.BlockSpec(memory_space=pltpu.SMEM),
            pl.BlockSpec((None, tl, D_MODEL), lambda b, t: (b, t, 0)),
            const((1, D_MODEL)), const((D_MODEL, W_ALL)), const((LANES, GLA_KW)), const((1, GLA_KW)),
            const(cmat.shape), const(lmask.shape), const(kmask.shape), const(qhot.shape), const(split.shape),
            const((1, GLA_W)), const((D_MODEL, D_MODEL)), const((1, D_MODEL)),
            smp_rows(W_MAIN), smp_rows(GLA_KW), const((SWA_HEADS, 1)),
            smp(GLA_KW, GLA_DV), smp(SWA_KVW, WINDOW), smp(SWA_KVW, WINDOW),
        ],
        out_specs=[
            pl.BlockSpec((None, tl, D_MODEL), lambda b, t: (b, t, 0)),
            pl.BlockSpec((None, GLA_KW, GLA_DV), lambda b, t: (b, 0, 0)),
            pl.BlockSpec((None, WINDOW, SWA_KVW), lambda b, t: (b, 0, 0)),
            pl.BlockSpec((None, WINDOW, SWA_KVW), lambda b, t: (b, 0, 0)),
            smp(GLA_KW, GLA_DV), smp(SWA_KVW, WINDOW), smp(SWA_KVW, WINDOW),
            smp(GLA_HEADS, GLA_DV), smp(SWA_HEADS, LANES),
        ],
        out_shape=[
            jax.ShapeDtypeStruct((bsz, seq, D_MODEL), f32),
            jax.ShapeDtypeStruct((bsz, GLA_KW, GLA_DV), f32),
            jax.ShapeDtypeStruct((bsz, WINDOW, SWA_KVW), f32),
            jax.ShapeDtypeStruct((bsz, WINDOW, SWA_KVW), f32),
            jax.ShapeDtypeStruct(smp_state.shape, f32), jax.ShapeDtypeStruct(smp_ck.shape, f32),
            jax.ShapeDtypeStruct(smp_cv.shape, f32),
            jax.ShapeDtypeStruct((n, GLA_HEADS, GLA_DV), f32), jax.ShapeDtypeStruct((n, SWA_HEADS, LANES), f32),
        ],
        scratch_shapes=[
            pltpu.VMEM((GLA_KW, GLA_DV), f32),
            pltpu.VMEM((WINDOW, SWA_KVW), f32), pltpu.VMEM((WINDOW, SWA_KVW), f32),
            pltpu.VMEM((tl, W_ALL), f32), pltpu.VMEM((tl, GLA_KW), f32),
            pltpu.VMEM((tl, GLA_W), f32), pltpu.VMEM((tl, SWA_W), f32),
        ],
        compiler_params=pltpu.CompilerParams(
            dimension_semantics=("arbitrary", "arbitrary"), vmem_limit_bytes=PROMPT_VMEM_LIMIT),
        name="prompt_layer",
    )(w["sinks"], x, w["norm_in"], w["w_all"], w["w_up"], w["b_gate"], cmat, lmask, kmask, qhot, split,
      w["gla_norm"], w["w_out"], w["norm_f"],
      smp_proj.reshape(bsz * n_t, g, W_MAIN), smp_decay.reshape(bsz * n_t, g, GLA_KW),
      w["sinks"].reshape(SWA_HEADS, 1), smp_state, smp_ck, smp_cv)


def _sample_proj_kernel(x_ref, nin_ref, w_ref, wup_ref, bg_ref, proj_ref, decay_ref):
    u = _rms(x_ref[...], nin_ref[...]).astype(bf16)
    proj_ref[...] = jnp.dot(u, w_ref[:, :W_MAIN], preferred_element_type=f32)
    glow = jnp.dot(u, w_ref[:, O_LOW:O_LOW + LANES], preferred_element_type=f32)
    decay_ref[...] = jnp.exp(_log_decay(glow, wup_ref, bg_ref))


def _sample_proj_call(xs, w):
    n = xs.shape[0]
    return pl.pallas_call(
        _sample_proj_kernel,
        out_shape=[jax.ShapeDtypeStruct((n, W_MAIN), f32), jax.ShapeDtypeStruct((n, GLA_KW), f32)],
        compiler_params=pltpu.CompilerParams(vmem_limit_bytes=VMEM_LIMIT),
        name="sample_proj",
    )(xs, w["norm_in"], w["w_all"], w["w_up"], w["b_gate"])


def _split3(x):
    as_bf16 = lambda v: v.astype(bf16).astype(f32)
    hi = as_bf16(x)
    mid = as_bf16(x - hi)
    return hi, mid, as_bf16(x - hi - mid)


def _sample_state_update(proj_ref, decay_ref, sink_ref, s_ref, ck_ref, cv_ref,
                         so_ref, cko_ref, cvo_ref, og_ref, os_ref):
    row = lax.broadcasted_iota(jnp.int32, (PACK, GLA_KW), 0)
    head_of_lane = lax.broadcasted_iota(jnp.int32, (PACK, GLA_KW), 1) // GLA_DK
    own_head = head_of_lane == row
    row_v = lax.broadcasted_iota(jnp.int32, (PACK, GLA_DV), 0)
    lane_v = lax.broadcasted_iota(jnp.int32, (PACK, GLA_DV), 1)
    piece_rows = ((row_v >= GLA_HEADS) & (row_v < GLA_HEADS + 3)).astype(f32)
    last_lane_rows = ((row_v < 3) & (lane_v == WINDOW - 1)).astype(bf16)
    newest = lax.broadcasted_iota(jnp.int32, (SWA_KVW, WINDOW), 1) == WINDOW - 1
    row_q = lax.broadcasted_iota(jnp.int32, (SWA_HEADS, LANES), 0)
    own_kv = (lax.broadcasted_iota(jnp.int32, (SWA_HEADS, LANES), 1) // SWA_HD) == (row_q // SWA_GROUP)
    sink = sink_ref[...]
    contract_rows = (((0,), (0,)), ((), ()))
    seqs = range(proj_ref.shape[0])
    lts, rts, qms, lt2s, q8s = [], [], [], [], []
    for j in seqs:
        pr = proj_ref[j:j + 1, :]
        bcast = lambda lo, width: jnp.broadcast_to(pr[:, lo:lo + width], (PACK, width))
        a_hi, a_mid, a_lo = _split3(jnp.broadcast_to(decay_ref[j:j + 1, :], (PACK, GLA_KW)))
        a_piece = jnp.where(row == GLA_HEADS, a_hi, jnp.where(row == GLA_HEADS + 1, a_mid, a_lo))
        lts.append(jnp.where(own_head, bcast(O_K, GLA_KW),
                             jnp.where((row >= GLA_HEADS) & (row < GLA_HEADS + 3), a_piece, 0.0)).astype(bf16))
        v_b = bcast(O_V, GLA_W)
        v_sel = jnp.zeros((PACK, GLA_DV), f32)
        for h in range(GLA_HEADS):
            v_sel = jnp.where(row_v == h, v_b[:, h * GLA_DV:(h + 1) * GLA_DV], v_sel)
        rts.append(jnp.concatenate([v_sel, piece_rows], axis=1).astype(bf16))
        qms.append(jnp.where(own_head, bcast(O_Q, GLA_KW), 0.0).astype(bf16))
        n_hi, n_mid, n_lo = _split3(bcast(O_SK, 2 * SWA_KVW))
        lt2s.append(jnp.where(row == 0, n_hi, jnp.where(row == 1, n_mid,
                                                        jnp.where(row == 2, n_lo, 0.0))).astype(bf16))
        sq_b = jnp.broadcast_to(pr[:, O_SQ:O_SQ + SWA_W], (SWA_HEADS, SWA_W))
        q8 = jnp.zeros((SWA_HEADS, LANES), f32)
        for gq in range(SWA_GROUP):
            q8 = jnp.where(row_q % SWA_GROUP == gq, sq_b[:, gq * LANES:(gq + 1) * LANES], q8)
        q8s.append(jnp.where(own_kv, q8, 0.0).astype(bf16))
    kv_as = [lax.dot_general(lts[j], rts[j], contract_rows, preferred_element_type=f32) for j in seqs]
    inss = [lax.dot_general(lt2s[j], last_lane_rows, contract_rows, preferred_element_type=f32) for j in seqs]
    s_news, kts, vts = [], [], []
    for j in seqs:
        s_new = kv_as[j][:, GLA_DV:] * s_ref[j] + kv_as[j][:, :GLA_DV]
        so_ref[j] = s_new
        s_news.append(s_new.astype(bf16))
        kt = jnp.where(newest, inss[j][:SWA_KVW], pltpu.roll(ck_ref[j], WINDOW - 1, axis=1))
        vt = jnp.where(newest, inss[j][SWA_KVW:], pltpu.roll(cv_ref[j], WINDOW - 1, axis=1))
        cko_ref[j] = kt
        cvo_ref[j] = vt
        kts.append(kt.astype(bf16))
        vts.append(vt.astype(bf16))
    for j in seqs:
        og_ref[j] = jnp.dot(qms[j], s_news[j], preferred_element_type=f32)[:GLA_HEADS]
    scores = [jnp.dot(q8s[j], kts[j], preferred_element_type=f32) for j in seqs]
    es, dens = [], []
    for j in seqs:
        m = jnp.maximum(jnp.max(scores[j], axis=-1, keepdims=True), sink)
        e = jnp.exp2(scores[j] - m)
        dens.append(jnp.sum(e, axis=-1, keepdims=True) + jnp.exp2(sink - m))
        es.append(e.astype(bf16))
    for j in seqs:
        o = lax.dot_general(es[j], vts[j], (((1,), (1,)), ((), ())), preferred_element_type=f32)
        os_ref[j] = o / dens[j]


def _sample_merge_kernel(x_ref, og_ref, gg_ref, os_ref, sg_ref, gn_ref, wout_ref, nf_ref, y_ref):
    y_ref[...] = _merge(x_ref[...], og_ref[...], gg_ref[...], os_ref[...], sg_ref[...],
                        gn_ref, wout_ref, nf_ref)


def _sample_merge_call(xs, og, gg, osw, sg, w):
    return pl.pallas_call(
        _sample_merge_kernel,
        out_shape=jax.ShapeDtypeStruct(xs.shape, f32),
        compiler_params=pltpu.CompilerParams(vmem_limit_bytes=VMEM_LIMIT),
        name="sample_merge",
    )(xs, og, gg, osw, sg, w["gla_norm"], w["w_out"], w["norm_f"])


def _sample_finish(xs, proj, og, os_raw, w):
    n = xs.shape[0]
    os5 = os_raw.reshape(n, SWA_KV, SWA_GROUP, SWA_KV, SWA_HD)
    os_il = jnp.stack([os5[:, kv, :, kv, :] for kv in range(SWA_KV)], axis=2).reshape(n, SWA_W)
    return _sample_merge_call(xs, og.reshape(n, GLA_W), proj[:, O_GG:O_GG + GLA_W], os_il,
                              proj[:, O_SG:O_SG + SWA_W], w)


def _cache_view(c):
    n = c.shape[1]
    return jnp.transpose(c[0], (0, 2, 3, 1)).reshape(n, SWA_KVW, WINDOW)


def _cache_unview(c):
    n = c.shape[0]
    return jnp.transpose(c.reshape(n, SWA_KV, SWA_HD, WINDOW), (0, 3, 1, 2))[None]


def kernel(x_prompt, x_sample, state_gla, cache_win_k, cache_win_v, norm_in, w_in, w_gate_up, b_gate,
           gla_norm, attn_sinks, w_out, norm_f):
    bsz = x_prompt.shape[0]
    n = x_sample.shape[0]
    w = _prep_weights(norm_in[0], w_in[0], w_gate_up[0], b_gate[0], gla_norm[0], attn_sinks[0],
                      w_out[0], norm_f)
    cmat, lmask = _chunk_tables()
    xs = x_sample.reshape(n, D_MODEL)
    proj, decay = _sample_proj_call(xs, w)
    y_p, s_p, k_p, v_p, s_s, k_s, v_s, og, os_raw = _prompt_call(
        x_prompt, w, cmat, lmask, *_swa_mask_tables(), proj, decay,
        state_gla[0].reshape(n, GLA_KW, GLA_DV), _cache_view(cache_win_k), _cache_view(cache_win_v))
    y_s = _sample_finish(xs, proj, og, os_raw, w)
    return (y_p, y_s.reshape(n, 1, D_MODEL),
            s_p.reshape(1, bsz, GLA_HEADS, GLA_DK, GLA_DV),
            _cache_unview(k_p), _cache_unview(v_p),
            s_s.reshape(1, n, GLA_HEADS, GLA_DK, GLA_DV),
            _cache_unview(k_s), _cache_unview(v_s))
```

```python
import functools

import numpy as np
import jax
import jax.numpy as jnp
from jax import lax
from jax.experimental import pallas as pl
from jax.experimental.pallas import tpu as pltpu

D_MODEL = 1024
GLA_HEADS = 4
GLA_DK = 64
GLA_DV = 128
GLA_KW = GLA_HEADS * GLA_DK
GLA_W = GLA_HEADS * GLA_DV
GLA_RANK = 16
GLA_TAU = 16.0
CHUNK = 64
SWA_HEADS = 8
SWA_HD = 64
SWA_KV = 2
SWA_GROUP = SWA_HEADS // SWA_KV
SWA_W = SWA_HEADS * SWA_HD
SWA_KVW = SWA_KV * SWA_HD
WINDOW = 128
EPS = 1e-6
NEG_INF = -1e30
LOG2E = 1.4426950408889634
LANES = 128

O_Q, O_K, O_V, O_GG = 0, 256, 512, 1024
O_SQ, O_SK, O_SV, O_SG, O_LOW = 1536, 2048, 2176, 2304, 2816
W_MAIN = 2816
W_ALL = W_MAIN + LANES

N_LEVELS = 6
TOK_BLOCK = 1024
GLA_GROUP = 4
PACK = 16
PROMPT_VMEM_LIMIT = 56 * 1024 * 1024
VMEM_LIMIT = 32 * 1024 * 1024

f32 = jnp.float32
bf16 = jnp.bfloat16


def _chunk_tables():
    c = CHUNK
    t = np.arange(c)[None, :]
    i = np.arange(c)[:, None]
    blocks = [(t <= i), (t > i)]
    masks = []
    for l in range(N_LEVELS):
        h = c >> (l + 1)
        m = (i // (2 * h)) * (2 * h) + h
        upper = i >= m
        blocks.append(np.where(upper, (t > m) & (t <= i), (t > i) & (t <= m)))
        jj = np.arange(c)[None, :]
        masks.append((i // (2 * h) == jj // (2 * h)) & (i % (2 * h) >= h) & (jj % (2 * h) < h))
    masks.append(np.eye(c, dtype=bool))
    cm = np.concatenate(blocks, axis=0).astype(np.float32)
    cm3 = np.concatenate([cm, cm, cm], axis=1)
    lm = np.stack(masks).astype(np.float32)
    lm = np.concatenate([lm, lm], axis=1)
    return jnp.asarray(cm3, dtype=bf16), jnp.asarray(lm, dtype=f32)


def _swa_mask_tables():
    key = np.arange(2 * WINDOW)[:, None]
    qi = np.arange(WINDOW)[None, :]
    prev_ok = (key < WINDOW) & (key > qi)
    cur_ok = (key >= WINDOW) & (key - WINDOW <= qi)
    full = np.where(prev_ok | cur_ok, 0.0, NEG_INF)
    first = np.where(cur_ok, 0.0, NEG_INF)
    onehot = (np.arange(2 * WINDOW)[:, None] % WINDOW == qi).astype(np.float32)
    row_q = np.arange(2 * WINDOW)[:, None] % WINDOW
    slot_is_prev = (np.arange(WINDOW)[None, :] > row_q).astype(np.float32)
    split = np.stack([slot_is_prev, 1.0 - slot_is_prev])
    return (jnp.asarray(np.stack([full, first]), dtype=bf16), jnp.asarray(onehot, dtype=bf16),
            jnp.asarray(split, dtype=bf16))


I_Q, I_K, I_V, I_GG, I_LOW, I_SQ, I_SK, I_SV, I_SG = (
    int(v) for v in np.cumsum([0, 256, 256, 512, 512, 16, 512, 128, 128])[:9])


def _interleaved_rows(base, tile):
    return [(base + kv * SWA_GROUP * SWA_HD + tile * SWA_HD, SWA_HD) for kv in range(SWA_KV)]


def _weight_layout_kernel(wt_ref, wo_ref, wall_ref, wop_ref):
    def put(col, pieces, scale=None):
        rows = [wt_ref[r:r + n, :] for r, n in pieces]
        missing = LANES - sum(n for _, n in pieces)
        if missing:
            rows.append(jnp.zeros((missing, D_MODEL), f32))
        blk = jnp.concatenate(rows, axis=0) if len(rows) > 1 else rows[0]
        if scale is not None:
            blk = blk * scale
        wall_ref[:, col:col + LANES] = blk.T.astype(bf16)

    for i in range(GLA_KW // LANES):
        put(O_Q + i * LANES, [(I_Q + i * LANES, LANES)], GLA_DK ** -0.5)
        put(O_K + i * LANES, [(I_K + i * LANES, LANES)])
    for i in range(GLA_W // LANES):
        put(O_V + i * LANES, [(I_V + i * LANES, LANES)])
        put(O_GG + i * LANES, [(I_GG + i * LANES, LANES)])
        put(O_SQ + i * LANES, _interleaved_rows(I_SQ, i), SWA_HD ** -0.5 * LOG2E)
        put(O_SG + i * LANES, _interleaved_rows(I_SG, i))
    put(O_SK, [(I_SK, SWA_KVW)])
    put(O_SV, [(I_SV, SWA_KVW)])
    put(O_LOW, [(I_LOW, GLA_RANK)])
    wop_ref[:GLA_W, :] = wo_ref[:GLA_W, :].astype(bf16)
    for i in range(SWA_GROUP):
        for kv, (r, n) in enumerate(_interleaved_rows(GLA_W, i)):
            dst = GLA_W + (i * SWA_KV + kv) * SWA_HD
            wop_ref[dst:dst + n, :] = wo_ref[r:r + n, :].astype(bf16)


def _prep_weights(norm_in, w_in, w_gate_up, b_gate, gla_norm, attn_sinks, w_out, norm_f):
    w_all, w_out_p = pl.pallas_call(
        _weight_layout_kernel,
        out_shape=[jax.ShapeDtypeStruct((D_MODEL, W_ALL), bf16), jax.ShapeDtypeStruct((D_MODEL, D_MODEL), bf16)],
        compiler_params=pltpu.CompilerParams(vmem_limit_bytes=VMEM_LIMIT),
        name="weight_layout",
    )(w_in.T, w_out)
    w_up = jnp.pad(w_gate_up, ((0, LANES - GLA_RANK), (0, 0))).astype(bf16)
    return dict(
        norm_in=norm_in.reshape(1, D_MODEL), w_all=w_all, w_up=w_up,
        b_gate=b_gate.reshape(1, GLA_KW), gla_norm=jnp.tile(gla_norm, GLA_HEADS).reshape(1, GLA_W),
        sinks=attn_sinks * LOG2E, w_out=w_out_p, norm_f=norm_f.reshape(1, D_MODEL))


def _rms(x, gain):
    return x * lax.rsqrt(jnp.mean(x * x, axis=-1, keepdims=True) + EPS) * gain


def _log_decay(glow, wup_ref, bg_ref):
    z = jnp.dot(glow.astype(bf16), wup_ref[...], preferred_element_type=f32) + bg_ref[...]
    return (jnp.minimum(z, 0.0) - jnp.log(1.0 + jnp.exp(-jnp.abs(z)))) * (1.0 / GLA_TAU)


def _silu(x):
    return x * jax.nn.sigmoid(x)


def _merge(x, o_gla, gg, o_swa, sg, gn_ref, wout_ref, nf_ref):
    parts = []
    for h in range(GLA_HEADS):
        sl = slice(h * GLA_DV, (h + 1) * GLA_DV)
        parts.append(_rms(o_gla[:, sl], gn_ref[:, sl]) * _silu(gg[:, sl]))
    parts.append(o_swa * _silu(sg))
    um = jnp.concatenate(parts, axis=1).astype(bf16)
    hres = x + jnp.dot(um, wout_ref[...], preferred_element_type=f32)
    return _rms(hres, nf_ref[...])


def _prompt_kernel(sinks_ref, x_ref, nin_ref, w_ref, wup_ref, bg_ref, cmat_ref, lmask_ref, kmask_ref, qhot_ref,
                   split_ref,
                   gn_ref, wout_ref, nf_ref,
                   smp_proj_ref, smp_decay_ref, smp_sink_ref, smp_s_ref, smp_ck_ref, smp_cv_ref,
                   y_ref, sp_ref, kn_ref, vn_ref,
                   smp_so_ref, smp_cko_ref, smp_cvo_ref, smp_og_ref, smp_os_ref,
                   s_ref, kprev_ref, vprev_ref, p_s, g_s, ogla_s, oswa_s, u_s, *, n_t):
    t = pl.program_id(1)
    tl = x_ref.shape[0]

    @pl.when(t == 0)
    def _():
        s_ref[...] = jnp.zeros_like(s_ref)
        kprev_ref[...] = jnp.zeros_like(kprev_ref)
        vprev_ref[...] = jnp.zeros_like(vprev_ref)

    def project(rg):
        u = _rms(x_ref[rg, :], nin_ref[...]).astype(bf16)
        u_s[rg, :] = u
        p_s[rg, :O_SG] = jnp.dot(u, w_ref[:, :O_SG], preferred_element_type=f32)
        p_s[rg, O_LOW:] = jnp.dot(u, w_ref[:, O_LOW:], preferred_element_type=f32)
        g_s[rg, :] = _log_decay(p_s[rg, O_LOW:O_LOW + LANES], wup_ref, bg_ref)

    def project_late(rg):
        p_s[rg, O_SG:O_LOW] = jnp.dot(u_s[rg, :], w_ref[:, O_SG:O_LOW], preferred_element_type=f32)

    lane_lo = lax.broadcasted_iota(jnp.int32, (CHUNK, LANES), 1) < GLA_DK

    n_pairs = GLA_HEADS // 2

    def group_terms(chunks):
        rows = {c: slice(c * CHUNK, (c + 1) * CHUNK) for c in chunks}
        sums = {}
        for c in chunks:
            gc = g_s[rows[c], :]
            g_hi = gc.astype(bf16)
            r1 = gc - g_hi.astype(f32)
            g_mid = r1.astype(bf16)
            g_lo = (r1 - g_mid.astype(f32)).astype(bf16)
            sums[c] = jnp.dot(cmat_ref[...], jnp.concatenate([g_hi, g_mid, g_lo], axis=0),
                              preferred_element_type=f32)
        level_ops, misc = {}, {}
        for c in chunks:
            qc, kc = p_s[rows[c], O_Q:O_Q + GLA_KW], p_s[rows[c], O_K:O_K + GLA_KW]
            e_b = jnp.exp(sums[c][0:CHUNK])
            misc[c] = (e_b, qc * e_b, kc * jnp.exp(sums[c][CHUNK:2 * CHUNK]),
                       p_s[rows[c], O_V:O_V + GLA_W].astype(bf16))
            for p in range(n_pairs):
                ln = slice(p * LANES, (p + 1) * LANES)
                qp, kp = qc[:, ln], kc[:, ln]
                for l in range(N_LEVELS + 1):
                    if l < N_LEVELS:
                        e = jnp.exp(sums[c][(2 + l) * CHUNK:(3 + l) * CHUNK, ln])
                        qe, ke = (qp * e).astype(bf16), (kp * e).astype(bf16)
                    else:
                        qe, ke = qp.astype(bf16), kp.astype(bf16)
                    lhs = jnp.concatenate([jnp.where(lane_lo, qe, 0), jnp.where(lane_lo, 0, qe)], axis=0)
                    level_ops[c, p, l] = (lhs, ke)
        scores = {key: lax.dot_general(lhs, ke, (((1,), (1,)), ((), ())), preferred_element_type=f32)
                  for key, (lhs, ke) in level_ops.items()}
        terms = {}
        for c in chunks:
            e_b, qb, k_suf, vb = misc[c]
            terms[c] = []
            for p in range(n_pairs):
                ln = slice(p * LANES, (p + 1) * LANES)
                a = scores[c, p, 0] * lmask_ref[0]
                for l in range(1, N_LEVELS + 1):
                    a = a + scores[c, p, l] * lmask_ref[l]
                ab = a.astype(bf16)
                qbp = qb[:, ln].astype(bf16)
                lhs_heads = []
                for hh in range(2):
                    qbm = jnp.where(lane_lo, qbp, 0) if hh == 0 else jnp.where(lane_lo, 0, qbp)
                    lhs_heads.append(jnp.concatenate([qbm, ab[hh * CHUNK:(hh + 1) * CHUNK, :]], axis=1))
                upd = lax.dot_general(k_suf[:, ln].astype(bf16), vb[:, p * 2 * GLA_DV:(p + 1) * 2 * GLA_DV],
                                      (((0,), (0,)), ((), ())), preferred_element_type=f32)
                upd = jnp.concatenate(
                    [upd[0:GLA_DK, 0:GLA_DV], upd[GLA_DK:2 * GLA_DK, GLA_DV:2 * GLA_DV]], axis=0)
                e_col = jnp.broadcast_to(e_b[CHUNK - 1:CHUNK, ln], (LANES, LANES)).T
                terms[c].append((lhs_heads, vb, upd, e_col))
        return terms

    def gla(chunks):
        terms = {}
        for c0 in range(chunks[0], chunks[-1] + 1, GLA_GROUP):
            terms.update(group_terms(range(c0, c0 + GLA_GROUP)))
        states = {}
        for p in range(n_pairs):
            ln = slice(p * LANES, (p + 1) * LANES)
            s_pair = s_ref[ln, :]
            for c in chunks:
                states[c, p] = s_pair.astype(bf16)
                _, _, upd, e_col = terms[c][p]
                s_pair = e_col * s_pair + upd
            s_ref[ln, :] = s_pair
        for c in chunks:
            for p in range(n_pairs):
                lhs_heads, vb, _, _ = terms[c][p]
                for hh in range(2):
                    h = 2 * p + hh
                    rhs = jnp.concatenate([states[c, p], vb[:, h * GLA_DV:(h + 1) * GLA_DV]], axis=0)
                    ogla_s[c * CHUNK:(c + 1) * CHUNK, h * GLA_DV:(h + 1) * GLA_DV] = jnp.dot(
                        lhs_heads[hh], rhs, preferred_element_type=f32)

    lane_lo_w = lax.broadcasted_iota(jnp.int32, (WINDOW, LANES), 1) < SWA_HD
    row_lo = lax.broadcasted_iota(jnp.int32, (2 * WINDOW, 1), 0) < WINDOW

    def swa(blocks):
        for blk in blocks:
            rs = slice(blk * WINDOW, (blk + 1) * WINDOW)
            sq = p_s[rs, O_SQ:O_SQ + SWA_W].astype(bf16)
            k_cur, v_cur = p_s[rs, O_SK:O_SK + SWA_KVW], p_s[rs, O_SV:O_SV + SWA_KVW]
            k2 = jnp.concatenate([kprev_ref[...], k_cur], axis=0).astype(bf16)
            v2 = jnp.concatenate([vprev_ref[...], v_cur], axis=0).astype(bf16)
            kmask = kmask_ref[jnp.where(t > 0, 0, 1)] if blk == 0 else kmask_ref[0]
            k2m = jnp.concatenate([k2, kmask], axis=1)
            for tt in range(SWA_GROUP):
                qt = sq[:, tt * LANES:(tt + 1) * LANES]
                lhs = jnp.concatenate([jnp.where(lane_lo_w, qt, 0), jnp.where(lane_lo_w, 0, qt)], axis=0)
                lhs = jnp.concatenate([lhs, qhot_ref[...]], axis=1)
                s = lax.dot_general(lhs, k2m, (((1,), (1,)), ((), ())), preferred_element_type=f32)
                s = jnp.maximum(s[:, :WINDOW], s[:, WINDOW:])
                sink = jnp.where(row_lo, sinks_ref[tt], sinks_ref[SWA_GROUP + tt])
                m = jnp.maximum(jnp.max(s, axis=-1, keepdims=True), sink)
                e = jnp.exp2(s - m)
                r = 1.0 / (jnp.sum(e, axis=-1, keepdims=True) + jnp.exp2(sink - m))
                eb = e.astype(bf16)
                e2 = jnp.concatenate([eb * split_ref[0], eb * split_ref[1]], axis=1)
                o2 = jnp.dot(e2, v2, preferred_element_type=f32)
                oswa_s[rs, tt * LANES:(tt + 1) * LANES] = jnp.where(
                    lane_lo_w, o2[:WINDOW] * r[:WINDOW], o2[WINDOW:] * r[WINDOW:])
            kprev_ref[...] = k_cur
            vprev_ref[...] = v_cur

    def merge_gla(rg):
        parts = [_rms(ogla_s[rg, h * GLA_DV:(h + 1) * GLA_DV], gn_ref[:, h * GLA_DV:(h + 1) * GLA_DV])
                 * _silu(p_s[rg, O_GG + h * GLA_DV:O_GG + (h + 1) * GLA_DV]) for h in range(GLA_HEADS)]
        um = jnp.concatenate(parts, axis=1).astype(bf16)
        y_ref[rg, :] = x_ref[rg, :] + jnp.dot(um, wout_ref[:GLA_W, :], preferred_element_type=f32)

    def merge_swa(rg):
        um = (oswa_s[rg, :] * _silu(p_s[rg, O_SG:O_SG + SWA_W])).astype(bf16)
        hres = y_ref[rg, :] + jnp.dot(um, wout_ref[GLA_W:, :], preferred_element_type=f32)
        y_ref[rg, :] = _rms(hres, nf_ref[...])

    whole = slice(0, tl)
    project(whole)
    _sample_state_update(smp_proj_ref, smp_decay_ref, smp_sink_ref, smp_s_ref, smp_ck_ref, smp_cv_ref,
                         smp_so_ref, smp_cko_ref, smp_cvo_ref, smp_og_ref, smp_os_ref)
    gla(range(tl // CHUNK))
    merge_gla(whole)
    project_late(whole)
    swa(range(tl // WINDOW))
    merge_swa(whole)

    @pl.when(t == n_t - 1)
    def _():
        sp_ref[...] = s_ref[...]
        kn_ref[...] = kprev_ref[...].T
        vn_ref[...] = vprev_ref[...].T


def _prompt_call(x, w, cmat, lmask, kmask, qhot, split, smp_proj, smp_decay, smp_state, smp_ck, smp_cv):
    bsz, seq, _ = x.shape
    tl = TOK_BLOCK
    n_t = seq // tl
    n = smp_state.shape[0]
    g = n // (bsz * n_t)
    assert g * bsz * n_t == n
    step = lambda b, t: b * n_t + t
    smp = lambda *tail: pl.BlockSpec((g,) + tail, lambda b, t: (step(b, t),) + (0,) * len(tail))
    smp_rows = lambda width: pl.BlockSpec((None, g, width), lambda b, t: (step(b, t), 0, 0))
    const = lambda shape: pl.BlockSpec(shape, lambda b, t: (0,) * len(shape), pipeline_mode=pl.Buffered(1))
    return pl.pallas_call(
        functools.partial(_prompt_kernel, n_t=n_t),
        grid=(bsz, n_t),
        in_specs=[
            pl.BlockSpec(memory_space=pltpu.SMEM),
            pl.BlockSpec((None, tl, D_MODEL), lambda b, t: (b, t, 0)),
            const((1, D_MODEL)), const((D_MODEL, W_ALL)), const((LANES, GLA_KW)), const((1, GLA_KW)),
            const(cmat.shape), const(lmask.shape), const(kmask.shape), const(qhot.shape), const(split.shape),
            const((1, GLA_W)), const((D_MODEL, D_MODEL)), const((1, D_MODEL)),
            smp_rows(W_MAIN), smp_rows(GLA_KW), const((SWA_HEADS, 1)),
            smp(GLA_KW, GLA_DV), smp(SWA_KVW, WINDOW), smp(SWA_KVW, WINDOW),
        ],
        out_specs=[
            pl.BlockSpec((None, tl, D_MODEL), lambda b, t: (b, t, 0)),
            pl.BlockSpec((None, GLA_KW, GLA_DV), lambda b, t: (b, 0, 0)),
            pl.BlockSpec((None, WINDOW, SWA_KVW), lambda b, t: (b, 0, 0)),
            pl.BlockSpec((None, WINDOW, SWA_KVW), lambda b, t: (b, 0, 0)),
            smp(GLA_KW, GLA_DV), smp(SWA_KVW, WINDOW), smp(SWA_KVW, WINDOW),
            smp(GLA_HEADS, GLA_DV), smp(SWA_HEADS, LANES),
        ],
        out_shape=[
            jax.ShapeDtypeStruct((bsz, seq, D_MODEL), f32),
            jax.ShapeDtypeStruct((bsz, GLA_KW, GLA_DV), f32),
            jax.ShapeDtypeStruct((bsz, WINDOW, SWA_KVW), f32),
            jax.ShapeDtypeStruct((bsz, WINDOW, SWA_KVW), f32),
            jax.ShapeDtypeStruct(smp_state.shape, f32), jax.ShapeDtypeStruct(smp_ck.shape, f32),
            jax.ShapeDtypeStruct(smp_cv.shape, f32),
            jax.ShapeDtypeStruct((n, GLA_HEADS, GLA_DV), f32), jax.ShapeDtypeStruct((n, SWA_HEADS, LANES), f32),
        ],
        scratch_shapes=[
            pltpu.VMEM((GLA_KW, GLA_DV), f32),
            pltpu.VMEM((WINDOW, SWA_KVW), f32), pltpu.VMEM((WINDOW, SWA_KVW), f32),
            pltpu.VMEM((tl, W_ALL), f32), pltpu.VMEM((tl, GLA_KW), f32),
            pltpu.VMEM((tl, GLA_W), f32), pltpu.VMEM((tl, SWA_W), f32),
            pltpu.VMEM((tl, D_MODEL), bf16),
        ],
        compiler_params=pltpu.CompilerParams(
            dimension_semantics=("arbitrary", "arbitrary"), vmem_limit_bytes=PROMPT_VMEM_LIMIT),
        name="prompt_layer",
    )(w["sinks"], x, w["norm_in"], w["w_all"], w["w_up"], w["b_gate"], cmat, lmask, kmask, qhot, split,
      w["gla_norm"], w["w_out"], w["norm_f"],
      smp_proj.reshape(bsz * n_t, g, W_MAIN), smp_decay.reshape(bsz * n_t, g, GLA_KW),
      w["sinks"].reshape(SWA_HEADS, 1), smp_state, smp_ck, smp_cv)


def _sample_proj_kernel(x_ref, nin_ref, w_ref, wup_ref, bg_ref, proj_ref, decay_ref):
    u = _rms(x_ref[...], nin_ref[...]).astype(bf16)
    proj_ref[...] = jnp.dot(u, w_ref[:, :W_MAIN], preferred_element_type=f32)
    glow = jnp.dot(u, w_ref[:, O_LOW:O_LOW + LANES], preferred_element_type=f32)
    decay_ref[...] = jnp.exp(_log_decay(glow, wup_ref, bg_ref))


def _sample_proj_call(xs, w):
    n = xs.shape[0]
    return pl.pallas_call(
        _sample_proj_kernel,
        out_shape=[jax.ShapeDtypeStruct((n, W_MAIN), f32), jax.ShapeDtypeStruct((n, GLA_KW), f32)],
        compiler_params=pltpu.CompilerParams(vmem_limit_bytes=VMEM_LIMIT),
        name="sample_proj",
    )(xs, w["norm_in"], w["w_all"], w["w_up"], w["b_gate"])


def _split3(x):
    as_bf16 = lambda v: v.astype(bf16).astype(f32)
    hi = as_bf16(x)
    mid = as_bf16(x - hi)
    return hi, mid, as_bf16(x - hi - mid)


def _sample_state_update(proj_ref, decay_ref, sink_ref, s_ref, ck_ref, cv_ref,
                         so_ref, cko_ref, cvo_ref, og_ref, os_ref):
    row = lax.broadcasted_iota(jnp.int32, (PACK, GLA_KW), 0)
    head_of_lane = lax.broadcasted_iota(jnp.int32, (PACK, GLA_KW), 1) // GLA_DK
    own_head = head_of_lane == row
    row_v = lax.broadcasted_iota(jnp.int32, (PACK, GLA_DV), 0)
    lane_v = lax.broadcasted_iota(jnp.int32, (PACK, GLA_DV), 1)
    piece_rows = ((row_v >= GLA_HEADS) & (row_v < GLA_HEADS + 3)).astype(f32)
    last_lane_rows = ((row_v < 3) & (lane_v == WINDOW - 1)).astype(bf16)
    newest = lax.broadcasted_iota(jnp.int32, (SWA_KVW, WINDOW), 1) == WINDOW - 1
    row_q = lax.broadcasted_iota(jnp.int32, (SWA_HEADS, LANES), 0)
    own_kv = (lax.broadcasted_iota(jnp.int32, (SWA_HEADS, LANES), 1) // SWA_HD) == (row_q // SWA_GROUP)
    sink = sink_ref[...]
    contract_rows = (((0,), (0,)), ((), ()))
    seqs = range(proj_ref.shape[0])
    lts, rts, qms, lt2s, q8s = [], [], [], [], []
    for j in seqs:
        pr = proj_ref[j:j + 1, :]
        bcast = lambda lo, width: jnp.broadcast_to(pr[:, lo:lo + width], (PACK, width))
        a_hi, a_mid, a_lo = _split3(jnp.broadcast_to(decay_ref[j:j + 1, :], (PACK, GLA_KW)))
        a_piece = jnp.where(row == GLA_HEADS, a_hi, jnp.where(row == GLA_HEADS + 1, a_mid, a_lo))
        lts.append(jnp.where(own_head, bcast(O_K, GLA_KW),
                             jnp.where((row >= GLA_HEADS) & (row < GLA_HEADS + 3), a_piece, 0.0)).astype(bf16))
        v_b = bcast(O_V, GLA_W)
        v_sel = jnp.zeros((PACK, GLA_DV), f32)
        for h in range(GLA_HEADS):
            v_sel = jnp.where(row_v == h, v_b[:, h * GLA_DV:(h + 1) * GLA_DV], v_sel)
        rts.append(jnp.concatenate([v_sel, piece_rows], axis=1).astype(bf16))
        qms.append(jnp.where(own_head, bcast(O_Q, GLA_KW), 0.0).astype(bf16))
        n_hi, n_mid, n_lo = _split3(bcast(O_SK, 2 * SWA_KVW))
        lt2s.append(jnp.where(row == 0, n_hi, jnp.where(row == 1, n_mid,
                                                        jnp.where(row == 2, n_lo, 0.0))).astype(bf16))
        sq_b = jnp.broadcast_to(pr[:, O_SQ:O_SQ + SWA_W], (SWA_HEADS, SWA_W))
        q8 = jnp.zeros((SWA_HEADS, LANES), f32)
        for gq in range(SWA_GROUP):
            q8 = jnp.where(row_q % SWA_GROUP == gq, sq_b[:, gq * LANES:(gq + 1) * LANES], q8)
        q8s.append(jnp.where(own_kv, q8, 0.0).astype(bf16))
    kv_as = [lax.dot_general(lts[j], rts[j], contract_rows, preferred_element_type=f32) for j in seqs]
    inss = [lax.dot_general(lt2s[j], last_lane_rows, contract_rows, preferred_element_type=f32) for j in seqs]
    s_news, kts, vts = [], [], []
    for j in seqs:
        s_new = kv_as[j][:, GLA_DV:] * s_ref[j] + kv_as[j][:, :GLA_DV]
        so_ref[j] = s_new
        s_news.append(s_new.astype(bf16))
        kt = jnp.where(newest, inss[j][:SWA_KVW], pltpu.roll(ck_ref[j], WINDOW - 1, axis=1))
        vt = jnp.where(newest, inss[j][SWA_KVW:], pltpu.roll(cv_ref[j], WINDOW - 1, axis=1))
        cko_ref[j] = kt
        cvo_ref[j] = vt
        kts.append(kt.astype(bf16))
        vts.append(vt.astype(bf16))
    for j in seqs:
        og_ref[j] = jnp.dot(qms[j], s_news[j], preferred_element_type=f32)[:GLA_HEADS]
    scores = [jnp.dot(q8s[j], kts[j], preferred_element_type=f32) for j in seqs]
    es, dens = [], []
    for j in seqs:
        m = jnp.maximum(jnp.max(scores[j], axis=-1, keepdims=True), sink)
        e = jnp.exp2(scores[j] - m)
        dens.append(jnp.sum(e, axis=-1, keepdims=True) + jnp.exp2(sink - m))
        es.append(e.astype(bf16))
    for j in seqs:
        o = lax.dot_general(es[j], vts[j], (((1,), (1,)), ((), ())), preferred_element_type=f32)
        os_ref[j] = o / dens[j]


def _sample_merge_kernel(x_ref, og_ref, gg_ref, os_ref, sg_ref, gn_ref, wout_ref, nf_ref, y_ref):
    y_ref[...] = _merge(x_ref[...], og_ref[...], gg_ref[...], os_ref[...], sg_ref[...],
                        gn_ref, wout_ref, nf_ref)


def _sample_merge_call(xs, og, gg, osw, sg, w):
    return pl.pallas_call(
        _sample_merge_kernel,
        out_shape=jax.ShapeDtypeStruct(xs.shape, f32),
        compiler_params=pltpu.CompilerParams(vmem_limit_bytes=VMEM_LIMIT),
        name="sample_merge",
    )(xs, og, gg, osw, sg, w["gla_norm"], w["w_out"], w["norm_f"])


def _sample_finish(xs, proj, og, os_raw, w):
    n = xs.shape[0]
    os5 = os_raw.reshape(n, SWA_KV, SWA_GROUP, SWA_KV, SWA_HD)
    os_il = jnp.stack([os5[:, kv, :, kv, :] for kv in range(SWA_KV)], axis=2).reshape(n, SWA_W)
    return _sample_merge_call(xs, og.reshape(n, GLA_W), proj[:, O_GG:O_GG + GLA_W], os_il,
                              proj[:, O_SG:O_SG + SWA_W], w)


def _cache_view(c):
    n = c.shape[1]
    return jnp.transpose(c[0], (0, 2, 3, 1)).reshape(n, SWA_KVW, WINDOW)


def _cache_unview(c):
    n = c.shape[0]
    return jnp.transpose(c.reshape(n, SWA_KV, SWA_HD, WINDOW), (0, 3, 1, 2))[None]


def kernel(x_prompt, x_sample, state_gla, cache_win_k, cache_win_v, norm_in, w_in, w_gate_up, b_gate,
           gla_norm, attn_sinks, w_out, norm_f):
    bsz = x_prompt.shape[0]
    n = x_sample.shape[0]
    w = _prep_weights(norm_in[0], w_in[0], w_gate_up[0], b_gate[0], gla_norm[0], attn_sinks[0],
                      w_out[0], norm_f)
    cmat, lmask = _chunk_tables()
    xs = x_sample.reshape(n, D_MODEL)
    proj, decay = _sample_proj_call(xs, w)
    y_p, s_p, k_p, v_p, s_s, k_s, v_s, og, os_raw = _prompt_call(
        x_prompt, w, cmat, lmask, *_swa_mask_tables(), proj, decay,
        state_gla[0].reshape(n, GLA_KW, GLA_DV), _cache_view(cache_win_k), _cache_view(cache_win_v))
    y_s = _sample_finish(xs, proj, og, os_raw, w)
    return (y_p, y_s.reshape(n, 1, D_MODEL),
            s_p.reshape(1, bsz, GLA_HEADS, GLA_DK, GLA_DV),
            _cache_unview(k_p), _cache_unview(v_p),
            s_s.reshape(1, n, GLA_HEADS, GLA_DK, GLA_DV),
            _cache_unview(k_s), _cache_unview(v_s))
```

```python
import functools

import numpy as np
import jax
import jax.numpy as jnp
from jax import lax
from jax.experimental import pallas as pl
from jax.experimental.pallas import tpu as pltpu

D_MODEL = 1024
GLA_HEADS = 4
GLA_DK = 64
GLA_DV = 128
GLA_KW = GLA_HEADS * GLA_DK
GLA_W = GLA_HEADS * GLA_DV
GLA_RANK = 16
GLA_TAU = 16.0
CHUNK = 64
SWA_HEADS = 8
SWA_HD = 64
SWA_KV = 2
SWA_GROUP = SWA_HEADS // SWA_KV
SWA_W = SWA_HEADS * SWA_HD
SWA_KVW = SWA_KV * SWA_HD
WINDOW = 128
EPS = 1e-6
NEG_INF = -1e30
LOG2E = 1.4426950408889634
LANES = 128

O_Q, O_K, O_V, O_GG = 0, 256, 512, 1024
O_SQ, O_SK, O_SV, O_SG, O_LOW = 1536, 2048, 2176, 2304, 2816
W_MAIN = 2816
W_ALL = W_MAIN + LANES

N_LEVELS = 6
TOK_BLOCK = 1024
GLA_GROUP = 2
PACK = 16
PROMPT_VMEM_LIMIT = 56 * 1024 * 1024
VMEM_LIMIT = 32 * 1024 * 1024

f32 = jnp.float32
bf16 = jnp.bfloat16


def _chunk_tables():
    c = CHUNK
    t = np.arange(c)[None, :]
    i = np.arange(c)[:, None]
    blocks = [(t <= i), (t > i)]
    masks = []
    for l in range(N_LEVELS):
        h = c >> (l + 1)
        m = (i // (2 * h)) * (2 * h) + h
        upper = i >= m
        blocks.append(np.where(upper, (t > m) & (t <= i), (t > i) & (t <= m)))
        jj = np.arange(c)[None, :]
        masks.append((i // (2 * h) == jj // (2 * h)) & (i % (2 * h) >= h) & (jj % (2 * h) < h))
    masks.append(np.eye(c, dtype=bool))
    cm = np.concatenate(blocks, axis=0).astype(np.float32)
    cm3 = np.concatenate([cm, cm, cm], axis=1)
    lm = np.stack(masks).astype(np.float32)
    lm = np.concatenate([lm, lm], axis=1)
    return jnp.asarray(cm3, dtype=bf16), jnp.asarray(lm, dtype=f32)


def _swa_mask_tables():
    key = np.arange(2 * WINDOW)[:, None]
    qi = np.arange(WINDOW)[None, :]
    prev_ok = (key < WINDOW) & (key > qi)
    cur_ok = (key >= WINDOW) & (key - WINDOW <= qi)
    full = np.where(prev_ok | cur_ok, 0.0, NEG_INF)
    first = np.where(cur_ok, 0.0, NEG_INF)
    onehot = (np.arange(2 * WINDOW)[:, None] % WINDOW == qi).astype(np.float32)
    row_q = np.arange(2 * WINDOW)[:, None] % WINDOW
    slot_is_prev = (np.arange(WINDOW)[None, :] > row_q).astype(np.float32)
    split = np.stack([slot_is_prev, 1.0 - slot_is_prev])
    return (jnp.asarray(np.stack([full, first]), dtype=bf16), jnp.asarray(onehot, dtype=bf16),
            jnp.asarray(split, dtype=bf16))


I_Q, I_K, I_V, I_GG, I_LOW, I_SQ, I_SK, I_SV, I_SG = (
    int(v) for v in np.cumsum([0, 256, 256, 512, 512, 16, 512, 128, 128])[:9])


def _interleaved_rows(base, tile):
    return [(base + kv * SWA_GROUP * SWA_HD + tile * SWA_HD, SWA_HD) for kv in range(SWA_KV)]


def _weight_layout_kernel(wt_ref, wo_ref, wall_ref, wop_ref):
    def put(col, pieces, scale=None):
        rows = [wt_ref[r:r + n, :] for r, n in pieces]
        missing = LANES - sum(n for _, n in pieces)
        if missing:
            rows.append(jnp.zeros((missing, D_MODEL), f32))
        blk = jnp.concatenate(rows, axis=0) if len(rows) > 1 else rows[0]
        if scale is not None:
            blk = blk * scale
        wall_ref[:, col:col + LANES] = blk.T.astype(bf16)

    for i in range(GLA_KW // LANES):
        put(O_Q + i * LANES, [(I_Q + i * LANES, LANES)], GLA_DK ** -0.5)
        put(O_K + i * LANES, [(I_K + i * LANES, LANES)])
    for i in range(GLA_W // LANES):
        put(O_V + i * LANES, [(I_V + i * LANES, LANES)])
        put(O_GG + i * LANES, [(I_GG + i * LANES, LANES)])
        put(O_SQ + i * LANES, _interleaved_rows(I_SQ, i), SWA_HD ** -0.5 * LOG2E)
        put(O_SG + i * LANES, _interleaved_rows(I_SG, i))
    put(O_SK, [(I_SK, SWA_KVW)])
    put(O_SV, [(I_SV, SWA_KVW)])
    put(O_LOW, [(I_LOW, GLA_RANK)])
    wop_ref[:GLA_W, :] = wo_ref[:GLA_W, :].astype(bf16)
    for i in range(SWA_GROUP):
        for kv, (r, n) in enumerate(_interleaved_rows(GLA_W, i)):
            dst = GLA_W + (i * SWA_KV + kv) * SWA_HD
            wop_ref[dst:dst + n, :] = wo_ref[r:r + n, :].astype(bf16)


def _prep_weights(norm_in, w_in, w_gate_up, b_gate, gla_norm, attn_sinks, w_out, norm_f):
    w_all, w_out_p = pl.pallas_call(
        _weight_layout_kernel,
        out_shape=[jax.ShapeDtypeStruct((D_MODEL, W_ALL), bf16), jax.ShapeDtypeStruct((D_MODEL, D_MODEL), bf16)],
        compiler_params=pltpu.CompilerParams(vmem_limit_bytes=VMEM_LIMIT),
        name="weight_layout",
    )(w_in.T, w_out)
    w_up = jnp.pad(w_gate_up, ((0, LANES - GLA_RANK), (0, 0))).astype(bf16)
    return dict(
        norm_in=norm_in.reshape(1, D_MODEL), w_all=w_all, w_up=w_up,
        b_gate=b_gate.reshape(1, GLA_KW), gla_norm=jnp.tile(gla_norm, GLA_HEADS).reshape(1, GLA_W),
        sinks=attn_sinks * LOG2E, w_out=w_out_p, norm_f=norm_f.reshape(1, D_MODEL))


def _rms(x, gain):
    return x * lax.rsqrt(jnp.mean(x * x, axis=-1, keepdims=True) + EPS) * gain


def _log_decay(glow, wup_ref, bg_ref):
    z = jnp.dot(glow.astype(bf16), wup_ref[...], preferred_element_type=f32) + bg_ref[...]
    return (jnp.minimum(z, 0.0) - jnp.log(1.0 + jnp.exp(-jnp.abs(z)))) * (1.0 / GLA_TAU)


def _silu(x):
    return x * jax.nn.sigmoid(x)


def _merge(x, o_gla, gg, o_swa, sg, gn_ref, wout_ref, nf_ref):
    parts = []
    for h in range(GLA_HEADS):
        sl = slice(h * GLA_DV, (h + 1) * GLA_DV)
        parts.append(_rms(o_gla[:, sl], gn_ref[:, sl]) * _silu(gg[:, sl]))
    parts.append(o_swa * _silu(sg))
    um = jnp.concatenate(parts, axis=1).astype(bf16)
    hres = x + jnp.dot(um, wout_ref[...], preferred_element_type=f32)
    return _rms(hres, nf_ref[...])


def _prompt_kernel(sinks_ref, x_ref, nin_ref, w_ref, wup_ref, bg_ref, cmat_ref, lmask_ref, kmask_ref, qhot_ref,
                   split_ref,
                   gn_ref, wout_ref, nf_ref,
                   smp_proj_ref, smp_decay_ref, smp_sink_ref, smp_s_ref, smp_ck_ref, smp_cv_ref,
                   y_ref, sp_ref, kn_ref, vn_ref,
                   smp_so_ref, smp_cko_ref, smp_cvo_ref, smp_og_ref, smp_os_ref,
                   s_ref, kprev_ref, vprev_ref, p_s, g_s, ogla_s, oswa_s, *, n_t):
    t = pl.program_id(1)
    tl = x_ref.shape[0]

    @pl.when(t == 0)
    def _():
        s_ref[...] = jnp.zeros_like(s_ref)
        kprev_ref[...] = jnp.zeros_like(kprev_ref)
        vprev_ref[...] = jnp.zeros_like(vprev_ref)

    def project(rg):
        u = _rms(x_ref[rg, :], nin_ref[...]).astype(bf16)
        p_s[rg, :] = jnp.dot(u, w_ref[...], preferred_element_type=f32)
        g_s[rg, :] = _log_decay(p_s[rg, O_LOW:O_LOW + LANES], wup_ref, bg_ref)

    lane_lo = lax.broadcasted_iota(jnp.int32, (CHUNK, LANES), 1) < GLA_DK

    n_pairs = GLA_HEADS // 2

    def group_terms(chunks):
        rows = {c: slice(c * CHUNK, (c + 1) * CHUNK) for c in chunks}
        sums = {}
        for c in chunks:
            gc = g_s[rows[c], :]
            g_hi = gc.astype(bf16)
            r1 = gc - g_hi.astype(f32)
            g_mid = r1.astype(bf16)
            g_lo = (r1 - g_mid.astype(f32)).astype(bf16)
            sums[c] = jnp.dot(cmat_ref[...], jnp.concatenate([g_hi, g_mid, g_lo], axis=0),
                              preferred_element_type=f32)
        level_ops, misc = {}, {}
        for c in chunks:
            qc, kc = p_s[rows[c], O_Q:O_Q + GLA_KW], p_s[rows[c], O_K:O_K + GLA_KW]
            e_b = jnp.exp(sums[c][0:CHUNK])
            misc[c] = (e_b, qc * e_b, kc * jnp.exp(sums[c][CHUNK:2 * CHUNK]),
                       p_s[rows[c], O_V:O_V + GLA_W].astype(bf16))
            for p in range(n_pairs):
                ln = slice(p * LANES, (p + 1) * LANES)
                qp, kp = qc[:, ln], kc[:, ln]
                for l in range(N_LEVELS + 1):
                    if l < N_LEVELS:
                        e = jnp.exp(sums[c][(2 + l) * CHUNK:(3 + l) * CHUNK, ln])
                        qe, ke = (qp * e).astype(bf16), (kp * e).astype(bf16)
                    else:
                        qe, ke = qp.astype(bf16), kp.astype(bf16)
                    lhs = jnp.concatenate([jnp.where(lane_lo, qe, 0), jnp.where(lane_lo, 0, qe)], axis=0)
                    level_ops[c, p, l] = (lhs, ke)
        scores = {key: lax.dot_general(lhs, ke, (((1,), (1,)), ((), ())), preferred_element_type=f32)
                  for key, (lhs, ke) in level_ops.items()}
        terms = {}
        for c in chunks:
            e_b, qb, k_suf, vb = misc[c]
            terms[c] = []
            for p in range(n_pairs):
                ln = slice(p * LANES, (p + 1) * LANES)
                a = scores[c, p, 0] * lmask_ref[0]
                for l in range(1, N_LEVELS + 1):
                    a = a + scores[c, p, l] * lmask_ref[l]
                ab = a.astype(bf16)
                qbp = qb[:, ln].astype(bf16)
                lhs_heads = []
                for hh in range(2):
                    qbm = jnp.where(lane_lo, qbp, 0) if hh == 0 else jnp.where(lane_lo, 0, qbp)
                    lhs_heads.append(jnp.concatenate([qbm, ab[hh * CHUNK:(hh + 1) * CHUNK, :]], axis=1))
                upd = lax.dot_general(k_suf[:, ln].astype(bf16), vb[:, p * 2 * GLA_DV:(p + 1) * 2 * GLA_DV],
                                      (((0,), (0,)), ((), ())), preferred_element_type=f32)
                upd = jnp.concatenate(
                    [upd[0:GLA_DK, 0:GLA_DV], upd[GLA_DK:2 * GLA_DK, GLA_DV:2 * GLA_DV]], axis=0)
                e_col = jnp.broadcast_to(e_b[CHUNK - 1:CHUNK, ln], (LANES, LANES)).T
                terms[c].append((lhs_heads, vb, upd, e_col))
        return terms

    def gla(chunks):
        terms = {}
        for c0 in range(chunks[0], chunks[-1] + 1, GLA_GROUP):
            terms.update(group_terms(range(c0, c0 + GLA_GROUP)))
        states = {}
        for p in range(n_pairs):
            ln = slice(p * LANES, (p + 1) * LANES)
            s_pair = s_ref[ln, :]
            for c in chunks:
                states[c, p] = s_pair.astype(bf16)
                _, _, upd, e_col = terms[c][p]
                s_pair = e_col * s_pair + upd
            s_ref[ln, :] = s_pair
        for c in chunks:
            for p in range(n_pairs):
                lhs_heads, vb, _, _ = terms[c][p]
                for hh in range(2):
                    h = 2 * p + hh
                    rhs = jnp.concatenate([states[c, p], vb[:, h * GLA_DV:(h + 1) * GLA_DV]], axis=0)
                    ogla_s[c * CHUNK:(c + 1) * CHUNK, h * GLA_DV:(h + 1) * GLA_DV] = jnp.dot(
                        lhs_heads[hh], rhs, preferred_element_type=f32)

    lane_lo_w = lax.broadcasted_iota(jnp.int32, (WINDOW, LANES), 1) < SWA_HD
    row_lo = lax.broadcasted_iota(jnp.int32, (2 * WINDOW, 1), 0) < WINDOW

    def swa(blocks):
        for blk in blocks:
            rs = slice(blk * WINDOW, (blk + 1) * WINDOW)
            sq = p_s[rs, O_SQ:O_SQ + SWA_W].astype(bf16)
            k_cur, v_cur = p_s[rs, O_SK:O_SK + SWA_KVW], p_s[rs, O_SV:O_SV + SWA_KVW]
            k2 = jnp.concatenate([kprev_ref[...], k_cur], axis=0).astype(bf16)
            v2 = jnp.concatenate([vprev_ref[...], v_cur], axis=0).astype(bf16)
            kmask = kmask_ref[jnp.where(t > 0, 0, 1)] if blk == 0 else kmask_ref[0]
            k2m = jnp.concatenate([k2, kmask], axis=1)
            for tt in range(SWA_GROUP):
                qt = sq[:, tt * LANES:(tt + 1) * LANES]
                lhs = jnp.concatenate([jnp.where(lane_lo_w, qt, 0), jnp.where(lane_lo_w, 0, qt)], axis=0)
                lhs = jnp.concatenate([lhs, qhot_ref[...]], axis=1)
                s = lax.dot_general(lhs, k2m, (((1,), (1,)), ((), ())), preferred_element_type=f32)
                s = jnp.maximum(s[:, :WINDOW], s[:, WINDOW:])
                sink = jnp.where(row_lo, sinks_ref[tt], sinks_ref[SWA_GROUP + tt])
                m = jnp.maximum(jnp.max(s, axis=-1, keepdims=True), sink)
                e = jnp.exp2(s - m)
                r = 1.0 / (jnp.sum(e, axis=-1, keepdims=True) + jnp.exp2(sink - m))
                eb = e.astype(bf16)
                e2 = jnp.concatenate([eb * split_ref[0], eb * split_ref[1]], axis=1)
                o2 = jnp.dot(e2, v2, preferred_element_type=f32)
                oswa_s[rs, tt * LANES:(tt + 1) * LANES] = jnp.where(
                    lane_lo_w, o2[:WINDOW] * r[:WINDOW], o2[WINDOW:] * r[WINDOW:])
            kprev_ref[...] = k_cur
            vprev_ref[...] = v_cur

    def merge_gla(rg):
        parts = [_rms(ogla_s[rg, h * GLA_DV:(h + 1) * GLA_DV], gn_ref[:, h * GLA_DV:(h + 1) * GLA_DV])
                 * _silu(p_s[rg, O_GG + h * GLA_DV:O_GG + (h + 1) * GLA_DV]) for h in range(GLA_HEADS)]
        um = jnp.concatenate(parts, axis=1).astype(bf16)
        y_ref[rg, :] = x_ref[rg, :] + jnp.dot(um, wout_ref[:GLA_W, :], preferred_element_type=f32)

    def merge_swa(rg):
        um = (oswa_s[rg, :] * _silu(p_s[rg, O_SG:O_SG + SWA_W])).astype(bf16)
        hres = y_ref[rg, :] + jnp.dot(um, wout_ref[GLA_W:, :], preferred_element_type=f32)
        y_ref[rg, :] = _rms(hres, nf_ref[...])

    whole = slice(0, tl)
    project(whole)
    _sample_state_update(smp_proj_ref, smp_decay_ref, smp_sink_ref, smp_s_ref, smp_ck_ref, smp_cv_ref,
                         smp_so_ref, smp_cko_ref, smp_cvo_ref, smp_og_ref, smp_os_ref)
    gla(range(tl // CHUNK))
    merge_gla(whole)
    swa(range(tl // WINDOW))
    merge_swa(whole)

    @pl.when(t == n_t - 1)
    def _():
        sp_ref[...] = s_ref[...]
        kn_ref[...] = kprev_ref[...].T
        vn_ref[...] = vprev_ref[...].T


def _prompt_call(x, w, cmat, lmask, kmask, qhot, split, smp_proj, smp_decay, smp_state, smp_ck, smp_cv):
    bsz, seq, _ = x.shape
    tl = TOK_BLOCK
    n_t = seq // tl
    n = smp_state.shape[0]
    g = n // (bsz * n_t)
    assert g * bsz * n_t == n
    step = lambda b, t: b * n_t + t
    smp = lambda *tail: pl.BlockSpec((g,) + tail, lambda b, t: (step(b, t),) + (0,) * len(tail))
    smp_rows = lambda width: pl.BlockSpec((None, g, width), lambda b, t: (step(b, t), 0, 0))
    const = lambda shape: pl.BlockSpec(shape, lambda b, t: (0,) * len(shape), pipeline_mode=pl.Buffered(1))
    return pl.pallas_call(
        functools.partial(_prompt_kernel, n_t=n_t),
        grid=(bsz, n_t),
        in_specs=[
            pl.BlockSpec(memory_space=pltpu.SMEM),
            pl.BlockSpec((None, tl, D_MODEL), lambda b, t: (b, t, 0)),
            const((1, D_MODEL)), const((D_MODEL, W_ALL)), const((LANES, GLA_KW)), const((1, GLA_KW)),
            const(cmat.shape), const(lmask.shape), const(kmask.shape), const(qhot.shape), const(split.shape),
            const((1, GLA_W)), const((D_MODEL, D_MODEL)), const((1, D_MODEL)),
            smp_rows(W_MAIN), smp_rows(GLA_KW), const((SWA_HEADS, 1)),
            smp(GLA_KW, GLA_DV), smp(SWA_KVW, WINDOW), smp(SWA_KVW, WINDOW),
        ],
        out_specs=[
            pl.BlockSpec((None, tl, D_MODEL), lambda b, t: (b, t, 0)),
            pl.BlockSpec((None, GLA_KW, GLA_DV), lambda b, t: (b, 0, 0)),
            pl.BlockSpec((None, WINDOW, SWA_KVW), lambda b, t: (b, 0, 0)),
            pl.BlockSpec((None, WINDOW, SWA_KVW), lambda b, t: (b, 0, 0)),
            smp(GLA_KW, GLA_DV), smp(SWA_KVW, WINDOW), smp(SWA_KVW, WINDOW),
            smp(GLA_HEADS, GLA_DV), smp(SWA_HEADS, LANES),
        ],
        out_shape=[
            jax.ShapeDtypeStruct((bsz, seq, D_MODEL), f32),
            jax.ShapeDtypeStruct((bsz, GLA_KW, GLA_DV), f32),
            jax.ShapeDtypeStruct((bsz, WINDOW, SWA_KVW), f32),
            jax.ShapeDtypeStruct((bsz, WINDOW, SWA_KVW), f32),
            jax.ShapeDtypeStruct(smp_state.shape, f32), jax.ShapeDtypeStruct(smp_ck.shape, f32),
            jax.ShapeDtypeStruct(smp_cv.shape, f32),
            jax.ShapeDtypeStruct((n, GLA_HEADS, GLA_DV), f32), jax.ShapeDtypeStruct((n, SWA_HEADS, LANES), f32),
        ],
        scratch_shapes=[
            pltpu.VMEM((GLA_KW, GLA_DV), f32),
            pltpu.VMEM((WINDOW, SWA_KVW), f32), pltpu.VMEM((WINDOW, SWA_KVW), f32),
            pltpu.VMEM((tl, W_ALL), f32), pltpu.VMEM((tl, GLA_KW), f32),
            pltpu.VMEM((tl, GLA_W), f32), pltpu.VMEM((tl, SWA_W), f32),
        ],
        compiler_params=pltpu.CompilerParams(
            dimension_semantics=("arbitrary", "arbitrary"), vmem_limit_bytes=PROMPT_VMEM_LIMIT),
        name="prompt_layer",
    )(w["sinks"], x, w["norm_in"], w["w_all"], w["w_up"], w["b_gate"], cmat, lmask, kmask, qhot, split,
      w["gla_norm"], w["w_out"], w["norm_f"],
      smp_proj.reshape(bsz * n_t, g, W_MAIN), smp_decay.reshape(bsz * n_t, g, GLA_KW),
      w["sinks"].reshape(SWA_HEADS, 1), smp_state, smp_ck, smp_cv)


def _sample_proj_kernel(x_ref, nin_ref, w_ref, wup_ref, bg_ref, proj_ref, decay_ref):
    u = _rms(x_ref[...], nin_ref[...]).astype(bf16)
    proj_ref[...] = jnp.dot(u, w_ref[:, :W_MAIN], preferred_element_type=f32)
    glow = jnp.dot(u, w_ref[:, O_LOW:O_LOW + LANES], preferred_element_type=f32)
    decay_ref[...] = jnp.exp(_log_decay(glow, wup_ref, bg_ref))


def _sample_proj_call(xs, w):
    n = xs.shape[0]
    return pl.pallas_call(
        _sample_proj_kernel,
        out_shape=[jax.ShapeDtypeStruct((n, W_MAIN), f32), jax.ShapeDtypeStruct((n, GLA_KW), f32)],
        compiler_params=pltpu.CompilerParams(vmem_limit_bytes=VMEM_LIMIT),
        name="sample_proj",
    )(xs, w["norm_in"], w["w_all"], w["w_up"], w["b_gate"])


def _split3(x):
    as_bf16 = lambda v: v.astype(bf16).astype(f32)
    hi = as_bf16(x)
    mid = as_bf16(x - hi)
    return hi, mid, as_bf16(x - hi - mid)


def _sample_state_update(proj_ref, decay_ref, sink_ref, s_ref, ck_ref, cv_ref,
                         so_ref, cko_ref, cvo_ref, og_ref, os_ref):
    row = lax.broadcasted_iota(jnp.int32, (PACK, GLA_KW), 0)
    head_of_lane = lax.broadcasted_iota(jnp.int32, (PACK, GLA_KW), 1) // GLA_DK
    own_head = head_of_lane == row
    row_v = lax.broadcasted_iota(jnp.int32, (PACK, GLA_DV), 0)
    lane_v = lax.broadcasted_iota(jnp.int32, (PACK, GLA_DV), 1)
    piece_rows = ((row_v >= GLA_HEADS) & (row_v < GLA_HEADS + 3)).astype(f32)
    last_lane_rows = ((row_v < 3) & (lane_v == WINDOW - 1)).astype(bf16)
    newest = lax.broadcasted_iota(jnp.int32, (SWA_KVW, WINDOW), 1) == WINDOW - 1
    row_q = lax.broadcasted_iota(jnp.int32, (SWA_HEADS, LANES), 0)
    own_kv = (lax.broadcasted_iota(jnp.int32, (SWA_HEADS, LANES), 1) // SWA_HD) == (row_q // SWA_GROUP)
    sink = sink_ref[...]
    contract_rows = (((0,), (0,)), ((), ()))
    seqs = range(proj_ref.shape[0])
    lts, rts, qms, lt2s, q8s = [], [], [], [], []
    for j in seqs:
        pr = proj_ref[j:j + 1, :]
        bcast = lambda lo, width: jnp.broadcast_to(pr[:, lo:lo + width], (PACK, width))
        a_hi, a_mid, a_lo = _split3(jnp.broadcast_to(decay_ref[j:j + 1, :], (PACK, GLA_KW)))
        a_piece = jnp.where(row == GLA_HEADS, a_hi, jnp.where(row == GLA_HEADS + 1, a_mid, a_lo))
        lts.append(jnp.where(own_head, bcast(O_K, GLA_KW),
                             jnp.where((row >= GLA_HEADS) & (row < GLA_HEADS + 3), a_piece, 0.0)).astype(bf16))
        v_b = bcast(O_V, GLA_W)
        v_sel = jnp.zeros((PACK, GLA_DV), f32)
        for h in range(GLA_HEADS):
            v_sel = jnp.where(row_v == h, v_b[:, h * GLA_DV:(h + 1) * GLA_DV], v_sel)
        rts.append(jnp.concatenate([v_sel, piece_rows], axis=1).astype(bf16))
        qms.append(jnp.where(own_head, bcast(O_Q, GLA_KW), 0.0).astype(bf16))
        n_hi, n_mid, n_lo = _split3(bcast(O_SK, 2 * SWA_KVW))
        lt2s.append(jnp.where(row == 0, n_hi, jnp.where(row == 1, n_mid,
                                                        jnp.where(row == 2, n_lo, 0.0))).astype(bf16))
        sq_b = jnp.broadcast_to(pr[:, O_SQ:O_SQ + SWA_W], (SWA_HEADS, SWA_W))
        q8 = jnp.zeros((SWA_HEADS, LANES), f32)
        for gq in range(SWA_GROUP):
            q8 = jnp.where(row_q % SWA_GROUP == gq, sq_b[:, gq * LANES:(gq + 1) * LANES], q8)
        q8s.append(jnp.where(own_kv, q8, 0.0).astype(bf16))
    kv_as = [lax.dot_general(lts[j], rts[j], contract_rows, preferred_element_type=f32) for j in seqs]
    inss = [lax.dot_general(lt2s[j], last_lane_rows, contract_rows, preferred_element_type=f32) for j in seqs]
    s_news, kts, vts = [], [], []
    for j in seqs:
        s_new = kv_as[j][:, GLA_DV:] * s_ref[j] + kv_as[j][:, :GLA_DV]
        so_ref[j] = s_new
        s_news.append(s_new.astype(bf16))
        kt = jnp.where(newest, inss[j][:SWA_KVW], pltpu.roll(ck_ref[j], WINDOW - 1, axis=1))
        vt = jnp.where(newest, inss[j][SWA_KVW:], pltpu.roll(cv_ref[j], WINDOW - 1, axis=1))
        cko_ref[j] = kt
        cvo_ref[j] = vt
        kts.append(kt.astype(bf16))
        vts.append(vt.astype(bf16))
    for j in seqs:
        og_ref[j] = jnp.dot(qms[j], s_news[j], preferred_element_type=f32)[:GLA_HEADS]
    scores = [jnp.dot(q8s[j], kts[j], preferred_element_type=f32) for j in seqs]
    es, dens = [], []
    for j in seqs:
        m = jnp.maximum(jnp.max(scores[j], axis=-1, keepdims=True), sink)
        e = jnp.exp2(scores[j] - m)
        dens.append(jnp.sum(e, axis=-1, keepdims=True) + jnp.exp2(sink - m))
        es.append(e.astype(bf16))
    for j in seqs:
        o = lax.dot_general(es[j], vts[j], (((1,), (1,)), ((), ())), preferred_element_type=f32)
        os_ref[j] = o / dens[j]


def _sample_merge_kernel(x_ref, og_ref, gg_ref, os_ref, sg_ref, gn_ref, wout_ref, nf_ref, y_ref):
    y_ref[...] = _merge(x_ref[...], og_ref[...], gg_ref[...], os_ref[...], sg_ref[...],
                        gn_ref, wout_ref, nf_ref)


def _sample_merge_call(xs, og, gg, osw, sg, w):
    return pl.pallas_call(
        _sample_merge_kernel,
        out_shape=jax.ShapeDtypeStruct(xs.shape, f32),
        compiler_params=pltpu.CompilerParams(vmem_limit_bytes=VMEM_LIMIT),
        name="sample_merge",
    )(xs, og, gg, osw, sg, w["gla_norm"], w["w_out"], w["norm_f"])


def _sample_finish(xs, proj, og, os_raw, w):
    n = xs.shape[0]
    os5 = os_raw.reshape(n, SWA_KV, SWA_GROUP, SWA_KV, SWA_HD)
    os_il = jnp.stack([os5[:, kv, :, kv, :] for kv in range(SWA_KV)], axis=2).reshape(n, SWA_W)
    return _sample_merge_call(xs, og.reshape(n, GLA_W), proj[:, O_GG:O_GG + GLA_W], os_il,
                              proj[:, O_SG:O_SG + SWA_W], w)


def _cache_view(c):
    n = c.shape[1]
    return jnp.transpose(c[0], (0, 2, 3, 1)).reshape(n, SWA_KVW, WINDOW)


def _cache_unview(c):
    n = c.shape[0]
    return jnp.transpose(c.reshape(n, SWA_KV, SWA_HD, WINDOW), (0, 3, 1, 2))[None]


def kernel(x_prompt, x_sample, state_gla, cache_win_k, cache_win_v, norm_in, w_in, w_gate_up, b_gate,
           gla_norm, attn_sinks, w_out, norm_f):
    bsz = x_prompt.shape[0]
    n = x_sample.shape[0]
    w = _prep_weights(norm_in[0], w_in[0], w_gate_up[0], b_gate[0], gla_norm[0], attn_sinks[0],
                      w_out[0], norm_f)
    cmat, lmask = _chunk_tables()
    xs = x_sample.reshape(n, D_MODEL)
    proj, decay = _sample_proj_call(xs, w)
    y_p, s_p, k_p, v_p, s_s, k_s, v_s, og, os_raw = _prompt_call(
        x_prompt, w, cmat, lmask, *_swa_mask_tables(), proj, decay,
        state_gla[0].reshape(n, GLA_KW, GLA_DV), _cache_view(cache_win_k), _cache_view(cache_win_v))
    y_s = _sample_finish(xs, proj, og, os_raw, w)
    return (y_p, y_s.reshape(n, 1, D_MODEL),
            s_p.reshape(1, bsz, GLA_HEADS, GLA_DK, GLA_DV),
            _cache_unview(k_p), _cache_unview(v_p),
            s_s.reshape(1, n, GLA_HEADS, GLA_DK, GLA_DV),
            _cache_unview(k_s), _cache_unview(v_s))
```

```python
import functools

import numpy as np
import jax
import jax.numpy as jnp
from jax import lax
from jax.experimental import pallas as pl
from jax.experimental.pallas import tpu as pltpu

D_MODEL = 1024
GLA_HEADS = 4
GLA_DK = 64
GLA_DV = 128
GLA_KW = GLA_HEADS * GLA_DK
GLA_W = GLA_HEADS * GLA_DV
GLA_RANK = 16
GLA_TAU = 16.0
CHUNK = 64
SWA_HEADS = 8
SWA_HD = 64
SWA_KV = 2
SWA_GROUP = SWA_HEADS // SWA_KV
SWA_W = SWA_HEADS * SWA_HD
SWA_KVW = SWA_KV * SWA_HD
WINDOW = 128
EPS = 1e-6
NEG_INF = -1e30
LOG2E = 1.4426950408889634
LANES = 128

O_Q, O_K, O_V, O_GG = 0, 256, 512, 1024
O_SQ, O_SK, O_SV, O_SG, O_LOW = 1536, 2048, 2176, 2304, 2816
W_MAIN = 2816
W_ALL = W_MAIN + LANES

N_LEVELS = 6
TOK_BLOCK = 1024
GLA_GROUP = 4
PACK = 16
PROMPT_VMEM_LIMIT = 56 * 1024 * 1024
VMEM_LIMIT = 32 * 1024 * 1024

f32 = jnp.float32
bf16 = jnp.bfloat16


def _chunk_tables():
    c = CHUNK
    t = np.arange(c)[None, :]
    i = np.arange(c)[:, None]
    blocks = [(t <= i), (t > i)]
    masks = []
    for l in range(N_LEVELS):
        h = c >> (l + 1)
        m = (i // (2 * h)) * (2 * h) + h
        upper = i >= m
        blocks.append(np.where(upper, (t > m) & (t <= i), (t > i) & (t <= m)))
        jj = np.arange(c)[None, :]
        masks.append((i // (2 * h) == jj // (2 * h)) & (i % (2 * h) >= h) & (jj % (2 * h) < h))
    masks.append(np.eye(c, dtype=bool))
    cm = np.concatenate(blocks, axis=0).astype(np.float32)
    cm3 = np.concatenate([cm, cm, cm], axis=1)
    lm = np.stack(masks).astype(np.float32)
    lm = np.concatenate([lm, lm], axis=1)
    return jnp.asarray(cm3, dtype=bf16), jnp.asarray(lm, dtype=f32)


def _swa_split_table():
    row_q = np.arange(2 * WINDOW)[:, None] % WINDOW
    slot_is_prev = (np.arange(WINDOW)[None, :] > row_q).astype(np.float32)
    return jnp.asarray(np.stack([slot_is_prev, 1.0 - slot_is_prev]), dtype=bf16)


I_Q, I_K, I_V, I_GG, I_LOW, I_SQ, I_SK, I_SV, I_SG = (
    int(v) for v in np.cumsum([0, 256, 256, 512, 512, 16, 512, 128, 128])[:9])


def _interleaved_rows(base, tile):
    return [(base + kv * SWA_GROUP * SWA_HD + tile * SWA_HD, SWA_HD) for kv in range(SWA_KV)]


def _weight_layout_kernel(wt_ref, wo_ref, wall_ref, wop_ref):
    def put(col, pieces, scale=None):
        rows = [wt_ref[r:r + n, :] for r, n in pieces]
        missing = LANES - sum(n for _, n in pieces)
        if missing:
            rows.append(jnp.zeros((missing, D_MODEL), f32))
        blk = jnp.concatenate(rows, axis=0) if len(rows) > 1 else rows[0]
        if scale is not None:
            blk = blk * scale
        wall_ref[:, col:col + LANES] = blk.T.astype(bf16)

    for i in range(GLA_KW // LANES):
        put(O_Q + i * LANES, [(I_Q + i * LANES, LANES)], GLA_DK ** -0.5)
        put(O_K + i * LANES, [(I_K + i * LANES, LANES)])
    for i in range(GLA_W // LANES):
        put(O_V + i * LANES, [(I_V + i * LANES, LANES)])
        put(O_GG + i * LANES, [(I_GG + i * LANES, LANES)])
        put(O_SQ + i * LANES, _interleaved_rows(I_SQ, i), SWA_HD ** -0.5 * LOG2E)
        put(O_SG + i * LANES, _interleaved_rows(I_SG, i))
    put(O_SK, [(I_SK, SWA_KVW)])
    put(O_SV, [(I_SV, SWA_KVW)])
    put(O_LOW, [(I_LOW, GLA_RANK)])
    wop_ref[:GLA_W, :] = wo_ref[:GLA_W, :].astype(bf16)
    for i in range(SWA_GROUP):
        for kv, (r, n) in enumerate(_interleaved_rows(GLA_W, i)):
            dst = GLA_W + (i * SWA_KV + kv) * SWA_HD
            wop_ref[dst:dst + n, :] = wo_ref[r:r + n, :].astype(bf16)


def _prep_weights(norm_in, w_in, w_gate_up, b_gate, gla_norm, attn_sinks, w_out, norm_f):
    w_all, w_out_p = pl.pallas_call(
        _weight_layout_kernel,
        out_shape=[jax.ShapeDtypeStruct((D_MODEL, W_ALL), bf16), jax.ShapeDtypeStruct((D_MODEL, D_MODEL), bf16)],
        compiler_params=pltpu.CompilerParams(vmem_limit_bytes=VMEM_LIMIT),
        name="weight_layout",
    )(w_in.T, w_out)
    w_up = jnp.pad(w_gate_up, ((0, LANES - GLA_RANK), (0, 0))).astype(bf16)
    return dict(
        norm_in=norm_in.reshape(1, D_MODEL), w_all=w_all, w_up=w_up,
        b_gate=b_gate.reshape(1, GLA_KW), gla_norm=jnp.tile(gla_norm, GLA_HEADS).reshape(1, GLA_W),
        sinks=attn_sinks * LOG2E, w_out=w_out_p, norm_f=norm_f.reshape(1, D_MODEL))


def _rms(x, gain):
    return x * lax.rsqrt(jnp.mean(x * x, axis=-1, keepdims=True) + EPS) * gain


def _log_decay(glow, wup_ref, bg_ref):
    z = jnp.dot(glow.astype(bf16), wup_ref[...], preferred_element_type=f32) + bg_ref[...]
    return (jnp.minimum(z, 0.0) - jnp.log(1.0 + jnp.exp(-jnp.abs(z)))) * (1.0 / GLA_TAU)


def _silu(x):
    return x * jax.nn.sigmoid(x)


def _merge(x, o_gla, gg, o_swa, sg, gn_ref, wout_ref, nf_ref):
    parts = []
    for h in range(GLA_HEADS):
        sl = slice(h * GLA_DV, (h + 1) * GLA_DV)
        parts.append(_rms(o_gla[:, sl], gn_ref[:, sl]) * _silu(gg[:, sl]))
    parts.append(o_swa * _silu(sg))
    um = jnp.concatenate(parts, axis=1).astype(bf16)
    hres = x + jnp.dot(um, wout_ref[...], preferred_element_type=f32)
    return _rms(hres, nf_ref[...])


def _prompt_kernel(sinks_ref, x_ref, nin_ref, w_ref, wup_ref, bg_ref, cmat_ref, lmask_ref, split_ref,
                   gn_ref, wout_ref, nf_ref,
                   smp_proj_ref, smp_decay_ref, smp_sink_ref, smp_s_ref, smp_ck_ref, smp_cv_ref,
                   y_ref, sp_ref, kn_ref, vn_ref,
                   smp_so_ref, smp_cko_ref, smp_cvo_ref, smp_og_ref, smp_os_ref,
                   s_ref, kprev_ref, vprev_ref, p_s, g_s, ogla_s, oswa_s, *, n_t):
    t = pl.program_id(1)
    tl = x_ref.shape[0]

    @pl.when(t == 0)
    def _():
        s_ref[...] = jnp.zeros_like(s_ref)
        kprev_ref[...] = jnp.zeros_like(kprev_ref)
        vprev_ref[...] = jnp.zeros_like(vprev_ref)

    def project(rg):
        u = _rms(x_ref[rg, :], nin_ref[...]).astype(bf16)
        p_s[rg, :] = jnp.dot(u, w_ref[...], preferred_element_type=f32)
        g_s[rg, :] = _log_decay(p_s[rg, O_LOW:O_LOW + LANES], wup_ref, bg_ref)

    lane_lo = lax.broadcasted_iota(jnp.int32, (CHUNK, LANES), 1) < GLA_DK

    n_pairs = GLA_HEADS // 2

    def group_terms(chunks):
        rows = {c: slice(c * CHUNK, (c + 1) * CHUNK) for c in chunks}
        sums = {}
        for c in chunks:
            gc = g_s[rows[c], :]
            g_hi = gc.astype(bf16)
            r1 = gc - g_hi.astype(f32)
            g_mid = r1.astype(bf16)
            g_lo = (r1 - g_mid.astype(f32)).astype(bf16)
            sums[c] = jnp.dot(cmat_ref[...], jnp.concatenate([g_hi, g_mid, g_lo], axis=0),
                              preferred_element_type=f32)
        level_ops, misc = {}, {}
        for c in chunks:
            qc, kc = p_s[rows[c], O_Q:O_Q + GLA_KW], p_s[rows[c], O_K:O_K + GLA_KW]
            e_b = jnp.exp(sums[c][0:CHUNK])
            misc[c] = (e_b, qc * e_b, kc * jnp.exp(sums[c][CHUNK:2 * CHUNK]),
                       p_s[rows[c], O_V:O_V + GLA_W].astype(bf16))
            for p in range(n_pairs):
                ln = slice(p * LANES, (p + 1) * LANES)
                qp, kp = qc[:, ln], kc[:, ln]
                for l in range(N_LEVELS + 1):
                    if l < N_LEVELS:
                        e = jnp.exp(sums[c][(2 + l) * CHUNK:(3 + l) * CHUNK, ln])
                        qe, ke = (qp * e).astype(bf16), (kp * e).astype(bf16)
                    else:
                        qe, ke = qp.astype(bf16), kp.astype(bf16)
                    lhs = jnp.concatenate([jnp.where(lane_lo, qe, 0), jnp.where(lane_lo, 0, qe)], axis=0)
                    level_ops[c, p, l] = (lhs, ke)
        scores = {key: lax.dot_general(lhs, ke, (((1,), (1,)), ((), ())), preferred_element_type=f32)
                  for key, (lhs, ke) in level_ops.items()}
        terms = {}
        for c in chunks:
            e_b, qb, k_suf, vb = misc[c]
            terms[c] = []
            for p in range(n_pairs):
                ln = slice(p * LANES, (p + 1) * LANES)
                a = scores[c, p, 0] * lmask_ref[0]
                for l in range(1, N_LEVELS + 1):
                    a = a + scores[c, p, l] * lmask_ref[l]
                ab = a.astype(bf16)
                qbp = qb[:, ln].astype(bf16)
                lhs_heads = []
                for hh in range(2):
                    qbm = jnp.where(lane_lo, qbp, 0) if hh == 0 else jnp.where(lane_lo, 0, qbp)
                    lhs_heads.append(jnp.concatenate([qbm, ab[hh * CHUNK:(hh + 1) * CHUNK, :]], axis=1))
                upd = lax.dot_general(k_suf[:, ln].astype(bf16), vb[:, p * 2 * GLA_DV:(p + 1) * 2 * GLA_DV],
                                      (((0,), (0,)), ((), ())), preferred_element_type=f32)
                upd = jnp.concatenate(
                    [upd[0:GLA_DK, 0:GLA_DV], upd[GLA_DK:2 * GLA_DK, GLA_DV:2 * GLA_DV]], axis=0)
                e_col = jnp.broadcast_to(e_b[CHUNK - 1:CHUNK, ln], (LANES, LANES)).T
                terms[c].append((lhs_heads, vb, upd, e_col))
        return terms

    def gla(chunks):
        terms = {}
        for c0 in range(chunks[0], chunks[-1] + 1, GLA_GROUP):
            terms.update(group_terms(range(c0, c0 + GLA_GROUP)))
        states = {}
        for p in range(n_pairs):
            ln = slice(p * LANES, (p + 1) * LANES)
            s_pair = s_ref[ln, :]
            for c in chunks:
                states[c, p] = s_pair.astype(bf16)
                _, _, upd, e_col = terms[c][p]
                s_pair = e_col * s_pair + upd
            s_ref[ln, :] = s_pair
        for c in chunks:
            for p in range(n_pairs):
                lhs_heads, vb, _, _ = terms[c][p]
                for hh in range(2):
                    h = 2 * p + hh
                    rhs = jnp.concatenate([states[c, p], vb[:, h * GLA_DV:(h + 1) * GLA_DV]], axis=0)
                    ogla_s[c * CHUNK:(c + 1) * CHUNK, h * GLA_DV:(h + 1) * GLA_DV] = jnp.dot(
                        lhs_heads[hh], rhs, preferred_element_type=f32)

    lane_lo_w = lax.broadcasted_iota(jnp.int32, (WINDOW, LANES), 1) < SWA_HD
    row_lo = lax.broadcasted_iota(jnp.int32, (2 * WINDOW, 1), 0) < WINDOW
    slot_is_prev = (lax.broadcasted_iota(jnp.int32, (2 * WINDOW, WINDOW), 1)
                    > lax.broadcasted_iota(jnp.int32, (2 * WINDOW, WINDOW), 0) % WINDOW)

    def swa(blocks):
        for blk in blocks:
            rs = slice(blk * WINDOW, (blk + 1) * WINDOW)
            sq = p_s[rs, O_SQ:O_SQ + SWA_W].astype(bf16)
            k_cur, v_cur = p_s[rs, O_SK:O_SK + SWA_KVW], p_s[rs, O_SV:O_SV + SWA_KVW]
            k2 = jnp.concatenate([kprev_ref[...], k_cur], axis=0).astype(bf16)
            v2 = jnp.concatenate([vprev_ref[...], v_cur], axis=0).astype(bf16)
            for tt in range(SWA_GROUP):
                qt = sq[:, tt * LANES:(tt + 1) * LANES]
                lhs = jnp.concatenate([jnp.where(lane_lo_w, qt, 0), jnp.where(lane_lo_w, 0, qt)], axis=0)
                s = lax.dot_general(lhs, k2, (((1,), (1,)), ((), ())), preferred_element_type=f32)
                s_prev = s[:, :WINDOW]
                if blk == 0:
                    s_prev = s_prev + jnp.where(t > 0, 0.0, NEG_INF)
                s = jnp.where(slot_is_prev, s_prev, s[:, WINDOW:])
                sink = jnp.where(row_lo, sinks_ref[tt], sinks_ref[SWA_GROUP + tt])
                m = jnp.maximum(jnp.max(s, axis=-1, keepdims=True), sink)
                e = jnp.exp2(s - m)
                r = 1.0 / (jnp.sum(e, axis=-1, keepdims=True) + jnp.exp2(sink - m))
                eb = e.astype(bf16)
                e2 = jnp.concatenate([eb * split_ref[0], eb * split_ref[1]], axis=1)
                o2 = jnp.dot(e2, v2, preferred_element_type=f32)
                oswa_s[rs, tt * LANES:(tt + 1) * LANES] = jnp.where(
                    lane_lo_w, o2[:WINDOW] * r[:WINDOW], o2[WINDOW:] * r[WINDOW:])
            kprev_ref[...] = k_cur
            vprev_ref[...] = v_cur

    def merge_gla(rg):
        parts = [_rms(ogla_s[rg, h * GLA_DV:(h + 1) * GLA_DV], gn_ref[:, h * GLA_DV:(h + 1) * GLA_DV])
                 * _silu(p_s[rg, O_GG + h * GLA_DV:O_GG + (h + 1) * GLA_DV]) for h in range(GLA_HEADS)]
        um = jnp.concatenate(parts, axis=1).astype(bf16)
        y_ref[rg, :] = x_ref[rg, :] + jnp.dot(um, wout_ref[:GLA_W, :], preferred_element_type=f32)

    def merge_swa(rg):
        um = (oswa_s[rg, :] * _silu(p_s[rg, O_SG:O_SG + SWA_W])).astype(bf16)
        hres = y_ref[rg, :] + jnp.dot(um, wout_ref[GLA_W:, :], preferred_element_type=f32)
        y_ref[rg, :] = _rms(hres, nf_ref[...])

    whole = slice(0, tl)
    project(whole)
    _sample_state_update(smp_proj_ref, smp_decay_ref, smp_sink_ref, smp_s_ref, smp_ck_ref, smp_cv_ref,
                         smp_so_ref, smp_cko_ref, smp_cvo_ref, smp_og_ref, smp_os_ref)
    gla(range(tl // CHUNK))
    merge_gla(whole)
    swa(range(tl // WINDOW))
    merge_swa(whole)

    @pl.when(t == n_t - 1)
    def _():
        sp_ref[...] = s_ref[...]
        kn_ref[...] = kprev_ref[...].T
        vn_ref[...] = vprev_ref[...].T


def _prompt_call(x, w, cmat, lmask, split, smp_proj, smp_decay, smp_state, smp_ck, smp_cv):
    bsz, seq, _ = x.shape
    tl = TOK_BLOCK
    n_t = seq // tl
    n = smp_state.shape[0]
    g = n // (bsz * n_t)
    assert g * bsz * n_t == n
    step = lambda b, t: b * n_t + t
    smp = lambda *tail: pl.BlockSpec((g,) + tail, lambda b, t: (step(b, t),) + (0,) * len(tail))
    smp_rows = lambda width: pl.BlockSpec((None, g, width), lambda b, t: (step(b, t), 0, 0))
    const = lambda shape: pl.BlockSpec(shape, lambda b, t: (0,) * len(shape), pipeline_mode=pl.Buffered(1))
    return pl.pallas_call(
        functools.partial(_prompt_kernel, n_t=n_t),
        grid=(bsz, n_t),
        in_specs=[
            pl.BlockSpec(memory_space=pltpu.SMEM),
            pl.BlockSpec((None, tl, D_MODEL), lambda b, t: (b, t, 0)),
            const((1, D_MODEL)), const((D_MODEL, W_ALL)), const((LANES, GLA_KW)), const((1, GLA_KW)),
            const(cmat.shape), const(lmask.shape), const(split.shape),
            const((1, GLA_W)), const((D_MODEL, D_MODEL)), const((1, D_MODEL)),
            smp_rows(W_MAIN), smp_rows(GLA_KW), const((SWA_HEADS, 1)),
            smp(GLA_KW, GLA_DV), smp(SWA_KVW, WINDOW), smp(SWA_KVW, WINDOW),
        ],
        out_specs=[
            pl.BlockSpec((None, tl, D_MODEL), lambda b, t: (b, t, 0)),
            pl.BlockSpec((None, GLA_KW, GLA_DV), lambda b, t: (b, 0, 0)),
            pl.BlockSpec((None, WINDOW, SWA_KVW), lambda b, t: (b, 0, 0)),
            pl.BlockSpec((None, WINDOW, SWA_KVW), lambda b, t: (b, 0, 0)),
            smp(GLA_KW, GLA_DV), smp(SWA_KVW, WINDOW), smp(SWA_KVW, WINDOW),
            smp(GLA_HEADS, GLA_DV), smp(SWA_HEADS, LANES),
        ],
        out_shape=[
            jax.ShapeDtypeStruct((bsz, seq, D_MODEL), f32),
            jax.ShapeDtypeStruct((bsz, GLA_KW, GLA_DV), f32),
            jax.ShapeDtypeStruct((bsz, WINDOW, SWA_KVW), f32),
            jax.ShapeDtypeStruct((bsz, WINDOW, SWA_KVW), f32),
            jax.ShapeDtypeStruct(smp_state.shape, f32), jax.ShapeDtypeStruct(smp_ck.shape, f32),
            jax.ShapeDtypeStruct(smp_cv.shape, f32),
            jax.ShapeDtypeStruct((n, GLA_HEADS, GLA_DV), f32), jax.ShapeDtypeStruct((n, SWA_HEADS, LANES), f32),
        ],
        scratch_shapes=[
            pltpu.VMEM((GLA_KW, GLA_DV), f32),
            pltpu.VMEM((WINDOW, SWA_KVW), f32), pltpu.VMEM((WINDOW, SWA_KVW), f32),
            pltpu.VMEM((tl, W_ALL), f32), pltpu.VMEM((tl, GLA_KW), f32),
            pltpu.VMEM((tl, GLA_W), f32), pltpu.VMEM((tl, SWA_W), f32),
        ],
        compiler_params=pltpu.CompilerParams(
            dimension_semantics=("arbitrary", "arbitrary"), vmem_limit_bytes=PROMPT_VMEM_LIMIT),
        name="prompt_layer",
    )(w["sinks"], x, w["norm_in"], w["w_all"], w["w_up"], w["b_gate"], cmat, lmask, split,
      w["gla_norm"], w["w_out"], w["norm_f"],
      smp_proj.reshape(bsz * n_t, g, W_MAIN), smp_decay.reshape(bsz * n_t, g, GLA_KW),
      w["sinks"].reshape(SWA_HEADS, 1), smp_state, smp_ck, smp_cv)


def _sample_proj_kernel(x_ref, nin_ref, w_ref, wup_ref, bg_ref, proj_ref, decay_ref):
    u = _rms(x_ref[...], nin_ref[...]).astype(bf16)
    proj_ref[...] = jnp.dot(u, w_ref[:, :W_MAIN], preferred_element_type=f32)
    glow = jnp.dot(u, w_ref[:, O_LOW:O_LOW + LANES], preferred_element_type=f32)
    decay_ref[...] = jnp.exp(_log_decay(glow, wup_ref, bg_ref))


def _sample_proj_call(xs, w):
    n = xs.shape[0]
    return pl.pallas_call(
        _sample_proj_kernel,
        out_shape=[jax.ShapeDtypeStruct((n, W_MAIN), f32), jax.ShapeDtypeStruct((n, GLA_KW), f32)],
        compiler_params=pltpu.CompilerParams(vmem_limit_bytes=VMEM_LIMIT),
        name="sample_proj",
    )(xs, w["norm_in"], w["w_all"], w["w_up"], w["b_gate"])


def _split3(x):
    as_bf16 = lambda v: v.astype(bf16).astype(f32)
    hi = as_bf16(x)
    mid = as_bf16(x - hi)
    return hi, mid, as_bf16(x - hi - mid)


def _sample_state_update(proj_ref, decay_ref, sink_ref, s_ref, ck_ref, cv_ref,
                         so_ref, cko_ref, cvo_ref, og_ref, os_ref):
    row = lax.broadcasted_iota(jnp.int32, (PACK, GLA_KW), 0)
    head_of_lane = lax.broadcasted_iota(jnp.int32, (PACK, GLA_KW), 1) // GLA_DK
    own_head = head_of_lane == row
    row_v = lax.broadcasted_iota(jnp.int32, (PACK, GLA_DV), 0)
    lane_v = lax.broadcasted_iota(jnp.int32, (PACK, GLA_DV), 1)
    piece_rows = ((row_v >= GLA_HEADS) & (row_v < GLA_HEADS + 3)).astype(f32)
    last_lane_rows = ((row_v < 3) & (lane_v == WINDOW - 1)).astype(bf16)
    newest = lax.broadcasted_iota(jnp.int32, (SWA_KVW, WINDOW), 1) == WINDOW - 1
    row_q = lax.broadcasted_iota(jnp.int32, (SWA_HEADS, LANES), 0)
    own_kv = (lax.broadcasted_iota(jnp.int32, (SWA_HEADS, LANES), 1) // SWA_HD) == (row_q // SWA_GROUP)
    sink = sink_ref[...]
    contract_rows = (((0,), (0,)), ((), ()))
    seqs = range(proj_ref.shape[0])
    lts, rts, qms, lt2s, q8s = [], [], [], [], []
    for j in seqs:
        pr = proj_ref[j:j + 1, :]
        bcast = lambda lo, width: jnp.broadcast_to(pr[:, lo:lo + width], (PACK, width))
        a_hi, a_mid, a_lo = _split3(jnp.broadcast_to(decay_ref[j:j + 1, :], (PACK, GLA_KW)))
        a_piece = jnp.where(row == GLA_HEADS, a_hi, jnp.where(row == GLA_HEADS + 1, a_mid, a_lo))
        lts.append(jnp.where(own_head, bcast(O_K, GLA_KW),
                             jnp.where((row >= GLA_HEADS) & (row < GLA_HEADS + 3), a_piece, 0.0)).astype(bf16))
        v_b = bcast(O_V, GLA_W)
        v_sel = jnp.zeros((PACK, GLA_DV), f32)
        for h in range(GLA_HEADS):
            v_sel = jnp.where(row_v == h, v_b[:, h * GLA_DV:(h + 1) * GLA_DV], v_sel)
        rts.append(jnp.concatenate([v_sel, piece_rows], axis=1).astype(bf16))
        qms.append(jnp.where(own_head, bcast(O_Q, GLA_KW), 0.0).astype(bf16))
        n_hi, n_mid, n_lo = _split3(bcast(O_SK, 2 * SWA_KVW))
        lt2s.append(jnp.where(row == 0, n_hi, jnp.where(row == 1, n_mid,
                                                        jnp.where(row == 2, n_lo, 0.0))).astype(bf16))
        sq_b = jnp.broadcast_to(pr[:, O_SQ:O_SQ + SWA_W], (SWA_HEADS, SWA_W))
        q8 = jnp.zeros((SWA_HEADS, LANES), f32)
        for gq in range(SWA_GROUP):
            q8 = jnp.where(row_q % SWA_GROUP == gq, sq_b[:, gq * LANES:(gq + 1) * LANES], q8)
        q8s.append(jnp.where(own_kv, q8, 0.0).astype(bf16))
    kv_as = [lax.dot_general(lts[j], rts[j], contract_rows, preferred_element_type=f32) for j in seqs]
    inss = [lax.dot_general(lt2s[j], last_lane_rows, contract_rows, preferred_element_type=f32) for j in seqs]
    s_news, kts, vts = [], [], []
    for j in seqs:
        s_new = kv_as[j][:, GLA_DV:] * s_ref[j] + kv_as[j][:, :GLA_DV]
        so_ref[j] = s_new
        s_news.append(s_new.astype(bf16))
        kt = jnp.where(newest, inss[j][:SWA_KVW], pltpu.roll(ck_ref[j], WINDOW - 1, axis=1))
        vt = jnp.where(newest, inss[j][SWA_KVW:], pltpu.roll(cv_ref[j], WINDOW - 1, axis=1))
        cko_ref[j] = kt
        cvo_ref[j] = vt
        kts.append(kt.astype(bf16))
        vts.append(vt.astype(bf16))
    for j in seqs:
        og_ref[j] = jnp.dot(qms[j], s_news[j], preferred_element_type=f32)[:GLA_HEADS]
    scores = [jnp.dot(q8s[j], kts[j], preferred_element_type=f32) for j in seqs]
    es, dens = [], []
    for j in seqs:
        m = jnp.maximum(jnp.max(scores[j], axis=-1, keepdims=True), sink)
        e = jnp.exp2(scores[j] - m)
        dens.append(jnp.sum(e, axis=-1, keepdims=True) + jnp.exp2(sink - m))
        es.append(e.astype(bf16))
    for j in seqs:
        o = lax.dot_general(es[j], vts[j], (((1,), (1,)), ((), ())), preferred_element_type=f32)
        os_ref[j] = o / dens[j]


def _sample_merge_kernel(x_ref, og_ref, gg_ref, os_ref, sg_ref, gn_ref, wout_ref, nf_ref, y_ref):
    y_ref[...] = _merge(x_ref[...], og_ref[...], gg_ref[...], os_ref[...], sg_ref[...],
                        gn_ref, wout_ref, nf_ref)


def _sample_merge_call(xs, og, gg, osw, sg, w):
    return pl.pallas_call(
        _sample_merge_kernel,
        out_shape=jax.ShapeDtypeStruct(xs.shape, f32),
        compiler_params=pltpu.CompilerParams(vmem_limit_bytes=VMEM_LIMIT),
        name="sample_merge",
    )(xs, og, gg, osw, sg, w["gla_norm"], w["w_out"], w["norm_f"])


def _sample_finish(xs, proj, og, os_raw, w):
    n = xs.shape[0]
    os5 = os_raw.reshape(n, SWA_KV, SWA_GROUP, SWA_KV, SWA_HD)
    os_il = jnp.stack([os5[:, kv, :, kv, :] for kv in range(SWA_KV)], axis=2).reshape(n, SWA_W)
    return _sample_merge_call(xs, og.reshape(n, GLA_W), proj[:, O_GG:O_GG + GLA_W], os_il,
                              proj[:, O_SG:O_SG + SWA_W], w)


def _cache_view(c):
    n = c.shape[1]
    return jnp.transpose(c[0], (0, 2, 3, 1)).reshape(n, SWA_KVW, WINDOW)


def _cache_unview(c):
    n = c.shape[0]
    return jnp.transpose(c.reshape(n, SWA_KV, SWA_HD, WINDOW), (0, 3, 1, 2))[None]


def kernel(x_prompt, x_sample, state_gla, cache_win_k, cache_win_v, norm_in, w_in, w_gate_up, b_gate,
           gla_norm, attn_sinks, w_out, norm_f):
    bsz = x_prompt.shape[0]
    n = x_sample.shape[0]
    w = _prep_weights(norm_in[0], w_in[0], w_gate_up[0], b_gate[0], gla_norm[0], attn_sinks[0],
                      w_out[0], norm_f)
    cmat, lmask = _chunk_tables()
    xs = x_sample.reshape(n, D_MODEL)
    proj, decay = _sample_proj_call(xs, w)
    y_p, s_p, k_p, v_p, s_s, k_s, v_s, og, os_raw = _prompt_call(
        x_prompt, w, cmat, lmask, _swa_split_table(), proj, decay,
        state_gla[0].reshape(n, GLA_KW, GLA_DV), _cache_view(cache_win_k), _cache_view(cache_win_v))
    y_s = _sample_finish(xs, proj, og, os_raw, w)
    return (y_p, y_s.reshape(n, 1, D_MODEL),
            s_p.reshape(1, bsz, GLA_HEADS, GLA_DK, GLA_DV),
            _cache_unview(k_p), _cache_unview(v_p),
            s_s.reshape(1, n, GLA_HEADS, GLA_DK, GLA_DV),
            _cache_unview(k_s), _cache_unview(v_s))
```

```python
import functools

import numpy as np
import jax
import jax.numpy as jnp
from jax import lax
from jax.experimental import pallas as pl
from jax.experimental.pallas import tpu as pltpu

D_MODEL = 1024
GLA_HEADS = 4
GLA_DK = 64
GLA_DV = 128
GLA_KW = GLA_HEADS * GLA_DK
GLA_W = GLA_HEADS * GLA_DV
GLA_RANK = 16
GLA_TAU = 16.0
CHUNK = 64
SWA_HEADS = 8
SWA_HD = 64
SWA_KV = 2
SWA_GROUP = SWA_HEADS // SWA_KV
SWA_W = SWA_HEADS * SWA_HD
SWA_KVW = SWA_KV * SWA_HD
WINDOW = 128
EPS = 1e-6
NEG_INF = -1e30
LOG2E = 1.4426950408889634
LANES = 128

O_Q, O_K, O_V, O_GG = 0, 256, 512, 1024
O_SQ, O_SK, O_SV, O_SG, O_LOW = 1536, 2048, 2176, 2304, 2816
W_MAIN = 2816
W_ALL = W_MAIN + LANES

N_LEVELS = 6
MILD_EXP_BOUND = 80.0
TOK_BLOCK = 512
GLA_GROUP = 4
PACK = 16
PROMPT_VMEM_LIMIT = 56 * 1024 * 1024
VMEM_LIMIT = 32 * 1024 * 1024

f32 = jnp.float32
bf16 = jnp.bfloat16


def _chunk_tables():
    c = CHUNK
    t = np.arange(c)[None, :]
    i = np.arange(c)[:, None]
    blocks = [(t <= i), (t > i)]
    masks = []
    for l in range(N_LEVELS):
        h = c >> (l + 1)
        m = (i // (2 * h)) * (2 * h) + h
        upper = i >= m
        blocks.append(np.where(upper, (t > m) & (t <= i), (t > i) & (t <= m)))
        jj = np.arange(c)[None, :]
        masks.append((i // (2 * h) == jj // (2 * h)) & (i % (2 * h) >= h) & (jj % (2 * h) < h))
    masks.append(np.eye(c, dtype=bool))
    masks.append(np.tril(np.ones((c, c), dtype=bool)))
    cm = np.concatenate(blocks, axis=0).astype(np.float32)
    cm3 = np.concatenate([cm, cm, cm], axis=1)
    lm = np.stack(masks).astype(np.float32)
    lm = np.concatenate([lm, lm], axis=1)
    return jnp.asarray(cm3, dtype=bf16), jnp.asarray(lm, dtype=f32)


def _swa_mask_tables():
    key = np.arange(2 * WINDOW)[:, None]
    qi = np.arange(WINDOW)[None, :]
    prev_ok = (key < WINDOW) & (key > qi)
    cur_ok = (key >= WINDOW) & (key - WINDOW <= qi)
    full = np.where(prev_ok | cur_ok, 0.0, NEG_INF)
    first = np.where(cur_ok, 0.0, NEG_INF)
    onehot = (np.arange(2 * WINDOW)[:, None] % WINDOW == qi).astype(np.float32)
    row_q = np.arange(2 * WINDOW)[:, None] % WINDOW
    slot_is_prev = (np.arange(WINDOW)[None, :] > row_q).astype(np.float32)
    split = np.stack([slot_is_prev, 1.0 - slot_is_prev])
    return (jnp.asarray(np.stack([full, first]), dtype=bf16), jnp.asarray(onehot, dtype=bf16),
            jnp.asarray(split, dtype=bf16))


I_Q, I_K, I_V, I_GG, I_LOW, I_SQ, I_SK, I_SV, I_SG = (
    int(v) for v in np.cumsum([0, 256, 256, 512, 512, 16, 512, 128, 128])[:9])


def _interleaved_rows(base, tile):
    return [(base + kv * SWA_GROUP * SWA_HD + tile * SWA_HD, SWA_HD) for kv in range(SWA_KV)]


def _weight_layout_kernel(wt_ref, wo_ref, wall_ref, wop_ref):
    def put(col, pieces, scale=None):
        rows = [wt_ref[r:r + n, :] for r, n in pieces]
        missing = LANES - sum(n for _, n in pieces)
        if missing:
            rows.append(jnp.zeros((missing, D_MODEL), f32))
        blk = jnp.concatenate(rows, axis=0) if len(rows) > 1 else rows[0]
        if scale is not None:
            blk = blk * scale
        wall_ref[:, col:col + LANES] = blk.T.astype(bf16)

    for i in range(GLA_KW // LANES):
        put(O_Q + i * LANES, [(I_Q + i * LANES, LANES)], GLA_DK ** -0.5)
        put(O_K + i * LANES, [(I_K + i * LANES, LANES)])
    for i in range(GLA_W // LANES):
        put(O_V + i * LANES, [(I_V + i * LANES, LANES)])
        put(O_GG + i * LANES, [(I_GG + i * LANES, LANES)])
        put(O_SQ + i * LANES, _interleaved_rows(I_SQ, i), SWA_HD ** -0.5 * LOG2E)
        put(O_SG + i * LANES, _interleaved_rows(I_SG, i))
    put(O_SK, [(I_SK, SWA_KVW)])
    put(O_SV, [(I_SV, SWA_KVW)])
    put(O_LOW, [(I_LOW, GLA_RANK)])
    wop_ref[:GLA_W, :] = wo_ref[:GLA_W, :].astype(bf16)
    for i in range(SWA_GROUP):
        for kv, (r, n) in enumerate(_interleaved_rows(GLA_W, i)):
            dst = GLA_W + (i * SWA_KV + kv) * SWA_HD
            wop_ref[dst:dst + n, :] = wo_ref[r:r + n, :].astype(bf16)


def _prep_weights(norm_in, w_in, w_gate_up, b_gate, gla_norm, attn_sinks, w_out, norm_f):
    w_all, w_out_p = pl.pallas_call(
        _weight_layout_kernel,
        out_shape=[jax.ShapeDtypeStruct((D_MODEL, W_ALL), bf16), jax.ShapeDtypeStruct((D_MODEL, D_MODEL), bf16)],
        compiler_params=pltpu.CompilerParams(vmem_limit_bytes=VMEM_LIMIT),
        name="weight_layout",
    )(w_in.T, w_out)
    w_up = jnp.pad(w_gate_up, ((0, LANES - GLA_RANK), (0, 0))).astype(bf16)
    return dict(
        norm_in=norm_in.reshape(1, D_MODEL), w_all=w_all, w_up=w_up,
        b_gate=b_gate.reshape(1, GLA_KW), gla_norm=jnp.tile(gla_norm, GLA_HEADS).reshape(1, GLA_W),
        sinks=attn_sinks * LOG2E, w_out=w_out_p, norm_f=norm_f.reshape(1, D_MODEL))


def _rms(x, gain):
    return x * lax.rsqrt(jnp.mean(x * x, axis=-1, keepdims=True) + EPS) * gain


def _log_decay(glow, wup_ref, bg_ref):
    z = jnp.dot(glow.astype(bf16), wup_ref[...], preferred_element_type=f32) + bg_ref[...]
    return (jnp.minimum(z, 0.0) - jnp.log(1.0 + jnp.exp(-jnp.abs(z)))) * (1.0 / GLA_TAU)


def _silu(x):
    return x * jax.nn.sigmoid(x)


def _merge(x, o_gla, gg, o_swa, sg, gn_ref, wout_ref, nf_ref):
    parts = []
    for h in range(GLA_HEADS):
        sl = slice(h * GLA_DV, (h + 1) * GLA_DV)
        parts.append(_rms(o_gla[:, sl], gn_ref[:, sl]) * _silu(gg[:, sl]))
    parts.append(o_swa * _silu(sg))
    um = jnp.concatenate(parts, axis=1).astype(bf16)
    hres = x + jnp.dot(um, wout_ref[...], preferred_element_type=f32)
    return _rms(hres, nf_ref[...])


def _prompt_kernel(sinks_ref, x_ref, nin_ref, w_ref, wup_ref, bg_ref, cmat_ref, lmask_ref, kmask_ref, qhot_ref,
                   split_ref,
                   gn_ref, wout_ref, nf_ref,
                   smp_proj_ref, smp_decay_ref, smp_sink_ref, smp_s_ref, smp_ck_ref, smp_cv_ref,
                   y_ref, sp_ref, kn_ref, vn_ref,
                   smp_so_ref, smp_cko_ref, smp_cvo_ref, smp_og_ref, smp_os_ref,
                   s_ref, kprev_ref, vprev_ref, p_s, g_s, ogla_s, oswa_s, *, n_t):
    t = pl.program_id(1)
    tl = x_ref.shape[0]

    @pl.when(t == 0)
    def _():
        s_ref[...] = jnp.zeros_like(s_ref)
        kprev_ref[...] = jnp.zeros_like(kprev_ref)
        vprev_ref[...] = jnp.zeros_like(vprev_ref)

    def project(rg):
        u = _rms(x_ref[rg, :], nin_ref[...]).astype(bf16)
        p_s[rg, :] = jnp.dot(u, w_ref[...], preferred_element_type=f32)
        g_s[rg, :] = _log_decay(p_s[rg, O_LOW:O_LOW + LANES], wup_ref, bg_ref)

    lane_lo = lax.broadcasted_iota(jnp.int32, (CHUNK, LANES), 1) < GLA_DK

    n_pairs = GLA_HEADS // 2

    def group_terms(chunks):
        rows = {c: slice(c * CHUNK, (c + 1) * CHUNK) for c in chunks}
        sums = {}
        for c in chunks:
            gc = g_s[rows[c], :]
            g_hi = gc.astype(bf16)
            r1 = gc - g_hi.astype(f32)
            g_mid = r1.astype(bf16)
            g_lo = (r1 - g_mid.astype(f32)).astype(bf16)
            sums[c] = jnp.dot(cmat_ref[...], jnp.concatenate([g_hi, g_mid, g_lo], axis=0),
                              preferred_element_type=f32)
        level_ops, misc = {}, {}
        for c in chunks:
            qc, kc = p_s[rows[c], O_Q:O_Q + GLA_KW], p_s[rows[c], O_K:O_K + GLA_KW]
            e_b = jnp.exp(sums[c][0:CHUNK])
            misc[c] = (e_b, qc * e_b, kc * jnp.exp(sums[c][CHUNK:2 * CHUNK]),
                       p_s[rows[c], O_V:O_V + GLA_W].astype(bf16))
            for p in range(n_pairs):
                ln = slice(p * LANES, (p + 1) * LANES)
                qp, kp = qc[:, ln], kc[:, ln]
                for l in range(N_LEVELS + 1):
                    if l < N_LEVELS:
                        e = jnp.exp(sums[c][(2 + l) * CHUNK:(3 + l) * CHUNK, ln])
                        qe, ke = (qp * e).astype(bf16), (kp * e).astype(bf16)
                    else:
                        qe, ke = qp.astype(bf16), kp.astype(bf16)
                    lhs = jnp.concatenate([jnp.where(lane_lo, qe, 0), jnp.where(lane_lo, 0, qe)], axis=0)
                    level_ops[c, p, l] = (lhs, ke)
        scores = {key: lax.dot_general(lhs, ke, (((1,), (1,)), ((), ())), preferred_element_type=f32)
                  for key, (lhs, ke) in level_ops.items()}
        terms = {}
        for c in chunks:
            e_b, qb, k_suf, vb = misc[c]
            terms[c] = []
            for p in range(n_pairs):
                ln = slice(p * LANES, (p + 1) * LANES)
                a = scores[c, p, 0] * lmask_ref[0]
                for l in range(1, N_LEVELS + 1):
                    a = a + scores[c, p, l] * lmask_ref[l]
                ab = a.astype(bf16)
                qbp = qb[:, ln].astype(bf16)
                lhs_heads = []
                for hh in range(2):
                    qbm = jnp.where(lane_lo, qbp, 0) if hh == 0 else jnp.where(lane_lo, 0, qbp)
                    lhs_heads.append(jnp.concatenate([qbm, ab[hh * CHUNK:(hh + 1) * CHUNK, :]], axis=1))
                upd = lax.dot_general(k_suf[:, ln].astype(bf16), vb[:, p * 2 * GLA_DV:(p + 1) * 2 * GLA_DV],
                                      (((0,), (0,)), ((), ())), preferred_element_type=f32)
                upd = jnp.concatenate(
                    [upd[0:GLA_DK, 0:GLA_DV], upd[GLA_DK:2 * GLA_DK, GLA_DV:2 * GLA_DV]], axis=0)
                e_col = jnp.broadcast_to(e_b[CHUNK - 1:CHUNK, ln], (LANES, LANES)).T
                terms[c].append((lhs_heads, vb, upd, e_col))
        return terms

    def group_terms_mild(chunks):
        rows = {c: slice(c * CHUNK, (c + 1) * CHUNK) for c in chunks}
        prefix = {}
        for c in chunks:
            gc = g_s[rows[c], :]
            g_hi = gc.astype(bf16)
            r1 = gc - g_hi.astype(f32)
            g_mid = r1.astype(bf16)
            g_lo = (r1 - g_mid.astype(f32)).astype(bf16)
            prefix[c] = jnp.dot(cmat_ref[0:CHUNK, :], jnp.concatenate([g_hi, g_mid, g_lo], axis=0),
                                preferred_element_type=f32)
        score_ops, misc = {}, {}
        for c in chunks:
            qc, kc = p_s[rows[c], O_Q:O_Q + GLA_KW], p_s[rows[c], O_K:O_K + GLA_KW]
            b = prefix[c]
            e_b = jnp.exp(b)
            qb, k_neg = qc * e_b, kc * jnp.exp(-b)
            misc[c] = (e_b, kc * jnp.exp(b[CHUNK - 1:CHUNK, :] - b), p_s[rows[c], O_V:O_V + GLA_W].astype(bf16))
            for p in range(n_pairs):
                ln = slice(p * LANES, (p + 1) * LANES)
                qbp = qb[:, ln].astype(bf16)
                qbms = [jnp.where(lane_lo, qbp, 0), jnp.where(lane_lo, 0, qbp)]
                score_ops[c, p] = (qbms, k_neg[:, ln].astype(bf16))
        scores = {key: lax.dot_general(jnp.concatenate(qbms, axis=0), ke, (((1,), (1,)), ((), ())),
                                       preferred_element_type=f32)
                  for key, (qbms, ke) in score_ops.items()}
        terms = {}
        for c in chunks:
            e_b, k_suf, vb = misc[c]
            terms[c] = []
            for p in range(n_pairs):
                ln = slice(p * LANES, (p + 1) * LANES)
                ab = (scores[c, p] * lmask_ref[N_LEVELS + 1]).astype(bf16)
                qbms = score_ops[c, p][0]
                lhs_heads = [jnp.concatenate([qbms[hh], ab[hh * CHUNK:(hh + 1) * CHUNK, :]], axis=1)
                             for hh in range(2)]
                upd = lax.dot_general(k_suf[:, ln].astype(bf16), vb[:, p * 2 * GLA_DV:(p + 1) * 2 * GLA_DV],
                                      (((0,), (0,)), ((), ())), preferred_element_type=f32)
                upd = jnp.concatenate(
                    [upd[0:GLA_DK, 0:GLA_DV], upd[GLA_DK:2 * GLA_DK, GLA_DV:2 * GLA_DV]], axis=0)
                e_col = jnp.broadcast_to(e_b[CHUNK - 1:CHUNK, ln], (LANES, LANES)).T
                terms[c].append((lhs_heads, vb, upd, e_col))
        return terms

    def gla(chunks, group_fn):
        terms = {}
        for c0 in range(chunks[0], chunks[-1] + 1, GLA_GROUP):
            terms.update(group_fn(range(c0, c0 + GLA_GROUP)))
        states = {}
        for p in range(n_pairs):
            ln = slice(p * LANES, (p + 1) * LANES)
            s_pair = s_ref[ln, :]
            for c in chunks:
                states[c, p] = s_pair.astype(bf16)
                _, _, upd, e_col = terms[c][p]
                s_pair = e_col * s_pair + upd
            s_ref[ln, :] = s_pair
        for c in chunks:
            for p in range(n_pairs):
                lhs_heads, vb, _, _ = terms[c][p]
                for hh in range(2):
                    h = 2 * p + hh
                    rhs = jnp.concatenate([states[c, p], vb[:, h * GLA_DV:(h + 1) * GLA_DV]], axis=0)
                    ogla_s[c * CHUNK:(c + 1) * CHUNK, h * GLA_DV:(h + 1) * GLA_DV] = jnp.dot(
                        lhs_heads[hh], rhs, preferred_element_type=f32)

    lane_lo_w = lax.broadcasted_iota(jnp.int32, (WINDOW, LANES), 1) < SWA_HD
    row_lo = lax.broadcasted_iota(jnp.int32, (2 * WINDOW, 1), 0) < WINDOW

    def swa(blocks):
        for blk in blocks:
            rs = slice(blk * WINDOW, (blk + 1) * WINDOW)
            sq = p_s[rs, O_SQ:O_SQ + SWA_W].astype(bf16)
            k_cur, v_cur = p_s[rs, O_SK:O_SK + SWA_KVW], p_s[rs, O_SV:O_SV + SWA_KVW]
            k2 = jnp.concatenate([kprev_ref[...], k_cur], axis=0).astype(bf16)
            v2 = jnp.concatenate([vprev_ref[...], v_cur], axis=0).astype(bf16)
            kmask = kmask_ref[jnp.where(t > 0, 0, 1)] if blk == 0 else kmask_ref[0]
            k2m = jnp.concatenate([k2, kmask], axis=1)
            for tt in range(SWA_GROUP):
                qt = sq[:, tt * LANES:(tt + 1) * LANES]
                lhs = jnp.concatenate([jnp.where(lane_lo_w, qt, 0), jnp.where(lane_lo_w, 0, qt)], axis=0)
                lhs = jnp.concatenate([lhs, qhot_ref[...]], axis=1)
                s = lax.dot_general(lhs, k2m, (((1,), (1,)), ((), ())), preferred_element_type=f32)
                s = jnp.maximum(s[:, :WINDOW], s[:, WINDOW:])
                sink = jnp.where(row_lo, sinks_ref[tt], sinks_ref[SWA_GROUP + tt])
                m = jnp.maximum(jnp.max(s, axis=-1, keepdims=True), sink)
                e = jnp.exp2(s - m)
                r = 1.0 / (jnp.sum(e, axis=-1, keepdims=True) + jnp.exp2(sink - m))
                eb = e.astype(bf16)
                e2 = jnp.concatenate([eb * split_ref[0], eb * split_ref[1]], axis=1)
                o2 = jnp.dot(e2, v2, preferred_element_type=f32)
                oswa_s[rs, tt * LANES:(tt + 1) * LANES] = jnp.where(
                    lane_lo_w, o2[:WINDOW] * r[:WINDOW], o2[WINDOW:] * r[WINDOW:])
            kprev_ref[...] = k_cur
            vprev_ref[...] = v_cur

    def merge_gla(rg):
        parts = [_rms(ogla_s[rg, h * GLA_DV:(h + 1) * GLA_DV], gn_ref[:, h * GLA_DV:(h + 1) * GLA_DV])
                 * _silu(p_s[rg, O_GG + h * GLA_DV:O_GG + (h + 1) * GLA_DV]) for h in range(GLA_HEADS)]
        um = jnp.concatenate(parts, axis=1).astype(bf16)
        y_ref[rg, :] = x_ref[rg, :] + jnp.dot(um, wout_ref[:GLA_W, :], preferred_element_type=f32)

    def merge_swa(rg):
        um = (oswa_s[rg, :] * _silu(p_s[rg, O_SG:O_SG + SWA_W])).astype(bf16)
        hres = y_ref[rg, :] + jnp.dot(um, wout_ref[GLA_W:, :], preferred_element_type=f32)
        y_ref[rg, :] = _rms(hres, nf_ref[...])

    whole = slice(0, tl)
    project(whole)
    _sample_state_update(smp_proj_ref, smp_decay_ref, smp_sink_ref, smp_s_ref, smp_ck_ref, smp_cv_ref,
                         smp_so_ref, smp_cko_ref, smp_cvo_ref, smp_og_ref, smp_os_ref)
    mild = jnp.min(g_s[...]) * CHUNK > -MILD_EXP_BOUND
    pl.when(mild)(lambda: gla(range(tl // CHUNK), group_terms_mild))
    pl.when(jnp.logical_not(mild))(lambda: gla(range(tl // CHUNK), group_terms))
    merge_gla(whole)
    swa(range(tl // WINDOW))
    merge_swa(whole)

    @pl.when(t == n_t - 1)
    def _():
        sp_ref[...] = s_ref[...]
        kn_ref[...] = kprev_ref[...].T
        vn_ref[...] = vprev_ref[...].T


def _prompt_call(x, w, cmat, lmask, kmask, qhot, split, smp_proj, smp_decay, smp_state, smp_ck, smp_cv):
    bsz, seq, _ = x.shape
    tl = TOK_BLOCK
    n_t = seq // tl
    n = smp_state.shape[0]
    g = n // (bsz * n_t)
    assert g * bsz * n_t == n
    step = lambda b, t: b * n_t + t
    smp = lambda *tail: pl.BlockSpec((g,) + tail, lambda b, t: (step(b, t),) + (0,) * len(tail))
    smp_rows = lambda width: pl.BlockSpec((None, g, width), lambda b, t: (step(b, t), 0, 0))
    const = lambda shape: pl.BlockSpec(shape, lambda b, t: (0,) * len(shape), pipeline_mode=pl.Buffered(1))
    return pl.pallas_call(
        functools.partial(_prompt_kernel, n_t=n_t),
        grid=(bsz, n_t),
        in_specs=[
            pl.BlockSpec(memory_space=pltpu.SMEM),
            pl.BlockSpec((None, tl, D_MODEL), lambda b, t: (b, t, 0)),
            const((1, D_MODEL)), const((D_MODEL, W_ALL)), const((LANES, GLA_KW)), const((1, GLA_KW)),
            const(cmat.shape), const(lmask.shape), const(kmask.shape), const(qhot.shape), const(split.shape),
            const((1, GLA_W)), const((D_MODEL, D_MODEL)), const((1, D_MODEL)),
            smp_rows(W_MAIN), smp_rows(GLA_KW), const((SWA_HEADS, 1)),
            smp(GLA_KW, GLA_DV), smp(SWA_KVW, WINDOW), smp(SWA_KVW, WINDOW),
        ],
        out_specs=[
            pl.BlockSpec((None, tl, D_MODEL), lambda b, t: (b, t, 0)),
            pl.BlockSpec((None, GLA_KW, GLA_DV), lambda b, t: (b, 0, 0)),
            pl.BlockSpec((None, WINDOW, SWA_KVW), lambda b, t: (b, 0, 0)),
            pl.BlockSpec((None, WINDOW, SWA_KVW), lambda b, t: (b, 0, 0)),
            smp(GLA_KW, GLA_DV), smp(SWA_KVW, WINDOW), smp(SWA_KVW, WINDOW),
            smp(GLA_HEADS, GLA_DV), smp(SWA_HEADS, LANES),
        ],
        out_shape=[
            jax.ShapeDtypeStruct((bsz, seq, D_MODEL), f32),
            jax.ShapeDtypeStruct((bsz, GLA_KW, GLA_DV), f32),
            jax.ShapeDtypeStruct((bsz, WINDOW, SWA_KVW), f32),
            jax.ShapeDtypeStruct((bsz, WINDOW, SWA_KVW), f32),
            jax.ShapeDtypeStruct(smp_state.shape, f32), jax.ShapeDtypeStruct(smp_ck.shape, f32),
            jax.ShapeDtypeStruct(smp_cv.shape, f32),
            jax.ShapeDtypeStruct((n, GLA_HEADS, GLA_DV), f32), jax.ShapeDtypeStruct((n, SWA_HEADS, LANES), f32),
        ],
        scratch_shapes=[
            pltpu.VMEM((GLA_KW, GLA_DV), f32),
            pltpu.VMEM((WINDOW, SWA_KVW), f32), pltpu.VMEM((WINDOW, SWA_KVW), f32),
            pltpu.VMEM((tl, W_ALL), f32), pltpu.VMEM((tl, GLA_KW), f32),
            pltpu.VMEM((tl, GLA_W), f32), pltpu.VMEM((tl, SWA_W), f32),
        ],
        compiler_params=pltpu.CompilerParams(
            dimension_semantics=("arbitrary", "arbitrary"), vmem_limit_bytes=PROMPT_VMEM_LIMIT),
        name="prompt_layer",
    )(w["sinks"], x, w["norm_in"], w["w_all"], w["w_up"], w["b_gate"], cmat, lmask, kmask, qhot, split,
      w["gla_norm"], w["w_out"], w["norm_f"],
      smp_proj.reshape(bsz * n_t, g, W_MAIN), smp_decay.reshape(bsz * n_t, g, GLA_KW),
      w["sinks"].reshape(SWA_HEADS, 1), smp_state, smp_ck, smp_cv)


def _sample_proj_kernel(x_ref, nin_ref, w_ref, wup_ref, bg_ref, proj_ref, decay_ref):
    u = _rms(x_ref[...], nin_ref[...]).astype(bf16)
    proj_ref[...] = jnp.dot(u, w_ref[:, :W_MAIN], preferred_element_type=f32)
    glow = jnp.dot(u, w_ref[:, O_LOW:O_LOW + LANES], preferred_element_type=f32)
    decay_ref[...] = jnp.exp(_log_decay(glow, wup_ref, bg_ref))


def _sample_proj_call(xs, w):
    n = xs.shape[0]
    return pl.pallas_call(
        _sample_proj_kernel,
        out_shape=[jax.ShapeDtypeStruct((n, W_MAIN), f32), jax.ShapeDtypeStruct((n, GLA_KW), f32)],
        compiler_params=pltpu.CompilerParams(vmem_limit_bytes=VMEM_LIMIT),
        name="sample_proj",
    )(xs, w["norm_in"], w["w_all"], w["w_up"], w["b_gate"])


def _split3(x):
    as_bf16 = lambda v: v.astype(bf16).astype(f32)
    hi = as_bf16(x)
    mid = as_bf16(x - hi)
    return hi, mid, as_bf16(x - hi - mid)


def _sample_state_update(proj_ref, decay_ref, sink_ref, s_ref, ck_ref, cv_ref,
                         so_ref, cko_ref, cvo_ref, og_ref, os_ref):
    row = lax.broadcasted_iota(jnp.int32, (PACK, GLA_KW), 0)
    head_of_lane = lax.broadcasted_iota(jnp.int32, (PACK, GLA_KW), 1) // GLA_DK
    own_head = head_of_lane == row
    row_v = lax.broadcasted_iota(jnp.int32, (PACK, GLA_DV), 0)
    lane_v = lax.broadcasted_iota(jnp.int32, (PACK, GLA_DV), 1)
    piece_rows = ((row_v >= GLA_HEADS) & (row_v < GLA_HEADS + 3)).astype(f32)
    last_lane_rows = ((row_v < 3) & (lane_v == WINDOW - 1)).astype(bf16)
    newest = lax.broadcasted_iota(jnp.int32, (SWA_KVW, WINDOW), 1) == WINDOW - 1
    row_q = lax.broadcasted_iota(jnp.int32, (SWA_HEADS, LANES), 0)
    own_kv = (lax.broadcasted_iota(jnp.int32, (SWA_HEADS, LANES), 1) // SWA_HD) == (row_q // SWA_GROUP)
    sink = sink_ref[...]
    contract_rows = (((0,), (0,)), ((), ()))
    seqs = range(proj_ref.shape[0])
    lts, rts, qms, lt2s, q8s = [], [], [], [], []
    for j in seqs:
        pr = proj_ref[j:j + 1, :]
        bcast = lambda lo, width: jnp.broadcast_to(pr[:, lo:lo + width], (PACK, width))
        a_hi, a_mid, a_lo = _split3(jnp.broadcast_to(decay_ref[j:j + 1, :], (PACK, GLA_KW)))
        a_piece = jnp.where(row == GLA_HEADS, a_hi, jnp.where(row == GLA_HEADS + 1, a_mid, a_lo))
        lts.append(jnp.where(own_head, bcast(O_K, GLA_KW),
                             jnp.where((row >= GLA_HEADS) & (row < GLA_HEADS + 3), a_piece, 0.0)).astype(bf16))
        v_b = bcast(O_V, GLA_W)
        v_sel = jnp.zeros((PACK, GLA_DV), f32)
        for h in range(GLA_HEADS):
            v_sel = jnp.where(row_v == h, v_b[:, h * GLA_DV:(h + 1) * GLA_DV], v_sel)
        rts.append(jnp.concatenate([v_sel, piece_rows], axis=1).astype(bf16))
        qms.append(jnp.where(own_head, bcast(O_Q, GLA_KW), 0.0).astype(bf16))
        n_hi, n_mid, n_lo = _split3(bcast(O_SK, 2 * SWA_KVW))
        lt2s.append(jnp.where(row == 0, n_hi, jnp.where(row == 1, n_mid,
                                                        jnp.where(row == 2, n_lo, 0.0))).astype(bf16))
        sq_b = jnp.broadcast_to(pr[:, O_SQ:O_SQ + SWA_W], (SWA_HEADS, SWA_W))
        q8 = jnp.zeros((SWA_HEADS, LANES), f32)
        for gq in range(SWA_GROUP):
            q8 = jnp.where(row_q % SWA_GROUP == gq, sq_b[:, gq * LANES:(gq + 1) * LANES], q8)
        q8s.append(jnp.where(own_kv, q8, 0.0).astype(bf16))
    kv_as = [lax.dot_general(lts[j], rts[j], contract_rows, preferred_element_type=f32) for j in seqs]
    inss = [lax.dot_general(lt2s[j], last_lane_rows, contract_rows, preferred_element_type=f32) for j in seqs]
    s_news, kts, vts = [], [], []
    for j in seqs:
        s_new = kv_as[j][:, GLA_DV:] * s_ref[j] + kv_as[j][:, :GLA_DV]
        so_ref[j] = s_new
        s_news.append(s_new.astype(bf16))
        kt = jnp.where(newest, inss[j][:SWA_KVW], pltpu.roll(ck_ref[j], WINDOW - 1, axis=1))
        vt = jnp.where(newest, inss[j][SWA_KVW:], pltpu.roll(cv_ref[j], WINDOW - 1, axis=1))
        cko_ref[j] = kt
        cvo_ref[j] = vt
        kts.append(kt.astype(bf16))
        vts.append(vt.astype(bf16))
    for j in seqs:
        og_ref[j] = jnp.dot(qms[j], s_news[j], preferred_element_type=f32)[:GLA_HEADS]
    scores = [jnp.dot(q8s[j], kts[j], preferred_element_type=f32) for j in seqs]
    es, dens = [], []
    for j in seqs:
        m = jnp.maximum(jnp.max(scores[j], axis=-1, keepdims=True), sink)
        e = jnp.exp2(scores[j] - m)
        dens.append(jnp.sum(e, axis=-1, keepdims=True) + jnp.exp2(sink - m))
        es.append(e.astype(bf16))
    for j in seqs:
        o = lax.dot_general(es[j], vts[j], (((1,), (1,)), ((), ())), preferred_element_type=f32)
        os_ref[j] = o / dens[j]


def _sample_merge_kernel(x_ref, og_ref, gg_ref, os_ref, sg_ref, gn_ref, wout_ref, nf_ref, y_ref):
    y_ref[...] = _merge(x_ref[...], og_ref[...], gg_ref[...], os_ref[...], sg_ref[...],
                        gn_ref, wout_ref, nf_ref)


def _sample_merge_call(xs, og, gg, osw, sg, w):
    return pl.pallas_call(
        _sample_merge_kernel,
        out_shape=jax.ShapeDtypeStruct(xs.shape, f32),
        compiler_params=pltpu.CompilerParams(vmem_limit_bytes=VMEM_LIMIT),
        name="sample_merge",
    )(xs, og, gg, osw, sg, w["gla_norm"], w["w_out"], w["norm_f"])


def _sample_finish(xs, proj, og, os_raw, w):
    n = xs.shape[0]
    os5 = os_raw.reshape(n, SWA_KV, SWA_GROUP, SWA_KV, SWA_HD)
    os_il = jnp.stack([os5[:, kv, :, kv, :] for kv in range(SWA_KV)], axis=2).reshape(n, SWA_W)
    return _sample_merge_call(xs, og.reshape(n, GLA_W), proj[:, O_GG:O_GG + GLA_W], os_il,
                              proj[:, O_SG:O_SG + SWA_W], w)


def _cache_view(c):
    n = c.shape[1]
    return jnp.transpose(c[0], (0, 2, 3, 1)).reshape(n, SWA_KVW, WINDOW)


def _cache_unview(c):
    n = c.shape[0]
    return jnp.transpose(c.reshape(n, SWA_KV, SWA_HD, WINDOW), (0, 3, 1, 2))[None]


def kernel(x_prompt, x_sample, state_gla, cache_win_k, cache_win_v, norm_in, w_in, w_gate_up, b_gate,
           gla_norm, attn_sinks, w_out, norm_f):
    bsz = x_prompt.shape[0]
    n = x_sample.shape[0]
    w = _prep_weights(norm_in[0], w_in[0], w_gate_up[0], b_gate[0], gla_norm[0], attn_sinks[0],
                      w_out[0], norm_f)
    cmat, lmask = _chunk_tables()
    xs = x_sample.reshape(n, D_MODEL)
    proj, decay = _sample_proj_call(xs, w)
    y_p, s_p, k_p, v_p, s_s, k_s, v_s, og, os_raw = _prompt_call(
        x_prompt, w, cmat, lmask, *_swa_mask_tables(), proj, decay,
        state_gla[0].reshape(n, GLA_KW, GLA_DV), _cache_view(cache_win_k), _cache_view(cache_win_v))
    y_s = _sample_finish(xs, proj, og, os_raw, w)
    return (y_p, y_s.reshape(n, 1, D_MODEL),
            s_p.reshape(1, bsz, GLA_HEADS, GLA_DK, GLA_DV),
            _cache_unview(k_p), _cache_unview(v_p),
            s_s.reshape(1, n, GLA_HEADS, GLA_DK, GLA_DV),
            _cache_unview(k_s), _cache_unview(v_s))
```

```python
import functools

import numpy as np
import jax
import jax.numpy as jnp
from jax import lax
from jax.experimental import pallas as pl
from jax.experimental.pallas import tpu as pltpu

D_MODEL = 1024
GLA_HEADS = 4
GLA_DK = 64
GLA_DV = 128
GLA_KW = GLA_HEADS * GLA_DK
GLA_W = GLA_HEADS * GLA_DV
GLA_RANK = 16
GLA_TAU = 16.0
CHUNK = 64
SWA_HEADS = 8
SWA_HD = 64
SWA_KV = 2
SWA_GROUP = SWA_HEADS // SWA_KV
SWA_W = SWA_HEADS * SWA_HD
SWA_KVW = SWA_KV * SWA_HD
WINDOW = 128
EPS = 1e-6
NEG_INF = -1e30
LOG2E = 1.4426950408889634
LANES = 128

O_Q, O_K, O_V, O_GG = 0, 256, 512, 1024
O_SQ, O_SK, O_SV, O_SG, O_LOW = 1536, 2048, 2176, 2304, 2816
W_MAIN = 2816
W_ALL = W_MAIN + LANES

N_LEVELS = 6
MILD_EXP_BOUND = 80.0
TOK_BLOCK = 1024
GLA_GROUP = 4
PACK = 16
PROMPT_VMEM_LIMIT = 60 * 1024 * 1024
VMEM_LIMIT = 32 * 1024 * 1024

f32 = jnp.float32
bf16 = jnp.bfloat16


def _chunk_tables():
    c = CHUNK
    t = np.arange(c)[None, :]
    i = np.arange(c)[:, None]
    blocks = [(t <= i), (t > i)]
    masks = []
    for l in range(N_LEVELS):
        h = c >> (l + 1)
        m = (i // (2 * h)) * (2 * h) + h
        upper = i >= m
        blocks.append(np.where(upper, (t > m) & (t <= i), (t > i) & (t <= m)))
        jj = np.arange(c)[None, :]
        masks.append((i // (2 * h) == jj // (2 * h)) & (i % (2 * h) >= h) & (jj % (2 * h) < h))
    masks.append(np.eye(c, dtype=bool))
    masks.append(np.tril(np.ones((c, c), dtype=bool)))
    cm = np.concatenate(blocks, axis=0).astype(np.float32)
    cm3 = np.concatenate([cm, cm, cm], axis=1)
    lm = np.stack(masks).astype(np.float32)
    lm = np.concatenate([lm, lm], axis=1)
    return jnp.asarray(cm3, dtype=bf16), jnp.asarray(lm, dtype=f32)


def _swa_mask_tables():
    key = np.arange(2 * WINDOW)[:, None]
    qi = np.arange(WINDOW)[None, :]
    prev_ok = (key < WINDOW) & (key > qi)
    cur_ok = (key >= WINDOW) & (key - WINDOW <= qi)
    full = np.where(prev_ok | cur_ok, 0.0, NEG_INF)
    first = np.where(cur_ok, 0.0, NEG_INF)
    onehot = (np.arange(2 * WINDOW)[:, None] % WINDOW == qi).astype(np.float32)
    row_q = np.arange(2 * WINDOW)[:, None] % WINDOW
    slot_is_prev = (np.arange(WINDOW)[None, :] > row_q).astype(np.float32)
    split = np.stack([slot_is_prev, 1.0 - slot_is_prev])
    return (jnp.asarray(np.stack([full, first]), dtype=bf16), jnp.asarray(onehot, dtype=bf16),
            jnp.asarray(split, dtype=bf16))


I_Q, I_K, I_V, I_GG, I_LOW, I_SQ, I_SK, I_SV, I_SG = (
    int(v) for v in np.cumsum([0, 256, 256, 512, 512, 16, 512, 128, 128])[:9])


def _interleaved_rows(base, tile):
    return [(base + kv * SWA_GROUP * SWA_HD + tile * SWA_HD, SWA_HD) for kv in range(SWA_KV)]


def _weight_layout_kernel(wt_ref, wo_ref, wall_ref, wop_ref):
    def put(col, pieces, scale=None):
        rows = [wt_ref[r:r + n, :] for r, n in pieces]
        missing = LANES - sum(n for _, n in pieces)
        if missing:
            rows.append(jnp.zeros((missing, D_MODEL), f32))
        blk = jnp.concatenate(rows, axis=0) if len(rows) > 1 else rows[0]
        if scale is not None:
            blk = blk * scale
        wall_ref[:, col:col + LANES] = blk.T.astype(bf16)

    for i in range(GLA_KW // LANES):
        put(O_Q + i * LANES, [(I_Q + i * LANES, LANES)], GLA_DK ** -0.5)
        put(O_K + i * LANES, [(I_K + i * LANES, LANES)])
    for i in range(GLA_W // LANES):
        put(O_V + i * LANES, [(I_V + i * LANES, LANES)])
        put(O_GG + i * LANES, [(I_GG + i * LANES, LANES)])
        put(O_SQ + i * LANES, _interleaved_rows(I_SQ, i), SWA_HD ** -0.5 * LOG2E)
        put(O_SG + i * LANES, _interleaved_rows(I_SG, i))
    put(O_SK, [(I_SK, SWA_KVW)])
    put(O_SV, [(I_SV, SWA_KVW)])
    put(O_LOW, [(I_LOW, GLA_RANK)])
    wop_ref[:GLA_W, :] = wo_ref[:GLA_W, :].astype(bf16)
    for i in range(SWA_GROUP):
        for kv, (r, n) in enumerate(_interleaved_rows(GLA_W, i)):
            dst = GLA_W + (i * SWA_KV + kv) * SWA_HD
            wop_ref[dst:dst + n, :] = wo_ref[r:r + n, :].astype(bf16)


def _prep_weights(norm_in, w_in, w_gate_up, b_gate, gla_norm, attn_sinks, w_out, norm_f):
    w_all, w_out_p = pl.pallas_call(
        _weight_layout_kernel,
        out_shape=[jax.ShapeDtypeStruct((D_MODEL, W_ALL), bf16), jax.ShapeDtypeStruct((D_MODEL, D_MODEL), bf16)],
        compiler_params=pltpu.CompilerParams(vmem_limit_bytes=VMEM_LIMIT),
        name="weight_layout",
    )(w_in.T, w_out)
    w_up = jnp.pad(w_gate_up, ((0, LANES - GLA_RANK), (0, 0))).astype(bf16)
    return dict(
        norm_in=norm_in.reshape(1, D_MODEL), w_all=w_all, w_up=w_up,
        b_gate=b_gate.reshape(1, GLA_KW), gla_norm=jnp.tile(gla_norm, GLA_HEADS).reshape(1, GLA_W),
        sinks=attn_sinks * LOG2E, w_out=w_out_p, norm_f=norm_f.reshape(1, D_MODEL))


def _rms(x, gain):
    return x * lax.rsqrt(jnp.mean(x * x, axis=-1, keepdims=True) + EPS) * gain


def _log_decay(glow, wup_ref, bg_ref):
    z = jnp.dot(glow.astype(bf16), wup_ref[...], preferred_element_type=f32) + bg_ref[...]
    return (jnp.minimum(z, 0.0) - jnp.log(1.0 + jnp.exp(-jnp.abs(z)))) * (1.0 / GLA_TAU)


def _silu(x):
    return x * jax.nn.sigmoid(x)


def _merge(x, o_gla, gg, o_swa, sg, gn_ref, wout_ref, nf_ref):
    parts = []
    for h in range(GLA_HEADS):
        sl = slice(h * GLA_DV, (h + 1) * GLA_DV)
        parts.append(_rms(o_gla[:, sl], gn_ref[:, sl]) * _silu(gg[:, sl]))
    parts.append(o_swa * _silu(sg))
    um = jnp.concatenate(parts, axis=1).astype(bf16)
    hres = x + jnp.dot(um, wout_ref[...], preferred_element_type=f32)
    return _rms(hres, nf_ref[...])


def _prompt_kernel(sinks_ref, x_ref, nin_ref, w_ref, wup_ref, bg_ref, cmat_ref, lmask_ref, kmask_ref, qhot_ref,
                   split_ref,
                   gn_ref, wout_ref, nf_ref,
                   smp_proj_ref, smp_decay_ref, smp_sink_ref, smp_s_ref, smp_ck_ref, smp_cv_ref,
                   y_ref, sp_ref, kn_ref, vn_ref,
                   smp_so_ref, smp_cko_ref, smp_cvo_ref, smp_og_ref, smp_os_ref,
                   s_ref, kprev_ref, vprev_ref, p_s, g_s, ogla_s, oswa_s, *, n_t):
    t = pl.program_id(1)
    tl = x_ref.shape[0]

    @pl.when(t == 0)
    def _():
        s_ref[...] = jnp.zeros_like(s_ref)
        kprev_ref[...] = jnp.zeros_like(kprev_ref)
        vprev_ref[...] = jnp.zeros_like(vprev_ref)

    def project(rg):
        u = _rms(x_ref[rg, :], nin_ref[...]).astype(bf16)
        p_s[rg, :] = jnp.dot(u, w_ref[...], preferred_element_type=f32)
        g_s[rg, :] = _log_decay(p_s[rg, O_LOW:O_LOW + LANES], wup_ref, bg_ref)

    lane_lo = lax.broadcasted_iota(jnp.int32, (CHUNK, LANES), 1) < GLA_DK

    n_pairs = GLA_HEADS // 2

    def group_terms(chunks):
        rows = {c: slice(c * CHUNK, (c + 1) * CHUNK) for c in chunks}
        sums = {}
        for c in chunks:
            gc = g_s[rows[c], :]
            g_hi = gc.astype(bf16)
            r1 = gc - g_hi.astype(f32)
            g_mid = r1.astype(bf16)
            g_lo = (r1 - g_mid.astype(f32)).astype(bf16)
            sums[c] = jnp.dot(cmat_ref[...], jnp.concatenate([g_hi, g_mid, g_lo], axis=0),
                              preferred_element_type=f32)
        level_ops, misc = {}, {}
        for c in chunks:
            qc, kc = p_s[rows[c], O_Q:O_Q + GLA_KW], p_s[rows[c], O_K:O_K + GLA_KW]
            e_b = jnp.exp(sums[c][0:CHUNK])
            misc[c] = (e_b, qc * e_b, kc * jnp.exp(sums[c][CHUNK:2 * CHUNK]),
                       p_s[rows[c], O_V:O_V + GLA_W].astype(bf16))
            for p in range(n_pairs):
                ln = slice(p * LANES, (p + 1) * LANES)
                qp, kp = qc[:, ln], kc[:, ln]
                for l in range(N_LEVELS + 1):
                    if l < N_LEVELS:
                        e = jnp.exp(sums[c][(2 + l) * CHUNK:(3 + l) * CHUNK, ln])
                        qe, ke = (qp * e).astype(bf16), (kp * e).astype(bf16)
                    else:
                        qe, ke = qp.astype(bf16), kp.astype(bf16)
                    lhs = jnp.concatenate([jnp.where(lane_lo, qe, 0), jnp.where(lane_lo, 0, qe)], axis=0)
                    level_ops[c, p, l] = (lhs, ke)
        scores = {key: lax.dot_general(lhs, ke, (((1,), (1,)), ((), ())), preferred_element_type=f32)
                  for key, (lhs, ke) in level_ops.items()}
        terms = {}
        for c in chunks:
            e_b, qb, k_suf, vb = misc[c]
            terms[c] = []
            for p in range(n_pairs):
                ln = slice(p * LANES, (p + 1) * LANES)
                a = scores[c, p, 0] * lmask_ref[0]
                for l in range(1, N_LEVELS + 1):
                    a = a + scores[c, p, l] * lmask_ref[l]
                ab = a.astype(bf16)
                qbp = qb[:, ln].astype(bf16)
                lhs_heads = []
                for hh in range(2):
                    qbm = jnp.where(lane_lo, qbp, 0) if hh == 0 else jnp.where(lane_lo, 0, qbp)
                    lhs_heads.append(jnp.concatenate([qbm, ab[hh * CHUNK:(hh + 1) * CHUNK, :]], axis=1))
                upd = lax.dot_general(k_suf[:, ln].astype(bf16), vb[:, p * 2 * GLA_DV:(p + 1) * 2 * GLA_DV],
                                      (((0,), (0,)), ((), ())), preferred_element_type=f32)
                upd = jnp.concatenate(
                    [upd[0:GLA_DK, 0:GLA_DV], upd[GLA_DK:2 * GLA_DK, GLA_DV:2 * GLA_DV]], axis=0)
                e_col = jnp.broadcast_to(e_b[CHUNK - 1:CHUNK, ln], (LANES, LANES)).T
                terms[c].append((lhs_heads, vb, upd, e_col))
        return terms

    def group_terms_mild(chunks):
        rows = {c: slice(c * CHUNK, (c + 1) * CHUNK) for c in chunks}
        prefix = {}
        for c in chunks:
            gc = g_s[rows[c], :]
            g_hi = gc.astype(bf16)
            r1 = gc - g_hi.astype(f32)
            g_mid = r1.astype(bf16)
            g_lo = (r1 - g_mid.astype(f32)).astype(bf16)
            prefix[c] = jnp.dot(cmat_ref[0:CHUNK, :], jnp.concatenate([g_hi, g_mid, g_lo], axis=0),
                                preferred_element_type=f32)
        score_ops, misc = {}, {}
        for c in chunks:
            qc, kc = p_s[rows[c], O_Q:O_Q + GLA_KW], p_s[rows[c], O_K:O_K + GLA_KW]
            b = prefix[c]
            e_b = jnp.exp(b)
            qb, k_neg = qc * e_b, kc * jnp.exp(-b)
            misc[c] = (e_b, kc * jnp.exp(b[CHUNK - 1:CHUNK, :] - b), p_s[rows[c], O_V:O_V + GLA_W].astype(bf16))
            for p in range(n_pairs):
                ln = slice(p * LANES, (p + 1) * LANES)
                qbp = qb[:, ln].astype(bf16)
                qbms = [jnp.where(lane_lo, qbp, 0), jnp.where(lane_lo, 0, qbp)]
                score_ops[c, p] = (qbms, k_neg[:, ln].astype(bf16))
        scores = {key: lax.dot_general(jnp.concatenate(qbms, axis=0), ke, (((1,), (1,)), ((), ())),
                                       preferred_element_type=f32)
                  for key, (qbms, ke) in score_ops.items()}
        terms = {}
        for c in chunks:
            e_b, k_suf, vb = misc[c]
            terms[c] = []
            for p in range(n_pairs):
                ln = slice(p * LANES, (p + 1) * LANES)
                ab = (scores[c, p] * lmask_ref[N_LEVELS + 1]).astype(bf16)
                qbms = score_ops[c, p][0]
                lhs_heads = [jnp.concatenate([qbms[hh], ab[hh * CHUNK:(hh + 1) * CHUNK, :]], axis=1)
                             for hh in range(2)]
                upd = lax.dot_general(k_suf[:, ln].astype(bf16), vb[:, p * 2 * GLA_DV:(p + 1) * 2 * GLA_DV],
                                      (((0,), (0,)), ((), ())), preferred_element_type=f32)
                upd = jnp.concatenate(
                    [upd[0:GLA_DK, 0:GLA_DV], upd[GLA_DK:2 * GLA_DK, GLA_DV:2 * GLA_DV]], axis=0)
                e_col = jnp.broadcast_to(e_b[CHUNK - 1:CHUNK, ln], (LANES, LANES)).T
                terms[c].append((lhs_heads, vb, upd, e_col))
        return terms

    def gla(chunks, group_fn):
        terms = {}
        for c0 in range(chunks[0], chunks[-1] + 1, GLA_GROUP):
            terms.update(group_fn(range(c0, c0 + GLA_GROUP)))
        states = {}
        for p in range(n_pairs):
            ln = slice(p * LANES, (p + 1) * LANES)
            s_pair = s_ref[ln, :]
            for c in chunks:
                states[c, p] = s_pair.astype(bf16)
                _, _, upd, e_col = terms[c][p]
                s_pair = e_col * s_pair + upd
            s_ref[ln, :] = s_pair
        for c in chunks:
            for p in range(n_pairs):
                lhs_heads, vb, _, _ = terms[c][p]
                for hh in range(2):
                    h = 2 * p + hh
                    rhs = jnp.concatenate([states[c, p], vb[:, h * GLA_DV:(h + 1) * GLA_DV]], axis=0)
                    ogla_s[c * CHUNK:(c + 1) * CHUNK, h * GLA_DV:(h + 1) * GLA_DV] = jnp.dot(
                        lhs_heads[hh], rhs, preferred_element_type=f32)

    lane_lo_w = lax.broadcasted_iota(jnp.int32, (WINDOW, LANES), 1) < SWA_HD
    row_lo = lax.broadcasted_iota(jnp.int32, (2 * WINDOW, 1), 0) < WINDOW

    def swa(blocks):
        for blk in blocks:
            rs = slice(blk * WINDOW, (blk + 1) * WINDOW)
            sq = p_s[rs, O_SQ:O_SQ + SWA_W].astype(bf16)
            k_cur, v_cur = p_s[rs, O_SK:O_SK + SWA_KVW], p_s[rs, O_SV:O_SV + SWA_KVW]
            k2 = jnp.concatenate([kprev_ref[...], k_cur], axis=0).astype(bf16)
            v2 = jnp.concatenate([vprev_ref[...], v_cur], axis=0).astype(bf16)
            kmask = kmask_ref[jnp.where(t > 0, 0, 1)] if blk == 0 else kmask_ref[0]
            k2m = jnp.concatenate([k2, kmask], axis=1)
            for tt in range(SWA_GROUP):
                qt = sq[:, tt * LANES:(tt + 1) * LANES]
                lhs = jnp.concatenate([jnp.where(lane_lo_w, qt, 0), jnp.where(lane_lo_w, 0, qt)], axis=0)
                lhs = jnp.concatenate([lhs, qhot_ref[...]], axis=1)
                s = lax.dot_general(lhs, k2m, (((1,), (1,)), ((), ())), preferred_element_type=f32)
                s = jnp.maximum(s[:, :WINDOW], s[:, WINDOW:])
                sink = jnp.where(row_lo, sinks_ref[tt], sinks_ref[SWA_GROUP + tt])
                m = jnp.maximum(jnp.max(s, axis=-1, keepdims=True), sink)
                e = jnp.exp2(s - m)
                r = 1.0 / (jnp.sum(e, axis=-1, keepdims=True) + jnp.exp2(sink - m))
                eb = e.astype(bf16)
                e2 = jnp.concatenate([eb * split_ref[0], eb * split_ref[1]], axis=1)
                o2 = jnp.dot(e2, v2, preferred_element_type=f32)
                oswa_s[rs, tt * LANES:(tt + 1) * LANES] = jnp.where(
                    lane_lo_w, o2[:WINDOW] * r[:WINDOW], o2[WINDOW:] * r[WINDOW:])
            kprev_ref[...] = k_cur
            vprev_ref[...] = v_cur

    def merge_gla(rg):
        parts = [_rms(ogla_s[rg, h * GLA_DV:(h + 1) * GLA_DV], gn_ref[:, h * GLA_DV:(h + 1) * GLA_DV])
                 * _silu(p_s[rg, O_GG + h * GLA_DV:O_GG + (h + 1) * GLA_DV]) for h in range(GLA_HEADS)]
        um = jnp.concatenate(parts, axis=1).astype(bf16)
        y_ref[rg, :] = x_ref[rg, :] + jnp.dot(um, wout_ref[:GLA_W, :], preferred_element_type=f32)

    def merge_swa(rg):
        um = (oswa_s[rg, :] * _silu(p_s[rg, O_SG:O_SG + SWA_W])).astype(bf16)
        hres = y_ref[rg, :] + jnp.dot(um, wout_ref[GLA_W:, :], preferred_element_type=f32)
        y_ref[rg, :] = _rms(hres, nf_ref[...])

    whole = slice(0, tl)
    project(whole)
    _sample_state_update(smp_proj_ref, smp_decay_ref, smp_sink_ref, smp_s_ref, smp_ck_ref, smp_cv_ref,
                         smp_so_ref, smp_cko_ref, smp_cvo_ref, smp_og_ref, smp_os_ref)
    mild = jnp.min(g_s[...]) * CHUNK > -MILD_EXP_BOUND
    pl.when(mild)(lambda: gla(range(tl // CHUNK), group_terms_mild))
    pl.when(jnp.logical_not(mild))(lambda: gla(range(tl // CHUNK), group_terms))
    merge_gla(whole)
    swa(range(tl // WINDOW))
    merge_swa(whole)

    @pl.when(t == n_t - 1)
    def _():
        sp_ref[...] = s_ref[...]
        kn_ref[...] = kprev_ref[...].T
        vn_ref[...] = vprev_ref[...].T


def _prompt_call(x, w, cmat, lmask, kmask, qhot, split, smp_proj, smp_decay, smp_state, smp_ck, smp_cv):
    bsz, seq, _ = x.shape
    tl = TOK_BLOCK
    n_t = seq // tl
    n = smp_state.shape[0]
    g = n // (bsz * n_t)
    assert g * bsz * n_t == n
    step = lambda b, t: b * n_t + t
    smp = lambda *tail: pl.BlockSpec((g,) + tail, lambda b, t: (step(b, t),) + (0,) * len(tail))
    smp_rows = lambda width: pl.BlockSpec((None, g, width), lambda b, t: (step(b, t), 0, 0))
    const = lambda shape: pl.BlockSpec(shape, lambda b, t: (0,) * len(shape), pipeline_mode=pl.Buffered(1))
    return pl.pallas_call(
        functools.partial(_prompt_kernel, n_t=n_t),
        grid=(bsz, n_t),
        in_specs=[
            pl.BlockSpec(memory_space=pltpu.SMEM),
            pl.BlockSpec((None, tl, D_MODEL), lambda b, t: (b, t, 0)),
            const((1, D_MODEL)), const((D_MODEL, W_ALL)), const((LANES, GLA_KW)), const((1, GLA_KW)),
            const(cmat.shape), const(lmask.shape), const(kmask.shape), const(qhot.shape), const(split.shape),
            const((1, GLA_W)), const((D_MODEL, D_MODEL)), const((1, D_MODEL)),
            smp_rows(W_MAIN), smp_rows(GLA_KW), const((SWA_HEADS, 1)),
            smp(GLA_KW, GLA_DV), smp(SWA_KVW, WINDOW), smp(SWA_KVW, WINDOW),
        ],
        out_specs=[
            pl.BlockSpec((None, tl, D_MODEL), lambda b, t: (b, t, 0)),
            pl.BlockSpec((None, GLA_KW, GLA_DV), lambda b, t: (b, 0, 0)),
            pl.BlockSpec((None, WINDOW, SWA_KVW), lambda b, t: (b, 0, 0)),
            pl.BlockSpec((None, WINDOW, SWA_KVW), lambda b, t: (b, 0, 0)),
            smp(GLA_KW, GLA_DV), smp(SWA_KVW, WINDOW), smp(SWA_KVW, WINDOW),
            smp(GLA_HEADS, GLA_DV), smp(SWA_HEADS, LANES),
        ],
        out_shape=[
            jax.ShapeDtypeStruct((bsz, seq, D_MODEL), f32),
            jax.ShapeDtypeStruct((bsz, GLA_KW, GLA_DV), f32),
            jax.ShapeDtypeStruct((bsz, WINDOW, SWA_KVW), f32),
            jax.ShapeDtypeStruct((bsz, WINDOW, SWA_KVW), f32),
            jax.ShapeDtypeStruct(smp_state.shape, f32), jax.ShapeDtypeStruct(smp_ck.shape, f32),
            jax.ShapeDtypeStruct(smp_cv.shape, f32),
            jax.ShapeDtypeStruct((n, GLA_HEADS, GLA_DV), f32), jax.ShapeDtypeStruct((n, SWA_HEADS, LANES), f32),
        ],
        scratch_shapes=[
            pltpu.VMEM((GLA_KW, GLA_DV), f32),
            pltpu.VMEM((WINDOW, SWA_KVW), f32), pltpu.VMEM((WINDOW, SWA_KVW), f32),
            pltpu.VMEM((tl, W_ALL), f32), pltpu.VMEM((tl, GLA_KW), f32),
            pltpu.VMEM((tl, GLA_W), f32), pltpu.VMEM((tl, SWA_W), f32),
        ],
        compiler_params=pltpu.CompilerParams(
            dimension_semantics=("arbitrary", "arbitrary"), vmem_limit_bytes=PROMPT_VMEM_LIMIT),
        name="prompt_layer",
    )(w["sinks"], x, w["norm_in"], w["w_all"], w["w_up"], w["b_gate"], cmat, lmask, kmask, qhot, split,
      w["gla_norm"], w["w_out"], w["norm_f"],
      smp_proj.reshape(bsz * n_t, g, W_MAIN), smp_decay.reshape(bsz * n_t, g, GLA_KW),
      w["sinks"].reshape(SWA_HEADS, 1), smp_state, smp_ck, smp_cv)


def _sample_proj_kernel(x_ref, nin_ref, w_ref, wup_ref, bg_ref, proj_ref, decay_ref):
    u = _rms(x_ref[...], nin_ref[...]).astype(bf16)
    proj_ref[...] = jnp.dot(u, w_ref[:, :W_MAIN], preferred_element_type=f32)
    glow = jnp.dot(u, w_ref[:, O_LOW:O_LOW + LANES], preferred_element_type=f32)
    decay_ref[...] = jnp.exp(_log_decay(glow, wup_ref, bg_ref))


def _sample_proj_call(xs, w):
    n = xs.shape[0]
    return pl.pallas_call(
        _sample_proj_kernel,
        out_shape=[jax.ShapeDtypeStruct((n, W_MAIN), f32), jax.ShapeDtypeStruct((n, GLA_KW), f32)],
        compiler_params=pltpu.CompilerParams(vmem_limit_bytes=VMEM_LIMIT),
        name="sample_proj",
    )(xs, w["norm_in"], w["w_all"], w["w_up"], w["b_gate"])


def _split3(x):
    as_bf16 = lambda v: v.astype(bf16).astype(f32)
    hi = as_bf16(x)
    mid = as_bf16(x - hi)
    return hi, mid, as_bf16(x - hi - mid)


def _sample_state_update(proj_ref, decay_ref, sink_ref, s_ref, ck_ref, cv_ref,
                         so_ref, cko_ref, cvo_ref, og_ref, os_ref):
    row = lax.broadcasted_iota(jnp.int32, (PACK, GLA_KW), 0)
    head_of_lane = lax.broadcasted_iota(jnp.int32, (PACK, GLA_KW), 1) // GLA_DK
    own_head = head_of_lane == row
    row_v = lax.broadcasted_iota(jnp.int32, (PACK, GLA_DV), 0)
    lane_v = lax.broadcasted_iota(jnp.int32, (PACK, GLA_DV), 1)
    piece_rows = ((row_v >= GLA_HEADS) & (row_v < GLA_HEADS + 3)).astype(f32)
    last_lane_rows = ((row_v < 3) & (lane_v == WINDOW - 1)).astype(bf16)
    newest = lax.broadcasted_iota(jnp.int32, (SWA_KVW, WINDOW), 1) == WINDOW - 1
    row_q = lax.broadcasted_iota(jnp.int32, (SWA_HEADS, LANES), 0)
    own_kv = (lax.broadcasted_iota(jnp.int32, (SWA_HEADS, LANES), 1) // SWA_HD) == (row_q // SWA_GROUP)
    sink = sink_ref[...]
    contract_rows = (((0,), (0,)), ((), ()))
    seqs = range(proj_ref.shape[0])
    lts, rts, qms, lt2s, q8s = [], [], [], [], []
    for j in seqs:
        pr = proj_ref[j:j + 1, :]
        bcast = lambda lo, width: jnp.broadcast_to(pr[:, lo:lo + width], (PACK, width))
        a_hi, a_mid, a_lo = _split3(jnp.broadcast_to(decay_ref[j:j + 1, :], (PACK, GLA_KW)))
        a_piece = jnp.where(row == GLA_HEADS, a_hi, jnp.where(row == GLA_HEADS + 1, a_mid, a_lo))
        lts.append(jnp.where(own_head, bcast(O_K, GLA_KW),
                             jnp.where((row >= GLA_HEADS) & (row < GLA_HEADS + 3), a_piece, 0.0)).astype(bf16))
        v_b = bcast(O_V, GLA_W)
        v_sel = jnp.zeros((PACK, GLA_DV), f32)
        for h in range(GLA_HEADS):
            v_sel = jnp.where(row_v == h, v_b[:, h * GLA_DV:(h + 1) * GLA_DV], v_sel)
        rts.append(jnp.concatenate([v_sel, piece_rows], axis=1).astype(bf16))
        qms.append(jnp.where(own_head, bcast(O_Q, GLA_KW), 0.0).astype(bf16))
        n_hi, n_mid, n_lo = _split3(bcast(O_SK, 2 * SWA_KVW))
        lt2s.append(jnp.where(row == 0, n_hi, jnp.where(row == 1, n_mid,
                                                        jnp.where(row == 2, n_lo, 0.0))).astype(bf16))
        sq_b = jnp.broadcast_to(pr[:, O_SQ:O_SQ + SWA_W], (SWA_HEADS, SWA_W))
        q8 = jnp.zeros((SWA_HEADS, LANES), f32)
        for gq in range(SWA_GROUP):
            q8 = jnp.where(row_q % SWA_GROUP == gq, sq_b[:, gq * LANES:(gq + 1) * LANES], q8)
        q8s.append(jnp.where(own_kv, q8, 0.0).astype(bf16))
    kv_as = [lax.dot_general(lts[j], rts[j], contract_rows, preferred_element_type=f32) for j in seqs]
    inss = [lax.dot_general(lt2s[j], last_lane_rows, contract_rows, preferred_element_type=f32) for j in seqs]
    s_news, kts, vts = [], [], []
    for j in seqs:
        s_new = kv_as[j][:, GLA_DV:] * s_ref[j] + kv_as[j][:, :GLA_DV]
        so_ref[j] = s_new
        s_news.append(s_new.astype(bf16))
        kt = jnp.where(newest, inss[j][:SWA_KVW], pltpu.roll(ck_ref[j], WINDOW - 1, axis=1))
        vt = jnp.where(newest, inss[j][SWA_KVW:], pltpu.roll(cv_ref[j], WINDOW - 1, axis=1))
        cko_ref[j] = kt
        cvo_ref[j] = vt
        kts.append(kt.astype(bf16))
        vts.append(vt.astype(bf16))
    for j in seqs:
        og_ref[j] = jnp.dot(qms[j], s_news[j], preferred_element_type=f32)[:GLA_HEADS]
    scores = [jnp.dot(q8s[j], kts[j], preferred_element_type=f32) for j in seqs]
    es, dens = [], []
    for j in seqs:
        m = jnp.maximum(jnp.max(scores[j], axis=-1, keepdims=True), sink)
        e = jnp.exp2(scores[j] - m)
        dens.append(jnp.sum(e, axis=-1, keepdims=True) + jnp.exp2(sink - m))
        es.append(e.astype(bf16))
    for j in seqs:
        o = lax.dot_general(es[j], vts[j], (((1,), (1,)), ((), ())), preferred_element_type=f32)
        os_ref[j] = o / dens[j]


def _sample_merge_kernel(x_ref, og_ref, gg_ref, os_ref, sg_ref, gn_ref, wout_ref, nf_ref, y_ref):
    y_ref[...] = _merge(x_ref[...], og_ref[...], gg_ref[...], os_ref[...], sg_ref[...],
                        gn_ref, wout_ref, nf_ref)


def _sample_merge_call(xs, og, gg, osw, sg, w):
    return pl.pallas_call(
        _sample_merge_kernel,
        out_shape=jax.ShapeDtypeStruct(xs.shape, f32),
        compiler_params=pltpu.CompilerParams(vmem_limit_bytes=VMEM_LIMIT),
        name="sample_merge",
    )(xs, og, gg, osw, sg, w["gla_norm"], w["w_out"], w["norm_f"])


def _sample_finish(xs, proj, og, os_raw, w):
    n = xs.shape[0]
    os5 = os_raw.reshape(n, SWA_KV, SWA_GROUP, SWA_KV, SWA_HD)
    os_il = jnp.stack([os5[:, kv, :, kv, :] for kv in range(SWA_KV)], axis=2).reshape(n, SWA_W)
    return _sample_merge_call(xs, og.reshape(n, GLA_W), proj[:, O_GG:O_GG + GLA_W], os_il,
                              proj[:, O_SG:O_SG + SWA_W], w)


def _cache_view(c):
    n = c.shape[1]
    return jnp.transpose(c[0], (0, 2, 3, 1)).reshape(n, SWA_KVW, WINDOW)


def _cache_unview(c):
    n = c.shape[0]
    return jnp.transpose(c.reshape(n, SWA_KV, SWA_HD, WINDOW), (0, 3, 1, 2))[None]


def kernel(x_prompt, x_sample, state_gla, cache_win_k, cache_win_v, norm_in, w_in, w_gate_up, b_gate,
           gla_norm, attn_sinks, w_out, norm_f):
    bsz = x_prompt.shape[0]
    n = x_sample.shape[0]
    w = _prep_weights(norm_in[0], w_in[0], w_gate_up[0], b_gate[0], gla_norm[0], attn_sinks[0],
                      w_out[0], norm_f)
    cmat, lmask = _chunk_tables()
    xs = x_sample.reshape(n, D_MODEL)
    proj, decay = _sample_proj_call(xs, w)
    y_p, s_p, k_p, v_p, s_s, k_s, v_s, og, os_raw = _prompt_call(
        x_prompt, w, cmat, lmask, *_swa_mask_tables(), proj, decay,
        state_gla[0].reshape(n, GLA_KW, GLA_DV), _cache_view(cache_win_k), _cache_view(cache_win_v))
    y_s = _sample_finish(xs, proj, og, os_raw, w)
    return (y_p, y_s.reshape(n, 1, D_MODEL),
            s_p.reshape(1, bsz, GLA_HEADS, GLA_DK, GLA_DV),
            _cache_unview(k_p), _cache_unview(v_p),
            s_s.reshape(1, n, GLA_HEADS, GLA_DK, GLA_DV),
            _cache_unview(k_s), _cache_unview(v_s))
```

```python
import functools

import numpy as np
import jax
import jax.numpy as jnp
from jax import lax
from jax.experimental import pallas as pl
from jax.experimental.pallas import tpu as pltpu

D_MODEL = 1024
GLA_HEADS = 4
GLA_DK = 64
GLA_DV = 128
GLA_KW = GLA_HEADS * GLA_DK
GLA_W = GLA_HEADS * GLA_DV
GLA_RANK = 16
GLA_TAU = 16.0
CHUNK = 64
SWA_HEADS = 8
SWA_HD = 64
SWA_KV = 2
SWA_GROUP = SWA_HEADS // SWA_KV
SWA_W = SWA_HEADS * SWA_HD
SWA_KVW = SWA_KV * SWA_HD
WINDOW = 128
EPS = 1e-6
NEG_INF = -1e30
LOG2E = 1.4426950408889634
LANES = 128

O_Q, O_K, O_V, O_GG = 0, 256, 512, 1024
O_SQ, O_SK, O_SV, O_SG, O_LOW = 1536, 2048, 2176, 2304, 2816
W_MAIN = 2816
W_ALL = W_MAIN + LANES

N_LEVELS = 6
MILD_EXP_BOUND = 80.0
TOK_BLOCK = 1024
GLA_GROUP = 4
CHUNK_MILD = 128
GLA_GROUP_MILD = 4
PACK = 16
PROMPT_VMEM_LIMIT = 60 * 1024 * 1024
VMEM_LIMIT = 32 * 1024 * 1024

f32 = jnp.float32
bf16 = jnp.bfloat16


def _chunk_tables():
    c = CHUNK
    t = np.arange(c)[None, :]
    i = np.arange(c)[:, None]
    blocks = [(t <= i), (t > i)]
    masks = []
    for l in range(N_LEVELS):
        h = c >> (l + 1)
        m = (i // (2 * h)) * (2 * h) + h
        upper = i >= m
        blocks.append(np.where(upper, (t > m) & (t <= i), (t > i) & (t <= m)))
        jj = np.arange(c)[None, :]
        masks.append((i // (2 * h) == jj // (2 * h)) & (i % (2 * h) >= h) & (jj % (2 * h) < h))
    masks.append(np.eye(c, dtype=bool))
    cm = np.concatenate(blocks, axis=0).astype(np.float32)
    cm3 = np.concatenate([cm, cm, cm], axis=1)
    lm = np.stack(masks).astype(np.float32)
    lm = np.concatenate([lm, lm], axis=1)
    tril = np.tril(np.ones((CHUNK_MILD, CHUNK_MILD), dtype=np.float32))
    tril = np.concatenate([tril, tril], axis=0)
    return jnp.asarray(cm3, dtype=bf16), jnp.asarray(lm, dtype=f32), jnp.asarray(tril, dtype=f32)


def _swa_mask_tables():
    key = np.arange(2 * WINDOW)[:, None]
    qi = np.arange(WINDOW)[None, :]
    prev_ok = (key < WINDOW) & (key > qi)
    cur_ok = (key >= WINDOW) & (key - WINDOW <= qi)
    full = np.where(prev_ok | cur_ok, 0.0, NEG_INF)
    first = np.where(cur_ok, 0.0, NEG_INF)
    onehot = (np.arange(2 * WINDOW)[:, None] % WINDOW == qi).astype(np.float32)
    row_q = np.arange(2 * WINDOW)[:, None] % WINDOW
    slot_is_prev = (np.arange(WINDOW)[None, :] > row_q).astype(np.float32)
    split = np.stack([slot_is_prev, 1.0 - slot_is_prev])
    return (jnp.asarray(np.stack([full, first]), dtype=bf16), jnp.asarray(onehot, dtype=bf16),
            jnp.asarray(split, dtype=bf16))


I_Q, I_K, I_V, I_GG, I_LOW, I_SQ, I_SK, I_SV, I_SG = (
    int(v) for v in np.cumsum([0, 256, 256, 512, 512, 16, 512, 128, 128])[:9])


def _interleaved_rows(base, tile):
    return [(base + kv * SWA_GROUP * SWA_HD + tile * SWA_HD, SWA_HD) for kv in range(SWA_KV)]


def _weight_layout_kernel(wt_ref, wo_ref, wall_ref, wop_ref):
    def put(col, pieces, scale=None):
        rows = [wt_ref[r:r + n, :] for r, n in pieces]
        missing = LANES - sum(n for _, n in pieces)
        if missing:
            rows.append(jnp.zeros((missing, D_MODEL), f32))
        blk = jnp.concatenate(rows, axis=0) if len(rows) > 1 else rows[0]
        if scale is not None:
            blk = blk * scale
        wall_ref[:, col:col + LANES] = blk.T.astype(bf16)

    for i in range(GLA_KW // LANES):
        put(O_Q + i * LANES, [(I_Q + i * LANES, LANES)], GLA_DK ** -0.5)
        put(O_K + i * LANES, [(I_K + i * LANES, LANES)])
    for i in range(GLA_W // LANES):
        put(O_V + i * LANES, [(I_V + i * LANES, LANES)])
        put(O_GG + i * LANES, [(I_GG + i * LANES, LANES)])
        put(O_SQ + i * LANES, _interleaved_rows(I_SQ, i), SWA_HD ** -0.5 * LOG2E)
        put(O_SG + i * LANES, _interleaved_rows(I_SG, i))
    put(O_SK, [(I_SK, SWA_KVW)])
    put(O_SV, [(I_SV, SWA_KVW)])
    put(O_LOW, [(I_LOW, GLA_RANK)])
    wop_ref[:GLA_W, :] = wo_ref[:GLA_W, :].astype(bf16)
    for i in range(SWA_GROUP):
        for kv, (r, n) in enumerate(_interleaved_rows(GLA_W, i)):
            dst = GLA_W + (i * SWA_KV + kv) * SWA_HD
            wop_ref[dst:dst + n, :] = wo_ref[r:r + n, :].astype(bf16)


def _prep_weights(norm_in, w_in, w_gate_up, b_gate, gla_norm, attn_sinks, w_out, norm_f):
    w_all, w_out_p = pl.pallas_call(
        _weight_layout_kernel,
        out_shape=[jax.ShapeDtypeStruct((D_MODEL, W_ALL), bf16), jax.ShapeDtypeStruct((D_MODEL, D_MODEL), bf16)],
        compiler_params=pltpu.CompilerParams(vmem_limit_bytes=VMEM_LIMIT),
        name="weight_layout",
    )(w_in.T, w_out)
    w_up = jnp.pad(w_gate_up, ((0, LANES - GLA_RANK), (0, 0))).astype(bf16)
    return dict(
        norm_in=norm_in.reshape(1, D_MODEL), w_all=w_all, w_up=w_up,
        b_gate=b_gate.reshape(1, GLA_KW), gla_norm=jnp.tile(gla_norm, GLA_HEADS).reshape(1, GLA_W),
        sinks=attn_sinks * LOG2E, w_out=w_out_p, norm_f=norm_f.reshape(1, D_MODEL))


def _rms(x, gain):
    return x * lax.rsqrt(jnp.mean(x * x, axis=-1, keepdims=True) + EPS) * gain


def _log_decay(glow, wup_ref, bg_ref):
    z = jnp.dot(glow.astype(bf16), wup_ref[...], preferred_element_type=f32) + bg_ref[...]
    return (jnp.minimum(z, 0.0) - jnp.log(1.0 + jnp.exp(-jnp.abs(z)))) * (1.0 / GLA_TAU)


def _silu(x):
    return x * jax.nn.sigmoid(x)


def _merge(x, o_gla, gg, o_swa, sg, gn_ref, wout_ref, nf_ref):
    parts = []
    for h in range(GLA_HEADS):
        sl = slice(h * GLA_DV, (h + 1) * GLA_DV)
        parts.append(_rms(o_gla[:, sl], gn_ref[:, sl]) * _silu(gg[:, sl]))
    parts.append(o_swa * _silu(sg))
    um = jnp.concatenate(parts, axis=1).astype(bf16)
    hres = x + jnp.dot(um, wout_ref[...], preferred_element_type=f32)
    return _rms(hres, nf_ref[...])


def _prompt_kernel(sinks_ref, x_ref, nin_ref, w_ref, wup_ref, bg_ref, cmat_ref, lmask_ref, tril_ref, kmask_ref, qhot_ref,
                   split_ref,
                   gn_ref, wout_ref, nf_ref,
                   smp_proj_ref, smp_decay_ref, smp_sink_ref, smp_s_ref, smp_ck_ref, smp_cv_ref,
                   y_ref, sp_ref, kn_ref, vn_ref,
                   smp_so_ref, smp_cko_ref, smp_cvo_ref, smp_og_ref, smp_os_ref,
                   s_ref, kprev_ref, vprev_ref, p_s, g_s, ogla_s, oswa_s, *, n_t):
    t = pl.program_id(1)
    tl = x_ref.shape[0]

    @pl.when(t == 0)
    def _():
        s_ref[...] = jnp.zeros_like(s_ref)
        kprev_ref[...] = jnp.zeros_like(kprev_ref)
        vprev_ref[...] = jnp.zeros_like(vprev_ref)

    def project(rg):
        u = _rms(x_ref[rg, :], nin_ref[...]).astype(bf16)
        p_s[rg, :] = jnp.dot(u, w_ref[...], preferred_element_type=f32)
        g_s[rg, :] = _log_decay(p_s[rg, O_LOW:O_LOW + LANES], wup_ref, bg_ref)

    lane_lo = lax.broadcasted_iota(jnp.int32, (CHUNK, LANES), 1) < GLA_DK
    lane_lo_m = lax.broadcasted_iota(jnp.int32, (CHUNK_MILD, LANES), 1) < GLA_DK

    n_pairs = GLA_HEADS // 2

    def group_terms(chunks):
        rows = {c: slice(c * CHUNK, (c + 1) * CHUNK) for c in chunks}
        sums = {}
        for c in chunks:
            gc = g_s[rows[c], :]
            g_hi = gc.astype(bf16)
            r1 = gc - g_hi.astype(f32)
            g_mid = r1.astype(bf16)
            g_lo = (r1 - g_mid.astype(f32)).astype(bf16)
            sums[c] = jnp.dot(cmat_ref[...], jnp.concatenate([g_hi, g_mid, g_lo], axis=0),
                              preferred_element_type=f32)
        level_ops, misc = {}, {}
        for c in chunks:
            qc, kc = p_s[rows[c], O_Q:O_Q + GLA_KW], p_s[rows[c], O_K:O_K + GLA_KW]
            e_b = jnp.exp(sums[c][0:CHUNK])
            misc[c] = (e_b, qc * e_b, kc * jnp.exp(sums[c][CHUNK:2 * CHUNK]),
                       p_s[rows[c], O_V:O_V + GLA_W].astype(bf16))
            for p in range(n_pairs):
                ln = slice(p * LANES, (p + 1) * LANES)
                qp, kp = qc[:, ln], kc[:, ln]
                for l in range(N_LEVELS + 1):
                    if l < N_LEVELS:
                        e = jnp.exp(sums[c][(2 + l) * CHUNK:(3 + l) * CHUNK, ln])
                        qe, ke = (qp * e).astype(bf16), (kp * e).astype(bf16)
                    else:
                        qe, ke = qp.astype(bf16), kp.astype(bf16)
                    lhs = jnp.concatenate([jnp.where(lane_lo, qe, 0), jnp.where(lane_lo, 0, qe)], axis=0)
                    level_ops[c, p, l] = (lhs, ke)
        scores = {key: lax.dot_general(lhs, ke, (((1,), (1,)), ((), ())), preferred_element_type=f32)
                  for key, (lhs, ke) in level_ops.items()}
        terms = {}
        for c in chunks:
            e_b, qb, k_suf, vb = misc[c]
            terms[c] = []
            for p in range(n_pairs):
                ln = slice(p * LANES, (p + 1) * LANES)
                a = scores[c, p, 0] * lmask_ref[0]
                for l in range(1, N_LEVELS + 1):
                    a = a + scores[c, p, l] * lmask_ref[l]
                ab = a.astype(bf16)
                qbp = qb[:, ln].astype(bf16)
                lhs_heads = []
                for hh in range(2):
                    qbm = jnp.where(lane_lo, qbp, 0) if hh == 0 else jnp.where(lane_lo, 0, qbp)
                    lhs_heads.append(jnp.concatenate([qbm, ab[hh * CHUNK:(hh + 1) * CHUNK, :]], axis=1))
                upd = lax.dot_general(k_suf[:, ln].astype(bf16), vb[:, p * 2 * GLA_DV:(p + 1) * 2 * GLA_DV],
                                      (((0,), (0,)), ((), ())), preferred_element_type=f32)
                upd = jnp.concatenate(
                    [upd[0:GLA_DK, 0:GLA_DV], upd[GLA_DK:2 * GLA_DK, GLA_DV:2 * GLA_DV]], axis=0)
                e_col = jnp.broadcast_to(e_b[CHUNK - 1:CHUNK, ln], (LANES, LANES)).T
                terms[c].append((lhs_heads, vb, upd, e_col))
        return terms

    def group_terms_mild(chunks):
        cs = CHUNK_MILD
        rows = {c: slice(c * cs, (c + 1) * cs) for c in chunks}
        prefix = {}
        for c in chunks:
            halves = []
            for r0 in range(c * cs, (c + 1) * cs, CHUNK):
                gc = g_s[r0:r0 + CHUNK, :]
                g_hi = gc.astype(bf16)
                r1 = gc - g_hi.astype(f32)
                g_mid = r1.astype(bf16)
                g_lo = (r1 - g_mid.astype(f32)).astype(bf16)
                b_half = jnp.dot(cmat_ref[0:CHUNK, :], jnp.concatenate([g_hi, g_mid, g_lo], axis=0),
                                 preferred_element_type=f32)
                halves.append(b_half if not halves else b_half + halves[-1][CHUNK - 1:CHUNK, :])
            prefix[c] = jnp.concatenate(halves, axis=0)
        score_ops, misc = {}, {}
        for c in chunks:
            qc, kc = p_s[rows[c], O_Q:O_Q + GLA_KW], p_s[rows[c], O_K:O_K + GLA_KW]
            b = prefix[c]
            e_b = jnp.exp(b)
            qb, k_neg = qc * e_b, kc * jnp.exp(-b)
            misc[c] = (e_b, kc * jnp.exp(b[cs - 1:cs, :] - b), p_s[rows[c], O_V:O_V + GLA_W].astype(bf16))
            for p in range(n_pairs):
                ln = slice(p * LANES, (p + 1) * LANES)
                qbp = qb[:, ln].astype(bf16)
                qbms = [jnp.where(lane_lo_m, qbp, 0), jnp.where(lane_lo_m, 0, qbp)]
                score_ops[c, p] = (qbms, k_neg[:, ln].astype(bf16))
        scores = {key: lax.dot_general(jnp.concatenate(qbms, axis=0), ke, (((1,), (1,)), ((), ())),
                                       preferred_element_type=f32)
                  for key, (qbms, ke) in score_ops.items()}
        terms = {}
        for c in chunks:
            e_b, k_suf, vb = misc[c]
            terms[c] = []
            for p in range(n_pairs):
                ln = slice(p * LANES, (p + 1) * LANES)
                ab = (scores[c, p] * tril_ref[...]).astype(bf16)
                qbms = score_ops[c, p][0]
                lhs_heads = [jnp.concatenate([qbms[hh], ab[hh * cs:(hh + 1) * cs, :]], axis=1)
                             for hh in range(2)]
                upd = lax.dot_general(k_suf[:, ln].astype(bf16), vb[:, p * 2 * GLA_DV:(p + 1) * 2 * GLA_DV],
                                      (((0,), (0,)), ((), ())), preferred_element_type=f32)
                upd = jnp.concatenate(
                    [upd[0:GLA_DK, 0:GLA_DV], upd[GLA_DK:2 * GLA_DK, GLA_DV:2 * GLA_DV]], axis=0)
                e_col = jnp.broadcast_to(e_b[cs - 1:cs, ln], (LANES, LANES)).T
                terms[c].append((lhs_heads, vb, upd, e_col))
        return terms

    def gla(chunks, group_fn, group, cs):
        terms = {}
        for c0 in range(chunks[0], chunks[-1] + 1, group):
            terms.update(group_fn(range(c0, c0 + group)))
        states = {}
        for p in range(n_pairs):
            ln = slice(p * LANES, (p + 1) * LANES)
            s_pair = s_ref[ln, :]
            for c in chunks:
                states[c, p] = s_pair.astype(bf16)
                _, _, upd, e_col = terms[c][p]
                s_pair = e_col * s_pair + upd
            s_ref[ln, :] = s_pair
        for c in chunks:
            for p in range(n_pairs):
                lhs_heads, vb, _, _ = terms[c][p]
                for hh in range(2):
                    h = 2 * p + hh
                    rhs = jnp.concatenate([states[c, p], vb[:, h * GLA_DV:(h + 1) * GLA_DV]], axis=0)
                    ogla_s[c * cs:(c + 1) * cs, h * GLA_DV:(h + 1) * GLA_DV] = jnp.dot(
                        lhs_heads[hh], rhs, preferred_element_type=f32)

    lane_lo_w = lax.broadcasted_iota(jnp.int32, (WINDOW, LANES), 1) < SWA_HD
    row_lo = lax.broadcasted_iota(jnp.int32, (2 * WINDOW, 1), 0) < WINDOW

    def swa(blocks):
        for blk in blocks:
            rs = slice(blk * WINDOW, (blk + 1) * WINDOW)
            sq = p_s[rs, O_SQ:O_SQ + SWA_W].astype(bf16)
            k_cur, v_cur = p_s[rs, O_SK:O_SK + SWA_KVW], p_s[rs, O_SV:O_SV + SWA_KVW]
            k2 = jnp.concatenate([kprev_ref[...], k_cur], axis=0).astype(bf16)
            v2 = jnp.concatenate([vprev_ref[...], v_cur], axis=0).astype(bf16)
            kmask = kmask_ref[jnp.where(t > 0, 0, 1)] if blk == 0 else kmask_ref[0]
            k2m = jnp.concatenate([k2, kmask], axis=1)
            for tt in range(SWA_GROUP):
                qt = sq[:, tt * LANES:(tt + 1) * LANES]
                lhs = jnp.concatenate([jnp.where(lane_lo_w, qt, 0), jnp.where(lane_lo_w, 0, qt)], axis=0)
                lhs = jnp.concatenate([lhs, qhot_ref[...]], axis=1)
                s = lax.dot_general(lhs, k2m, (((1,), (1,)), ((), ())), preferred_element_type=f32)
                s = jnp.maximum(s[:, :WINDOW], s[:, WINDOW:])
                sink = jnp.where(row_lo, sinks_ref[tt], sinks_ref[SWA_GROUP + tt])
                m = jnp.maximum(jnp.max(s, axis=-1, keepdims=True), sink)
                e = jnp.exp2(s - m)
                r = 1.0 / (jnp.sum(e, axis=-1, keepdims=True) + jnp.exp2(sink - m))
                eb = e.astype(bf16)
                e2 = jnp.concatenate([eb * split_ref[0], eb * split_ref[1]], axis=1)
                o2 = jnp.dot(e2, v2, preferred_element_type=f32)
                oswa_s[rs, tt * LANES:(tt + 1) * LANES] = jnp.where(
                    lane_lo_w, o2[:WINDOW] * r[:WINDOW], o2[WINDOW:] * r[WINDOW:])
            kprev_ref[...] = k_cur
            vprev_ref[...] = v_cur

    def merge_gla(rg):
        parts = [_rms(ogla_s[rg, h * GLA_DV:(h + 1) * GLA_DV], gn_ref[:, h * GLA_DV:(h + 1) * GLA_DV])
                 * _silu(p_s[rg, O_GG + h * GLA_DV:O_GG + (h + 1) * GLA_DV]) for h in range(GLA_HEADS)]
        um = jnp.concatenate(parts, axis=1).astype(bf16)
        y_ref[rg, :] = x_ref[rg, :] + jnp.dot(um, wout_ref[:GLA_W, :], preferred_element_type=f32)

    def merge_swa(rg):
        um = (oswa_s[rg, :] * _silu(p_s[rg, O_SG:O_SG + SWA_W])).astype(bf16)
        hres = y_ref[rg, :] + jnp.dot(um, wout_ref[GLA_W:, :], preferred_element_type=f32)
        y_ref[rg, :] = _rms(hres, nf_ref[...])

    whole = slice(0, tl)
    project(whole)
    _sample_state_update(smp_proj_ref, smp_decay_ref, smp_sink_ref, smp_s_ref, smp_ck_ref, smp_cv_ref,
                         smp_so_ref, smp_cko_ref, smp_cvo_ref, smp_og_ref, smp_os_ref)
    mild = jnp.min(g_s[...]) * CHUNK_MILD > -MILD_EXP_BOUND
    pl.when(mild)(lambda: gla(range(tl // CHUNK_MILD), group_terms_mild, GLA_GROUP_MILD, CHUNK_MILD))
    pl.when(jnp.logical_not(mild))(lambda: gla(range(tl // CHUNK), group_terms, GLA_GROUP, CHUNK))
    merge_gla(whole)
    swa(range(tl // WINDOW))
    merge_swa(whole)

    @pl.when(t == n_t - 1)
    def _():
        sp_ref[...] = s_ref[...]
        kn_ref[...] = kprev_ref[...].T
        vn_ref[...] = vprev_ref[...].T


def _prompt_call(x, w, cmat, lmask, tril, kmask, qhot, split, smp_proj, smp_decay, smp_state, smp_ck, smp_cv):
    bsz, seq, _ = x.shape
    tl = TOK_BLOCK
    n_t = seq // tl
    n = smp_state.shape[0]
    g = n // (bsz * n_t)
    assert g * bsz * n_t == n
    step = lambda b, t: b * n_t + t
    smp = lambda *tail: pl.BlockSpec((g,) + tail, lambda b, t: (step(b, t),) + (0,) * len(tail))
    smp_rows = lambda width: pl.BlockSpec((None, g, width), lambda b, t: (step(b, t), 0, 0))
    const = lambda shape: pl.BlockSpec(shape, lambda b, t: (0,) * len(shape), pipeline_mode=pl.Buffered(1))
    return pl.pallas_call(
        functools.partial(_prompt_kernel, n_t=n_t),
        grid=(bsz, n_t),
        in_specs=[
            pl.BlockSpec(memory_space=pltpu.SMEM),
            pl.BlockSpec((None, tl, D_MODEL), lambda b, t: (b, t, 0)),
            const((1, D_MODEL)), const((D_MODEL, W_ALL)), const((LANES, GLA_KW)), const((1, GLA_KW)),
            const(cmat.shape), const(lmask.shape), const(tril.shape), const(kmask.shape), const(qhot.shape),
            const(split.shape),
            const((1, GLA_W)), const((D_MODEL, D_MODEL)), const((1, D_MODEL)),
            smp_rows(W_MAIN), smp_rows(GLA_KW), const((SWA_HEADS, 1)),
            smp(GLA_KW, GLA_DV), smp(SWA_KVW, WINDOW), smp(SWA_KVW, WINDOW),
        ],
        out_specs=[
            pl.BlockSpec((None, tl, D_MODEL), lambda b, t: (b, t, 0)),
            pl.BlockSpec((None, GLA_KW, GLA_DV), lambda b, t: (b, 0, 0)),
            pl.BlockSpec((None, WINDOW, SWA_KVW), lambda b, t: (b, 0, 0)),
            pl.BlockSpec((None, WINDOW, SWA_KVW), lambda b, t: (b, 0, 0)),
            smp(GLA_KW, GLA_DV), smp(SWA_KVW, WINDOW), smp(SWA_KVW, WINDOW),
            smp(GLA_HEADS, GLA_DV), smp(SWA_HEADS, LANES),
        ],
        out_shape=[
            jax.ShapeDtypeStruct((bsz, seq, D_MODEL), f32),
            jax.ShapeDtypeStruct((bsz, GLA_KW, GLA_DV), f32),
            jax.ShapeDtypeStruct((bsz, WINDOW, SWA_KVW), f32),
            jax.ShapeDtypeStruct((bsz, WINDOW, SWA_KVW), f32),
            jax.ShapeDtypeStruct(smp_state.shape, f32), jax.ShapeDtypeStruct(smp_ck.shape, f32),
            jax.ShapeDtypeStruct(smp_cv.shape, f32),
            jax.ShapeDtypeStruct((n, GLA_HEADS, GLA_DV), f32), jax.ShapeDtypeStruct((n, SWA_HEADS, LANES), f32),
        ],
        scratch_shapes=[
            pltpu.VMEM((GLA_KW, GLA_DV), f32),
            pltpu.VMEM((WINDOW, SWA_KVW), f32), pltpu.VMEM((WINDOW, SWA_KVW), f32),
            pltpu.VMEM((tl, W_ALL), f32), pltpu.VMEM((tl, GLA_KW), f32),
            pltpu.VMEM((tl, GLA_W), f32), pltpu.VMEM((tl, SWA_W), f32),
        ],
        compiler_params=pltpu.CompilerParams(
            dimension_semantics=("arbitrary", "arbitrary"), vmem_limit_bytes=PROMPT_VMEM_LIMIT),
        name="prompt_layer",
    )(w["sinks"], x, w["norm_in"], w["w_all"], w["w_up"], w["b_gate"], cmat, lmask, tril, kmask, qhot, split,
      w["gla_norm"], w["w_out"], w["norm_f"],
      smp_proj.reshape(bsz * n_t, g, W_MAIN), smp_decay.reshape(bsz * n_t, g, GLA_KW),
      w["sinks"].reshape(SWA_HEADS, 1), smp_state, smp_ck, smp_cv)


def _sample_proj_kernel(x_ref, nin_ref, w_ref, wup_ref, bg_ref, proj_ref, decay_ref):
    u = _rms(x_ref[...], nin_ref[...]).astype(bf16)
    proj_ref[...] = jnp.dot(u, w_ref[:, :W_MAIN], preferred_element_type=f32)
    glow = jnp.dot(u, w_ref[:, O_LOW:O_LOW + LANES], preferred_element_type=f32)
    decay_ref[...] = jnp.exp(_log_decay(glow, wup_ref, bg_ref))


def _sample_proj_call(xs, w):
    n = xs.shape[0]
    return pl.pallas_call(
        _sample_proj_kernel,
        out_shape=[jax.ShapeDtypeStruct((n, W_MAIN), f32), jax.ShapeDtypeStruct((n, GLA_KW), f32)],
        compiler_params=pltpu.CompilerParams(vmem_limit_bytes=VMEM_LIMIT),
        name="sample_proj",
    )(xs, w["norm_in"], w["w_all"], w["w_up"], w["b_gate"])


def _split3(x):
    as_bf16 = lambda v: v.astype(bf16).astype(f32)
    hi = as_bf16(x)
    mid = as_bf16(x - hi)
    return hi, mid, as_bf16(x - hi - mid)


def _sample_state_update(proj_ref, decay_ref, sink_ref, s_ref, ck_ref, cv_ref,
                         so_ref, cko_ref, cvo_ref, og_ref, os_ref):
    row = lax.broadcasted_iota(jnp.int32, (PACK, GLA_KW), 0)
    head_of_lane = lax.broadcasted_iota(jnp.int32, (PACK, GLA_KW), 1) // GLA_DK
    own_head = head_of_lane == row
    row_v = lax.broadcasted_iota(jnp.int32, (PACK, GLA_DV), 0)
    lane_v = lax.broadcasted_iota(jnp.int32, (PACK, GLA_DV), 1)
    piece_rows = ((row_v >= GLA_HEADS) & (row_v < GLA_HEADS + 3)).astype(f32)
    last_lane_rows = ((row_v < 3) & (lane_v == WINDOW - 1)).astype(bf16)
    newest = lax.broadcasted_iota(jnp.int32, (SWA_KVW, WINDOW), 1) == WINDOW - 1
    row_q = lax.broadcasted_iota(jnp.int32, (SWA_HEADS, LANES), 0)
    own_kv = (lax.broadcasted_iota(jnp.int32, (SWA_HEADS, LANES), 1) // SWA_HD) == (row_q // SWA_GROUP)
    sink = sink_ref[...]
    contract_rows = (((0,), (0,)), ((), ()))
    seqs = range(proj_ref.shape[0])
    lts, rts, qms, lt2s, q8s = [], [], [], [], []
    for j in seqs:
        pr = proj_ref[j:j + 1, :]
        bcast = lambda lo, width: jnp.broadcast_to(pr[:, lo:lo + width], (PACK, width))
        a_hi, a_mid, a_lo = _split3(jnp.broadcast_to(decay_ref[j:j + 1, :], (PACK, GLA_KW)))
        a_piece = jnp.where(row == GLA_HEADS, a_hi, jnp.where(row == GLA_HEADS + 1, a_mid, a_lo))
        lts.append(jnp.where(own_head, bcast(O_K, GLA_KW),
                             jnp.where((row >= GLA_HEADS) & (row < GLA_HEADS + 3), a_piece, 0.0)).astype(bf16))
        v_b = bcast(O_V, GLA_W)
        v_sel = jnp.zeros((PACK, GLA_DV), f32)
        for h in range(GLA_HEADS):
            v_sel = jnp.where(row_v == h, v_b[:, h * GLA_DV:(h + 1) * GLA_DV], v_sel)
        rts.append(jnp.concatenate([v_sel, piece_rows], axis=1).astype(bf16))
        qms.append(jnp.where(own_head, bcast(O_Q, GLA_KW), 0.0).astype(bf16))
        n_hi, n_mid, n_lo = _split3(bcast(O_SK, 2 * SWA_KVW))
        lt2s.append(jnp.where(row == 0, n_hi, jnp.where(row == 1, n_mid,
                                                        jnp.where(row == 2, n_lo, 0.0))).astype(bf16))
        sq_b = jnp.broadcast_to(pr[:, O_SQ:O_SQ + SWA_W], (SWA_HEADS, SWA_W))
        q8 = jnp.zeros((SWA_HEADS, LANES), f32)
        for gq in range(SWA_GROUP):
            q8 = jnp.where(row_q % SWA_GROUP == gq, sq_b[:, gq * LANES:(gq + 1) * LANES], q8)
        q8s.append(jnp.where(own_kv, q8, 0.0).astype(bf16))
    kv_as = [lax.dot_general(lts[j], rts[j], contract_rows, preferred_element_type=f32) for j in seqs]
    inss = [lax.dot_general(lt2s[j], last_lane_rows, contract_rows, preferred_element_type=f32) for j in seqs]
    s_news, kts, vts = [], [], []
    for j in seqs:
        s_new = kv_as[j][:, GLA_DV:] * s_ref[j] + kv_as[j][:, :GLA_DV]
        so_ref[j] = s_new
        s_news.append(s_new.astype(bf16))
        kt = jnp.where(newest, inss[j][:SWA_KVW], pltpu.roll(ck_ref[j], WINDOW - 1, axis=1))
        vt = jnp.where(newest, inss[j][SWA_KVW:], pltpu.roll(cv_ref[j], WINDOW - 1, axis=1))
        cko_ref[j] = kt
        cvo_ref[j] = vt
        kts.append(kt.astype(bf16))
        vts.append(vt.astype(bf16))
    for j in seqs:
        og_ref[j] = jnp.dot(qms[j], s_news[j], preferred_element_type=f32)[:GLA_HEADS]
    scores = [jnp.dot(q8s[j], kts[j], preferred_element_type=f32) for j in seqs]
    es, dens = [], []
    for j in seqs:
        m = jnp.maximum(jnp.max(scores[j], axis=-1, keepdims=True), sink)
        e = jnp.exp2(scores[j] - m)
        dens.append(jnp.sum(e, axis=-1, keepdims=True) + jnp.exp2(sink - m))
        es.append(e.astype(bf16))
    for j in seqs:
        o = lax.dot_general(es[j], vts[j], (((1,), (1,)), ((), ())), preferred_element_type=f32)
        os_ref[j] = o / dens[j]


def _sample_merge_kernel(x_ref, og_ref, gg_ref, os_ref, sg_ref, gn_ref, wout_ref, nf_ref, y_ref):
    y_ref[...] = _merge(x_ref[...], og_ref[...], gg_ref[...], os_ref[...], sg_ref[...],
                        gn_ref, wout_ref, nf_ref)


def _sample_merge_call(xs, og, gg, osw, sg, w):
    return pl.pallas_call(
        _sample_merge_kernel,
        out_shape=jax.ShapeDtypeStruct(xs.shape, f32),
        compiler_params=pltpu.CompilerParams(vmem_limit_bytes=VMEM_LIMIT),
        name="sample_merge",
    )(xs, og, gg, osw, sg, w["gla_norm"], w["w_out"], w["norm_f"])


def _sample_finish(xs, proj, og, os_raw, w):
    n = xs.shape[0]
    os5 = os_raw.reshape(n, SWA_KV, SWA_GROUP, SWA_KV, SWA_HD)
    os_il = jnp.stack([os5[:, kv, :, kv, :] for kv in range(SWA_KV)], axis=2).reshape(n, SWA_W)
    return _sample_merge_call(xs, og.reshape(n, GLA_W), proj[:, O_GG:O_GG + GLA_W], os_il,
                              proj[:, O_SG:O_SG + SWA_W], w)


def _cache_view(c):
    n = c.shape[1]
    return jnp.transpose(c[0], (0, 2, 3, 1)).reshape(n, SWA_KVW, WINDOW)


def _cache_unview(c):
    n = c.shape[0]
    return jnp.transpose(c.reshape(n, SWA_KV, SWA_HD, WINDOW), (0, 3, 1, 2))[None]


def kernel(x_prompt, x_sample, state_gla, cache_win_k, cache_win_v, norm_in, w_in, w_gate_up, b_gate,
           gla_norm, attn_sinks, w_out, norm_f):
    bsz = x_prompt.shape[0]
    n = x_sample.shape[0]
    w = _prep_weights(norm_in[0], w_in[0], w_gate_up[0], b_gate[0], gla_norm[0], attn_sinks[0],
                      w_out[0], norm_f)
    cmat, lmask, tril = _chunk_tables()
    xs = x_sample.reshape(n, D_MODEL)
    proj, decay = _sample_proj_call(xs, w)
    y_p, s_p, k_p, v_p, s_s, k_s, v_s, og, os_raw = _prompt_call(
        x_prompt, w, cmat, lmask, tril, *_swa_mask_tables(), proj, decay,
        state_gla[0].reshape(n, GLA_KW, GLA_DV), _cache_view(cache_win_k), _cache_view(cache_win_v))
    y_s = _sample_finish(xs, proj, og, os_raw, w)
    return (y_p, y_s.reshape(n, 1, D_MODEL),
            s_p.reshape(1, bsz, GLA_HEADS, GLA_DK, GLA_DV),
            _cache_unview(k_p), _cache_unview(v_p),
            s_s.reshape(1, n, GLA_HEADS, GLA_DK, GLA_DV),
            _cache_unview(k_s), _cache_unview(v_s))
```

```python
import functools

import numpy as np
import jax
import jax.numpy as jnp
from jax import lax
from jax.experimental import pallas as pl
from jax.experimental.pallas import tpu as pltpu

D_MODEL = 1024
GLA_HEADS = 4
GLA_DK = 64
GLA_DV = 128
GLA_KW = GLA_HEADS * GLA_DK
GLA_W = GLA_HEADS * GLA_DV
GLA_RANK = 16
GLA_TAU = 16.0
CHUNK = 64
SWA_HEADS = 8
SWA_HD = 64
SWA_KV = 2
SWA_GROUP = SWA_HEADS // SWA_KV
SWA_W = SWA_HEADS * SWA_HD
SWA_KVW = SWA_KV * SWA_HD
WINDOW = 128
EPS = 1e-6
NEG_INF = -1e30
LOG2E = 1.4426950408889634
LANES = 128

O_Q, O_K, O_V, O_GG = 0, 256, 512, 1024
O_SQ, O_SK, O_SV, O_SG, O_LOW = 1536, 2048, 2176, 2304, 2816
W_MAIN = 2816
W_ALL = W_MAIN + LANES

N_LEVELS = 6
MILD_EXP_BOUND = 80.0
TOK_BLOCK = 1024
GLA_GROUP = 4
CHUNK_MILD = 128
GLA_GROUP_MILD = 8
PACK = 16
PROMPT_VMEM_LIMIT = 60 * 1024 * 1024
VMEM_LIMIT = 32 * 1024 * 1024

f32 = jnp.float32
bf16 = jnp.bfloat16


def _chunk_tables():
    c = CHUNK
    t = np.arange(c)[None, :]
    i = np.arange(c)[:, None]
    blocks = [(t <= i), (t > i)]
    masks = []
    for l in range(N_LEVELS):
        h = c >> (l + 1)
        m = (i // (2 * h)) * (2 * h) + h
        upper = i >= m
        blocks.append(np.where(upper, (t > m) & (t <= i), (t > i) & (t <= m)))
        jj = np.arange(c)[None, :]
        masks.append((i // (2 * h) == jj // (2 * h)) & (i % (2 * h) >= h) & (jj % (2 * h) < h))
    masks.append(np.eye(c, dtype=bool))
    cm = np.concatenate(blocks, axis=0).astype(np.float32)
    cm3 = np.concatenate([cm, cm, cm], axis=1)
    lm = np.stack(masks).astype(np.float32)
    lm = np.concatenate([lm, lm], axis=1)
    tril = np.tril(np.ones((CHUNK_MILD, CHUNK_MILD), dtype=np.float32))
    tril = np.concatenate([tril, tril], axis=0)
    return jnp.asarray(cm3, dtype=bf16), jnp.asarray(lm, dtype=f32), jnp.asarray(tril, dtype=f32)


def _swa_mask_tables():
    key = np.arange(2 * WINDOW)[:, None]
    qi = np.arange(WINDOW)[None, :]
    prev_ok = (key < WINDOW) & (key > qi)
    cur_ok = (key >= WINDOW) & (key - WINDOW <= qi)
    full = np.where(prev_ok | cur_ok, 0.0, NEG_INF)
    first = np.where(cur_ok, 0.0, NEG_INF)
    onehot = (np.arange(2 * WINDOW)[:, None] % WINDOW == qi).astype(np.float32)
    row_q = np.arange(2 * WINDOW)[:, None] % WINDOW
    slot_is_prev = (np.arange(WINDOW)[None, :] > row_q).astype(np.float32)
    split = np.stack([slot_is_prev, 1.0 - slot_is_prev])
    return (jnp.asarray(np.stack([full, first]), dtype=bf16), jnp.asarray(onehot, dtype=bf16),
            jnp.asarray(split, dtype=bf16))


I_Q, I_K, I_V, I_GG, I_LOW, I_SQ, I_SK, I_SV, I_SG = (
    int(v) for v in np.cumsum([0, 256, 256, 512, 512, 16, 512, 128, 128])[:9])


def _interleaved_rows(base, tile):
    return [(base + kv * SWA_GROUP * SWA_HD + tile * SWA_HD, SWA_HD) for kv in range(SWA_KV)]


def _weight_layout_kernel(wt_ref, wo_ref, wall_ref, wop_ref):
    def put(col, pieces, scale=None):
        rows = [wt_ref[r:r + n, :] for r, n in pieces]
        missing = LANES - sum(n for _, n in pieces)
        if missing:
            rows.append(jnp.zeros((missing, D_MODEL), f32))
        blk = jnp.concatenate(rows, axis=0) if len(rows) > 1 else rows[0]
        if scale is not None:
            blk = blk * scale
        wall_ref[:, col:col + LANES] = blk.T.astype(bf16)

    for i in range(GLA_KW // LANES):
        put(O_Q + i * LANES, [(I_Q + i * LANES, LANES)], GLA_DK ** -0.5)
        put(O_K + i * LANES, [(I_K + i * LANES, LANES)])
    for i in range(GLA_W // LANES):
        put(O_V + i * LANES, [(I_V + i * LANES, LANES)])
        put(O_GG + i * LANES, [(I_GG + i * LANES, LANES)])
        put(O_SQ + i * LANES, _interleaved_rows(I_SQ, i), SWA_HD ** -0.5 * LOG2E)
        put(O_SG + i * LANES, _interleaved_rows(I_SG, i))
    put(O_SK, [(I_SK, SWA_KVW)])
    put(O_SV, [(I_SV, SWA_KVW)])
    put(O_LOW, [(I_LOW, GLA_RANK)])
    wop_ref[:GLA_W, :] = wo_ref[:GLA_W, :].astype(bf16)
    for i in range(SWA_GROUP):
        for kv, (r, n) in enumerate(_interleaved_rows(GLA_W, i)):
            dst = GLA_W + (i * SWA_KV + kv) * SWA_HD
            wop_ref[dst:dst + n, :] = wo_ref[r:r + n, :].astype(bf16)


def _prep_weights(norm_in, w_in, w_gate_up, b_gate, gla_norm, attn_sinks, w_out, norm_f):
    w_all, w_out_p = pl.pallas_call(
        _weight_layout_kernel,
        out_shape=[jax.ShapeDtypeStruct((D_MODEL, W_ALL), bf16), jax.ShapeDtypeStruct((D_MODEL, D_MODEL), bf16)],
        compiler_params=pltpu.CompilerParams(vmem_limit_bytes=VMEM_LIMIT),
        name="weight_layout",
    )(w_in.T, w_out)
    w_up = jnp.pad(w_gate_up, ((0, LANES - GLA_RANK), (0, 0))).astype(bf16)
    return dict(
        norm_in=norm_in.reshape(1, D_MODEL), w_all=w_all, w_up=w_up,
        b_gate=b_gate.reshape(1, GLA_KW), gla_norm=jnp.tile(gla_norm, GLA_HEADS).reshape(1, GLA_W),
        sinks=attn_sinks * LOG2E, w_out=w_out_p, norm_f=norm_f.reshape(1, D_MODEL))


def _rms(x, gain):
    return x * lax.rsqrt(jnp.mean(x * x, axis=-1, keepdims=True) + EPS) * gain


def _log_decay(glow, wup_ref, bg_ref):
    z = jnp.dot(glow.astype(bf16), wup_ref[...], preferred_element_type=f32) + bg_ref[...]
    return (jnp.minimum(z, 0.0) - jnp.log(1.0 + jnp.exp(-jnp.abs(z)))) * (1.0 / GLA_TAU)


def _silu(x):
    return x * jax.nn.sigmoid(x)


def _merge(x, o_gla, gg, o_swa, sg, gn_ref, wout_ref, nf_ref):
    parts = []
    for h in range(GLA_HEADS):
        sl = slice(h * GLA_DV, (h + 1) * GLA_DV)
        parts.append(_rms(o_gla[:, sl], gn_ref[:, sl]) * _silu(gg[:, sl]))
    parts.append(o_swa * _silu(sg))
    um = jnp.concatenate(parts, axis=1).astype(bf16)
    hres = x + jnp.dot(um, wout_ref[...], preferred_element_type=f32)
    return _rms(hres, nf_ref[...])


def _prompt_kernel(sinks_ref, x_ref, nin_ref, w_ref, wup_ref, bg_ref, cmat_ref, lmask_ref, tril_ref, kmask_ref, qhot_ref,
                   split_ref,
                   gn_ref, wout_ref, nf_ref,
                   smp_proj_ref, smp_decay_ref, smp_sink_ref, smp_s_ref, smp_ck_ref, smp_cv_ref,
                   y_ref, sp_ref, kn_ref, vn_ref,
                   smp_so_ref, smp_cko_ref, smp_cvo_ref, smp_og_ref, smp_os_ref,
                   s_ref, kprev_ref, vprev_ref, p_s, g_s, ogla_s, oswa_s, *, n_t):
    t = pl.program_id(1)
    tl = x_ref.shape[0]

    @pl.when(t == 0)
    def _():
        s_ref[...] = jnp.zeros_like(s_ref)
        kprev_ref[...] = jnp.zeros_like(kprev_ref)
        vprev_ref[...] = jnp.zeros_like(vprev_ref)

    def project(rg):
        u = _rms(x_ref[rg, :], nin_ref[...]).astype(bf16)
        p_s[rg, :] = jnp.dot(u, w_ref[...], preferred_element_type=f32)
        g_s[rg, :] = _log_decay(p_s[rg, O_LOW:O_LOW + LANES], wup_ref, bg_ref)

    lane_lo = lax.broadcasted_iota(jnp.int32, (CHUNK, LANES), 1) < GLA_DK
    lane_lo_m = lax.broadcasted_iota(jnp.int32, (CHUNK_MILD, LANES), 1) < GLA_DK

    n_pairs = GLA_HEADS // 2

    def group_terms(chunks):
        rows = {c: slice(c * CHUNK, (c + 1) * CHUNK) for c in chunks}
        sums = {}
        for c in chunks:
            gc = g_s[rows[c], :]
            g_hi = gc.astype(bf16)
            r1 = gc - g_hi.astype(f32)
            g_mid = r1.astype(bf16)
            g_lo = (r1 - g_mid.astype(f32)).astype(bf16)
            sums[c] = jnp.dot(cmat_ref[...], jnp.concatenate([g_hi, g_mid, g_lo], axis=0),
                              preferred_element_type=f32)
        level_ops, misc = {}, {}
        for c in chunks:
            qc, kc = p_s[rows[c], O_Q:O_Q + GLA_KW], p_s[rows[c], O_K:O_K + GLA_KW]
            e_b = jnp.exp(sums[c][0:CHUNK])
            misc[c] = (e_b, qc * e_b, kc * jnp.exp(sums[c][CHUNK:2 * CHUNK]),
                       p_s[rows[c], O_V:O_V + GLA_W].astype(bf16))
            for p in range(n_pairs):
                ln = slice(p * LANES, (p + 1) * LANES)
                qp, kp = qc[:, ln], kc[:, ln]
                for l in range(N_LEVELS + 1):
                    if l < N_LEVELS:
                        e = jnp.exp(sums[c][(2 + l) * CHUNK:(3 + l) * CHUNK, ln])
                        qe, ke = (qp * e).astype(bf16), (kp * e).astype(bf16)
                    else:
                        qe, ke = qp.astype(bf16), kp.astype(bf16)
                    lhs = jnp.concatenate([jnp.where(lane_lo, qe, 0), jnp.where(lane_lo, 0, qe)], axis=0)
                    level_ops[c, p, l] = (lhs, ke)
        scores = {key: lax.dot_general(lhs, ke, (((1,), (1,)), ((), ())), preferred_element_type=f32)
                  for key, (lhs, ke) in level_ops.items()}
        terms = {}
        for c in chunks:
            e_b, qb, k_suf, vb = misc[c]
            terms[c] = []
            for p in range(n_pairs):
                ln = slice(p * LANES, (p + 1) * LANES)
                a = scores[c, p, 0] * lmask_ref[0]
                for l in range(1, N_LEVELS + 1):
                    a = a + scores[c, p, l] * lmask_ref[l]
                ab = a.astype(bf16)
                qbp = qb[:, ln].astype(bf16)
                lhs_heads = []
                for hh in range(2):
                    qbm = jnp.where(lane_lo, qbp, 0) if hh == 0 else jnp.where(lane_lo, 0, qbp)
                    lhs_heads.append(jnp.concatenate([qbm, ab[hh * CHUNK:(hh + 1) * CHUNK, :]], axis=1))
                upd = lax.dot_general(k_suf[:, ln].astype(bf16), vb[:, p * 2 * GLA_DV:(p + 1) * 2 * GLA_DV],
                                      (((0,), (0,)), ((), ())), preferred_element_type=f32)
                upd = jnp.concatenate(
                    [upd[0:GLA_DK, 0:GLA_DV], upd[GLA_DK:2 * GLA_DK, GLA_DV:2 * GLA_DV]], axis=0)
                e_col = jnp.broadcast_to(e_b[CHUNK - 1:CHUNK, ln], (LANES, LANES)).T
                terms[c].append((lhs_heads, vb, upd, e_col))
        return terms

    def group_terms_mild(chunks):
        cs = CHUNK_MILD
        rows = {c: slice(c * cs, (c + 1) * cs) for c in chunks}
        prefix = {}
        for c in chunks:
            halves = []
            for r0 in range(c * cs, (c + 1) * cs, CHUNK):
                gc = g_s[r0:r0 + CHUNK, :]
                g_hi = gc.astype(bf16)
                r1 = gc - g_hi.astype(f32)
                g_mid = r1.astype(bf16)
                g_lo = (r1 - g_mid.astype(f32)).astype(bf16)
                b_half = jnp.dot(cmat_ref[0:CHUNK, :], jnp.concatenate([g_hi, g_mid, g_lo], axis=0),
                                 preferred_element_type=f32)
                halves.append(b_half if not halves else b_half + halves[-1][CHUNK - 1:CHUNK, :])
            prefix[c] = jnp.concatenate(halves, axis=0)
        score_ops, misc = {}, {}
        for c in chunks:
            qc, kc = p_s[rows[c], O_Q:O_Q + GLA_KW], p_s[rows[c], O_K:O_K + GLA_KW]
            b = prefix[c]
            e_b = jnp.exp(b)
            qb, k_neg = qc * e_b, kc * jnp.exp(-b)
            misc[c] = (e_b, kc * jnp.exp(b[cs - 1:cs, :] - b), p_s[rows[c], O_V:O_V + GLA_W].astype(bf16))
            for p in range(n_pairs):
                ln = slice(p * LANES, (p + 1) * LANES)
                qbp = qb[:, ln].astype(bf16)
                qbms = [jnp.where(lane_lo_m, qbp, 0), jnp.where(lane_lo_m, 0, qbp)]
                score_ops[c, p] = (qbms, k_neg[:, ln].astype(bf16))
        scores = {key: lax.dot_general(jnp.concatenate(qbms, axis=0), ke, (((1,), (1,)), ((), ())),
                                       preferred_element_type=f32)
                  for key, (qbms, ke) in score_ops.items()}
        terms = {}
        for c in chunks:
            e_b, k_suf, vb = misc[c]
            terms[c] = []
            for p in range(n_pairs):
                ln = slice(p * LANES, (p + 1) * LANES)
                ab = (scores[c, p] * tril_ref[...]).astype(bf16)
                qbms = score_ops[c, p][0]
                lhs_heads = [jnp.concatenate([qbms[hh], ab[hh * cs:(hh + 1) * cs, :]], axis=1)
                             for hh in range(2)]
                upd = lax.dot_general(k_suf[:, ln].astype(bf16), vb[:, p * 2 * GLA_DV:(p + 1) * 2 * GLA_DV],
                                      (((0,), (0,)), ((), ())), preferred_element_type=f32)
                upd = jnp.concatenate(
                    [upd[0:GLA_DK, 0:GLA_DV], upd[GLA_DK:2 * GLA_DK, GLA_DV:2 * GLA_DV]], axis=0)
                e_col = jnp.broadcast_to(e_b[cs - 1:cs, ln], (LANES, LANES)).T
                terms[c].append((lhs_heads, vb, upd, e_col))
        return terms

    def gla(chunks, group_fn, group, cs):
        terms = {}
        for c0 in range(chunks[0], chunks[-1] + 1, group):
            terms.update(group_fn(range(c0, c0 + group)))
        states = {}
        for p in range(n_pairs):
            ln = slice(p * LANES, (p + 1) * LANES)
            s_pair = s_ref[ln, :]
            for c in chunks:
                states[c, p] = s_pair.astype(bf16)
                _, _, upd, e_col = terms[c][p]
                s_pair = e_col * s_pair + upd
            s_ref[ln, :] = s_pair
        for c in chunks:
            for p in range(n_pairs):
                lhs_heads, vb, _, _ = terms[c][p]
                for hh in range(2):
                    h = 2 * p + hh
                    rhs = jnp.concatenate([states[c, p], vb[:, h * GLA_DV:(h + 1) * GLA_DV]], axis=0)
                    ogla_s[c * cs:(c + 1) * cs, h * GLA_DV:(h + 1) * GLA_DV] = jnp.dot(
                        lhs_heads[hh], rhs, preferred_element_type=f32)

    lane_lo_w = lax.broadcasted_iota(jnp.int32, (WINDOW, LANES), 1) < SWA_HD
    row_lo = lax.broadcasted_iota(jnp.int32, (2 * WINDOW, 1), 0) < WINDOW

    def swa(blocks):
        for blk in blocks:
            rs = slice(blk * WINDOW, (blk + 1) * WINDOW)
            sq = p_s[rs, O_SQ:O_SQ + SWA_W].astype(bf16)
            k_cur, v_cur = p_s[rs, O_SK:O_SK + SWA_KVW], p_s[rs, O_SV:O_SV + SWA_KVW]
            k2 = jnp.concatenate([kprev_ref[...], k_cur], axis=0).astype(bf16)
            v2 = jnp.concatenate([vprev_ref[...], v_cur], axis=0).astype(bf16)
            kmask = kmask_ref[jnp.where(t > 0, 0, 1)] if blk == 0 else kmask_ref[0]
            k2m = jnp.concatenate([k2, kmask], axis=1)
            for tt in range(SWA_GROUP):
                qt = sq[:, tt * LANES:(tt + 1) * LANES]
                lhs = jnp.concatenate([jnp.where(lane_lo_w, qt, 0), jnp.where(lane_lo_w, 0, qt)], axis=0)
                lhs = jnp.concatenate([lhs, qhot_ref[...]], axis=1)
                s = lax.dot_general(lhs, k2m, (((1,), (1,)), ((), ())), preferred_element_type=f32)
                s = jnp.maximum(s[:, :WINDOW], s[:, WINDOW:])
                sink = jnp.where(row_lo, sinks_ref[tt], sinks_ref[SWA_GROUP + tt])
                m = jnp.maximum(jnp.max(s, axis=-1, keepdims=True), sink)
                e = jnp.exp2(s - m)
                r = 1.0 / (jnp.sum(e, axis=-1, keepdims=True) + jnp.exp2(sink - m))
                eb = e.astype(bf16)
                e2 = jnp.concatenate([eb * split_ref[0], eb * split_ref[1]], axis=1)
                o2 = jnp.dot(e2, v2, preferred_element_type=f32)
                oswa_s[rs, tt * LANES:(tt + 1) * LANES] = jnp.where(
                    lane_lo_w, o2[:WINDOW] * r[:WINDOW], o2[WINDOW:] * r[WINDOW:])
            kprev_ref[...] = k_cur
            vprev_ref[...] = v_cur

    def merge_gla(rg):
        parts = [_rms(ogla_s[rg, h * GLA_DV:(h + 1) * GLA_DV], gn_ref[:, h * GLA_DV:(h + 1) * GLA_DV])
                 * _silu(p_s[rg, O_GG + h * GLA_DV:O_GG + (h + 1) * GLA_DV]) for h in range(GLA_HEADS)]
        um = jnp.concatenate(parts, axis=1).astype(bf16)
        y_ref[rg, :] = x_ref[rg, :] + jnp.dot(um, wout_ref[:GLA_W, :], preferred_element_type=f32)

    def merge_swa(rg):
        um = (oswa_s[rg, :] * _silu(p_s[rg, O_SG:O_SG + SWA_W])).astype(bf16)
        hres = y_ref[rg, :] + jnp.dot(um, wout_ref[GLA_W:, :], preferred_element_type=f32)
        y_ref[rg, :] = _rms(hres, nf_ref[...])

    whole = slice(0, tl)
    project(whole)
    _sample_state_update(smp_proj_ref, smp_decay_ref, smp_sink_ref, smp_s_ref, smp_ck_ref, smp_cv_ref,
                         smp_so_ref, smp_cko_ref, smp_cvo_ref, smp_og_ref, smp_os_ref)
    mild = jnp.min(g_s[...]) * CHUNK_MILD > -MILD_EXP_BOUND
    pl.when(mild)(lambda: gla(range(tl // CHUNK_MILD), group_terms_mild, GLA_GROUP_MILD, CHUNK_MILD))
    pl.when(jnp.logical_not(mild))(lambda: gla(range(tl // CHUNK), group_terms, GLA_GROUP, CHUNK))
    merge_gla(whole)
    swa(range(tl // WINDOW))
    merge_swa(whole)

    @pl.when(t == n_t - 1)
    def _():
        sp_ref[...] = s_ref[...]
        kn_ref[...] = kprev_ref[...].T
        vn_ref[...] = vprev_ref[...].T


def _prompt_call(x, w, cmat, lmask, tril, kmask, qhot, split, smp_proj, smp_decay, smp_state, smp_ck, smp_cv):
    bsz, seq, _ = x.shape
    tl = TOK_BLOCK
    n_t = seq // tl
    n = smp_state.shape[0]
    g = n // (bsz * n_t)
    assert g * bsz * n_t == n
    step = lambda b, t: b * n_t + t
    smp = lambda *tail: pl.BlockSpec((g,) + tail, lambda b, t: (step(b, t),) + (0,) * len(tail))
    smp_rows = lambda width: pl.BlockSpec((None, g, width), lambda b, t: (step(b, t), 0, 0))
    const = lambda shape: pl.BlockSpec(shape, lambda b, t: (0,) * len(shape), pipeline_mode=pl.Buffered(1))
    return pl.pallas_call(
        functools.partial(_prompt_kernel, n_t=n_t),
        grid=(bsz, n_t),
        in_specs=[
            pl.BlockSpec(memory_space=pltpu.SMEM),
            pl.BlockSpec((None, tl, D_MODEL), lambda b, t: (b, t, 0)),
            const((1, D_MODEL)), const((D_MODEL, W_ALL)), const((LANES, GLA_KW)), const((1, GLA_KW)),
            const(cmat.shape), const(lmask.shape), const(tril.shape), const(kmask.shape), const(qhot.shape),
            const(split.shape),
            const((1, GLA_W)), const((D_MODEL, D_MODEL)), const((1, D_MODEL)),
            smp_rows(W_MAIN), smp_rows(GLA_KW), const((SWA_HEADS, 1)),
            smp(GLA_KW, GLA_DV), smp(SWA_KVW, WINDOW), smp(SWA_KVW, WINDOW),
        ],
        out_specs=[
            pl.BlockSpec((None, tl, D_MODEL), lambda b, t: (b, t, 0)),
            pl.BlockSpec((None, GLA_KW, GLA_DV), lambda b, t: (b, 0, 0)),
            pl.BlockSpec((None, WINDOW, SWA_KVW), lambda b, t: (b, 0, 0)),
            pl.BlockSpec((None, WINDOW, SWA_KVW), lambda b, t: (b, 0, 0)),
            smp(GLA_KW, GLA_DV), smp(SWA_KVW, WINDOW), smp(SWA_KVW, WINDOW),
            smp(GLA_HEADS, GLA_DV), smp(SWA_HEADS, LANES),
        ],
        out_shape=[
            jax.ShapeDtypeStruct((bsz, seq, D_MODEL), f32),
            jax.ShapeDtypeStruct((bsz, GLA_KW, GLA_DV), f32),
            jax.ShapeDtypeStruct((bsz, WINDOW, SWA_KVW), f32),
            jax.ShapeDtypeStruct((bsz, WINDOW, SWA_KVW), f32),
            jax.ShapeDtypeStruct(smp_state.shape, f32), jax.ShapeDtypeStruct(smp_ck.shape, f32),
            jax.ShapeDtypeStruct(smp_cv.shape, f32),
            jax.ShapeDtypeStruct((n, GLA_HEADS, GLA_DV), f32), jax.ShapeDtypeStruct((n, SWA_HEADS, LANES), f32),
        ],
        scratch_shapes=[
            pltpu.VMEM((GLA_KW, GLA_DV), f32),
            pltpu.VMEM((WINDOW, SWA_KVW), f32), pltpu.VMEM((WINDOW, SWA_KVW), f32),
            pltpu.VMEM((tl, W_ALL), f32), pltpu.VMEM((tl, GLA_KW), f32),
            pltpu.VMEM((tl, GLA_W), f32), pltpu.VMEM((tl, SWA_W), f32),
        ],
        compiler_params=pltpu.CompilerParams(
            dimension_semantics=("arbitrary", "arbitrary"), vmem_limit_bytes=PROMPT_VMEM_LIMIT),
        name="prompt_layer",
    )(w["sinks"], x, w["norm_in"], w["w_all"], w["w_up"], w["b_gate"], cmat, lmask, tril, kmask, qhot, split,
      w["gla_norm"], w["w_out"], w["norm_f"],
      smp_proj.reshape(bsz * n_t, g, W_MAIN), smp_decay.reshape(bsz * n_t, g, GLA_KW),
      w["sinks"].reshape(SWA_HEADS, 1), smp_state, smp_ck, smp_cv)


def _sample_proj_kernel(x_ref, nin_ref, w_ref, wup_ref, bg_ref, proj_ref, decay_ref):
    u = _rms(x_ref[...], nin_ref[...]).astype(bf16)
    proj_ref[...] = jnp.dot(u, w_ref[:, :W_MAIN], preferred_element_type=f32)
    glow = jnp.dot(u, w_ref[:, O_LOW:O_LOW + LANES], preferred_element_type=f32)
    decay_ref[...] = jnp.exp(_log_decay(glow, wup_ref, bg_ref))


def _sample_proj_call(xs, w):
    n = xs.shape[0]
    return pl.pallas_call(
        _sample_proj_kernel,
        out_shape=[jax.ShapeDtypeStruct((n, W_MAIN), f32), jax.ShapeDtypeStruct((n, GLA_KW), f32)],
        compiler_params=pltpu.CompilerParams(vmem_limit_bytes=VMEM_LIMIT),
        name="sample_proj",
    )(xs, w["norm_in"], w["w_all"], w["w_up"], w["b_gate"])


def _split3(x):
    as_bf16 = lambda v: v.astype(bf16).astype(f32)
    hi = as_bf16(x)
    mid = as_bf16(x - hi)
    return hi, mid, as_bf16(x - hi - mid)


def _sample_state_update(proj_ref, decay_ref, sink_ref, s_ref, ck_ref, cv_ref,
                         so_ref, cko_ref, cvo_ref, og_ref, os_ref):
    row = lax.broadcasted_iota(jnp.int32, (PACK, GLA_KW), 0)
    head_of_lane = lax.broadcasted_iota(jnp.int32, (PACK, GLA_KW), 1) // GLA_DK
    own_head = head_of_lane == row
    row_v = lax.broadcasted_iota(jnp.int32, (PACK, GLA_DV), 0)
    lane_v = lax.broadcasted_iota(jnp.int32, (PACK, GLA_DV), 1)
    piece_rows = ((row_v >= GLA_HEADS) & (row_v < GLA_HEADS + 3)).astype(f32)
    last_lane_rows = ((row_v < 3) & (lane_v == WINDOW - 1)).astype(bf16)
    newest = lax.broadcasted_iota(jnp.int32, (SWA_KVW, WINDOW), 1) == WINDOW - 1
    row_q = lax.broadcasted_iota(jnp.int32, (SWA_HEADS, LANES), 0)
    own_kv = (lax.broadcasted_iota(jnp.int32, (SWA_HEADS, LANES), 1) // SWA_HD) == (row_q // SWA_GROUP)
    sink = sink_ref[...]
    contract_rows = (((0,), (0,)), ((), ()))
    seqs = range(proj_ref.shape[0])
    lts, rts, qms, lt2s, q8s = [], [], [], [], []
    for j in seqs:
        pr = proj_ref[j:j + 1, :]
        bcast = lambda lo, width: jnp.broadcast_to(pr[:, lo:lo + width], (PACK, width))
        a_hi, a_mid, a_lo = _split3(jnp.broadcast_to(decay_ref[j:j + 1, :], (PACK, GLA_KW)))
        a_piece = jnp.where(row == GLA_HEADS, a_hi, jnp.where(row == GLA_HEADS + 1, a_mid, a_lo))
        lts.append(jnp.where(own_head, bcast(O_K, GLA_KW),
                             jnp.where((row >= GLA_HEADS) & (row < GLA_HEADS + 3), a_piece, 0.0)).astype(bf16))
        v_b = bcast(O_V, GLA_W)
        v_sel = jnp.zeros((PACK, GLA_DV), f32)
        for h in range(GLA_HEADS):
            v_sel = jnp.where(row_v == h, v_b[:, h * GLA_DV:(h + 1) * GLA_DV], v_sel)
        rts.append(jnp.concatenate([v_sel, piece_rows], axis=1).astype(bf16))
        qms.append(jnp.where(own_head, bcast(O_Q, GLA_KW), 0.0).astype(bf16))
        n_hi, n_mid, n_lo = _split3(bcast(O_SK, 2 * SWA_KVW))
        lt2s.append(jnp.where(row == 0, n_hi, jnp.where(row == 1, n_mid,
                                                        jnp.where(row == 2, n_lo, 0.0))).astype(bf16))
        sq_b = jnp.broadcast_to(pr[:, O_SQ:O_SQ + SWA_W], (SWA_HEADS, SWA_W))
        q8 = jnp.zeros((SWA_HEADS, LANES), f32)
        for gq in range(SWA_GROUP):
            q8 = jnp.where(row_q % SWA_GROUP == gq, sq_b[:, gq * LANES:(gq + 1) * LANES], q8)
        q8s.append(jnp.where(own_kv, q8, 0.0).astype(bf16))
    kv_as = [lax.dot_general(lts[j], rts[j], contract_rows, preferred_element_type=f32) for j in seqs]
    inss = [lax.dot_general(lt2s[j], last_lane_rows, contract_rows, preferred_element_type=f32) for j in seqs]
    s_news, kts, vts = [], [], []
    for j in seqs:
        s_new = kv_as[j][:, GLA_DV:] * s_ref[j] + kv_as[j][:, :GLA_DV]
        so_ref[j] = s_new
        s_news.append(s_new.astype(bf16))
        kt = jnp.where(newest, inss[j][:SWA_KVW], pltpu.roll(ck_ref[j], WINDOW - 1, axis=1))
        vt = jnp.where(newest, inss[j][SWA_KVW:], pltpu.roll(cv_ref[j], WINDOW - 1, axis=1))
        cko_ref[j] = kt
        cvo_ref[j] = vt
        kts.append(kt.astype(bf16))
        vts.append(vt.astype(bf16))
    for j in seqs:
        og_ref[j] = jnp.dot(qms[j], s_news[j], preferred_element_type=f32)[:GLA_HEADS]
    scores = [jnp.dot(q8s[j], kts[j], preferred_element_type=f32) for j in seqs]
    es, dens = [], []
    for j in seqs:
        m = jnp.maximum(jnp.max(scores[j], axis=-1, keepdims=True), sink)
        e = jnp.exp2(scores[j] - m)
        dens.append(jnp.sum(e, axis=-1, keepdims=True) + jnp.exp2(sink - m))
        es.append(e.astype(bf16))
    for j in seqs:
        o = lax.dot_general(es[j], vts[j], (((1,), (1,)), ((), ())), preferred_element_type=f32)
        os_ref[j] = o / dens[j]


def _sample_merge_kernel(x_ref, og_ref, gg_ref, os_ref, sg_ref, gn_ref, wout_ref, nf_ref, y_ref):
    y_ref[...] = _merge(x_ref[...], og_ref[...], gg_ref[...], os_ref[...], sg_ref[...],
                        gn_ref, wout_ref, nf_ref)


def _sample_merge_call(xs, og, gg, osw, sg, w):
    return pl.pallas_call(
        _sample_merge_kernel,
        out_shape=jax.ShapeDtypeStruct(xs.shape, f32),
        compiler_params=pltpu.CompilerParams(vmem_limit_bytes=VMEM_LIMIT),
        name="sample_merge",
    )(xs, og, gg, osw, sg, w["gla_norm"], w["w_out"], w["norm_f"])


def _sample_finish(xs, proj, og, os_raw, w):
    n = xs.shape[0]
    os5 = os_raw.reshape(n, SWA_KV, SWA_GROUP, SWA_KV, SWA_HD)
    os_il = jnp.stack([os5[:, kv, :, kv, :] for kv in range(SWA_KV)], axis=2).reshape(n, SWA_W)
    return _sample_merge_call(xs, og.reshape(n, GLA_W), proj[:, O_GG:O_GG + GLA_W], os_il,
                              proj[:, O_SG:O_SG + SWA_W], w)


def _cache_view(c):
    n = c.shape[1]
    return jnp.transpose(c[0], (0, 2, 3, 1)).reshape(n, SWA_KVW, WINDOW)


def _cache_unview(c):
    n = c.shape[0]
    return jnp.transpose(c.reshape(n, SWA_KV, SWA_HD, WINDOW), (0, 3, 1, 2))[None]


def kernel(x_prompt, x_sample, state_gla, cache_win_k, cache_win_v, norm_in, w_in, w_gate_up, b_gate,
           gla_norm, attn_sinks, w_out, norm_f):
    bsz = x_prompt.shape[0]
    n = x_sample.shape[0]
    w = _prep_weights(norm_in[0], w_in[0], w_gate_up[0], b_gate[0], gla_norm[0], attn_sinks[0],
                      w_out[0], norm_f)
    cmat, lmask, tril = _chunk_tables()
    xs = x_sample.reshape(n, D_MODEL)
    proj, decay = _sample_proj_call(xs, w)
    y_p, s_p, k_p, v_p, s_s, k_s, v_s, og, os_raw = _prompt_call(
        x_prompt, w, cmat, lmask, tril, *_swa_mask_tables(), proj, decay,
        state_gla[0].reshape(n, GLA_KW, GLA_DV), _cache_view(cache_win_k), _cache_view(cache_win_v))
    y_s = _sample_finish(xs, proj, og, os_raw, w)
    return (y_p, y_s.reshape(n, 1, D_MODEL),
            s_p.reshape(1, bsz, GLA_HEADS, GLA_DK, GLA_DV),
            _cache_unview(k_p), _cache_unview(v_p),
            s_s.reshape(1, n, GLA_HEADS, GLA_DK, GLA_DV),
            _cache_unview(k_s), _cache_unview(v_s))
```

```python
import functools

import numpy as np
import jax
import jax.numpy as jnp
from jax import lax
from jax.experimental import pallas as pl
from jax.experimental.pallas import tpu as pltpu

D_MODEL = 1024
GLA_HEADS = 4
GLA_DK = 64
GLA_DV = 128
GLA_KW = GLA_HEADS * GLA_DK
GLA_W = GLA_HEADS * GLA_DV
GLA_RANK = 16
GLA_TAU = 16.0
CHUNK = 64
SWA_HEADS = 8
SWA_HD = 64
SWA_KV = 2
SWA_GROUP = SWA_HEADS // SWA_KV
SWA_W = SWA_HEADS * SWA_HD
SWA_KVW = SWA_KV * SWA_HD
WINDOW = 128
EPS = 1e-6
NEG_INF = -1e30
LOG2E = 1.4426950408889634
LANES = 128

O_Q, O_K, O_V, O_GG = 0, 256, 512, 1024
O_SQ, O_SK, O_SV, O_SG, O_LOW = 1536, 2048, 2176, 2304, 2816
W_MAIN = 2816
W_ALL = W_MAIN + LANES

N_LEVELS = 6
MILD_EXP_BOUND = 80.0
TOK_BLOCK = 1024
GLA_GROUP = 4
CHUNK_MILD = 128
GLA_GROUP_MILD = 2
PACK = 16
PROMPT_VMEM_LIMIT = 60 * 1024 * 1024
VMEM_LIMIT = 32 * 1024 * 1024

f32 = jnp.float32
bf16 = jnp.bfloat16


def _chunk_tables():
    c = CHUNK
    t = np.arange(c)[None, :]
    i = np.arange(c)[:, None]
    blocks = [(t <= i), (t > i)]
    masks = []
    for l in range(N_LEVELS):
        h = c >> (l + 1)
        m = (i // (2 * h)) * (2 * h) + h
        upper = i >= m
        blocks.append(np.where(upper, (t > m) & (t <= i), (t > i) & (t <= m)))
        jj = np.arange(c)[None, :]
        masks.append((i // (2 * h) == jj // (2 * h)) & (i % (2 * h) >= h) & (jj % (2 * h) < h))
    masks.append(np.eye(c, dtype=bool))
    cm = np.concatenate(blocks, axis=0).astype(np.float32)
    cm3 = np.concatenate([cm, cm, cm], axis=1)
    lm = np.stack(masks).astype(np.float32)
    lm = np.concatenate([lm, lm], axis=1)
    tril = np.tril(np.ones((CHUNK_MILD, CHUNK_MILD), dtype=np.float32))
    tril = np.concatenate([tril, tril], axis=0)
    return jnp.asarray(cm3, dtype=bf16), jnp.asarray(lm, dtype=f32), jnp.asarray(tril, dtype=f32)


def _swa_mask_tables():
    key = np.arange(2 * WINDOW)[:, None]
    qi = np.arange(WINDOW)[None, :]
    prev_ok = (key < WINDOW) & (key > qi)
    cur_ok = (key >= WINDOW) & (key - WINDOW <= qi)
    full = np.where(prev_ok | cur_ok, 0.0, NEG_INF)
    first = np.where(cur_ok, 0.0, NEG_INF)
    onehot = (np.arange(2 * WINDOW)[:, None] % WINDOW == qi).astype(np.float32)
    row_q = np.arange(2 * WINDOW)[:, None] % WINDOW
    slot_is_prev = (np.arange(WINDOW)[None, :] > row_q).astype(np.float32)
    split = np.stack([slot_is_prev, 1.0 - slot_is_prev])
    return (jnp.asarray(np.stack([full, first]), dtype=bf16), jnp.asarray(onehot, dtype=bf16),
            jnp.asarray(split, dtype=bf16))


I_Q, I_K, I_V, I_GG, I_LOW, I_SQ, I_SK, I_SV, I_SG = (
    int(v) for v in np.cumsum([0, 256, 256, 512, 512, 16, 512, 128, 128])[:9])


def _interleaved_rows(base, tile):
    return [(base + kv * SWA_GROUP * SWA_HD + tile * SWA_HD, SWA_HD) for kv in range(SWA_KV)]


def _weight_layout_kernel(wt_ref, wo_ref, wall_ref, wop_ref):
    def put(col, pieces, scale=None):
        rows = [wt_ref[r:r + n, :] for r, n in pieces]
        missing = LANES - sum(n for _, n in pieces)
        if missing:
            rows.append(jnp.zeros((missing, D_MODEL), f32))
        blk = jnp.concatenate(rows, axis=0) if len(rows) > 1 else rows[0]
        if scale is not None:
            blk = blk * scale
        wall_ref[:, col:col + LANES] = blk.T.astype(bf16)

    for i in range(GLA_KW // LANES):
        put(O_Q + i * LANES, [(I_Q + i * LANES, LANES)], GLA_DK ** -0.5)
        put(O_K + i * LANES, [(I_K + i * LANES, LANES)])
    for i in range(GLA_W // LANES):
        put(O_V + i * LANES, [(I_V + i * LANES, LANES)])
        put(O_GG + i * LANES, [(I_GG + i * LANES, LANES)])
        put(O_SQ + i * LANES, _interleaved_rows(I_SQ, i), SWA_HD ** -0.5 * LOG2E)
        put(O_SG + i * LANES, _interleaved_rows(I_SG, i))
    put(O_SK, [(I_SK, SWA_KVW)])
    put(O_SV, [(I_SV, SWA_KVW)])
    put(O_LOW, [(I_LOW, GLA_RANK)])
    wop_ref[:GLA_W, :] = wo_ref[:GLA_W, :].astype(bf16)
    for i in range(SWA_GROUP):
        for kv, (r, n) in enumerate(_interleaved_rows(GLA_W, i)):
            dst = GLA_W + (i * SWA_KV + kv) * SWA_HD
            wop_ref[dst:dst + n, :] = wo_ref[r:r + n, :].astype(bf16)


def _prep_weights(norm_in, w_in, w_gate_up, b_gate, gla_norm, attn_sinks, w_out, norm_f):
    w_all, w_out_p = pl.pallas_call(
        _weight_layout_kernel,
        out_shape=[jax.ShapeDtypeStruct((D_MODEL, W_ALL), bf16), jax.ShapeDtypeStruct((D_MODEL, D_MODEL), bf16)],
        compiler_params=pltpu.CompilerParams(vmem_limit_bytes=VMEM_LIMIT),
        name="weight_layout",
    )(w_in.T, w_out)
    w_up = jnp.pad(w_gate_up, ((0, LANES - GLA_RANK), (0, 0))).astype(bf16)
    return dict(
        norm_in=norm_in.reshape(1, D_MODEL), w_all=w_all, w_up=w_up,
        b_gate=b_gate.reshape(1, GLA_KW), gla_norm=jnp.tile(gla_norm, GLA_HEADS).reshape(1, GLA_W),
        sinks=attn_sinks * LOG2E, w_out=w_out_p, norm_f=norm_f.reshape(1, D_MODEL))


def _rms(x, gain):
    return x * lax.rsqrt(jnp.mean(x * x, axis=-1, keepdims=True) + EPS) * gain


def _log_decay(glow, wup_ref, bg_ref):
    z = jnp.dot(glow.astype(bf16), wup_ref[...], preferred_element_type=f32) + bg_ref[...]
    return (jnp.minimum(z, 0.0) - jnp.log(1.0 + jnp.exp(-jnp.abs(z)))) * (1.0 / GLA_TAU)


def _silu(x):
    return x * jax.nn.sigmoid(x)


def _merge(x, o_gla, gg, o_swa, sg, gn_ref, wout_ref, nf_ref):
    parts = []
    for h in range(GLA_HEADS):
        sl = slice(h * GLA_DV, (h + 1) * GLA_DV)
        parts.append(_rms(o_gla[:, sl], gn_ref[:, sl]) * _silu(gg[:, sl]))
    parts.append(o_swa * _silu(sg))
    um = jnp.concatenate(parts, axis=1).astype(bf16)
    hres = x + jnp.dot(um, wout_ref[...], preferred_element_type=f32)
    return _rms(hres, nf_ref[...])


def _prompt_kernel(sinks_ref, x_ref, nin_ref, w_ref, wup_ref, bg_ref, cmat_ref, lmask_ref, tril_ref, kmask_ref, qhot_ref,
                   split_ref,
                   gn_ref, wout_ref, nf_ref,
                   smp_proj_ref, smp_decay_ref, smp_sink_ref, smp_s_ref, smp_ck_ref, smp_cv_ref,
                   y_ref, sp_ref, kn_ref, vn_ref,
                   smp_so_ref, smp_cko_ref, smp_cvo_ref, smp_og_ref, smp_os_ref,
                   s_ref, kprev_ref, vprev_ref, p_s, g_s, ogla_s, oswa_s, *, n_t):
    t = pl.program_id(1)
    tl = x_ref.shape[0]

    @pl.when(t == 0)
    def _():
        s_ref[...] = jnp.zeros_like(s_ref)
        kprev_ref[...] = jnp.zeros_like(kprev_ref)
        vprev_ref[...] = jnp.zeros_like(vprev_ref)

    def project(rg):
        u = _rms(x_ref[rg, :], nin_ref[...]).astype(bf16)
        p_s[rg, :] = jnp.dot(u, w_ref[...], preferred_element_type=f32)
        g_s[rg, :] = _log_decay(p_s[rg, O_LOW:O_LOW + LANES], wup_ref, bg_ref)

    lane_lo = lax.broadcasted_iota(jnp.int32, (CHUNK, LANES), 1) < GLA_DK
    lane_lo_m = lax.broadcasted_iota(jnp.int32, (CHUNK_MILD, LANES), 1) < GLA_DK

    n_pairs = GLA_HEADS // 2

    def group_terms(chunks):
        rows = {c: slice(c * CHUNK, (c + 1) * CHUNK) for c in chunks}
        sums = {}
        for c in chunks:
            gc = g_s[rows[c], :]
            g_hi = gc.astype(bf16)
            r1 = gc - g_hi.astype(f32)
            g_mid = r1.astype(bf16)
            g_lo = (r1 - g_mid.astype(f32)).astype(bf16)
            sums[c] = jnp.dot(cmat_ref[...], jnp.concatenate([g_hi, g_mid, g_lo], axis=0),
                              preferred_element_type=f32)
        level_ops, misc = {}, {}
        for c in chunks:
            qc, kc = p_s[rows[c], O_Q:O_Q + GLA_KW], p_s[rows[c], O_K:O_K + GLA_KW]
            e_b = jnp.exp(sums[c][0:CHUNK])
            misc[c] = (e_b, qc * e_b, kc * jnp.exp(sums[c][CHUNK:2 * CHUNK]),
                       p_s[rows[c], O_V:O_V + GLA_W].astype(bf16))
            for p in range(n_pairs):
                ln = slice(p * LANES, (p + 1) * LANES)
                qp, kp = qc[:, ln], kc[:, ln]
                for l in range(N_LEVELS + 1):
                    if l < N_LEVELS:
                        e = jnp.exp(sums[c][(2 + l) * CHUNK:(3 + l) * CHUNK, ln])
                        qe, ke = (qp * e).astype(bf16), (kp * e).astype(bf16)
                    else:
                        qe, ke = qp.astype(bf16), kp.astype(bf16)
                    lhs = jnp.concatenate([jnp.where(lane_lo, qe, 0), jnp.where(lane_lo, 0, qe)], axis=0)
                    level_ops[c, p, l] = (lhs, ke)
        scores = {key: lax.dot_general(lhs, ke, (((1,), (1,)), ((), ())), preferred_element_type=f32)
                  for key, (lhs, ke) in level_ops.items()}
        terms = {}
        for c in chunks:
            e_b, qb, k_suf, vb = misc[c]
            terms[c] = []
            for p in range(n_pairs):
                ln = slice(p * LANES, (p + 1) * LANES)
                a = scores[c, p, 0] * lmask_ref[0]
                for l in range(1, N_LEVELS + 1):
                    a = a + scores[c, p, l] * lmask_ref[l]
                ab = a.astype(bf16)
                qbp = qb[:, ln].astype(bf16)
                lhs_heads = []
                for hh in range(2):
                    qbm = jnp.where(lane_lo, qbp, 0) if hh == 0 else jnp.where(lane_lo, 0, qbp)
                    lhs_heads.append(jnp.concatenate([qbm, ab[hh * CHUNK:(hh + 1) * CHUNK, :]], axis=1))
                upd = lax.dot_general(k_suf[:, ln].astype(bf16), vb[:, p * 2 * GLA_DV:(p + 1) * 2 * GLA_DV],
                                      (((0,), (0,)), ((), ())), preferred_element_type=f32)
                upd = jnp.concatenate(
                    [upd[0:GLA_DK, 0:GLA_DV], upd[GLA_DK:2 * GLA_DK, GLA_DV:2 * GLA_DV]], axis=0)
                e_col = jnp.broadcast_to(e_b[CHUNK - 1:CHUNK, ln], (LANES, LANES)).T
                terms[c].append((lhs_heads, vb, upd, e_col))
        return terms

    def group_terms_mild(chunks):
        cs = CHUNK_MILD
        rows = {c: slice(c * cs, (c + 1) * cs) for c in chunks}
        prefix = {}
        for c in chunks:
            halves = []
            for r0 in range(c * cs, (c + 1) * cs, CHUNK):
                gc = g_s[r0:r0 + CHUNK, :]
                g_hi = gc.astype(bf16)
                r1 = gc - g_hi.astype(f32)
                g_mid = r1.astype(bf16)
                g_lo = (r1 - g_mid.astype(f32)).astype(bf16)
                b_half = jnp.dot(cmat_ref[0:CHUNK, :], jnp.concatenate([g_hi, g_mid, g_lo], axis=0),
                                 preferred_element_type=f32)
                halves.append(b_half if not halves else b_half + halves[-1][CHUNK - 1:CHUNK, :])
            prefix[c] = jnp.concatenate(halves, axis=0)
        score_ops, misc = {}, {}
        for c in chunks:
            qc, kc = p_s[rows[c], O_Q:O_Q + GLA_KW], p_s[rows[c], O_K:O_K + GLA_KW]
            b = prefix[c]
            e_b = jnp.exp(b)
            qb, k_neg = qc * e_b, kc * jnp.exp(-b)
            misc[c] = (e_b, kc * jnp.exp(b[cs - 1:cs, :] - b), p_s[rows[c], O_V:O_V + GLA_W].astype(bf16))
            for p in range(n_pairs):
                ln = slice(p * LANES, (p + 1) * LANES)
                qbp = qb[:, ln].astype(bf16)
                qbms = [jnp.where(lane_lo_m, qbp, 0), jnp.where(lane_lo_m, 0, qbp)]
                score_ops[c, p] = (qbms, k_neg[:, ln].astype(bf16))
        scores = {key: lax.dot_general(jnp.concatenate(qbms, axis=0), ke, (((1,), (1,)), ((), ())),
                                       preferred_element_type=f32)
                  for key, (qbms, ke) in score_ops.items()}
        terms = {}
        for c in chunks:
            e_b, k_suf, vb = misc[c]
            terms[c] = []
            for p in range(n_pairs):
                ln = slice(p * LANES, (p + 1) * LANES)
                ab = (scores[c, p] * tril_ref[...]).astype(bf16)
                qbms = score_ops[c, p][0]
                lhs_heads = [jnp.concatenate([qbms[hh], ab[hh * cs:(hh + 1) * cs, :]], axis=1)
                             for hh in range(2)]
                upd = lax.dot_general(k_suf[:, ln].astype(bf16), vb[:, p * 2 * GLA_DV:(p + 1) * 2 * GLA_DV],
                                      (((0,), (0,)), ((), ())), preferred_element_type=f32)
                upd = jnp.concatenate(
                    [upd[0:GLA_DK, 0:GLA_DV], upd[GLA_DK:2 * GLA_DK, GLA_DV:2 * GLA_DV]], axis=0)
                e_col = jnp.broadcast_to(e_b[cs - 1:cs, ln], (LANES, LANES)).T
                terms[c].append((lhs_heads, vb, upd, e_col))
        return terms

    def gla(chunks, group_fn, group, cs):
        terms = {}
        for c0 in range(chunks[0], chunks[-1] + 1, group):
            terms.update(group_fn(range(c0, c0 + group)))
        states = {}
        for p in range(n_pairs):
            ln = slice(p * LANES, (p + 1) * LANES)
            s_pair = s_ref[ln, :]
            for c in chunks:
                states[c, p] = s_pair.astype(bf16)
                _, _, upd, e_col = terms[c][p]
                s_pair = e_col * s_pair + upd
            s_ref[ln, :] = s_pair
        for c in chunks:
            for p in range(n_pairs):
                lhs_heads, vb, _, _ = terms[c][p]
                for hh in range(2):
                    h = 2 * p + hh
                    rhs = jnp.concatenate([states[c, p], vb[:, h * GLA_DV:(h + 1) * GLA_DV]], axis=0)
                    ogla_s[c * cs:(c + 1) * cs, h * GLA_DV:(h + 1) * GLA_DV] = jnp.dot(
                        lhs_heads[hh], rhs, preferred_element_type=f32)

    lane_lo_w = lax.broadcasted_iota(jnp.int32, (WINDOW, LANES), 1) < SWA_HD
    row_lo = lax.broadcasted_iota(jnp.int32, (2 * WINDOW, 1), 0) < WINDOW

    def swa(blocks):
        for blk in blocks:
            rs = slice(blk * WINDOW, (blk + 1) * WINDOW)
            sq = p_s[rs, O_SQ:O_SQ + SWA_W].astype(bf16)
            k_cur, v_cur = p_s[rs, O_SK:O_SK + SWA_KVW], p_s[rs, O_SV:O_SV + SWA_KVW]
            k2 = jnp.concatenate([kprev_ref[...], k_cur], axis=0).astype(bf16)
            v2 = jnp.concatenate([vprev_ref[...], v_cur], axis=0).astype(bf16)
            kmask = kmask_ref[jnp.where(t > 0, 0, 1)] if blk == 0 else kmask_ref[0]
            k2m = jnp.concatenate([k2, kmask], axis=1)
            for tt in range(SWA_GROUP):
                qt = sq[:, tt * LANES:(tt + 1) * LANES]
                lhs = jnp.concatenate([jnp.where(lane_lo_w, qt, 0), jnp.where(lane_lo_w, 0, qt)], axis=0)
                lhs = jnp.concatenate([lhs, qhot_ref[...]], axis=1)
                s = lax.dot_general(lhs, k2m, (((1,), (1,)), ((), ())), preferred_element_type=f32)
                s = jnp.maximum(s[:, :WINDOW], s[:, WINDOW:])
                sink = jnp.where(row_lo, sinks_ref[tt], sinks_ref[SWA_GROUP + tt])
                m = jnp.maximum(jnp.max(s, axis=-1, keepdims=True), sink)
                e = jnp.exp2(s - m)
                r = 1.0 / (jnp.sum(e, axis=-1, keepdims=True) + jnp.exp2(sink - m))
                eb = e.astype(bf16)
                e2 = jnp.concatenate([eb * split_ref[0], eb * split_ref[1]], axis=1)
                o2 = jnp.dot(e2, v2, preferred_element_type=f32)
                oswa_s[rs, tt * LANES:(tt + 1) * LANES] = jnp.where(
                    lane_lo_w, o2[:WINDOW] * r[:WINDOW], o2[WINDOW:] * r[WINDOW:])
            kprev_ref[...] = k_cur
            vprev_ref[...] = v_cur

    def merge_gla(rg):
        parts = [_rms(ogla_s[rg, h * GLA_DV:(h + 1) * GLA_DV], gn_ref[:, h * GLA_DV:(h + 1) * GLA_DV])
                 * _silu(p_s[rg, O_GG + h * GLA_DV:O_GG + (h + 1) * GLA_DV]) for h in range(GLA_HEADS)]
        um = jnp.concatenate(parts, axis=1).astype(bf16)
        y_ref[rg, :] = x_ref[rg, :] + jnp.dot(um, wout_ref[:GLA_W, :], preferred_element_type=f32)

    def merge_swa(rg):
        um = (oswa_s[rg, :] * _silu(p_s[rg, O_SG:O_SG + SWA_W])).astype(bf16)
        hres = y_ref[rg, :] + jnp.dot(um, wout_ref[GLA_W:, :], preferred_element_type=f32)
        y_ref[rg, :] = _rms(hres, nf_ref[...])

    whole = slice(0, tl)
    project(whole)
    _sample_state_update(smp_proj_ref, smp_decay_ref, smp_sink_ref, smp_s_ref, smp_ck_ref, smp_cv_ref,
                         smp_so_ref, smp_cko_ref, smp_cvo_ref, smp_og_ref, smp_os_ref)
    mild = jnp.min(g_s[...]) * CHUNK_MILD > -MILD_EXP_BOUND
    pl.when(mild)(lambda: gla(range(tl // CHUNK_MILD), group_terms_mild, GLA_GROUP_MILD, CHUNK_MILD))
    pl.when(jnp.logical_not(mild))(lambda: gla(range(tl // CHUNK), group_terms, GLA_GROUP, CHUNK))
    merge_gla(whole)
    swa(range(tl // WINDOW))
    merge_swa(whole)

    @pl.when(t == n_t - 1)
    def _():
        sp_ref[...] = s_ref[...]
        kn_ref[...] = kprev_ref[...].T
        vn_ref[...] = vprev_ref[...].T


def _prompt_call(x, w, cmat, lmask, tril, kmask, qhot, split, smp_proj, smp_decay, smp_state, smp_ck, smp_cv):
    bsz, seq, _ = x.shape
    tl = TOK_BLOCK
    n_t = seq // tl
    n = smp_state.shape[0]
    g = n // (bsz * n_t)
    assert g * bsz * n_t == n
    step = lambda b, t: b * n_t + t
    smp = lambda *tail: pl.BlockSpec((g,) + tail, lambda b, t: (step(b, t),) + (0,) * len(tail))
    smp_rows = lambda width: pl.BlockSpec((None, g, width), lambda b, t: (step(b, t), 0, 0))
    const = lambda shape: pl.BlockSpec(shape, lambda b, t: (0,) * len(shape), pipeline_mode=pl.Buffered(1))
    return pl.pallas_call(
        functools.partial(_prompt_kernel, n_t=n_t),
        grid=(bsz, n_t),
        in_specs=[
            pl.BlockSpec(memory_space=pltpu.SMEM),
            pl.BlockSpec((None, tl, D_MODEL), lambda b, t: (b, t, 0)),
            const((1, D_MODEL)), const((D_MODEL, W_ALL)), const((LANES, GLA_KW)), const((1, GLA_KW)),
            const(cmat.shape), const(lmask.shape), const(tril.shape), const(kmask.shape), const(qhot.shape),
            const(split.shape),
            const((1, GLA_W)), const((D_MODEL, D_MODEL)), const((1, D_MODEL)),
            smp_rows(W_MAIN), smp_rows(GLA_KW), const((SWA_HEADS, 1)),
            smp(GLA_KW, GLA_DV), smp(SWA_KVW, WINDOW), smp(SWA_KVW, WINDOW),
        ],
        out_specs=[
            pl.BlockSpec((None, tl, D_MODEL), lambda b, t: (b, t, 0)),
            pl.BlockSpec((None, GLA_KW, GLA_DV), lambda b, t: (b, 0, 0)),
            pl.BlockSpec((None, WINDOW, SWA_KVW), lambda b, t: (b, 0, 0)),
            pl.BlockSpec((None, WINDOW, SWA_KVW), lambda b, t: (b, 0, 0)),
            smp(GLA_KW, GLA_DV), smp(SWA_KVW, WINDOW), smp(SWA_KVW, WINDOW),
            smp(GLA_HEADS, GLA_DV), smp(SWA_HEADS, LANES),
        ],
        out_shape=[
            jax.ShapeDtypeStruct((bsz, seq, D_MODEL), f32),
            jax.ShapeDtypeStruct((bsz, GLA_KW, GLA_DV), f32),
            jax.ShapeDtypeStruct((bsz, WINDOW, SWA_KVW), f32),
            jax.ShapeDtypeStruct((bsz, WINDOW, SWA_KVW), f32),
            jax.ShapeDtypeStruct(smp_state.shape, f32), jax.ShapeDtypeStruct(smp_ck.shape, f32),
            jax.ShapeDtypeStruct(smp_cv.shape, f32),
            jax.ShapeDtypeStruct((n, GLA_HEADS, GLA_DV), f32), jax.ShapeDtypeStruct((n, SWA_HEADS, LANES), f32),
        ],
        scratch_shapes=[
            pltpu.VMEM((GLA_KW, GLA_DV), f32),
            pltpu.VMEM((WINDOW, SWA_KVW), f32), pltpu.VMEM((WINDOW, SWA_KVW), f32),
            pltpu.VMEM((tl, W_ALL), f32), pltpu.VMEM((tl, GLA_KW), f32),
            pltpu.VMEM((tl, GLA_W), f32), pltpu.VMEM((tl, SWA_W), f32),
        ],
        compiler_params=pltpu.CompilerParams(
            dimension_semantics=("arbitrary", "arbitrary"), vmem_limit_bytes=PROMPT_VMEM_LIMIT),
        name="prompt_layer",
    )(w["sinks"], x, w["norm_in"], w["w_all"], w["w_up"], w["b_gate"], cmat, lmask, tril, kmask, qhot, split,
      w["gla_norm"], w["w_out"], w["norm_f"],
      smp_proj.reshape(bsz * n_t, g, W_MAIN), smp_decay.reshape(bsz * n_t, g, GLA_KW),
      w["sinks"].reshape(SWA_HEADS, 1), smp_state, smp_ck, smp_cv)


def _sample_proj_kernel(x_ref, nin_ref, w_ref, wup_ref, bg_ref, proj_ref, decay_ref):
    u = _rms(x_ref[...], nin_ref[...]).astype(bf16)
    proj_ref[...] = jnp.dot(u, w_ref[:, :W_MAIN], preferred_element_type=f32)
    glow = jnp.dot(u, w_ref[:, O_LOW:O_LOW + LANES], preferred_element_type=f32)
    decay_ref[...] = jnp.exp(_log_decay(glow, wup_ref, bg_ref))


def _sample_proj_call(xs, w):
    n = xs.shape[0]
    return pl.pallas_call(
        _sample_proj_kernel,
        out_shape=[jax.ShapeDtypeStruct((n, W_MAIN), f32), jax.ShapeDtypeStruct((n, GLA_KW), f32)],
        compiler_params=pltpu.CompilerParams(vmem_limit_bytes=VMEM_LIMIT),
        name="sample_proj",
    )(xs, w["norm_in"], w["w_all"], w["w_up"], w["b_gate"])


def _split3(x):
    as_bf16 = lambda v: v.astype(bf16).astype(f32)
    hi = as_bf16(x)
    mid = as_bf16(x - hi)
    return hi, mid, as_bf16(x - hi - mid)


def _sample_state_update(proj_ref, decay_ref, sink_ref, s_ref, ck_ref, cv_ref,
                         so_ref, cko_ref, cvo_ref, og_ref, os_ref):
    row = lax.broadcasted_iota(jnp.int32, (PACK, GLA_KW), 0)
    head_of_lane = lax.broadcasted_iota(jnp.int32, (PACK, GLA_KW), 1) // GLA_DK
    own_head = head_of_lane == row
    row_v = lax.broadcasted_iota(jnp.int32, (PACK, GLA_DV), 0)
    lane_v = lax.broadcasted_iota(jnp.int32, (PACK, GLA_DV), 1)
    piece_rows = ((row_v >= GLA_HEADS) & (row_v < GLA_HEADS + 3)).astype(f32)
    last_lane_rows = ((row_v < 3) & (lane_v == WINDOW - 1)).astype(bf16)
    newest = lax.broadcasted_iota(jnp.int32, (SWA_KVW, WINDOW), 1) == WINDOW - 1
    row_q = lax.broadcasted_iota(jnp.int32, (SWA_HEADS, LANES), 0)
    own_kv = (lax.broadcasted_iota(jnp.int32, (SWA_HEADS, LANES), 1) // SWA_HD) == (row_q // SWA_GROUP)
    sink = sink_ref[...]
    contract_rows = (((0,), (0,)), ((), ()))
    seqs = range(proj_ref.shape[0])
    lts, rts, qms, lt2s, q8s = [], [], [], [], []
    for j in seqs:
        pr = proj_ref[j:j + 1, :]
        bcast = lambda lo, width: jnp.broadcast_to(pr[:, lo:lo + width], (PACK, width))
        a_hi, a_mid, a_lo = _split3(jnp.broadcast_to(decay_ref[j:j + 1, :], (PACK, GLA_KW)))
        a_piece = jnp.where(row == GLA_HEADS, a_hi, jnp.where(row == GLA_HEADS + 1, a_mid, a_lo))
        lts.append(jnp.where(own_head, bcast(O_K, GLA_KW),
                             jnp.where((row >= GLA_HEADS) & (row < GLA_HEADS + 3), a_piece, 0.0)).astype(bf16))
        v_b = bcast(O_V, GLA_W)
        v_sel = jnp.zeros((PACK, GLA_DV), f32)
        for h in range(GLA_HEADS):
            v_sel = jnp.where(row_v == h, v_b[:, h * GLA_DV:(h + 1) * GLA_DV], v_sel)
        rts.append(jnp.concatenate([v_sel, piece_rows], axis=1).astype(bf16))
        qms.append(jnp.where(own_head, bcast(O_Q, GLA_KW), 0.0).astype(bf16))
        n_hi, n_mid, n_lo = _split3(bcast(O_SK, 2 * SWA_KVW))
        lt2s.append(jnp.where(row == 0, n_hi, jnp.where(row == 1, n_mid,
                                                        jnp.where(row == 2, n_lo, 0.0))).astype(bf16))
        sq_b = jnp.broadcast_to(pr[:, O_SQ:O_SQ + SWA_W], (SWA_HEADS, SWA_W))
        q8 = jnp.zeros((SWA_HEADS, LANES), f32)
        for gq in range(SWA_GROUP):
            q8 = jnp.where(row_q % SWA_GROUP == gq, sq_b[:, gq * LANES:(gq + 1) * LANES], q8)
        q8s.append(jnp.where(own_kv, q8, 0.0).astype(bf16))
    kv_as = [lax.dot_general(lts[j], rts[j], contract_rows, preferred_element_type=f32) for j in seqs]
    inss = [lax.dot_general(lt2s[j], last_lane_rows, contract_rows, preferred_element_type=f32) for j in seqs]
    s_news, kts, vts = [], [], []
    for j in seqs:
        s_new = kv_as[j][:, GLA_DV:] * s_ref[j] + kv_as[j][:, :GLA_DV]
        so_ref[j] = s_new
        s_news.append(s_new.astype(bf16))
        kt = jnp.where(newest, inss[j][:SWA_KVW], pltpu.roll(ck_ref[j], WINDOW - 1, axis=1))
        vt = jnp.where(newest, inss[j][SWA_KVW:], pltpu.roll(cv_ref[j], WINDOW - 1, axis=1))
        cko_ref[j] = kt
        cvo_ref[j] = vt
        kts.append(kt.astype(bf16))
        vts.append(vt.astype(bf16))
    for j in seqs:
        og_ref[j] = jnp.dot(qms[j], s_news[j], preferred_element_type=f32)[:GLA_HEADS]
    scores = [jnp.dot(q8s[j], kts[j], preferred_element_type=f32) for j in seqs]
    es, dens = [], []
    for j in seqs:
        m = jnp.maximum(jnp.max(scores[j], axis=-1, keepdims=True), sink)
        e = jnp.exp2(scores[j] - m)
        dens.append(jnp.sum(e, axis=-1, keepdims=True) + jnp.exp2(sink - m))
        es.append(e.astype(bf16))
    for j in seqs:
        o = lax.dot_general(es[j], vts[j], (((1,), (1,)), ((), ())), preferred_element_type=f32)
        os_ref[j] = o / dens[j]


def _sample_merge_kernel(x_ref, og_ref, gg_ref, os_ref, sg_ref, gn_ref, wout_ref, nf_ref, y_ref):
    y_ref[...] = _merge(x_ref[...], og_ref[...], gg_ref[...], os_ref[...], sg_ref[...],
                        gn_ref, wout_ref, nf_ref)


def _sample_merge_call(xs, og, gg, osw, sg, w):
    return pl.pallas_call(
        _sample_merge_kernel,
        out_shape=jax.ShapeDtypeStruct(xs.shape, f32),
        compiler_params=pltpu.CompilerParams(vmem_limit_bytes=VMEM_LIMIT),
        name="sample_merge",
    )(xs, og, gg, osw, sg, w["gla_norm"], w["w_out"], w["norm_f"])


def _sample_finish(xs, proj, og, os_raw, w):
    n = xs.shape[0]
    os5 = os_raw.reshape(n, SWA_KV, SWA_GROUP, SWA_KV, SWA_HD)
    os_il = jnp.stack([os5[:, kv, :, kv, :] for kv in range(SWA_KV)], axis=2).reshape(n, SWA_W)
    return _sample_merge_call(xs, og.reshape(n, GLA_W), proj[:, O_GG:O_GG + GLA_W], os_il,
                              proj[:, O_SG:O_SG + SWA_W], w)


def _cache_view(c):
    n = c.shape[1]
    return jnp.transpose(c[0], (0, 2, 3, 1)).reshape(n, SWA_KVW, WINDOW)


def _cache_unview(c):
    n = c.shape[0]
    return jnp.transpose(c.reshape(n, SWA_KV, SWA_HD, WINDOW), (0, 3, 1, 2))[None]


def kernel(x_prompt, x_sample, state_gla, cache_win_k, cache_win_v, norm_in, w_in, w_gate_up, b_gate,
           gla_norm, attn_sinks, w_out, norm_f):
    bsz = x_prompt.shape[0]
    n = x_sample.shape[0]
    w = _prep_weights(norm_in[0], w_in[0], w_gate_up[0], b_gate[0], gla_norm[0], attn_sinks[0],
                      w_out[0], norm_f)
    cmat, lmask, tril = _chunk_tables()
    xs = x_sample.reshape(n, D_MODEL)
    proj, decay = _sample_proj_call(xs, w)
    y_p, s_p, k_p, v_p, s_s, k_s, v_s, og, os_raw = _prompt_call(
        x_prompt, w, cmat, lmask, tril, *_swa_mask_tables(), proj, decay,
        state_gla[0].reshape(n, GLA_KW, GLA_DV), _cache_view(cache_win_k), _cache_view(cache_win_v))
    y_s = _sample_finish(xs, proj, og, os_raw, w)
    return (y_p, y_s.reshape(n, 1, D_MODEL),
            s_p.reshape(1, bsz, GLA_HEADS, GLA_DK, GLA_DV),
            _cache_unview(k_p), _cache_unview(v_p),
            s_s.reshape(1, n, GLA_HEADS, GLA_DK, GLA_DV),
            _cache_unview(k_s), _cache_unview(v_s))
```
